```python
import math
import jax, jax.numpy as jnp
from jax import lax
import numpy as np

D_MODEL = 1024
BATCH = 8
SEQ = 2048
DEPTH = 2
DEC_BATCH = 128
DEC_SEQ = 4
PAST_LEN = 16384
PAGE_SIZE = 128

N_MIXERS = 2
N_CONV_LAYERS = (DEPTH + 1) // 2
N_RET_LAYERS = DEPTH // 2
CONV_DIM = D_MODEL
CONV_WIDTH = 31
RET_HEADS = 4
RET_DK = D_MODEL // RET_HEADS
RET_DV = 2 * D_MODEL // RET_HEADS
RET_CHUNK = 128
ROPE_BASE = 10000.0
N_EXPERTS = 32
TOP_K = 4
D_FF = D_MODEL
SWIGLU_LIMIT = 7.0
SWIGLU_ALPHA = 1.702
MOE_BLOCK = 128
N_MOD = 6
NORM_EPS = 1e-5

kernel_name = 'hybrid_conv_retention_moe_adaln_step'


def rmsnorm(x, g):
    xf = x.astype(jnp.float32)
    y = xf * lax.rsqrt(jnp.mean(xf * xf, axis=-1, keepdims=True) + NORM_EPS)
    return (y * g.astype(jnp.float32)).astype(x.dtype)


def adaln(c, w, b):
    m = jax.nn.silu(c) @ w + b
    return jnp.split(m[:, None, :], N_MOD, axis=-1)


def conv_module(h, buf, w1, b1, dw, dw_b, ln_g, ln_b, w2, b2):
    a, gt = jnp.split(h @ w1 + b1, 2, axis=-1)
    u = a * jax.nn.sigmoid(gt)
    ext = jnp.concatenate([buf.astype(u.dtype), u], axis=1)
    z = lax.conv_general_dilated(
        ext, dw[:, None, :].astype(ext.dtype), window_strides=(1,), padding='VALID',
        dimension_numbers=('NWC', 'WIO', 'NWC'), feature_group_count=CONV_DIM) + dw_b
    zf = z.astype(jnp.float32)
    mu = jnp.mean(zf, axis=-1, keepdims=True)
    var = jnp.mean(jnp.square(zf - mu), axis=-1, keepdims=True)
    zn = ((zf - mu) * lax.rsqrt(var + NORM_EPS) * ln_g.astype(jnp.float32)
          + ln_b.astype(jnp.float32)).astype(h.dtype)
    y = jax.nn.silu(zn) @ w2 + b2
    return y, ext[:, -(CONV_WIDTH - 1):]


def rope(x, pos):
    half = x.shape[-1] // 2
    inv_freq = ROPE_BASE ** (-jnp.arange(half, dtype=jnp.float32) / half)
    ang = pos.astype(jnp.float32)[:, None] * inv_freq[None, :]
    cos = jnp.cos(ang)[None, :, None, :]
    sin = jnp.sin(ang)[None, :, None, :]
    xf = x.astype(jnp.float32)
    x1, x2 = xf[..., :half], xf[..., half:]
    return jnp.concatenate([x1 * cos - x2 * sin, x1 * sin + x2 * cos], axis=-1)


def retention_chunk(S, q, k, v, log_gamma):
    C = q.shape[1]
    idx = jnp.arange(C, dtype=jnp.float32)
    diff = idx[:, None] - idx[None, :]
    decay = jnp.where(diff[None] >= 0,
                      jnp.exp(log_gamma[:, None, None] * jnp.maximum(diff, 0.0)[None]), 0.0)
    scores = jnp.einsum('bihd,bjhd->bhij', q, k) * decay[None]
    o = jnp.einsum('bhij,bjhv->bihv', scores, v)
    cross_decay = jnp.exp(log_gamma[None, :] * (idx[:, None] + 1.0))[None, :, :, None]
    o = o + jnp.einsum('bihd,bhdv->bihv', q, S) * cross_decay
    k_dec = k * jnp.exp(log_gamma[None, :] * (C - 1.0 - idx[:, None]))[None, :, :, None]
    S_new = jnp.exp(log_gamma * C)[None, :, None, None] * S + jnp.einsum('bjhd,bjhv->bhdv', k_dec, v)
    return S_new, o


def retention_mixer(h, S0, pos, w_in, w_o, chunk):
    B, T, _ = h.shape
    qk = RET_HEADS * RET_DK
    vd = RET_HEADS * RET_DV
    q, k, v, g = jnp.split(h @ w_in, [qk, 2 * qk, 2 * qk + vd], axis=-1)
    q = rope(q.reshape(B, T, RET_HEADS, RET_DK), pos)
    k = rope(k.reshape(B, T, RET_HEADS, RET_DK), pos) * (RET_DK ** -0.5)
    v = v.reshape(B, T, RET_HEADS, RET_DV).astype(jnp.float32)
    log_gamma = jnp.log1p(-jnp.exp2(-5.0 - jnp.arange(RET_HEADS, dtype=jnp.float32)))
    n_chunks = T // chunk

    def to_chunks(a):
        return jnp.swapaxes(a.reshape(B, n_chunks, chunk, *a.shape[2:]), 0, 1)

    def step(S, qkv):
        return retention_chunk(S, qkv[0], qkv[1], qkv[2], log_gamma)

    S, o = lax.scan(step, S0.astype(jnp.float32), (to_chunks(q), to_chunks(k), to_chunks(v)))
    o = jnp.swapaxes(o, 0, 1).reshape(B, T, RET_HEADS, RET_DV)
    mu = jnp.mean(o, axis=-1, keepdims=True)
    var = jnp.mean(jnp.square(o - mu), axis=-1, keepdims=True)
    on = ((o - mu) * lax.rsqrt(var + NORM_EPS)).reshape(B, T, vd).astype(h.dtype)
    y = (jax.nn.silu(g) * on) @ w_o
    return y, S.astype(h.dtype)


def moe(h, router_w, router_b, w_gu, b_gu, w_dn, b_dn):
    shp = h.shape
    x = h.reshape(-1, shp[-1])
    T = x.shape[0]
    logits = (x @ router_w + router_b).astype(jnp.float32)
    top_v, top_e = lax.top_k(logits, TOP_K)
    top_w = jax.nn.softmax(top_v, axis=-1)
    A = T * TOP_K
    flat_e = top_e.reshape(-1).astype(jnp.int32)
    flat_tok = jnp.repeat(jnp.arange(T, dtype=jnp.int32), TOP_K)
    flat_w = top_w.reshape(-1)
    order = jnp.argsort(flat_e)
    se = flat_e[order]
    counts = jnp.bincount(flat_e, length=N_EXPERTS).astype(jnp.int32)
    starts = jnp.cumsum(counts) - counts
    pcounts = (counts + MOE_BLOCK - 1) // MOE_BLOCK * MOE_BLOCK
    pends = jnp.cumsum(pcounts)
    pstarts = pends - pcounts
    dest = pstarts[se] + jnp.arange(A, dtype=jnp.int32) - starts[se]
    n_blocks = -(-A // MOE_BLOCK) + N_EXPERTS
    P = n_blocks * MOE_BLOCK
    row_tok = jnp.full((P,), T, jnp.int32).at[dest].set(flat_tok[order])
    row_w = jnp.zeros((P,), jnp.float32).at[dest].set(flat_w[order])
    block_start = jnp.arange(n_blocks, dtype=jnp.int32) * MOE_BLOCK
    block_e = jnp.minimum(jnp.sum(pends[None, :] <= block_start[:, None], axis=-1),
                          N_EXPERTS - 1).astype(jnp.int32)
    x_pad = jnp.concatenate([x, jnp.zeros((1, x.shape[1]), x.dtype)], axis=0)
    xb = x_pad[row_tok].reshape(n_blocks, MOE_BLOCK, x.shape[1])

    def expert_block(args):
        xblk, e = args
        hg = xblk @ w_gu[e] + b_gu[e]
        gate, up = jnp.split(hg, 2, axis=-1)
        gate = jnp.minimum(gate, SWIGLU_LIMIT)
        up = jnp.clip(up, -SWIGLU_LIMIT, SWIGLU_LIMIT)
        act = (up + 1.0) * (gate * jax.nn.sigmoid(SWIGLU_ALPHA * gate))
        return act @ w_dn[e] + b_dn[e]

    yb = lax.map(expert_block, (xb, block_e)).reshape(P, x.shape[1])
    y = jax.ops.segment_sum(yb.astype(jnp.float32) * row_w[:, None], row_tok, num_segments=T + 1)[:T]
    return y.astype(h.dtype).reshape(shp)


def setup_inputs(seed: int = 0) -> dict:
    key = jax.random.key(seed)
    ks = iter(jax.random.split(key, 32))
    f32 = jnp.float32

    def nrm(shape, scale):
        return scale * jax.random.normal(next(ks), shape, f32)

    D, C = D_MODEL, CONV_DIM
    w_in_cols = 2 * RET_HEADS * RET_DK + 2 * RET_HEADS * RET_DV
    return {
        'x_prompt': nrm((BATCH, SEQ, D), 1.0),
        'x_sample': nrm((DEC_BATCH, DEC_SEQ, D), 1.0),
        'c_prompt': nrm((BATCH, D), 1.0),
        'c_sample': nrm((DEC_BATCH, D), 1.0),
        'state_conv': nrm((N_CONV_LAYERS, DEC_BATCH, CONV_WIDTH - 1, C), 0.5),
        'state_ret': nrm((N_RET_LAYERS, DEC_BATCH, RET_HEADS, RET_DK, RET_DV), 0.1),
        'norm_mix_g': 1.0 + nrm((DEPTH, D), 0.02),
        'norm_ff_g': 1.0 + nrm((DEPTH, D), 0.02),
        'w_mod': nrm((DEPTH, D, N_MOD * D), 0.3 * D ** -0.5),
        'b_mod': nrm((DEPTH, N_MOD * D), 0.02),
        'conv_w1': nrm((N_CONV_LAYERS, D, 2 * C), D ** -0.5),
        'conv_b1': nrm((N_CONV_LAYERS, 2 * C), 0.02),
        'conv_dw': nrm((N_CONV_LAYERS, CONV_WIDTH, C), CONV_WIDTH ** -0.5),
        'conv_dw_b': nrm((N_CONV_LAYERS, C), 0.02),
        'conv_ln_g': 1.0 + nrm((N_CONV_LAYERS, C), 0.02),
        'conv_ln_b': nrm((N_CONV_LAYERS, C), 0.02),
        'conv_w2': nrm((N_CONV_LAYERS, C, D), C ** -0.5),
        'conv_b2': nrm((N_CONV_LAYERS, D), 0.02),
        'ret_w_in': nrm((N_RET_LAYERS, D, w_in_cols), D ** -0.5),
        'ret_w_o': nrm((N_RET_LAYERS, RET_HEADS * RET_DV, D), (RET_HEADS * RET_DV) ** -0.5),
        'router_w': nrm((DEPTH, D, N_EXPERTS), D ** -0.5),
        'router_b': nrm((DEPTH, N_EXPERTS), 0.01),
        'moe_w_gu': nrm((DEPTH, N_EXPERTS, D, 2 * D_FF), D ** -0.5),
        'moe_b_gu': nrm((DEPTH, N_EXPERTS, 2 * D_FF), 0.02),
        'moe_w_dn': nrm((DEPTH, N_EXPERTS, D_FF, D), D_FF ** -0.5),
        'moe_b_dn': nrm((DEPTH, N_EXPERTS, D), 0.02),
        'final_g': 1.0 + nrm((D,), 0.02),
    }


def reference(x_prompt, x_sample, c_prompt, c_sample, state_conv, state_ret,
              norm_mix_g, norm_ff_g, w_mod, b_mod,
              conv_w1, conv_b1, conv_dw, conv_dw_b, conv_ln_g, conv_ln_b, conv_w2, conv_b2,
              ret_w_in, ret_w_o,
              router_w, router_b, moe_w_gu, moe_b_gu, moe_w_dn, moe_b_dn, final_g):

    def run(x, c, conv_init, ret_init, pos, ret_chunk):
        conv_out, ret_out = [], []
        for l in range(DEPTH):
            sh_m, sc_m, g_m, sh_f, sc_f, g_f = adaln(c, w_mod[l], b_mod[l])
            h = rmsnorm(x, norm_mix_g[l]) * (1.0 + sc_m) + sh_m
            j = l // N_MIXERS
            if l % N_MIXERS == 0:
                y, st = conv_module(h, conv_init[j], conv_w1[j], conv_b1[j], conv_dw[j], conv_dw_b[j],
                                    conv_ln_g[j], conv_ln_b[j], conv_w2[j], conv_b2[j])
                conv_out.append(st)
            else:
                y, st = retention_mixer(h, ret_init[j], pos, ret_w_in[j], ret_w_o[j], ret_chunk)
                ret_out.append(st)
            x = x + g_m * y
            h = rmsnorm(x, norm_ff_g[l]) * (1.0 + sc_f) + sh_f
            x = x + g_f * moe(h, router_w[l], router_b[l], moe_w_gu[l], moe_b_gu[l],
                              moe_w_dn[l], moe_b_dn[l])
        return rmsnorm(x, final_g), jnp.stack(conv_out), jnp.stack(ret_out)

    bp, tp = x_prompt.shape[0], x_prompt.shape[1]
    ts = x_sample.shape[1]
    pos_p = jnp.arange(tp, dtype=jnp.int32)
    pos_s = PAST_LEN + jnp.arange(ts, dtype=jnp.int32)
    conv0 = jnp.zeros((N_CONV_LAYERS, bp, CONV_WIDTH - 1, CONV_DIM), x_prompt.dtype)
    ret0 = jnp.zeros((N_RET_LAYERS, bp, RET_HEADS, RET_DK, RET_DV), jnp.float32)
    y_prompt, conv_p, ret_p = run(x_prompt, c_prompt, conv0, ret0, pos_p, RET_CHUNK)
    y_sample, conv_s, ret_s = run(x_sample, c_sample, state_conv, state_ret, pos_s, ts)
    return (y_prompt, y_sample, conv_p, conv_s, ret_p, ret_s)
```

```python
import functools

import jax
import jax.numpy as jnp
from jax import lax
from jax.experimental import pallas as pl
from jax.experimental.pallas import tpu as pltpu

F32 = jnp.float32
BF16 = jnp.bfloat16
I32 = jnp.int32

D_MODEL = 1024
SEQ = 2048
BATCH = 8
DEC_BATCH = 128
DEC_SEQ = 4
PAST_LEN = 16384
CONV_WIDTH = 31
CONV_STATE = CONV_WIDTH - 1
RET_HEADS = 4
RET_DK = 256
RET_DV = 512
ROPE_BASE = 10000.0
N_EXPERTS = 32
TOP_K = 4
D_FF = 1024
SWIGLU_LIMIT = 7.0
SWIGLU_ALPHA = 1.702
N_MOD = 6
NORM_EPS = 1e-5

N_PROMPT = BATCH * SEQ
N_SAMPLE = DEC_BATCH * DEC_SEQ
N_TOK = N_PROMPT + N_SAMPLE
N_ASSIGN = N_TOK * TOP_K

TM_PROMPT = 512
TM_SAMPLE = DEC_BATCH
HALO = 32
CONV_RB = 32
CONV_LW = 256
RET_CHUNK = 256
RET_PAD = 16
MOE_TB = 256
MOE_NBLK = N_ASSIGN // MOE_TB + N_EXPERTS
MOE_ROWS = MOE_NBLK * MOE_TB
ROUTE_CH = 512
FF_CHUNK = 256
VMEM_LIMIT = 56 * 1024 * 1024


def _cparams(*sem):
    return pltpu.CompilerParams(dimension_semantics=sem, vmem_limit_bytes=VMEM_LIMIT)


def _silu(x):
    return x * jax.nn.sigmoid(x)


def _rms_mod(x, g, sc, sh):
    y = x * lax.rsqrt(jnp.mean(x * x, axis=-1, keepdims=True) + NORM_EPS) * g
    return y * (1.0 + sc) + sh


def _split_bf16(x):
    hi = x.astype(BF16)
    lo = (x - hi.astype(F32)).astype(BF16)
    return hi, lo


def _adaln_kernel(c_ref, w_ref, b_ref, o_ref):
    a = _silu(c_ref[...]).astype(BF16)
    o_ref[...] = jnp.dot(a, w_ref[...].astype(BF16), preferred_element_type=F32) + b_ref[...]


def _adaln(c_all, w_mod, b_mod):
    depth, d, nm = w_mod.shape
    n = c_all.shape[0]
    return pl.pallas_call(
        _adaln_kernel,
        grid=(depth, nm // d),
        in_specs=[
            pl.BlockSpec((n, d), lambda l, j: (0, 0)),
            pl.BlockSpec((None, d, d), lambda l, j: (l, 0, j)),
            pl.BlockSpec((None, 1, d), lambda l, j: (l, 0, j)),
        ],
        out_specs=pl.BlockSpec((None, n, d), lambda l, j: (l, 0, j)),
        out_shape=jax.ShapeDtypeStruct((depth, n, nm), F32),
        compiler_params=_cparams("parallel", "parallel"),
        name="adaln",
    )(c_all, w_mod, b_mod.reshape(depth, 1, nm))


class _Stream:
    def __init__(self, n_rows, tm, tiles_per_seq, mod):
        self.n_rows = n_rows
        self.tm = tm
        self.n_tiles = n_rows // tm
        self.tiles_per_seq = tiles_per_seq
        self.mod = mod

    def mod_spec(self, j):
        mr = self.mod.shape[1]
        if self.tiles_per_seq is None:
            return pl.BlockSpec((None, mr, D_MODEL), lambda i: (0, 0, j))
        tps = self.tiles_per_seq
        return pl.BlockSpec((None, mr, D_MODEL), lambda i: (i // tps, 0, j))

    def row_spec(self, width, col=0):
        return pl.BlockSpec((self.tm, width), lambda i: (i, col))


def _const_spec(shape):
    nd = len(shape)
    return pl.BlockSpec(shape, lambda i: (0,) * nd)


def _layer_spec(shape, layer):
    nd = len(shape)
    return pl.BlockSpec((None,) + shape, lambda i: (layer,) + (0,) * nd)


def _pre_conv_kernel(x_ref, sh_ref, sc_ref, g_ref, w1_ref, b1_ref, u_ref, w1b_ref):
    @pl.when(pl.program_id(0) == 0)
    def _():
        w1b_ref[...] = w1_ref[...].astype(BF16)

    h = _rms_mod(x_ref[...], g_ref[...], sc_ref[...], sh_ref[...])
    ag = jnp.dot(h.astype(BF16), w1b_ref[...], preferred_element_type=F32) + b1_ref[...]
    u_ref[...] = ag[:, :D_MODEL] * jax.nn.sigmoid(ag[:, D_MODEL:])


def _pre_conv(st, x, norm_g, w1, b1, layer, j):
    c2 = w1.shape[-1]
    return pl.pallas_call(
        _pre_conv_kernel,
        grid=(st.n_tiles,),
        in_specs=[
            st.row_spec(D_MODEL),
            st.mod_spec(0), st.mod_spec(1),
            _layer_spec((1, D_MODEL), layer),
            _layer_spec((D_MODEL, c2), j),
            _layer_spec((1, c2), j),
        ],
        out_specs=st.row_spec(D_MODEL),
        out_shape=jax.ShapeDtypeStruct((st.n_rows, D_MODEL), F32),
        scratch_shapes=[pltpu.VMEM((D_MODEL, c2), BF16)],
        compiler_params=_cparams("arbitrary"),
        name="pre_conv",
    )(x, st.mod, st.mod, norm_g.reshape(-1, 1, D_MODEL), w1, b1.reshape(-1, 1, c2))


def _conv_sample_kernel(buf_ref, u_ref, dw_ref, dwb_ref, z_ref, ns_ref):
    for t in range(DEC_SEQ):
        acc = jnp.broadcast_to(dwb_ref[...], u_ref.shape[1:])
        for j in range(t, CONV_STATE):
            acc = acc + buf_ref[j] * dw_ref[j - t:j - t + 1, :]
        for s in range(t + 1):
            k = CONV_STATE + s - t
            acc = acc + u_ref[s] * dw_ref[k:k + 1, :]
        z_ref[t] = acc
    ns_ref[0:CONV_STATE - DEC_SEQ] = buf_ref[DEC_SEQ:CONV_STATE]
    ns_ref[CONV_STATE - DEC_SEQ:CONV_STATE] = u_ref[...]


def _conv_sample(buf_tm, u_tm, dw, dwb, j):
    sb = 32
    c = u_tm.shape[-1]
    return pl.pallas_call(
        _conv_sample_kernel,
        grid=(DEC_BATCH // sb,),
        in_specs=[
            pl.BlockSpec((CONV_STATE, sb, c), lambda i: (0, i, 0)),
            pl.BlockSpec((DEC_SEQ, sb, c), lambda i: (0, i, 0)),
            _layer_spec((CONV_WIDTH, c), j),
            _layer_spec((1, c), j),
        ],
        out_specs=[
            pl.BlockSpec((DEC_SEQ, sb, c), lambda i: (0, i, 0)),
            pl.BlockSpec((CONV_STATE, sb, c), lambda i: (0, i, 0)),
        ],
        out_shape=[
            jax.ShapeDtypeStruct((DEC_SEQ, DEC_BATCH, c), F32),
            jax.ShapeDtypeStruct((CONV_STATE, DEC_BATCH, c), F32),
        ],
        compiler_params=_cparams("parallel"),
        name="conv_sample",
    )(buf_tm, u_tm, dw, dwb.reshape(-1, 1, c))


def _route_topk(h2, rwt_ref, rb_ref, te_ref, tw_ref):
    h_hi, h_lo = _split_bf16(h2)
    w_hi, w_lo = _split_bf16(rwt_ref[...])
    nt = (((1,), (1,)), ((), ()))
    logits = (lax.dot_general(w_hi, h_hi, nt, preferred_element_type=F32)
              + lax.dot_general(w_hi, h_lo, nt, preferred_element_type=F32)
              + lax.dot_general(w_lo, h_hi, nt, preferred_element_type=F32)
              + rb_ref[...])
    iota_e = lax.broadcasted_iota(I32, logits.shape, 0).astype(F32)
    vals, idxs = [], []
    for _ in range(TOP_K):
        m = jnp.max(logits, axis=0, keepdims=True)
        idx = jnp.min(jnp.where(logits == m, iota_e, float(N_EXPERTS)), axis=0, keepdims=True)
        logits = jnp.where(iota_e == idx, -jnp.inf, logits)
        vals.append(m)
        idxs.append(idx)
    es = [jnp.exp(v - vals[0]) for v in vals]
    tot = es[0] + es[1] + es[2] + es[3]
    for k in range(TOP_K):
        te_ref[k:k + 1, :] = idxs[k].astype(I32)
        tw_ref[k:k + 1, :] = es[k] / tot


def _mixer_post_kernel(mode, tiles_per_seq, *refs):
    it = iter(refs)
    if mode == "conv":
        u_ref, halo_ref, dw_ref, dwb_ref = next(it), next(it), next(it), next(it)
    elif mode == "z":
        z_in_ref = next(it)
    else:
        gated_ref = next(it)
    if mode in ("conv", "z"):
        lng_ref, lnb_ref = next(it), next(it)
    w_ref, b_ref = next(it), next(it)
    x_ref, gm_ref, shf_ref, scf_ref, gff_ref, rwt_ref, rb_ref = (next(it) for _ in range(7))
    x1_ref, h2_ref, te_ref, tw_ref = (next(it) for _ in range(4))
    wb_ref = next(it)
    if mode == "conv":
        ext_ref, z_ref = next(it), next(it)

    i = pl.program_id(0)

    @pl.when(i == 0)
    def _():
        wb_ref[...] = w_ref[...].astype(BF16)

    if mode == "conv":
        tm = u_ref.shape[0]
        first = (i % tiles_per_seq) == 0
        ext_ref[0:HALO, :] = jnp.where(first, 0.0, halo_ref[...])
        ext_ref[HALO:HALO + tm, :] = u_ref[...]
        off = HALO - CONV_STATE

        def row_block(r, carry):
            r0 = pl.multiple_of(r * CONV_RB, CONV_RB)
            for lc in range(D_MODEL // CONV_LW):
                ls = slice(lc * CONV_LW, (lc + 1) * CONV_LW)
                win = ext_ref[pl.ds(r0, CONV_RB + HALO), ls]
                acc = jnp.broadcast_to(dwb_ref[:, ls], (CONV_RB, CONV_LW))
                for k in range(CONV_WIDTH):
                    acc = acc + win[off + k:off + k + CONV_RB, :] * dw_ref[k:k + 1, ls]
                z_ref[pl.ds(r0, CONV_RB), ls] = acc
            return carry

        lax.fori_loop(0, tm // CONV_RB, row_block, 0)
        z = z_ref[...]
    elif mode == "z":
        z = z_in_ref[...]

    if mode in ("conv", "z"):
        mu = jnp.mean(z, axis=-1, keepdims=True)
        zc = z - mu
        var = jnp.mean(zc * zc, axis=-1, keepdims=True)
        zn = zc * lax.rsqrt(var + NORM_EPS) * lng_ref[...] + lnb_ref[...]
        a = _silu(zn).astype(BF16)
        y = jnp.dot(a, wb_ref[...], preferred_element_type=F32) + b_ref[...]
    else:
        y = jnp.dot(gated_ref[...], wb_ref[...], preferred_element_type=F32)

    x1 = x_ref[...] + gm_ref[...] * y
    x1_ref[...] = x1
    h2 = _rms_mod(x1, gff_ref[...], scf_ref[...], shf_ref[...])
    h2_ref[...] = h2
    _route_topk(h2, rwt_ref, rb_ref, te_ref, tw_ref)


def _mixer_post(mode, st, layer, j, x, mix_in, norm_ff_g, router_wt, router_b, w, b=None,
                dw=None, dwb=None, ln_g=None, ln_b=None):
    tm = st.tm
    k_in = w.shape[-2]
    args, specs = [], []
    if mode == "conv":
        per32 = tm // HALO
        args += [mix_in, mix_in, dw, dwb.reshape(-1, 1, D_MODEL)]
        specs += [st.row_spec(D_MODEL),
                  pl.BlockSpec((HALO, D_MODEL), lambda i: (jnp.maximum(i * per32 - 1, 0), 0)),
                  _layer_spec((CONV_WIDTH, D_MODEL), j), _layer_spec((1, D_MODEL), j)]
    elif mode == "z":
        args += [mix_in]
        specs += [st.row_spec(D_MODEL)]
    else:
        args += [mix_in]
        specs += [st.row_spec(k_in)]
    if mode in ("conv", "z"):
        args += [ln_g.reshape(-1, 1, D_MODEL), ln_b.reshape(-1, 1, D_MODEL)]
        specs += [_layer_spec((1, D_MODEL), j), _layer_spec((1, D_MODEL), j)]
    if b is None:
        b = jnp.zeros((w.shape[0], D_MODEL), F32)
    args += [w, b.reshape(-1, 1, D_MODEL)]
    specs += [_layer_spec((k_in, D_MODEL), j), _layer_spec((1, D_MODEL), j)]
    args += [x, st.mod, st.mod, st.mod, norm_ff_g.reshape(-1, 1, D_MODEL), router_wt,
             router_b.reshape(-1, N_EXPERTS, 1)]
    specs += [st.row_spec(D_MODEL), st.mod_spec(2), st.mod_spec(3), st.mod_spec(4),
              _layer_spec((1, D_MODEL), layer), _layer_spec((N_EXPERTS, D_MODEL), layer),
              _layer_spec((N_EXPERTS, 1), layer)]
    scratch = [pltpu.VMEM((k_in, D_MODEL), BF16)]
    if mode == "conv":
        scratch += [pltpu.VMEM((tm + HALO, D_MODEL), F32), pltpu.VMEM((tm, D_MODEL), F32)]
    return pl.pallas_call(
        functools.partial(_mixer_post_kernel, mode, st.tiles_per_seq),
        grid=(st.n_tiles,),
        in_specs=specs,
        out_specs=[st.row_spec(D_MODEL), st.row_spec(D_MODEL),
                   pl.BlockSpec((TOP_K, tm), lambda i: (0, i)),
                   pl.BlockSpec((TOP_K, tm), lambda i: (0, i))],
        out_shape=[jax.ShapeDtypeStruct((st.n_rows, D_MODEL), F32),
                   jax.ShapeDtypeStruct((st.n_rows, D_MODEL), F32),
                   jax.ShapeDtypeStruct((TOP_K, st.n_rows), I32),
                   jax.ShapeDtypeStruct((TOP_K, st.n_rows), F32)],
        scratch_shapes=scratch,
        compiler_params=_cparams("arbitrary"),
        name="mixer_post_" + mode,
    )(*args)


def _pre_ret_kernel(x_ref, sh_ref, sc_ref, g_ref, w_ref, cos_ref, sin_ref,
                    q_ref, k_ref, v_ref, sg_ref):
    h = _rms_mod(x_ref[...], g_ref[...], sc_ref[...], sh_ref[...]).astype(BF16)
    cos = cos_ref[...]
    sin = sin_ref[...]
    half = RET_DK // 2
    qk = RET_HEADS * RET_DK
    vd = RET_HEADS * RET_DV
    for out_ref, base, scale in ((q_ref, 0, 1.0), (k_ref, qk, RET_DK ** -0.5)):
        for hh in range(RET_HEADS):
            c0 = base + hh * RET_DK
            p = jnp.dot(h, w_ref[:, c0:c0 + RET_DK], preferred_element_type=F32)
            p1, p2 = p[:, :half], p[:, half:]
            o0 = hh * RET_DK
            out_ref[:, o0:o0 + half] = ((p1 * cos - p2 * sin) * scale).astype(BF16)
            out_ref[:, o0 + half:o0 + RET_DK] = ((p1 * sin + p2 * cos) * scale).astype(BF16)
    for hh in range(RET_HEADS):
        c0 = 2 * qk + hh * RET_DV
        v_ref[:, hh * RET_DV:(hh + 1) * RET_DV] = jnp.dot(
            h, w_ref[:, c0:c0 + RET_DV], preferred_element_type=F32).astype(BF16)
        g = jnp.dot(h, w_ref[:, vd + c0:vd + c0 + RET_DV], preferred_element_type=F32)
        sg_ref[:, hh * RET_DV:(hh + 1) * RET_DV] = _silu(g).astype(BF16)


def _pre_ret(st, x, norm_g, w_in_b, layer, j, cos, sin, tbl_spec):
    qk = RET_HEADS * RET_DK
    vd = RET_HEADS * RET_DV
    return pl.pallas_call(
        _pre_ret_kernel,
        grid=(st.n_tiles,),
        in_specs=[
            st.row_spec(D_MODEL),
            st.mod_spec(0), st.mod_spec(1),
            _layer_spec((1, D_MODEL), layer),
            _layer_spec((D_MODEL, 2 * qk + 2 * vd), j),
            tbl_spec, tbl_spec,
        ],
        out_specs=[st.row_spec(qk), st.row_spec(qk), st.row_spec(vd), st.row_spec(vd)],
        out_shape=[jax.ShapeDtypeStruct((st.n_rows, qk), BF16),
                   jax.ShapeDtypeStruct((st.n_rows, qk), BF16),
                   jax.ShapeDtypeStruct((st.n_rows, vd), BF16),
                   jax.ShapeDtypeStruct((st.n_rows, vd), BF16)],
        compiler_params=_cparams("parallel"),
        name="pre_ret",
    )(x, st.mod, st.mod, norm_g.reshape(-1, 1, D_MODEL), w_in_b, cos, sin)


def _retention_kernel(valid, lg_ref, q_ref, k_ref, v_ref, sg_ref, s0_ref,
                      o_ref, s_out_ref, s_ref, dec_ref):
    c = pl.program_id(2)
    ch = q_ref.shape[0]
    lg = lg_ref[0:1, 0:1]
    idx = lax.broadcasted_iota(I32, (ch, 1), 0).astype(F32)

    @pl.when(c == 0)
    def _():
        s_ref[...] = s0_ref[...]
        diff = (lax.broadcasted_iota(I32, (ch, ch), 0)
                - lax.broadcasted_iota(I32, (ch, ch), 1)).astype(F32)
        dec_ref[...] = jnp.where(diff >= 0.0, jnp.exp(lg * jnp.maximum(diff, 0.0)), 0.0)

    q = q_ref[...]
    k = k_ref[...]
    v = v_ref[...]
    s_prev = s_ref[...]
    scores = lax.dot_general(q, k, (((1,), (1,)), ((), ())),
                             preferred_element_type=F32) * dec_ref[...]
    o = jnp.dot(scores.astype(BF16), v, preferred_element_type=F32)
    cross = jnp.exp(lg * (idx + 1.0))
    o = o + jnp.dot(q, s_prev.astype(BF16), preferred_element_type=F32) * cross
    kd = (k.astype(F32) * jnp.exp(lg * (valid - 1.0 - idx))).astype(BF16)
    s_new = jnp.exp(lg * float(valid)) * s_prev + lax.dot_general(
        kd, v, (((0,), (0,)), ((), ())), preferred_element_type=F32)
    s_ref[...] = s_new

    @pl.when(c == pl.num_programs(2) - 1)
    def _():
        s_out_ref[...] = s_new

    mu = jnp.mean(o, axis=-1, keepdims=True)
    oc = o - mu
    var = jnp.mean(oc * oc, axis=-1, keepdims=True)
    on = oc * lax.rsqrt(var + NORM_EPS)
    o_ref[...] = (sg_ref[...].astype(F32) * on).astype(BF16)


def _retention(q, k, v, sg, s0, s0_layer, lg_tbl, n_seq, n_chunks, ch, valid):
    n_rows = q.shape[0]
    row = lambda b, h, c: (b * n_chunks + c, h)
    state_spec = pl.BlockSpec((None, None, None, RET_DK, RET_DV),
                              lambda b, h, c: (s0_layer, b, h, 0, 0))
    out_state_spec = pl.BlockSpec((None, None, None, RET_DK, RET_DV),
                                  lambda b, h, c: (0, b, h, 0, 0))
    return pl.pallas_call(
        functools.partial(_retention_kernel, valid),
        grid=(n_seq, RET_HEADS, n_chunks),
        in_specs=[
            pl.BlockSpec((None, 8, 128), lambda b, h, c: (h, 0, 0)),
            pl.BlockSpec((ch, RET_DK), row), pl.BlockSpec((ch, RET_DK), row),
            pl.BlockSpec((ch, RET_DV), row), pl.BlockSpec((ch, RET_DV), row),
            state_spec,
        ],
        out_specs=[pl.BlockSpec((ch, RET_DV), row), out_state_spec],
        out_shape=[jax.ShapeDtypeStruct((n_rows, RET_HEADS * RET_DV), BF16),
                   jax.ShapeDtypeStruct((1, n_seq, RET_HEADS, RET_DK, RET_DV), F32)],
        scratch_shapes=[pltpu.VMEM((RET_DK, RET_DV), F32), pltpu.VMEM((ch, ch), F32)],
        compiler_params=_cparams("parallel", "parallel", "arbitrary"),
        name="retention",
    )(lg_tbl, q, k, v, sg, s0)


def _route_kernel(te_ref, dest_ref, cnt_ref, base_ref, tri_ref):
    p = pl.program_id(0)
    c = pl.program_id(1)
    ch = te_ref.shape[1]
    iota_e = lax.broadcasted_iota(I32, (N_EXPERTS, ch), 0)
    onehots = [(te_ref[k:k + 1, :] == iota_e).astype(F32) for k in range(TOP_K)]

    @pl.when((p == 0) & (c == 0))
    def _():
        cnt_ref[...] = jnp.zeros_like(cnt_ref)

    @pl.when(p == 0)
    def _():
        tot = onehots[0] + onehots[1] + onehots[2] + onehots[3]
        cnt_ref[...] += jnp.sum(tot, axis=1, keepdims=True)

    @pl.when((p == 1) & (c == 0))
    def _():
        cnt = cnt_ref[...]
        pc = jnp.floor((cnt + (MOE_TB - 1.0)) * (1.0 / MOE_TB)) * MOE_TB
        r = lax.broadcasted_iota(I32, (N_EXPERTS, N_EXPERTS), 0)
        cc = lax.broadcasted_iota(I32, (N_EXPERTS, N_EXPERTS), 1)
        start_lane = jnp.sum(jnp.where(r < cc, pc, 0.0), axis=0, keepdims=True)
        base_ref[...] = jnp.sum(jnp.where(r == cc, start_lane, 0.0), axis=1, keepdims=True)
        s = lax.broadcasted_iota(I32, (ch, ch), 0)
        t = lax.broadcasted_iota(I32, (ch, ch), 1)
        tri_ref[...] = (s < t).astype(BF16)

    @pl.when(p == 1)
    def _():
        base = base_ref[...]
        for k in range(TOP_K):
            oh = onehots[k]
            before = jnp.dot(oh.astype(BF16), tri_ref[...], preferred_element_type=F32)
            slot = jnp.sum(oh * (base + before), axis=0, keepdims=True)
            dest_ref[k:k + 1, :] = slot.astype(I32)
            base = base + jnp.sum(oh, axis=1, keepdims=True)
        base_ref[...] = base


def _route(te):
    n = te.shape[1]
    nch = n // ROUTE_CH
    return pl.pallas_call(
        _route_kernel,
        grid=(2, nch),
        in_specs=[pl.BlockSpec((TOP_K, ROUTE_CH), lambda p, c: (0, c))],
        out_specs=[pl.BlockSpec((TOP_K, ROUTE_CH), lambda p, c: (0, c * p)),
                   pl.BlockSpec((N_EXPERTS, 1), lambda p, c: (0, 0))],
        out_shape=[jax.ShapeDtypeStruct((TOP_K, n), I32),
                   jax.ShapeDtypeStruct((N_EXPERTS, 1), F32)],
        scratch_shapes=[pltpu.VMEM((N_EXPERTS, 1), F32), pltpu.VMEM((ROUTE_CH, ROUTE_CH), BF16)],
        compiler_params=_cparams("arbitrary", "arbitrary"),
        name="moe_route",
    )(te)


def _dispatch_kernel(tok_off, dest_ref, h_ref, xb_in_ref, xb_ref, sem):
    del xb_in_ref
    tm = h_ref.shape[0]
    base = tok_off + pl.program_id(0) * tm

    def row_copy(r, d):
        return pltpu.make_async_copy(h_ref.at[pl.ds(r, 1)], xb_ref.at[pl.ds(d, 1)], sem)

    def issue(r, carry):
        for k in range(TOP_K):
            row_copy(r, dest_ref[k * N_TOK + base + r]).start()
        return carry

    def drain(r, carry):
        for k in range(TOP_K):
            row_copy(0, 0).wait()
        return carry

    lax.fori_loop(0, tm, issue, 0)
    lax.fori_loop(0, tm, drain, 0)


def _dispatch(st, tok_off, dest_flat, h, xb):
    return pl.pallas_call(
        functools.partial(_dispatch_kernel, tok_off),
        grid_spec=pltpu.PrefetchScalarGridSpec(
            num_scalar_prefetch=1,
            grid=(st.n_tiles,),
            in_specs=[pl.BlockSpec((st.tm, D_MODEL), lambda i, d: (i, 0)),
                      pl.BlockSpec(memory_space=pl.ANY)],
            out_specs=pl.BlockSpec(memory_space=pl.ANY),
            scratch_shapes=[pltpu.SemaphoreType.DMA(())],
        ),
        out_shape=jax.ShapeDtypeStruct(xb.shape, xb.dtype),
        input_output_aliases={2: 0},
        compiler_params=_cparams("arbitrary"),
        name="moe_dispatch",
    )(dest_flat, h, xb)


def _ffn_kernel(be_ref, nu_ref, x_ref, wgu_ref, bgu_ref, wdn_ref, bdn_ref, y_ref,
                wgu_b_ref, wdn_b_ref):
    j = pl.program_id(0)
    e = be_ref[j]
    e_prev = be_ref[jnp.maximum(j - 1, 0)]

    @pl.when((j == 0) | (e != e_prev))
    def _():
        wgu_b_ref[...] = wgu_ref[...].astype(BF16)
        wdn_b_ref[...] = wdn_ref[...].astype(BF16)

    @pl.when(j < nu_ref[0])
    def _():
        x = x_ref[...].astype(BF16)
        acc = jnp.broadcast_to(bdn_ref[...], y_ref.shape)
        for c in range(D_FF // FF_CHUNK):
            gs = slice(c * FF_CHUNK, (c + 1) * FF_CHUNK)
            us = slice(D_FF + c * FF_CHUNK, D_FF + (c + 1) * FF_CHUNK)
            gate = jnp.dot(x, wgu_b_ref[:, gs], preferred_element_type=F32) + bgu_ref[:, gs]
            up = jnp.dot(x, wgu_b_ref[:, us], preferred_element_type=F32) + bgu_ref[:, us]
            gate = jnp.minimum(gate, SWIGLU_LIMIT)
            up = jnp.clip(up, -SWIGLU_LIMIT, SWIGLU_LIMIT)
            act = (up + 1.0) * (gate * jax.nn.sigmoid(SWIGLU_ALPHA * gate))
            acc = acc + jnp.dot(act.astype(BF16), wdn_b_ref[gs, :], preferred_element_type=F32)
        y_ref[...] = acc


def _ffn(layer, block_e, n_used, xb, w_gu, b_gu, w_dn, b_dn):
    row_map = lambda j, be, nu: (jnp.minimum(j, nu[0] - 1), 0)
    exp_map = lambda j, be, nu: (layer, be[j], 0, 0)
    return pl.pallas_call(
        _ffn_kernel,
        grid_spec=pltpu.PrefetchScalarGridSpec(
            num_scalar_prefetch=2,
            grid=(MOE_NBLK,),
            in_specs=[
                pl.BlockSpec((MOE_TB, D_MODEL), row_map),
                pl.BlockSpec((None, None, D_MODEL, 2 * D_FF), exp_map),
                pl.BlockSpec((None, None, 1, 2 * D_FF), exp_map),
                pl.BlockSpec((None, None, D_FF, D_MODEL), exp_map),
                pl.BlockSpec((None, None, 1, D_MODEL), exp_map),
            ],
            out_specs=pl.BlockSpec((MOE_TB, D_MODEL), row_map),
            scratch_shapes=[pltpu.VMEM((D_MODEL, 2 * D_FF), BF16),
                            pltpu.VMEM((D_FF, D_MODEL), BF16)],
        ),
        out_shape=jax.ShapeDtypeStruct((MOE_ROWS, D_MODEL), F32),
        compiler_params=_cparams("arbitrary"),
        name="moe_ffn",
    )(block_e, n_used, xb, w_gu, b_gu.reshape(b_gu.shape[0], N_EXPERTS, 1, 2 * D_FF),
      w_dn, b_dn.reshape(b_dn.shape[0], N_EXPERTS, 1, D_MODEL))


def _combine_kernel(tok_off, final, dest_ref, x_ref, gf_ref, wt_ref, fg_ref, yb_ref,
                    out_ref, buf_ref, sem):
    tm = x_ref.shape[0]
    base = tok_off + pl.program_id(0) * tm

    def row_copy(k, r, d):
        return pltpu.make_async_copy(yb_ref.at[pl.ds(d, 1)], buf_ref.at[k, pl.ds(r, 1)], sem)

    def issue(r, carry):
        for k in range(TOP_K):
            row_copy(k, r, dest_ref[k * N_TOK + base + r]).start()
        return carry

    def drain(r, carry):
        for k in range(TOP_K):
            row_copy(k, 0, 0).wait()
        return carry

    lax.fori_loop(0, tm, issue, 0)
    lax.fori_loop(0, tm, drain, 0)
    wt = wt_ref[...]
    y = buf_ref[0] * wt[:, 0:1]
    for k in range(1, TOP_K):
        y = y + buf_ref[k] * wt[:, k:k + 1]
    xn = x_ref[...] + gf_ref[...] * y
    if final:
        xn = xn * lax.rsqrt(jnp.mean(xn * xn, axis=-1, keepdims=True) + NORM_EPS) * fg_ref[...]
    out_ref[...] = xn


def _combine(st, tok_off, final, dest_flat, x, wt, final_g, yb):
    tiles_off = tok_off // st.tm
    mod5 = st.mod_spec(5)
    return pl.pallas_call(
        functools.partial(_combine_kernel, tok_off, final),
        grid_spec=pltpu.PrefetchScalarGridSpec(
            num_scalar_prefetch=1,
            grid=(st.n_tiles,),
            in_specs=[
                pl.BlockSpec((st.tm, D_MODEL), lambda i, d: (i, 0)),
                pl.BlockSpec(mod5.block_shape, lambda i, d: mod5.index_map(i)),
                pl.BlockSpec((st.tm, TOP_K), lambda i, d: (i + tiles_off, 0)),
                pl.BlockSpec((1, D_MODEL), lambda i, d: (0, 0)),
                pl.BlockSpec(memory_space=pl.ANY),
            ],
            out_specs=pl.BlockSpec((st.tm, D_MODEL), lambda i, d: (i, 0)),
            scratch_shapes=[pltpu.VMEM((TOP_K, st.tm, D_MODEL), F32),
                            pltpu.SemaphoreType.DMA(())],
        ),
        out_shape=jax.ShapeDtypeStruct((st.n_rows, D_MODEL), F32),
        compiler_params=_cparams("arbitrary"),
        name="moe_combine",
    )(dest_flat, x, st.mod, wt, final_g.reshape(1, D_MODEL), yb)


def _moe(layer, final, st_p, st_s, x_p, x_s, post_p, post_s,
         w_gu, b_gu, w_dn, b_dn, final_g):
    te = jnp.concatenate([post_p[1], post_s[1]], axis=1)
    tw = jnp.concatenate([post_p[2], post_s[2]], axis=1)
    dest, counts = _route(te)
    cnt = counts[:, 0].astype(I32)
    pends = jnp.cumsum((cnt + MOE_TB - 1) // MOE_TB * MOE_TB)
    n_used = (pends[-1] // MOE_TB).reshape(1)
    blk = jnp.minimum(jnp.arange(MOE_NBLK, dtype=I32), n_used[0] - 1) * MOE_TB
    block_e = jnp.minimum(jnp.sum(pends[None, :] <= blk[:, None], axis=-1),
                          N_EXPERTS - 1).astype(I32)
    dest_flat = dest.reshape(-1)
    xb = jnp.zeros((MOE_ROWS, D_MODEL), F32)
    xb = _dispatch(st_p, 0, dest_flat, post_p[0], xb)
    xb = _dispatch(st_s, N_PROMPT, dest_flat, post_s[0], xb)
    yb = _ffn(layer, block_e, n_used, xb, w_gu, b_gu, w_dn, b_dn)
    wt = tw.T
    out_p = _combine(st_p, 0, final, dest_flat, x_p, wt, final_g, yb)
    out_s = _combine(st_s, N_PROMPT, final, dest_flat, x_s, wt, final_g, yb)
    return out_p, out_s


def _rope_tables(pos):
    half = RET_DK // 2
    inv_freq = ROPE_BASE ** (-jnp.arange(half, dtype=F32) / half)
    ang = pos.astype(F32)[:, None] * inv_freq[None, :]
    return jnp.cos(ang), jnp.sin(ang)


def kernel(x_prompt, x_sample, c_prompt, c_sample, state_conv, state_ret, norm_mix_g, norm_ff_g,
           w_mod, b_mod, conv_w1, conv_b1, conv_dw, conv_dw_b, conv_ln_g, conv_ln_b, conv_w2,
           conv_b2, ret_w_in, ret_w_o, router_w, router_b, moe_w_gu, moe_b_gu, moe_w_dn,
           moe_b_dn, final_g):
    mod = _adaln(jnp.concatenate([c_prompt, c_sample], axis=0), w_mod, b_mod)
    x_p = x_prompt.reshape(N_PROMPT, D_MODEL)
    x_s = jnp.swapaxes(x_sample, 0, 1).reshape(N_SAMPLE, D_MODEL)
    router_wt = jnp.swapaxes(router_w, 1, 2)

    def streams(layer):
        st_p = _Stream(N_PROMPT, TM_PROMPT, SEQ // TM_PROMPT,
                       mod[layer, :BATCH].reshape(BATCH, 1, N_MOD * D_MODEL))
        st_s = _Stream(N_SAMPLE, TM_SAMPLE, None,
                       mod[layer, BATCH:].reshape(1, DEC_BATCH, N_MOD * D_MODEL))
        return st_p, st_s

    st_p, st_s = streams(0)
    u_p = _pre_conv(st_p, x_p, norm_mix_g, conv_w1, conv_b1, 0, 0)
    u_s = _pre_conv(st_s, x_s, norm_mix_g, conv_w1, conv_b1, 0, 0)
    buf_tm = jnp.swapaxes(state_conv[0], 0, 1)
    z_s, ns_tm = _conv_sample(buf_tm, u_s.reshape(DEC_SEQ, DEC_BATCH, D_MODEL),
                              conv_dw, conv_dw_b, 0)
    post_p = _mixer_post("conv", st_p, 0, 0, x_p, u_p, norm_ff_g, router_wt, router_b,
                         conv_w2, conv_b2, conv_dw, conv_dw_b, conv_ln_g, conv_ln_b)
    post_s = _mixer_post("z", st_s, 0, 0, x_s, z_s.reshape(N_SAMPLE, D_MODEL), norm_ff_g,
                         router_wt, router_b, conv_w2, conv_b2, None, None, conv_ln_g, conv_ln_b)
    x_p, x_s = _moe(0, False, st_p, st_s, post_p[0], post_s[0], post_p[1:], post_s[1:],
                    moe_w_gu, moe_b_gu, moe_w_dn, moe_b_dn, final_g)
    conv_p = u_p.reshape(BATCH, SEQ, D_MODEL)[:, SEQ - CONV_STATE:][None]
    conv_s = jnp.swapaxes(ns_tm, 0, 1)[None]

    st_p, st_s = streams(1)
    w_in_b = ret_w_in.astype(BF16)
    cos_p, sin_p = _rope_tables(jnp.arange(SEQ, dtype=I32))
    cos_s, sin_s = _rope_tables(PAST_LEN + jnp.arange(DEC_SEQ, dtype=I32))
    half = RET_DK // 2
    tps = SEQ // TM_PROMPT
    tbl_p = pl.BlockSpec((TM_PROMPT, half), lambda i: (i % tps, 0))
    tbl_s = pl.BlockSpec((None, 1, half), lambda i: (i, 0, 0))
    q_p, k_p, v_p, sg_p = _pre_ret(st_p, x_p, norm_mix_g, w_in_b, 1, 0, cos_p, sin_p, tbl_p)
    q_s, k_s, v_s, sg_s = _pre_ret(st_s, x_s, norm_mix_g, w_in_b, 1, 0,
                                   cos_s.reshape(DEC_SEQ, 1, half),
                                   sin_s.reshape(DEC_SEQ, 1, half), tbl_s)
    lg = jnp.log1p(-jnp.exp2(-5.0 - jnp.arange(RET_HEADS, dtype=F32)))
    lg_tbl = jnp.broadcast_to(lg[:, None, None], (RET_HEADS, 8, 128))
    zero_state = jnp.zeros((1, BATCH, RET_HEADS, RET_DK, RET_DV), F32)
    gated_p, ret_p = _retention(q_p, k_p, v_p, sg_p, zero_state, 0, lg_tbl,
                                BATCH, SEQ // RET_CHUNK, RET_CHUNK, RET_CHUNK)

    def to_seq_major(a):
        a = jnp.swapaxes(a.reshape(DEC_SEQ, DEC_BATCH, -1), 0, 1)
        a = jnp.pad(a, ((0, 0), (0, RET_PAD - DEC_SEQ), (0, 0)))
        return a.reshape(DEC_BATCH * RET_PAD, -1)

    gated_s, ret_s = _retention(to_seq_major(q_s), to_seq_major(k_s), to_seq_major(v_s),
                                to_seq_major(sg_s), state_ret, 0, lg_tbl,
                                DEC_BATCH, 1, RET_PAD, DEC_SEQ)
    gated_s = jnp.swapaxes(gated_s.reshape(DEC_BATCH, RET_PAD, -1)[:, :DEC_SEQ], 0, 1)
    gated_s = gated_s.reshape(N_SAMPLE, -1)
    post_p = _mixer_post("ret", st_p, 1, 0, x_p, gated_p, norm_ff_g, router_wt, router_b, ret_w_o)
    post_s = _mixer_post("ret", st_s, 1, 0, x_s, gated_s, norm_ff_g, router_wt, router_b, ret_w_o)
    y_p, y_s = _moe(1, True, st_p, st_s, post_p[0], post_s[0], post_p[1:], post_s[1:],
                    moe_w_gu, moe_b_gu, moe_w_dn, moe_b_dn, final_g)

    y_prompt = y_p.reshape(BATCH, SEQ, D_MODEL)
    y_sample = jnp.swapaxes(y_s.reshape(DEC_SEQ, DEC_BATCH, D_MODEL), 0, 1)
    return (y_prompt, y_sample, conv_p, conv_s, ret_p, ret_s)
```

```python
import functools

import jax
import jax.numpy as jnp
from jax import lax
from jax.experimental import pallas as pl
from jax.experimental.pallas import tpu as pltpu

F32 = jnp.float32
BF16 = jnp.bfloat16
I32 = jnp.int32

D_MODEL = 1024
SEQ = 2048
BATCH = 8
DEC_BATCH = 128
DEC_SEQ = 4
PAST_LEN = 16384
CONV_WIDTH = 31
CONV_STATE = CONV_WIDTH - 1
RET_HEADS = 4
RET_DK = 256
RET_DV = 512
ROPE_BASE = 10000.0
N_EXPERTS = 32
TOP_K = 4
D_FF = 1024
SWIGLU_LIMIT = 7.0
SWIGLU_ALPHA = 1.702
N_MOD = 6
NORM_EPS = 1e-5

N_PROMPT = BATCH * SEQ
N_SAMPLE = DEC_BATCH * DEC_SEQ
N_TOK = N_PROMPT + N_SAMPLE
N_ASSIGN = N_TOK * TOP_K

TM_PROMPT = 512
TM_SAMPLE = DEC_BATCH
HALO = 32
CONV_RB = 32
CONV_LW = 256
RET_CHUNK = 256
RET_PAD = 16
MOE_TB = 256
MOE_RT = 256
MOE_NT = N_TOK // MOE_RT
RUN_ALIGN = 8
RUN_PIECES = (256, 128, 64, 32, 16, 8)
TAIL_PIECES = (128, 64, 32, 16, 8)
MOE_CAP = -(-(TOP_K * MOE_RT + N_EXPERTS * (RUN_ALIGN - 1)) // 256) * 256
MOE_MAX_ROWS = N_ASSIGN + MOE_NT * N_EXPERTS * (RUN_ALIGN - 1)
MOE_NBLK = -(-MOE_MAX_ROWS // MOE_TB) + N_EXPERTS
MOE_ROWS = MOE_NBLK * MOE_TB
FF_CHUNK = 256
VMEM_LIMIT = 56 * 1024 * 1024


def _cparams(*sem):
    return pltpu.CompilerParams(dimension_semantics=sem, vmem_limit_bytes=VMEM_LIMIT)


def _silu(x):
    return x * jax.nn.sigmoid(x)


def _rms_mod(x, g, sc, sh):
    y = x * lax.rsqrt(jnp.mean(x * x, axis=-1, keepdims=True) + NORM_EPS) * g
    return y * (1.0 + sc) + sh


def _split_bf16(x):
    hi = x.astype(BF16)
    lo = (x - hi.astype(F32)).astype(BF16)
    return hi, lo


def _adaln_kernel(c_ref, w_ref, b_ref, o_ref):
    a = _silu(c_ref[...]).astype(BF16)
    o_ref[...] = jnp.dot(a, w_ref[...].astype(BF16), preferred_element_type=F32) + b_ref[...]


def _adaln(c_all, w_mod, b_mod):
    depth, d, nm = w_mod.shape
    n = c_all.shape[0]
    return pl.pallas_call(
        _adaln_kernel,
        grid=(depth, nm // d),
        in_specs=[
            pl.BlockSpec((n, d), lambda l, j: (0, 0)),
            pl.BlockSpec((None, d, d), lambda l, j: (l, 0, j)),
            pl.BlockSpec((None, 1, d), lambda l, j: (l, 0, j)),
        ],
        out_specs=pl.BlockSpec((None, n, d), lambda l, j: (l, 0, j)),
        out_shape=jax.ShapeDtypeStruct((depth, n, nm), F32),
        compiler_params=_cparams("parallel", "parallel"),
        name="adaln",
    )(c_all, w_mod, b_mod.reshape(depth, 1, nm))


class _Stream:
    def __init__(self, n_rows, tm, tiles_per_seq, mod):
        self.n_rows = n_rows
        self.tm = tm
        self.n_tiles = n_rows // tm
        self.tiles_per_seq = tiles_per_seq
        self.mod = mod

    def mod_spec(self, j):
        mr = self.mod.shape[1]
        if self.tiles_per_seq is None:
            return pl.BlockSpec((None, mr, D_MODEL), lambda i: (0, 0, j))
        tps = self.tiles_per_seq
        return pl.BlockSpec((None, mr, D_MODEL), lambda i: (i // tps, 0, j))

    def row_spec(self, width, col=0):
        return pl.BlockSpec((self.tm, width), lambda i: (i, col))


def _const_spec(shape):
    nd = len(shape)
    return pl.BlockSpec(shape, lambda i: (0,) * nd)


def _layer_spec(shape, layer):
    nd = len(shape)
    return pl.BlockSpec((None,) + shape, lambda i: (layer,) + (0,) * nd)


def _pre_conv_kernel(x_ref, sh_ref, sc_ref, g_ref, w1_ref, b1_ref, u_ref, w1b_ref):
    @pl.when(pl.program_id(0) == 0)
    def _():
        w1b_ref[...] = w1_ref[...].astype(BF16)

    h = _rms_mod(x_ref[...], g_ref[...], sc_ref[...], sh_ref[...])
    ag = jnp.dot(h.astype(BF16), w1b_ref[...], preferred_element_type=F32) + b1_ref[...]
    u_ref[...] = ag[:, :D_MODEL] * jax.nn.sigmoid(ag[:, D_MODEL:])


def _pre_conv(st, x, norm_g, w1, b1, layer, j):
    c2 = w1.shape[-1]
    return pl.pallas_call(
        _pre_conv_kernel,
        grid=(st.n_tiles,),
        in_specs=[
            st.row_spec(D_MODEL),
            st.mod_spec(0), st.mod_spec(1),
            _layer_spec((1, D_MODEL), layer),
            _layer_spec((D_MODEL, c2), j),
            _layer_spec((1, c2), j),
        ],
        out_specs=st.row_spec(D_MODEL),
        out_shape=jax.ShapeDtypeStruct((st.n_rows, D_MODEL), F32),
        scratch_shapes=[pltpu.VMEM((D_MODEL, c2), BF16)],
        compiler_params=_cparams("arbitrary"),
        name="pre_conv",
    )(x, st.mod, st.mod, norm_g.reshape(-1, 1, D_MODEL), w1, b1.reshape(-1, 1, c2))


def _conv_sample_kernel(buf_ref, u_ref, dw_ref, dwb_ref, z_ref, ns_ref):
    for t in range(DEC_SEQ):
        acc = jnp.broadcast_to(dwb_ref[...], u_ref.shape[1:])
        for j in range(t, CONV_STATE):
            acc = acc + buf_ref[j] * dw_ref[j - t:j - t + 1, :]
        for s in range(t + 1):
            k = CONV_STATE + s - t
            acc = acc + u_ref[s] * dw_ref[k:k + 1, :]
        z_ref[t] = acc
    ns_ref[0:CONV_STATE - DEC_SEQ] = buf_ref[DEC_SEQ:CONV_STATE]
    ns_ref[CONV_STATE - DEC_SEQ:CONV_STATE] = u_ref[...]


def _conv_sample(buf_tm, u_tm, dw, dwb, j):
    sb = 32
    c = u_tm.shape[-1]
    return pl.pallas_call(
        _conv_sample_kernel,
        grid=(DEC_BATCH // sb,),
        in_specs=[
            pl.BlockSpec((CONV_STATE, sb, c), lambda i: (0, i, 0)),
            pl.BlockSpec((DEC_SEQ, sb, c), lambda i: (0, i, 0)),
            _layer_spec((CONV_WIDTH, c), j),
            _layer_spec((1, c), j),
        ],
        out_specs=[
            pl.BlockSpec((DEC_SEQ, sb, c), lambda i: (0, i, 0)),
            pl.BlockSpec((CONV_STATE, sb, c), lambda i: (0, i, 0)),
        ],
        out_shape=[
            jax.ShapeDtypeStruct((DEC_SEQ, DEC_BATCH, c), F32),
            jax.ShapeDtypeStruct((CONV_STATE, DEC_BATCH, c), F32),
        ],
        compiler_params=_cparams("parallel"),
        name="conv_sample",
    )(buf_tm, u_tm, dw, dwb.reshape(-1, 1, c))


def _route_topk(h2, rwt_ref, rb_ref, te_ref, tw_ref):
    h_hi, h_lo = _split_bf16(h2)
    w_hi, w_lo = _split_bf16(rwt_ref[...])
    nt = (((1,), (1,)), ((), ()))
    logits = (lax.dot_general(w_hi, h_hi, nt, preferred_element_type=F32)
              + lax.dot_general(w_hi, h_lo, nt, preferred_element_type=F32)
              + lax.dot_general(w_lo, h_hi, nt, preferred_element_type=F32)
              + rb_ref[...])
    iota_e = lax.broadcasted_iota(I32, logits.shape, 0).astype(F32)
    vals, idxs = [], []
    for _ in range(TOP_K):
        m = jnp.max(logits, axis=0, keepdims=True)
        idx = jnp.min(jnp.where(logits == m, iota_e, float(N_EXPERTS)), axis=0, keepdims=True)
        logits = jnp.where(iota_e == idx, -jnp.inf, logits)
        vals.append(m)
        idxs.append(idx)
    es = [jnp.exp(v - vals[0]) for v in vals]
    tot = es[0] + es[1] + es[2] + es[3]
    for k in range(TOP_K):
        te_ref[k:k + 1, :] = idxs[k].astype(I32)
        tw_ref[k:k + 1, :] = es[k] / tot


def _mixer_post_kernel(mode, tiles_per_seq, *refs):
    it = iter(refs)
    if mode == "conv":
        u_ref, halo_ref, dw_ref, dwb_ref = next(it), next(it), next(it), next(it)
    elif mode == "z":
        z_in_ref = next(it)
    else:
        gated_ref = next(it)
    if mode in ("conv", "z"):
        lng_ref, lnb_ref = next(it), next(it)
    w_ref, b_ref = next(it), next(it)
    x_ref, gm_ref, shf_ref, scf_ref, gff_ref, rwt_ref, rb_ref = (next(it) for _ in range(7))
    x1_ref, h2_ref, te_ref, tw_ref = (next(it) for _ in range(4))
    wb_ref = next(it)
    if mode == "conv":
        ext_ref, z_ref = next(it), next(it)

    i = pl.program_id(0)

    @pl.when(i == 0)
    def _():
        wb_ref[...] = w_ref[...].astype(BF16)

    if mode == "conv":
        tm = u_ref.shape[0]
        first = (i % tiles_per_seq) == 0
        ext_ref[0:HALO, :] = jnp.where(first, 0.0, halo_ref[...])
        ext_ref[HALO:HALO + tm, :] = u_ref[...]
        off = HALO - CONV_STATE

        def row_block(r, carry):
            r0 = pl.multiple_of(r * CONV_RB, CONV_RB)
            for lc in range(D_MODEL // CONV_LW):
                ls = slice(lc * CONV_LW, (lc + 1) * CONV_LW)
                win = ext_ref[pl.ds(r0, CONV_RB + HALO), ls]
                acc = jnp.broadcast_to(dwb_ref[:, ls], (CONV_RB, CONV_LW))
                for k in range(CONV_WIDTH):
                    acc = acc + win[off + k:off + k + CONV_RB, :] * dw_ref[k:k + 1, ls]
                z_ref[pl.ds(r0, CONV_RB), ls] = acc
            return carry

        lax.fori_loop(0, tm // CONV_RB, row_block, 0)
        z = z_ref[...]
    elif mode == "z":
        z = z_in_ref[...]

    if mode in ("conv", "z"):
        mu = jnp.mean(z, axis=-1, keepdims=True)
        zc = z - mu
        var = jnp.mean(zc * zc, axis=-1, keepdims=True)
        zn = zc * lax.rsqrt(var + NORM_EPS) * lng_ref[...] + lnb_ref[...]
        a = _silu(zn).astype(BF16)
        y = jnp.dot(a, wb_ref[...], preferred_element_type=F32) + b_ref[...]
    else:
        y = jnp.dot(gated_ref[...], wb_ref[...], preferred_element_type=F32)

    x1 = x_ref[...] + gm_ref[...] * y
    x1_ref[...] = x1
    h2 = _rms_mod(x1, gff_ref[...], scf_ref[...], shf_ref[...])
    h2_ref[...] = h2
    _route_topk(h2, rwt_ref, rb_ref, te_ref, tw_ref)


def _mixer_post(mode, st, layer, j, x, mix_in, norm_ff_g, router_wt, router_b, w, b=None,
                dw=None, dwb=None, ln_g=None, ln_b=None):
    tm = st.tm
    k_in = w.shape[-2]
    args, specs = [], []
    if mode == "conv":
        per32 = tm // HALO
        args += [mix_in, mix_in, dw, dwb.reshape(-1, 1, D_MODEL)]
        specs += [st.row_spec(D_MODEL),
                  pl.BlockSpec((HALO, D_MODEL), lambda i: (jnp.maximum(i * per32 - 1, 0), 0)),
                  _layer_spec((CONV_WIDTH, D_MODEL), j), _layer_spec((1, D_MODEL), j)]
    elif mode == "z":
        args += [mix_in]
        specs += [st.row_spec(D_MODEL)]
    else:
        args += [mix_in]
        specs += [st.row_spec(k_in)]
    if mode in ("conv", "z"):
        args += [ln_g.reshape(-1, 1, D_MODEL), ln_b.reshape(-1, 1, D_MODEL)]
        specs += [_layer_spec((1, D_MODEL), j), _layer_spec((1, D_MODEL), j)]
    if b is None:
        b = jnp.zeros((w.shape[0], D_MODEL), F32)
    args += [w, b.reshape(-1, 1, D_MODEL)]
    specs += [_layer_spec((k_in, D_MODEL), j), _layer_spec((1, D_MODEL), j)]
    args += [x, st.mod, st.mod, st.mod, norm_ff_g.reshape(-1, 1, D_MODEL), router_wt,
             router_b.reshape(-1, N_EXPERTS, 1)]
    specs += [st.row_spec(D_MODEL), st.mod_spec(2), st.mod_spec(3), st.mod_spec(4),
              _layer_spec((1, D_MODEL), layer), _layer_spec((N_EXPERTS, D_MODEL), layer),
              _layer_spec((N_EXPERTS, 1), layer)]
    scratch = [pltpu.VMEM((k_in, D_MODEL), BF16)]
    if mode == "conv":
        scratch += [pltpu.VMEM((tm + HALO, D_MODEL), F32), pltpu.VMEM((tm, D_MODEL), F32)]
    return pl.pallas_call(
        functools.partial(_mixer_post_kernel, mode, st.tiles_per_seq),
        grid=(st.n_tiles,),
        in_specs=specs,
        out_specs=[st.row_spec(D_MODEL), st.row_spec(D_MODEL),
                   pl.BlockSpec((TOP_K, tm), lambda i: (0, i)),
                   pl.BlockSpec((TOP_K, tm), lambda i: (0, i))],
        out_shape=[jax.ShapeDtypeStruct((st.n_rows, D_MODEL), F32),
                   jax.ShapeDtypeStruct((st.n_rows, D_MODEL), F32),
                   jax.ShapeDtypeStruct((TOP_K, st.n_rows), I32),
                   jax.ShapeDtypeStruct((TOP_K, st.n_rows), F32)],
        scratch_shapes=scratch,
        compiler_params=_cparams("arbitrary"),
        name="mixer_post_" + mode,
    )(*args)


def _pre_ret_kernel(x_ref, sh_ref, sc_ref, g_ref, w_ref, cos_ref, sin_ref,
                    q_ref, k_ref, v_ref, sg_ref):
    h = _rms_mod(x_ref[...], g_ref[...], sc_ref[...], sh_ref[...]).astype(BF16)
    cos = cos_ref[...]
    sin = sin_ref[...]
    half = RET_DK // 2
    qk = RET_HEADS * RET_DK
    vd = RET_HEADS * RET_DV
    for out_ref, base, scale in ((q_ref, 0, 1.0), (k_ref, qk, RET_DK ** -0.5)):
        for hh in range(RET_HEADS):
            c0 = base + hh * RET_DK
            p = jnp.dot(h, w_ref[:, c0:c0 + RET_DK], preferred_element_type=F32)
            p1, p2 = p[:, :half], p[:, half:]
            o0 = hh * RET_DK
            out_ref[:, o0:o0 + half] = ((p1 * cos - p2 * sin) * scale).astype(BF16)
            out_ref[:, o0 + half:o0 + RET_DK] = ((p1 * sin + p2 * cos) * scale).astype(BF16)
    for hh in range(RET_HEADS):
        c0 = 2 * qk + hh * RET_DV
        v_ref[:, hh * RET_DV:(hh + 1) * RET_DV] = jnp.dot(
            h, w_ref[:, c0:c0 + RET_DV], preferred_element_type=F32).astype(BF16)
        g = jnp.dot(h, w_ref[:, vd + c0:vd + c0 + RET_DV], preferred_element_type=F32)
        sg_ref[:, hh * RET_DV:(hh + 1) * RET_DV] = _silu(g).astype(BF16)


def _pre_ret(st, x, norm_g, w_in_b, layer, j, cos, sin, tbl_spec):
    qk = RET_HEADS * RET_DK
    vd = RET_HEADS * RET_DV
    return pl.pallas_call(
        _pre_ret_kernel,
        grid=(st.n_tiles,),
        in_specs=[
            st.row_spec(D_MODEL),
            st.mod_spec(0), st.mod_spec(1),
            _layer_spec((1, D_MODEL), layer),
            _layer_spec((D_MODEL, 2 * qk + 2 * vd), j),
            tbl_spec, tbl_spec,
        ],
        out_specs=[st.row_spec(qk), st.row_spec(qk), st.row_spec(vd), st.row_spec(vd)],
        out_shape=[jax.ShapeDtypeStruct((st.n_rows, qk), BF16),
                   jax.ShapeDtypeStruct((st.n_rows, qk), BF16),
                   jax.ShapeDtypeStruct((st.n_rows, vd), BF16),
                   jax.ShapeDtypeStruct((st.n_rows, vd), BF16)],
        compiler_params=_cparams("parallel"),
        name="pre_ret",
    )(x, st.mod, st.mod, norm_g.reshape(-1, 1, D_MODEL), w_in_b, cos, sin)


def _retention_kernel(valid, lg_ref, q_ref, k_ref, v_ref, sg_ref, s0_ref,
                      o_ref, s_out_ref, s_ref, dec_ref):
    c = pl.program_id(2)
    ch = q_ref.shape[0]
    lg = lg_ref[0:1, 0:1]
    idx = lax.broadcasted_iota(I32, (ch, 1), 0).astype(F32)

    @pl.when(c == 0)
    def _():
        s_ref[...] = s0_ref[...]
        diff = (lax.broadcasted_iota(I32, (ch, ch), 0)
                - lax.broadcasted_iota(I32, (ch, ch), 1)).astype(F32)
        dec_ref[...] = jnp.where(diff >= 0.0, jnp.exp(lg * jnp.maximum(diff, 0.0)), 0.0)

    q = q_ref[...]
    k = k_ref[...]
    v = v_ref[...]
    s_prev = s_ref[...]
    scores = lax.dot_general(q, k, (((1,), (1,)), ((), ())),
                             preferred_element_type=F32) * dec_ref[...]
    o = jnp.dot(scores.astype(BF16), v, preferred_element_type=F32)
    cross = jnp.exp(lg * (idx + 1.0))
    o = o + jnp.dot(q, s_prev.astype(BF16), preferred_element_type=F32) * cross
    kd = (k.astype(F32) * jnp.exp(lg * (valid - 1.0 - idx))).astype(BF16)
    s_new = jnp.exp(lg * float(valid)) * s_prev + lax.dot_general(
        kd, v, (((0,), (0,)), ((), ())), preferred_element_type=F32)
    s_ref[...] = s_new

    @pl.when(c == pl.num_programs(2) - 1)
    def _():
        s_out_ref[...] = s_new

    mu = jnp.mean(o, axis=-1, keepdims=True)
    oc = o - mu
    var = jnp.mean(oc * oc, axis=-1, keepdims=True)
    on = oc * lax.rsqrt(var + NORM_EPS)
    o_ref[...] = (sg_ref[...].astype(F32) * on).astype(BF16)


def _retention(q, k, v, sg, s0, s0_layer, lg_tbl, n_seq, n_chunks, ch, valid):
    n_rows = q.shape[0]
    row = lambda b, h, c: (b * n_chunks + c, h)
    state_spec = pl.BlockSpec((None, None, None, RET_DK, RET_DV),
                              lambda b, h, c: (s0_layer, b, h, 0, 0))
    out_state_spec = pl.BlockSpec((None, None, None, RET_DK, RET_DV),
                                  lambda b, h, c: (0, b, h, 0, 0))
    return pl.pallas_call(
        functools.partial(_retention_kernel, valid),
        grid=(n_seq, RET_HEADS, n_chunks),
        in_specs=[
            pl.BlockSpec((None, 8, 128), lambda b, h, c: (h, 0, 0)),
            pl.BlockSpec((ch, RET_DK), row), pl.BlockSpec((ch, RET_DK), row),
            pl.BlockSpec((ch, RET_DV), row), pl.BlockSpec((ch, RET_DV), row),
            state_spec,
        ],
        out_specs=[pl.BlockSpec((ch, RET_DV), row), out_state_spec],
        out_shape=[jax.ShapeDtypeStruct((n_rows, RET_HEADS * RET_DV), BF16),
                   jax.ShapeDtypeStruct((1, n_seq, RET_HEADS, RET_DK, RET_DV), F32)],
        scratch_shapes=[pltpu.VMEM((RET_DK, RET_DV), F32), pltpu.VMEM((ch, ch), F32)],
        compiler_params=_cparams("parallel", "parallel", "arbitrary"),
        name="retention",
    )(lg_tbl, q, k, v, sg, s0)


def _excl_prefix(col):
    r = lax.broadcasted_iota(I32, (N_EXPERTS, N_EXPERTS), 0)
    c = lax.broadcasted_iota(I32, (N_EXPERTS, N_EXPERTS), 1)
    lane = jnp.sum(jnp.where(r < c, col, 0.0), axis=0, keepdims=True)
    return jnp.sum(jnp.where(r == c, lane, 0.0), axis=1, keepdims=True)


def _ceil_to(x, m):
    return jnp.floor((x + (m - 1.0)) * (1.0 / m)) * m


def _route_kernel(te_ref, lpos_ref, rstart_ref, rlen_ref, tot_ref, gbase_ref, tri_ref):
    p = pl.program_id(0)
    c = pl.program_id(1)
    rt = te_ref.shape[1]
    iota_e = lax.broadcasted_iota(I32, (N_EXPERTS, rt), 0)
    onehots = [(te_ref[k:k + 1, :] == iota_e).astype(F32) for k in range(TOP_K)]
    cnt = jnp.sum(onehots[0] + onehots[1] + onehots[2] + onehots[3], axis=1, keepdims=True)
    run = _ceil_to(cnt, RUN_ALIGN)

    @pl.when((p == 0) & (c == 0))
    def _():
        tot_ref[...] = jnp.zeros_like(tot_ref)

    @pl.when(p == 0)
    def _():
        tot_ref[...] += run

    @pl.when((p == 1) & (c == 0))
    def _():
        gbase_ref[...] = _excl_prefix(_ceil_to(tot_ref[...], MOE_TB))
        s = lax.broadcasted_iota(I32, (rt, rt), 0)
        t = lax.broadcasted_iota(I32, (rt, rt), 1)
        tri_ref[...] = (s < t).astype(BF16)

    @pl.when(p == 1)
    def _():
        base = _excl_prefix(run)
        for k in range(TOP_K):
            oh = onehots[k]
            before = jnp.dot(oh.astype(BF16), tri_ref[...], preferred_element_type=F32)
            lpos_ref[k:k + 1, :] = jnp.sum(oh * (base + before), axis=0, keepdims=True).astype(I32)
            base = base + jnp.sum(oh, axis=1, keepdims=True)
        gbase = gbase_ref[...]
        rstart_ref[...] = jnp.broadcast_to(gbase, rstart_ref.shape).astype(I32)
        rlen_ref[...] = jnp.broadcast_to(run, rlen_ref.shape).astype(I32)
        gbase_ref[...] = gbase + run


def _route(te):
    run_spec = pl.BlockSpec((None, N_EXPERTS, 128), lambda p, c: (c * p, 0, 0))
    return pl.pallas_call(
        _route_kernel,
        grid=(2, MOE_NT),
        in_specs=[pl.BlockSpec((TOP_K, MOE_RT), lambda p, c: (0, c))],
        out_specs=[pl.BlockSpec((TOP_K, MOE_RT), lambda p, c: (0, c * p)), run_spec, run_spec,
                   pl.BlockSpec((N_EXPERTS, 1), lambda p, c: (0, 0))],
        out_shape=[jax.ShapeDtypeStruct((TOP_K, N_TOK), I32),
                   jax.ShapeDtypeStruct((MOE_NT, N_EXPERTS, 128), I32),
                   jax.ShapeDtypeStruct((MOE_NT, N_EXPERTS, 128), I32),
                   jax.ShapeDtypeStruct((N_EXPERTS, 1), F32)],
        scratch_shapes=[pltpu.VMEM((N_EXPERTS, 1), F32), pltpu.VMEM((MOE_RT, MOE_RT), BF16)],
        compiler_params=_cparams("arbitrary", "arbitrary"),
        name="moe_route",
    )(te)


def _for_each_piece(n, sizes, fn):
    off = 0
    for b in sizes:
        @pl.when((n & b) != 0)
        def _(off=off, b=b):
            fn(off, b)
        off = off + (n & b)


def _dispatch_kernel(n_prompt_tiles, rs_ref, rl_ref, tzs_ref, tzl_ref, nu_ref, hp_ref, hs_ref,
                     lpos_ref, xb_ref, xs_ref, sem, zero_ref):
    i = pl.program_id(0)
    g = i * N_EXPERTS

    def tail_copy(e, off, b):
        dst = pl.multiple_of(tzs_ref[e] + off, RUN_ALIGN)
        return pltpu.make_async_copy(zero_ref.at[pl.ds(0, b)], xb_ref.at[pl.ds(dst, b)], sem)

    def block_copy(j):
        dst = pl.multiple_of(j * MOE_TB, MOE_TB)
        return pltpu.make_async_copy(zero_ref, xb_ref.at[pl.ds(dst, MOE_TB)], sem)

    @pl.when(i == 0)
    def _():
        zero_ref[...] = jnp.zeros_like(zero_ref)

        def start(e, carry):
            _for_each_piece(tzl_ref[e], TAIL_PIECES, lambda off, b: tail_copy(e, off, b).start())
            return carry

        def wait(e, carry):
            _for_each_piece(tzl_ref[e], TAIL_PIECES, lambda off, b: tail_copy(e, off, b).wait())
            return carry

        lax.fori_loop(0, N_EXPERTS, start, 0)
        lax.fori_loop(nu_ref[0], MOE_NBLK, lambda j, c: (block_copy(j).start(), c)[1], 0)
        lax.fori_loop(0, N_EXPERTS, wait, 0)
        lax.fori_loop(nu_ref[0], MOE_NBLK, lambda j, c: (block_copy(j).wait(), c)[1], 0)

    rows = lax.broadcasted_iota(I32, (MOE_CAP, MOE_RT), 0)
    perm = jnp.where(rows == lpos_ref[0:1, :], 1.0, 0.0)
    for k in range(1, TOP_K):
        perm = perm + jnp.where(rows == lpos_ref[k:k + 1, :], 1.0, 0.0)
    h = jnp.where(i < n_prompt_tiles, hp_ref[...], hs_ref[...])
    xs_ref[...] = jnp.dot(perm.astype(BF16), h.astype(BF16), preferred_element_type=F32)

    def run_copy(e, lo, off, b):
        src = pl.multiple_of(lo + off, RUN_ALIGN)
        dst = pl.multiple_of(rs_ref[g + e] + off, RUN_ALIGN)
        return pltpu.make_async_copy(xs_ref.at[pl.ds(src, b)], xb_ref.at[pl.ds(dst, b)], sem)

    def start(e, lo):
        n = rl_ref[g + e]
        _for_each_piece(n, RUN_PIECES, lambda off, b: run_copy(e, lo, off, b).start())
        return lo + n

    def wait(e, lo):
        n = rl_ref[g + e]
        _for_each_piece(n, RUN_PIECES, lambda off, b: run_copy(e, lo, off, b).wait())
        return lo + n

    lax.fori_loop(0, N_EXPERTS, start, 0)
    lax.fori_loop(0, N_EXPERTS, wait, 0)


def _dispatch(tables, h_p, h_s, lpos):
    nt_p = h_p.shape[0] // MOE_RT
    return pl.pallas_call(
        functools.partial(_dispatch_kernel, nt_p),
        grid_spec=pltpu.PrefetchScalarGridSpec(
            num_scalar_prefetch=len(tables),
            grid=(MOE_NT,),
            in_specs=[
                pl.BlockSpec((MOE_RT, D_MODEL), lambda i, *_: (jnp.minimum(i, nt_p - 1), 0)),
                pl.BlockSpec((MOE_RT, D_MODEL), lambda i, *_: (jnp.maximum(i - nt_p, 0), 0)),
                pl.BlockSpec((TOP_K, MOE_RT), lambda i, *_: (0, i)),
            ],
            out_specs=pl.BlockSpec(memory_space=pl.ANY),
            scratch_shapes=[pltpu.VMEM((MOE_CAP, D_MODEL), F32), pltpu.SemaphoreType.DMA(()),
                            pltpu.VMEM((MOE_TB, D_MODEL), F32)],
        ),
        out_shape=jax.ShapeDtypeStruct((MOE_ROWS, D_MODEL), F32),
        compiler_params=_cparams("arbitrary"),
        name="moe_dispatch",
    )(*tables, h_p, h_s, lpos)


def _ffn_kernel(layer, be_ref, nu_ref, nxt_ref, x_ref, bgu_ref, bdn_ref, wgu_hbm, wdn_hbm, y_ref,
                wgu_f_ref, wdn_f_ref, wgu_b_ref, wdn_b_ref, sem):
    j = pl.program_id(0)
    e = be_ref[j]
    e_prev = be_ref[jnp.maximum(j - 1, 0)]
    active = j < nu_ref[0]

    def weight_copies(ex):
        return (pltpu.make_async_copy(wgu_hbm.at[layer, ex], wgu_f_ref, sem.at[0]),
                pltpu.make_async_copy(wdn_hbm.at[layer, ex], wdn_f_ref, sem.at[1]))

    @pl.when(j == 0)
    def _():
        for cp in weight_copies(e):
            cp.start()

    @pl.when(active & ((j == 0) | (e != e_prev)))
    def _():
        for cp in weight_copies(e):
            cp.wait()
        wgu_b_ref[...] = wgu_f_ref[...].astype(BF16)
        wdn_b_ref[...] = wdn_f_ref[...].astype(BF16)
        nx = nxt_ref[j]

        @pl.when(nx >= 0)
        def _():
            for cp in weight_copies(nx):
                cp.start()

    @pl.when(active)
    def _():
        x = x_ref[...].astype(BF16)
        acc = jnp.broadcast_to(bdn_ref[...], y_ref.shape)
        for c in range(D_FF // FF_CHUNK):
            gs = slice(c * FF_CHUNK, (c + 1) * FF_CHUNK)
            us = slice(D_FF + c * FF_CHUNK, D_FF + (c + 1) * FF_CHUNK)
            gate = jnp.dot(x, wgu_b_ref[:, gs], preferred_element_type=F32) + bgu_ref[:, gs]
            up = jnp.dot(x, wgu_b_ref[:, us], preferred_element_type=F32) + bgu_ref[:, us]
            gate = jnp.minimum(gate, SWIGLU_LIMIT)
            up = jnp.clip(up, -SWIGLU_LIMIT, SWIGLU_LIMIT)
            act = (up + 1.0) * (gate * jax.nn.sigmoid(SWIGLU_ALPHA * gate))
            acc = acc + jnp.dot(act.astype(BF16), wdn_b_ref[gs, :], preferred_element_type=F32)
        y_ref[...] = acc

    @pl.when(jnp.logical_not(active))
    def _():
        y_ref[...] = jnp.zeros_like(y_ref)


def _ffn(layer, block_e, n_used, nxt, xb, w_gu, b_gu, w_dn, b_dn):
    row_map = lambda j, be, nu, nx: (jnp.minimum(j, nu[0] - 1), 0)
    out_map = lambda j, be, nu, nx: (j, 0)
    exp_map = lambda j, be, nu, nx: (layer, be[j], 0, 0)
    return pl.pallas_call(
        functools.partial(_ffn_kernel, layer),
        grid_spec=pltpu.PrefetchScalarGridSpec(
            num_scalar_prefetch=3,
            grid=(MOE_NBLK,),
            in_specs=[
                pl.BlockSpec((MOE_TB, D_MODEL), row_map),
                pl.BlockSpec((None, None, 1, 2 * D_FF), exp_map),
                pl.BlockSpec((None, None, 1, D_MODEL), exp_map),
                pl.BlockSpec(memory_space=pl.ANY),
                pl.BlockSpec(memory_space=pl.ANY),
            ],
            out_specs=pl.BlockSpec((MOE_TB, D_MODEL), out_map),
            scratch_shapes=[pltpu.VMEM((D_MODEL, 2 * D_FF), F32),
                            pltpu.VMEM((D_FF, D_MODEL), F32),
                            pltpu.VMEM((D_MODEL, 2 * D_FF), BF16),
                            pltpu.VMEM((D_FF, D_MODEL), BF16),
                            pltpu.SemaphoreType.DMA((2,))],
        ),
        out_shape=jax.ShapeDtypeStruct((MOE_ROWS, D_MODEL), F32),
        compiler_params=_cparams("arbitrary"),
        name="moe_ffn",
    )(block_e, n_used, nxt, xb, b_gu.reshape(b_gu.shape[0], N_EXPERTS, 1, 2 * D_FF),
      b_dn.reshape(b_dn.shape[0], N_EXPERTS, 1, D_MODEL), w_gu, w_dn)


def _tile_rows(v, n_rows):
    mr = v.shape[0]
    if mr in (1, n_rows):
        return v
    return jnp.concatenate([v] * (n_rows // mr), axis=0)


def _combine_kernel(tile_off, final, rs_ref, rl_ref, x_ref, gf_ref, lpt_ref, wt_ref, fg_ref,
                    yb_ref, out_ref, ys_ref, sem):
    i = pl.program_id(0)
    g = (tile_off + i) * N_EXPERTS

    @pl.when(i == 0)
    def _():
        ys_ref[...] = jnp.zeros_like(ys_ref)

    def run_copy(e, lo, off, b):
        src = pl.multiple_of(rs_ref[g + e] + off, RUN_ALIGN)
        dst = pl.multiple_of(lo + off, RUN_ALIGN)
        return pltpu.make_async_copy(yb_ref.at[pl.ds(src, b)], ys_ref.at[pl.ds(dst, b)], sem)

    def start(e, lo):
        n = rl_ref[g + e]
        _for_each_piece(n, RUN_PIECES, lambda off, b: run_copy(e, lo, off, b).start())
        return lo + n

    def wait(e, lo):
        n = rl_ref[g + e]
        _for_each_piece(n, RUN_PIECES, lambda off, b: run_copy(e, lo, off, b).wait())
        return lo + n

    lax.fori_loop(0, N_EXPERTS, start, 0)
    lax.fori_loop(0, N_EXPERTS, wait, 0)

    cols = lax.broadcasted_iota(I32, (MOE_RT, MOE_CAP), 1)
    w = jnp.where(cols == lpt_ref[:, 0:1], wt_ref[:, 0:1], 0.0)
    for k in range(1, TOP_K):
        w = w + jnp.where(cols == lpt_ref[:, k:k + 1], wt_ref[:, k:k + 1], 0.0)
    w_hi, w_lo = _split_bf16(w)
    ys = ys_ref[...].astype(BF16)
    y = (jnp.dot(w_hi, ys, preferred_element_type=F32)
         + jnp.dot(w_lo, ys, preferred_element_type=F32))
    xn = x_ref[...] + _tile_rows(gf_ref[...], MOE_RT) * y
    if final:
        xn = xn * lax.rsqrt(jnp.mean(xn * xn, axis=-1, keepdims=True) + NORM_EPS) * fg_ref[...]
    out_ref[...] = xn


def _combine(tile_off, n_tiles, final, tables, x, mod, mod_map, lpos_t, wt, final_g, yb):
    pm = lambda i, *_: (i, 0)
    tm = lambda i, *_: (i + tile_off, 0)
    return pl.pallas_call(
        functools.partial(_combine_kernel, tile_off, final),
        grid_spec=pltpu.PrefetchScalarGridSpec(
            num_scalar_prefetch=len(tables),
            grid=(n_tiles,),
            in_specs=[
                pl.BlockSpec((MOE_RT, D_MODEL), pm),
                pl.BlockSpec((None, mod.shape[1], D_MODEL), mod_map),
                pl.BlockSpec((MOE_RT, TOP_K), tm),
                pl.BlockSpec((MOE_RT, TOP_K), tm),
                pl.BlockSpec((1, D_MODEL), lambda i, *_: (0, 0)),
                pl.BlockSpec(memory_space=pl.ANY),
            ],
            out_specs=pl.BlockSpec((MOE_RT, D_MODEL), pm),
            scratch_shapes=[pltpu.VMEM((MOE_CAP, D_MODEL), F32), pltpu.SemaphoreType.DMA(())],
        ),
        out_shape=jax.ShapeDtypeStruct(x.shape, F32),
        compiler_params=_cparams("arbitrary"),
        name="moe_combine",
    )(*tables, x, mod, lpos_t, wt, final_g.reshape(1, D_MODEL), yb)


def _moe(layer, final, st_p, st_s, x_p, x_s, post_p, post_s,
         w_gu, b_gu, w_dn, b_dn, final_g):
    te = jnp.concatenate([post_p[1], post_s[1]], axis=1)
    tw = jnp.concatenate([post_p[2], post_s[2]], axis=1)
    lpos, rstart, rlen, tot = _route(te)
    tot = tot[:, 0].astype(I32)
    padded = (tot + MOE_TB - 1) // MOE_TB * MOE_TB
    pends = jnp.cumsum(padded)
    n_used = (pends[-1] // MOE_TB).reshape(1)
    blk = jnp.minimum(jnp.arange(MOE_NBLK, dtype=I32), n_used[0] - 1) * MOE_TB
    block_e = jnp.minimum(jnp.sum(pends[None, :] <= blk[:, None], axis=-1),
                          N_EXPERTS - 1).astype(I32)
    ids = jnp.where(tot > 0, jnp.arange(N_EXPERTS, dtype=I32), N_EXPERTS)
    later = jnp.concatenate([lax.cummin(ids[::-1])[::-1][1:], jnp.full((1,), N_EXPERTS, I32)])
    nxt = jnp.where(later < N_EXPERTS, later, -1)[block_e]
    run_tables = (rstart[:, :, 0].reshape(-1), rlen[:, :, 0].reshape(-1))
    tail_tables = (pends - padded + tot, padded - tot)

    nt_p = N_PROMPT // MOE_RT
    nt_s = N_SAMPLE // MOE_RT
    xb = _dispatch(run_tables + tail_tables + (n_used,), post_p[0], post_s[0], lpos)
    yb = _ffn(layer, block_e, n_used, nxt, xb, w_gu, b_gu, w_dn, b_dn)
    lpos_t = lpos.T
    wt = tw.T
    tiles_per_seq = SEQ // MOE_RT
    out_p = _combine(0, nt_p, final, run_tables, x_p, st_p.mod,
                     lambda i, *_: (i // tiles_per_seq, 0, N_MOD - 1), lpos_t, wt, final_g, yb)
    out_s = _combine(nt_p, nt_s, final, run_tables, x_s, st_s.mod,
                     lambda i, *_: (0, 0, N_MOD - 1), lpos_t, wt, final_g, yb)
    return out_p, out_s


def _rope_tables(pos):
    half = RET_DK // 2
    inv_freq = ROPE_BASE ** (-jnp.arange(half, dtype=F32) / half)
    ang = pos.astype(F32)[:, None] * inv_freq[None, :]
    return jnp.cos(ang), jnp.sin(ang)


def kernel(x_prompt, x_sample, c_prompt, c_sample, state_conv, state_ret, norm_mix_g, norm_ff_g,
           w_mod, b_mod, conv_w1, conv_b1, conv_dw, conv_dw_b, conv_ln_g, conv_ln_b, conv_w2,
           conv_b2, ret_w_in, ret_w_o, router_w, router_b, moe_w_gu, moe_b_gu, moe_w_dn,
           moe_b_dn, final_g):
    mod = _adaln(jnp.concatenate([c_prompt, c_sample], axis=0), w_mod, b_mod)
    x_p = x_prompt.reshape(N_PROMPT, D_MODEL)
    x_s = jnp.swapaxes(x_sample, 0, 1).reshape(N_SAMPLE, D_MODEL)
    router_wt = jnp.swapaxes(router_w, 1, 2)

    def streams(layer):
        st_p = _Stream(N_PROMPT, TM_PROMPT, SEQ // TM_PROMPT,
                       mod[layer, :BATCH].reshape(BATCH, 1, N_MOD * D_MODEL))
        st_s = _Stream(N_SAMPLE, TM_SAMPLE, None,
                       mod[layer, BATCH:].reshape(1, DEC_BATCH, N_MOD * D_MODEL))
        return st_p, st_s

    st_p, st_s = streams(0)
    u_p = _pre_conv(st_p, x_p, norm_mix_g, conv_w1, conv_b1, 0, 0)
    u_s = _pre_conv(st_s, x_s, norm_mix_g, conv_w1, conv_b1, 0, 0)
    buf_tm = jnp.swapaxes(state_conv[0], 0, 1)
    z_s, ns_tm = _conv_sample(buf_tm, u_s.reshape(DEC_SEQ, DEC_BATCH, D_MODEL),
                              conv_dw, conv_dw_b, 0)
    post_p = _mixer_post("conv", st_p, 0, 0, x_p, u_p, norm_ff_g, router_wt, router_b,
                         conv_w2, conv_b2, conv_dw, conv_dw_b, conv_ln_g, conv_ln_b)
    post_s = _mixer_post("z", st_s, 0, 0, x_s, z_s.reshape(N_SAMPLE, D_MODEL), norm_ff_g,
                         router_wt, router_b, conv_w2, conv_b2, None, None, conv_ln_g, conv_ln_b)
    x_p, x_s = _moe(0, False, st_p, st_s, post_p[0], post_s[0], post_p[1:], post_s[1:],
                    moe_w_gu, moe_b_gu, moe_w_dn, moe_b_dn, final_g)
    conv_p = u_p.reshape(BATCH, SEQ, D_MODEL)[:, SEQ - CONV_STATE:][None]
    conv_s = jnp.swapaxes(ns_tm, 0, 1)[None]

    st_p, st_s = streams(1)
    w_in_b = ret_w_in.astype(BF16)
    cos_p, sin_p = _rope_tables(jnp.arange(SEQ, dtype=I32))
    cos_s, sin_s = _rope_tables(PAST_LEN + jnp.arange(DEC_SEQ, dtype=I32))
    half = RET_DK // 2
    tps = SEQ // TM_PROMPT
    tbl_p = pl.BlockSpec((TM_PROMPT, half), lambda i: (i % tps, 0))
    tbl_s = pl.BlockSpec((None, 1, half), lambda i: (i, 0, 0))
    q_p, k_p, v_p, sg_p = _pre_ret(st_p, x_p, norm_mix_g, w_in_b, 1, 0, cos_p, sin_p, tbl_p)
    q_s, k_s, v_s, sg_s = _pre_ret(st_s, x_s, norm_mix_g, w_in_b, 1, 0,
                                   cos_s.reshape(DEC_SEQ, 1, half),
                                   sin_s.reshape(DEC_SEQ, 1, half), tbl_s)
    lg = jnp.log1p(-jnp.exp2(-5.0 - jnp.arange(RET_HEADS, dtype=F32)))
    lg_tbl = jnp.broadcast_to(lg[:, None, None], (RET_HEADS, 8, 128))
    zero_state = jnp.zeros((1, BATCH, RET_HEADS, RET_DK, RET_DV), F32)
    gated_p, ret_p = _retention(q_p, k_p, v_p, sg_p, zero_state, 0, lg_tbl,
                                BATCH, SEQ // RET_CHUNK, RET_CHUNK, RET_CHUNK)

    def to_seq_major(a):
        a = jnp.swapaxes(a.reshape(DEC_SEQ, DEC_BATCH, -1), 0, 1)
        a = jnp.pad(a, ((0, 0), (0, RET_PAD - DEC_SEQ), (0, 0)))
        return a.reshape(DEC_BATCH * RET_PAD, -1)

    gated_s, ret_s = _retention(to_seq_major(q_s), to_seq_major(k_s), to_seq_major(v_s),
                                to_seq_major(sg_s), state_ret, 0, lg_tbl,
                                DEC_BATCH, 1, RET_PAD, DEC_SEQ)
    gated_s = jnp.swapaxes(gated_s.reshape(DEC_BATCH, RET_PAD, -1)[:, :DEC_SEQ], 0, 1)
    gated_s = gated_s.reshape(N_SAMPLE, -1)
    post_p = _mixer_post("ret", st_p, 1, 0, x_p, gated_p, norm_ff_g, router_wt, router_b, ret_w_o)
    post_s = _mixer_post("ret", st_s, 1, 0, x_s, gated_s, norm_ff_g, router_wt, router_b, ret_w_o)
    y_p, y_s = _moe(1, True, st_p, st_s, post_p[0], post_s[0], post_p[1:], post_s[1:],
                    moe_w_gu, moe_b_gu, moe_w_dn, moe_b_dn, final_g)

    y_prompt = y_p.reshape(BATCH, SEQ, D_MODEL)
    y_sample = jnp.swapaxes(y_s.reshape(DEC_SEQ, DEC_BATCH, D_MODEL), 0, 1)
    return (y_prompt, y_sample, conv_p, conv_s, ret_p, ret_s)
```

```python
import functools

import jax
import jax.numpy as jnp
from jax import lax
from jax.experimental import pallas as pl
from jax.experimental.pallas import tpu as pltpu

F32 = jnp.float32
BF16 = jnp.bfloat16
I32 = jnp.int32

D_MODEL = 1024
SEQ = 2048
BATCH = 8
DEC_BATCH = 128
DEC_SEQ = 4
PAST_LEN = 16384
CONV_WIDTH = 31
CONV_STATE = CONV_WIDTH - 1
RET_HEADS = 4
RET_DK = 256
RET_DV = 512
ROPE_BASE = 10000.0
N_EXPERTS = 32
TOP_K = 4
D_FF = 1024
SWIGLU_LIMIT = 7.0
SWIGLU_ALPHA = 1.702
N_MOD = 6
NORM_EPS = 1e-5

N_PROMPT = BATCH * SEQ
N_SAMPLE = DEC_BATCH * DEC_SEQ
N_TOK = N_PROMPT + N_SAMPLE
N_ASSIGN = N_TOK * TOP_K

TM_PROMPT = 512
TM_SAMPLE = DEC_BATCH
HALO = 32
CONV_RB = 64
CONV_LW = 128
RET_CHUNK = 256
RET_PAD = 16
RET_SB = 8
MOE_TB = 256
MOE_RT = 256
MOE_NT = N_TOK // MOE_RT
RUN_ALIGN = 8
RUN_PIECES = (256, 128, 64, 32, 16, 8)
TAIL_PIECES = (128, 64, 32, 16, 8)
MOE_CAP = -(-(TOP_K * MOE_RT + N_EXPERTS * (RUN_ALIGN - 1)) // 256) * 256
MOE_MAX_ROWS = N_ASSIGN + MOE_NT * N_EXPERTS * (RUN_ALIGN - 1)
MOE_NBLK = -(-MOE_MAX_ROWS // MOE_TB) + N_EXPERTS
MOE_ROWS = MOE_NBLK * MOE_TB
FF_CHUNK = 256
VMEM_LIMIT = 56 * 1024 * 1024


def _cparams(*sem):
    return pltpu.CompilerParams(dimension_semantics=sem, vmem_limit_bytes=VMEM_LIMIT)


def _silu(x):
    return x * jax.nn.sigmoid(x)


def _rms_mod(x, g, sc, sh):
    y = x * lax.rsqrt(jnp.mean(x * x, axis=-1, keepdims=True) + NORM_EPS) * g
    return y * (1.0 + sc) + sh


def _split_bf16(x):
    hi = x.astype(BF16)
    lo = (x - hi.astype(F32)).astype(BF16)
    return hi, lo


def _adaln_kernel(c_ref, w_ref, b_ref, o_ref):
    a = _silu(c_ref[...]).astype(BF16)
    o_ref[...] = jnp.dot(a, w_ref[...].astype(BF16), preferred_element_type=F32) + b_ref[...]


def _adaln(c_all, w_mod, b_mod):
    depth, d, nm = w_mod.shape
    n = c_all.shape[0]
    return pl.pallas_call(
        _adaln_kernel,
        grid=(depth, nm // d),
        in_specs=[
            pl.BlockSpec((n, d), lambda l, j: (0, 0)),
            pl.BlockSpec((None, d, d), lambda l, j: (l, 0, j)),
            pl.BlockSpec((None, 1, d), lambda l, j: (l, 0, j)),
        ],
        out_specs=pl.BlockSpec((None, n, d), lambda l, j: (l, 0, j)),
        out_shape=jax.ShapeDtypeStruct((depth, n, nm), F32),
        compiler_params=_cparams("parallel", "parallel"),
        name="adaln",
    )(c_all, w_mod, b_mod.reshape(depth, 1, nm))


class _Stream:
    def __init__(self, n_rows, tm, tiles_per_seq, mod):
        self.n_rows = n_rows
        self.tm = tm
        self.n_tiles = n_rows // tm
        self.tiles_per_seq = tiles_per_seq
        self.mod = mod

    def mod_spec(self, j):
        mr = self.mod.shape[1]
        if self.tiles_per_seq is None:
            return pl.BlockSpec((None, mr, D_MODEL), lambda i: (0, 0, j))
        tps = self.tiles_per_seq
        return pl.BlockSpec((None, mr, D_MODEL), lambda i: (i // tps, 0, j))

    def row_spec(self, width, col=0):
        return pl.BlockSpec((self.tm, width), lambda i: (i, col))


def _const_spec(shape):
    nd = len(shape)
    return pl.BlockSpec(shape, lambda i: (0,) * nd)


def _layer_spec(shape, layer):
    nd = len(shape)
    return pl.BlockSpec((None,) + shape, lambda i: (layer,) + (0,) * nd)


def _pre_conv_kernel(x_ref, sh_ref, sc_ref, g_ref, w1_ref, b1_ref, u_ref, w1b_ref):
    @pl.when(pl.program_id(0) == 0)
    def _():
        w1b_ref[...] = w1_ref[...].astype(BF16)

    h = _rms_mod(x_ref[...], g_ref[...], sc_ref[...], sh_ref[...])
    ag = jnp.dot(h.astype(BF16), w1b_ref[...], preferred_element_type=F32) + b1_ref[...]
    u_ref[...] = ag[:, :D_MODEL] * jax.nn.sigmoid(ag[:, D_MODEL:])


def _pre_conv(st, x, norm_g, w1, b1, layer, j):
    c2 = w1.shape[-1]
    return pl.pallas_call(
        _pre_conv_kernel,
        grid=(st.n_tiles,),
        in_specs=[
            st.row_spec(D_MODEL),
            st.mod_spec(0), st.mod_spec(1),
            _layer_spec((1, D_MODEL), layer),
            _layer_spec((D_MODEL, c2), j),
            _layer_spec((1, c2), j),
        ],
        out_specs=st.row_spec(D_MODEL),
        out_shape=jax.ShapeDtypeStruct((st.n_rows, D_MODEL), F32),
        scratch_shapes=[pltpu.VMEM((D_MODEL, c2), BF16)],
        compiler_params=_cparams("arbitrary"),
        name="pre_conv",
    )(x, st.mod, st.mod, norm_g.reshape(-1, 1, D_MODEL), w1, b1.reshape(-1, 1, c2))


def _conv_sample_kernel(buf_ref, u_ref, dw_ref, dwb_ref, z_ref, ns_ref):
    for t in range(DEC_SEQ):
        acc = jnp.broadcast_to(dwb_ref[...], u_ref.shape[1:])
        for j in range(t, CONV_STATE):
            acc = acc + buf_ref[j] * dw_ref[j - t:j - t + 1, :]
        for s in range(t + 1):
            k = CONV_STATE + s - t
            acc = acc + u_ref[s] * dw_ref[k:k + 1, :]
        z_ref[t] = acc
    ns_ref[0:CONV_STATE - DEC_SEQ] = buf_ref[DEC_SEQ:CONV_STATE]
    ns_ref[CONV_STATE - DEC_SEQ:CONV_STATE] = u_ref[...]


def _conv_sample(buf_tm, u_tm, dw, dwb, j):
    sb = 32
    c = u_tm.shape[-1]
    return pl.pallas_call(
        _conv_sample_kernel,
        grid=(DEC_BATCH // sb,),
        in_specs=[
            pl.BlockSpec((CONV_STATE, sb, c), lambda i: (0, i, 0)),
            pl.BlockSpec((DEC_SEQ, sb, c), lambda i: (0, i, 0)),
            _layer_spec((CONV_WIDTH, c), j),
            _layer_spec((1, c), j),
        ],
        out_specs=[
            pl.BlockSpec((DEC_SEQ, sb, c), lambda i: (0, i, 0)),
            pl.BlockSpec((CONV_STATE, sb, c), lambda i: (0, i, 0)),
        ],
        out_shape=[
            jax.ShapeDtypeStruct((DEC_SEQ, DEC_BATCH, c), F32),
            jax.ShapeDtypeStruct((CONV_STATE, DEC_BATCH, c), F32),
        ],
        compiler_params=_cparams("parallel"),
        name="conv_sample",
    )(buf_tm, u_tm, dw, dwb.reshape(-1, 1, c))


def _route_topk(h2, rwt_ref, rb_ref, te_ref, tw_ref):
    h_hi, h_lo = _split_bf16(h2)
    w_hi, w_lo = _split_bf16(rwt_ref[...])
    nt = (((1,), (1,)), ((), ()))
    logits = (lax.dot_general(w_hi, h_hi, nt, preferred_element_type=F32)
              + lax.dot_general(w_hi, h_lo, nt, preferred_element_type=F32)
              + lax.dot_general(w_lo, h_hi, nt, preferred_element_type=F32)
              + rb_ref[...])
    iota_e = lax.broadcasted_iota(I32, logits.shape, 0).astype(F32)
    vals, idxs = [], []
    for _ in range(TOP_K):
        m = jnp.max(logits, axis=0, keepdims=True)
        idx = jnp.min(jnp.where(logits == m, iota_e, float(N_EXPERTS)), axis=0, keepdims=True)
        logits = jnp.where(iota_e == idx, -jnp.inf, logits)
        vals.append(m)
        idxs.append(idx)
    es = [jnp.exp(v - vals[0]) for v in vals]
    tot = es[0] + es[1] + es[2] + es[3]
    for k in range(TOP_K):
        te_ref[k:k + 1, :] = idxs[k].astype(I32)
        tw_ref[k:k + 1, :] = es[k] / tot


def _mixer_post_kernel(mode, tiles_per_seq, *refs):
    it = iter(refs)
    if mode == "conv":
        u_ref, halo_ref, dw_ref, dwb_ref = next(it), next(it), next(it), next(it)
    elif mode == "z":
        z_in_ref = next(it)
    else:
        gated_ref = next(it)
    if mode in ("conv", "z"):
        lng_ref, lnb_ref = next(it), next(it)
    w_ref, b_ref = next(it), next(it)
    x_ref, gm_ref, shf_ref, scf_ref, gff_ref, rwt_ref, rb_ref = (next(it) for _ in range(7))
    x1_ref, h2_ref, te_ref, tw_ref = (next(it) for _ in range(4))
    wb_ref = next(it)
    if mode == "conv":
        ext_ref, z_ref = next(it), next(it)

    i = pl.program_id(0)

    @pl.when(i == 0)
    def _():
        wb_ref[...] = w_ref[...].astype(BF16)

    if mode == "conv":
        tm = u_ref.shape[0]
        first = (i % tiles_per_seq) == 0
        ext_ref[0:HALO, :] = jnp.where(first, 0.0, halo_ref[...])
        ext_ref[HALO:HALO + tm, :] = u_ref[...]
        off = HALO - CONV_STATE

        def row_block(r, carry):
            r0 = pl.multiple_of(r * CONV_RB, CONV_RB)
            for lc in range(D_MODEL // CONV_LW):
                ls = slice(lc * CONV_LW, (lc + 1) * CONV_LW)
                win = ext_ref[pl.ds(r0, CONV_RB + HALO), ls]
                acc = jnp.broadcast_to(dwb_ref[:, ls], (CONV_RB, CONV_LW))
                for s in range(8):
                    ws = win if s == 0 else pltpu.roll(win, CONV_RB + HALO - s, 0)
                    for k in range(CONV_WIDTH):
                        if (off + k) % 8 == s:
                            a = off + k - s
                            acc = acc + ws[a:a + CONV_RB, :] * dw_ref[k:k + 1, ls]
                z_ref[pl.ds(r0, CONV_RB), ls] = acc
            return carry

        lax.fori_loop(0, tm // CONV_RB, row_block, 0)
        z = z_ref[...]
    elif mode == "z":
        z = z_in_ref[...]

    if mode in ("conv", "z"):
        mu = jnp.mean(z, axis=-1, keepdims=True)
        zc = z - mu
        var = jnp.mean(zc * zc, axis=-1, keepdims=True)
        zn = zc * lax.rsqrt(var + NORM_EPS) * lng_ref[...] + lnb_ref[...]
        a = _silu(zn).astype(BF16)
        y = jnp.dot(a, wb_ref[...], preferred_element_type=F32) + b_ref[...]
    else:
        y = jnp.dot(gated_ref[...], wb_ref[...], preferred_element_type=F32)

    x1 = x_ref[...] + gm_ref[...] * y
    x1_ref[...] = x1
    h2 = _rms_mod(x1, gff_ref[...], scf_ref[...], shf_ref[...])
    h2_ref[...] = h2
    _route_topk(h2, rwt_ref, rb_ref, te_ref, tw_ref)


def _mixer_post(mode, st, layer, j, x, mix_in, norm_ff_g, router_wt, router_b, w, b=None,
                dw=None, dwb=None, ln_g=None, ln_b=None):
    tm = st.tm
    k_in = w.shape[-2]
    args, specs = [], []
    if mode == "conv":
        per32 = tm // HALO
        args += [mix_in, mix_in, dw, dwb.reshape(-1, 1, D_MODEL)]
        specs += [st.row_spec(D_MODEL),
                  pl.BlockSpec((HALO, D_MODEL), lambda i: (jnp.maximum(i * per32 - 1, 0), 0)),
                  _layer_spec((CONV_WIDTH, D_MODEL), j), _layer_spec((1, D_MODEL), j)]
    elif mode == "z":
        args += [mix_in]
        specs += [st.row_spec(D_MODEL)]
    else:
        args += [mix_in]
        specs += [st.row_spec(k_in)]
    if mode in ("conv", "z"):
        args += [ln_g.reshape(-1, 1, D_MODEL), ln_b.reshape(-1, 1, D_MODEL)]
        specs += [_layer_spec((1, D_MODEL), j), _layer_spec((1, D_MODEL), j)]
    if b is None:
        b = jnp.zeros((w.shape[0], D_MODEL), F32)
    args += [w, b.reshape(-1, 1, D_MODEL)]
    specs += [_layer_spec((k_in, D_MODEL), j), _layer_spec((1, D_MODEL), j)]
    args += [x, st.mod, st.mod, st.mod, norm_ff_g.reshape(-1, 1, D_MODEL), router_wt,
             router_b.reshape(-1, N_EXPERTS, 1)]
    specs += [st.row_spec(D_MODEL), st.mod_spec(2), st.mod_spec(3), st.mod_spec(4),
              _layer_spec((1, D_MODEL), layer), _layer_spec((N_EXPERTS, D_MODEL), layer),
              _layer_spec((N_EXPERTS, 1), layer)]
    scratch = [pltpu.VMEM((k_in, D_MODEL), BF16)]
    if mode == "conv":
        scratch += [pltpu.VMEM((tm + HALO, D_MODEL), F32), pltpu.VMEM((tm, D_MODEL), F32)]
    return pl.pallas_call(
        functools.partial(_mixer_post_kernel, mode, st.tiles_per_seq),
        grid=(st.n_tiles,),
        in_specs=specs,
        out_specs=[st.row_spec(D_MODEL), st.row_spec(D_MODEL),
                   pl.BlockSpec((TOP_K, tm), lambda i: (0, i)),
                   pl.BlockSpec((TOP_K, tm), lambda i: (0, i))],
        out_shape=[jax.ShapeDtypeStruct((st.n_rows, D_MODEL), F32),
                   jax.ShapeDtypeStruct((st.n_rows, D_MODEL), F32),
                   jax.ShapeDtypeStruct((TOP_K, st.n_rows), I32),
                   jax.ShapeDtypeStruct((TOP_K, st.n_rows), F32)],
        scratch_shapes=scratch,
        compiler_params=_cparams("arbitrary"),
        name="mixer_post_" + mode,
    )(*args)


def _pre_ret_kernel(x_ref, sh_ref, sc_ref, g_ref, w_ref, cos_ref, sin_ref,
                    q_ref, k_ref, v_ref, sg_ref):
    h = _rms_mod(x_ref[...], g_ref[...], sc_ref[...], sh_ref[...]).astype(BF16)
    cos = cos_ref[...]
    sin = sin_ref[...]
    half = RET_DK // 2
    qk = RET_HEADS * RET_DK
    vd = RET_HEADS * RET_DV
    for out_ref, base, scale in ((q_ref, 0, 1.0), (k_ref, qk, RET_DK ** -0.5)):
        for hh in range(RET_HEADS):
            c0 = base + hh * RET_DK
            p = jnp.dot(h, w_ref[:, c0:c0 + RET_DK], preferred_element_type=F32)
            p1, p2 = p[:, :half], p[:, half:]
            o0 = hh * RET_DK
            out_ref[:, o0:o0 + half] = ((p1 * cos - p2 * sin) * scale).astype(BF16)
            out_ref[:, o0 + half:o0 + RET_DK] = ((p1 * sin + p2 * cos) * scale).astype(BF16)
    for hh in range(RET_HEADS):
        c0 = 2 * qk + hh * RET_DV
        v_ref[:, hh * RET_DV:(hh + 1) * RET_DV] = jnp.dot(
            h, w_ref[:, c0:c0 + RET_DV], preferred_element_type=F32).astype(BF16)
        g = jnp.dot(h, w_ref[:, vd + c0:vd + c0 + RET_DV], preferred_element_type=F32)
        sg_ref[:, hh * RET_DV:(hh + 1) * RET_DV] = _silu(g).astype(BF16)


def _pre_ret(st, x, norm_g, w_in_b, layer, j, cos, sin, tbl_spec):
    qk = RET_HEADS * RET_DK
    vd = RET_HEADS * RET_DV
    return pl.pallas_call(
        _pre_ret_kernel,
        grid=(st.n_tiles,),
        in_specs=[
            st.row_spec(D_MODEL),
            st.mod_spec(0), st.mod_spec(1),
            _layer_spec((1, D_MODEL), layer),
            _layer_spec((D_MODEL, 2 * qk + 2 * vd), j),
            tbl_spec, tbl_spec,
        ],
        out_specs=[st.row_spec(qk), st.row_spec(qk), st.row_spec(vd), st.row_spec(vd)],
        out_shape=[jax.ShapeDtypeStruct((st.n_rows, qk), BF16),
                   jax.ShapeDtypeStruct((st.n_rows, qk), BF16),
                   jax.ShapeDtypeStruct((st.n_rows, vd), BF16),
                   jax.ShapeDtypeStruct((st.n_rows, vd), BF16)],
        compiler_params=_cparams("parallel"),
        name="pre_ret",
    )(x, st.mod, st.mod, norm_g.reshape(-1, 1, D_MODEL), w_in_b, cos, sin)


def _decay_matrix(lg, ch):
    diff = (lax.broadcasted_iota(I32, (ch, ch), 0)
            - lax.broadcasted_iota(I32, (ch, ch), 1)).astype(F32)
    return jnp.where(diff >= 0.0, jnp.exp(lg * jnp.maximum(diff, 0.0)), 0.0)


def _retention_chunk(valid, lg, dec, q, k, v, sg, s_prev):
    ch = q.shape[0]
    idx = lax.broadcasted_iota(I32, (ch, 1), 0).astype(F32)
    scores = lax.dot_general(q, k, (((1,), (1,)), ((), ())), preferred_element_type=F32) * dec
    o = jnp.dot(scores.astype(BF16), v, preferred_element_type=F32)
    cross = jnp.exp(lg * (idx + 1.0))
    o = o + jnp.dot(q, s_prev.astype(BF16), preferred_element_type=F32) * cross
    kd = (k.astype(F32) * jnp.exp(lg * (valid - 1.0 - idx))).astype(BF16)
    s_new = jnp.exp(lg * float(valid)) * s_prev + lax.dot_general(
        kd, v, (((0,), (0,)), ((), ())), preferred_element_type=F32)
    mu = jnp.mean(o, axis=-1, keepdims=True)
    oc = o - mu
    var = jnp.mean(oc * oc, axis=-1, keepdims=True)
    on = oc * lax.rsqrt(var + NORM_EPS)
    return (sg.astype(F32) * on).astype(BF16), s_new


def _retention_kernel(valid, lg_ref, q_ref, k_ref, v_ref, sg_ref, s0_ref,
                      o_ref, s_out_ref, s_ref, dec_ref):
    c = pl.program_id(2)
    lg = lg_ref[0:1, 0:1]

    @pl.when(c == 0)
    def _():
        s_ref[...] = s0_ref[...]
        dec_ref[...] = _decay_matrix(lg, q_ref.shape[0])

    gated, s_new = _retention_chunk(valid, lg, dec_ref[...], q_ref[...], k_ref[...], v_ref[...],
                                    sg_ref[...], s_ref[...])
    o_ref[...] = gated
    s_ref[...] = s_new

    @pl.when(c == pl.num_programs(2) - 1)
    def _():
        s_out_ref[...] = s_new


def _retention_sample_kernel(lg_ref, q_ref, k_ref, v_ref, sg_ref, s0_ref, o_ref, s_out_ref):
    lg = lg_ref[0:1, 0:1]
    dec = _decay_matrix(lg, RET_PAD)
    for b in range(RET_SB):
        rows = slice(b * RET_PAD, (b + 1) * RET_PAD)
        gated, s_new = _retention_chunk(DEC_SEQ, lg, dec, q_ref[rows, :], k_ref[rows, :],
                                        v_ref[rows, :], sg_ref[rows, :], s0_ref[b])
        o_ref[rows, :] = gated
        s_out_ref[b] = s_new


def _retention_sample(q, k, v, sg, s0, s0_layer, lg_tbl):
    rows = RET_SB * RET_PAD
    row = lambda i, h: (i, h)
    state_block = (None, RET_SB, None, RET_DK, RET_DV)
    return pl.pallas_call(
        _retention_sample_kernel,
        grid=(DEC_BATCH // RET_SB, RET_HEADS),
        in_specs=[
            pl.BlockSpec((None, 8, 128), lambda i, h: (h, 0, 0)),
            pl.BlockSpec((rows, RET_DK), row), pl.BlockSpec((rows, RET_DK), row),
            pl.BlockSpec((rows, RET_DV), row), pl.BlockSpec((rows, RET_DV), row),
            pl.BlockSpec(state_block, lambda i, h: (s0_layer, i, h, 0, 0)),
        ],
        out_specs=[pl.BlockSpec((rows, RET_DV), row),
                   pl.BlockSpec(state_block, lambda i, h: (0, i, h, 0, 0))],
        out_shape=[jax.ShapeDtypeStruct((DEC_BATCH * RET_PAD, RET_HEADS * RET_DV), BF16),
                   jax.ShapeDtypeStruct((1, DEC_BATCH, RET_HEADS, RET_DK, RET_DV), F32)],
        compiler_params=_cparams("parallel", "parallel"),
        name="retention_sample",
    )(lg_tbl, q, k, v, sg, s0)


def _retention(q, k, v, sg, s0, s0_layer, lg_tbl, n_seq, n_chunks, ch, valid):
    n_rows = q.shape[0]
    row = lambda b, h, c: (b * n_chunks + c, h)
    state_spec = pl.BlockSpec((None, None, None, RET_DK, RET_DV),
                              lambda b, h, c: (s0_layer, b, h, 0, 0))
    out_state_spec = pl.BlockSpec((None, None, None, RET_DK, RET_DV),
                                  lambda b, h, c: (0, b, h, 0, 0))
    return pl.pallas_call(
        functools.partial(_retention_kernel, valid),
        grid=(n_seq, RET_HEADS, n_chunks),
        in_specs=[
            pl.BlockSpec((None, 8, 128), lambda b, h, c: (h, 0, 0)),
            pl.BlockSpec((ch, RET_DK), row), pl.BlockSpec((ch, RET_DK), row),
            pl.BlockSpec((ch, RET_DV), row), pl.BlockSpec((ch, RET_DV), row),
            state_spec,
        ],
        out_specs=[pl.BlockSpec((ch, RET_DV), row), out_state_spec],
        out_shape=[jax.ShapeDtypeStruct((n_rows, RET_HEADS * RET_DV), BF16),
                   jax.ShapeDtypeStruct((1, n_seq, RET_HEADS, RET_DK, RET_DV), F32)],
        scratch_shapes=[pltpu.VMEM((RET_DK, RET_DV), F32), pltpu.VMEM((ch, ch), F32)],
        compiler_params=_cparams("parallel", "parallel", "arbitrary"),
        name="retention",
    )(lg_tbl, q, k, v, sg, s0)


def _excl_prefix(col):
    r = lax.broadcasted_iota(I32, (N_EXPERTS, N_EXPERTS), 0)
    c = lax.broadcasted_iota(I32, (N_EXPERTS, N_EXPERTS), 1)
    lane = jnp.sum(jnp.where(r < c, col, 0.0), axis=0, keepdims=True)
    return jnp.sum(jnp.where(r == c, lane, 0.0), axis=1, keepdims=True)


def _ceil_to(x, m):
    return jnp.floor((x + (m - 1.0)) * (1.0 / m)) * m


def _route_kernel(te_ref, lpos_ref, rstart_ref, rlen_ref, tot_ref, gbase_ref, tri_ref):
    p = pl.program_id(0)
    c = pl.program_id(1)
    rt = te_ref.shape[1]
    iota_e = lax.broadcasted_iota(I32, (N_EXPERTS, rt), 0)
    onehots = [(te_ref[k:k + 1, :] == iota_e).astype(F32) for k in range(TOP_K)]
    cnt = jnp.sum(onehots[0] + onehots[1] + onehots[2] + onehots[3], axis=1, keepdims=True)
    run = _ceil_to(cnt, RUN_ALIGN)

    @pl.when((p == 0) & (c == 0))
    def _():
        tot_ref[...] = jnp.zeros_like(tot_ref)

    @pl.when(p == 0)
    def _():
        tot_ref[...] += run

    @pl.when((p == 1) & (c == 0))
    def _():
        gbase_ref[...] = _excl_prefix(_ceil_to(tot_ref[...], MOE_TB))
        s = lax.broadcasted_iota(I32, (rt, rt), 0)
        t = lax.broadcasted_iota(I32, (rt, rt), 1)
        tri_ref[...] = (s < t).astype(BF16)

    @pl.when(p == 1)
    def _():
        base = _excl_prefix(run)
        for k in range(TOP_K):
            oh = onehots[k]
            before = jnp.dot(oh.astype(BF16), tri_ref[...], preferred_element_type=F32)
            lpos_ref[k:k + 1, :] = jnp.sum(oh * (base + before), axis=0, keepdims=True).astype(I32)
            base = base + jnp.sum(oh, axis=1, keepdims=True)
        gbase = gbase_ref[...]
        rstart_ref[...] = jnp.broadcast_to(gbase, rstart_ref.shape).astype(I32)
        rlen_ref[...] = jnp.broadcast_to(run, rlen_ref.shape).astype(I32)
        gbase_ref[...] = gbase + run


def _route(te):
    run_spec = pl.BlockSpec((None, N_EXPERTS, 128), lambda p, c: (c * p, 0, 0))
    return pl.pallas_call(
        _route_kernel,
        grid=(2, MOE_NT),
        in_specs=[pl.BlockSpec((TOP_K, MOE_RT), lambda p, c: (0, c))],
        out_specs=[pl.BlockSpec((TOP_K, MOE_RT), lambda p, c: (0, c * p)), run_spec, run_spec,
                   pl.BlockSpec((N_EXPERTS, 1), lambda p, c: (0, 0))],
        out_shape=[jax.ShapeDtypeStruct((TOP_K, N_TOK), I32),
                   jax.ShapeDtypeStruct((MOE_NT, N_EXPERTS, 128), I32),
                   jax.ShapeDtypeStruct((MOE_NT, N_EXPERTS, 128), I32),
                   jax.ShapeDtypeStruct((N_EXPERTS, 1), F32)],
        scratch_shapes=[pltpu.VMEM((N_EXPERTS, 1), F32), pltpu.VMEM((MOE_RT, MOE_RT), BF16)],
        compiler_params=_cparams("arbitrary", "arbitrary"),
        name="moe_route",
    )(te)


def _for_each_piece(n, sizes, fn):
    off = 0
    for b in sizes:
        @pl.when((n & b) != 0)
        def _(off=off, b=b):
            fn(off, b)
        off = off + (n & b)


def _dispatch_kernel(n_prompt_tiles, rs_ref, rl_ref, tzs_ref, tzl_ref, nu_ref, hp_ref, hs_ref,
                     lpos_ref, xb_ref, xs_ref, sem, zero_ref):
    i = pl.program_id(0)
    g = i * N_EXPERTS

    def tail_copy(e, off, b):
        dst = pl.multiple_of(tzs_ref[e] + off, RUN_ALIGN)
        return pltpu.make_async_copy(zero_ref.at[pl.ds(0, b)], xb_ref.at[pl.ds(dst, b)], sem)

    def block_copy(j):
        dst = pl.multiple_of(j * MOE_TB, MOE_TB)
        return pltpu.make_async_copy(zero_ref, xb_ref.at[pl.ds(dst, MOE_TB)], sem)

    @pl.when(i == 0)
    def _():
        zero_ref[...] = jnp.zeros_like(zero_ref)

        def start(e, carry):
            _for_each_piece(tzl_ref[e], TAIL_PIECES, lambda off, b: tail_copy(e, off, b).start())
            return carry

        def wait(e, carry):
            _for_each_piece(tzl_ref[e], TAIL_PIECES, lambda off, b: tail_copy(e, off, b).wait())
            return carry

        lax.fori_loop(0, N_EXPERTS, start, 0)
        lax.fori_loop(nu_ref[0], MOE_NBLK, lambda j, c: (block_copy(j).start(), c)[1], 0)
        lax.fori_loop(0, N_EXPERTS, wait, 0)
        lax.fori_loop(nu_ref[0], MOE_NBLK, lambda j, c: (block_copy(j).wait(), c)[1], 0)

    rows = lax.broadcasted_iota(I32, (MOE_CAP, MOE_RT), 0)
    perm = jnp.where(rows == lpos_ref[0:1, :], 1.0, 0.0)
    for k in range(1, TOP_K):
        perm = perm + jnp.where(rows == lpos_ref[k:k + 1, :], 1.0, 0.0)
    h = jnp.where(i < n_prompt_tiles, hp_ref[...], hs_ref[...])
    xs_ref[...] = jnp.dot(perm.astype(BF16), h.astype(BF16), preferred_element_type=F32)

    def run_copy(e, lo, off, b):
        src = pl.multiple_of(lo + off, RUN_ALIGN)
        dst = pl.multiple_of(rs_ref[g + e] + off, RUN_ALIGN)
        return pltpu.make_async_copy(xs_ref.at[pl.ds(src, b)], xb_ref.at[pl.ds(dst, b)], sem)

    def start(e, lo):
        n = rl_ref[g + e]
        _for_each_piece(n, RUN_PIECES, lambda off, b: run_copy(e, lo, off, b).start())
        return lo + n

    def wait(e, lo):
        n = rl_ref[g + e]
        _for_each_piece(n, RUN_PIECES, lambda off, b: run_copy(e, lo, off, b).wait())
        return lo + n

    lax.fori_loop(0, N_EXPERTS, start, 0)
    lax.fori_loop(0, N_EXPERTS, wait, 0)


def _dispatch(tables, h_p, h_s, lpos):
    nt_p = h_p.shape[0] // MOE_RT
    return pl.pallas_call(
        functools.partial(_dispatch_kernel, nt_p),
        grid_spec=pltpu.PrefetchScalarGridSpec(
            num_scalar_prefetch=len(tables),
            grid=(MOE_NT,),
            in_specs=[
                pl.BlockSpec((MOE_RT, D_MODEL), lambda i, *_: (jnp.minimum(i, nt_p - 1), 0)),
                pl.BlockSpec((MOE_RT, D_MODEL), lambda i, *_: (jnp.maximum(i - nt_p, 0), 0)),
                pl.BlockSpec((TOP_K, MOE_RT), lambda i, *_: (0, i)),
            ],
            out_specs=pl.BlockSpec(memory_space=pl.ANY),
            scratch_shapes=[pltpu.VMEM((MOE_CAP, D_MODEL), F32), pltpu.SemaphoreType.DMA(()),
                            pltpu.VMEM((MOE_TB, D_MODEL), F32)],
        ),
        out_shape=jax.ShapeDtypeStruct((MOE_ROWS, D_MODEL), F32),
        compiler_params=_cparams("arbitrary"),
        name="moe_dispatch",
    )(*tables, h_p, h_s, lpos)


def _ffn_kernel(layer, be_ref, nu_ref, nxt_ref, x_ref, bgu_ref, bdn_ref, wgu_hbm, wdn_hbm, y_ref,
                wgu_f_ref, wdn_f_ref, wgu_b_ref, wdn_b_ref, act_ref, sem):
    j = pl.program_id(0)
    e = be_ref[j]
    e_prev = be_ref[jnp.maximum(j - 1, 0)]
    active = j < nu_ref[0]

    def weight_copies(ex):
        return (pltpu.make_async_copy(wgu_hbm.at[layer, ex], wgu_f_ref, sem.at[0]),
                pltpu.make_async_copy(wdn_hbm.at[layer, ex], wdn_f_ref, sem.at[1]))

    @pl.when(j == 0)
    def _():
        for cp in weight_copies(e):
            cp.start()

    @pl.when(active & ((j == 0) | (e != e_prev)))
    def _():
        for cp in weight_copies(e):
            cp.wait()
        wgu_b_ref[...] = wgu_f_ref[...].astype(BF16)
        wdn_b_ref[...] = wdn_f_ref[...].astype(BF16)
        nx = nxt_ref[j]

        @pl.when(nx >= 0)
        def _():
            for cp in weight_copies(nx):
                cp.start()

    @pl.when(active)
    def _():
        x = x_ref[...].astype(BF16)
        for c in range(D_FF // FF_CHUNK):
            gs = slice(c * FF_CHUNK, (c + 1) * FF_CHUNK)
            us = slice(D_FF + c * FF_CHUNK, D_FF + (c + 1) * FF_CHUNK)
            gate = jnp.dot(x, wgu_b_ref[:, gs], preferred_element_type=F32) + bgu_ref[:, gs]
            up = jnp.dot(x, wgu_b_ref[:, us], preferred_element_type=F32) + bgu_ref[:, us]
            gate = jnp.minimum(gate, SWIGLU_LIMIT)
            up = jnp.clip(up, -SWIGLU_LIMIT, SWIGLU_LIMIT)
            act = (up + 1.0) * (gate * jax.nn.sigmoid(SWIGLU_ALPHA * gate))
            act_ref[:, gs] = act.astype(BF16)
        y_ref[...] = jnp.dot(act_ref[...], wdn_b_ref[...],
                             preferred_element_type=F32) + bdn_ref[...]

    @pl.when(jnp.logical_not(active))
    def _():
        y_ref[...] = jnp.zeros_like(y_ref)


def _ffn(layer, block_e, n_used, nxt, xb, w_gu, b_gu, w_dn, b_dn):
    row_map = lambda j, be, nu, nx: (jnp.minimum(j, nu[0] - 1), 0)
    out_map = lambda j, be, nu, nx: (j, 0)
    exp_map = lambda j, be, nu, nx: (layer, be[j], 0, 0)
    return pl.pallas_call(
        functools.partial(_ffn_kernel, layer),
        grid_spec=pltpu.PrefetchScalarGridSpec(
            num_scalar_prefetch=3,
            grid=(MOE_NBLK,),
            in_specs=[
                pl.BlockSpec((MOE_TB, D_MODEL), row_map),
                pl.BlockSpec((None, None, 1, 2 * D_FF), exp_map),
                pl.BlockSpec((None, None, 1, D_MODEL), exp_map),
                pl.BlockSpec(memory_space=pl.ANY),
                pl.BlockSpec(memory_space=pl.ANY),
            ],
            out_specs=pl.BlockSpec((MOE_TB, D_MODEL), out_map),
            scratch_shapes=[pltpu.VMEM((D_MODEL, 2 * D_FF), F32),
                            pltpu.VMEM((D_FF, D_MODEL), F32),
                            pltpu.VMEM((D_MODEL, 2 * D_FF), BF16),
                            pltpu.VMEM((D_FF, D_MODEL), BF16),
                            pltpu.VMEM((MOE_TB, D_FF), BF16),
                            pltpu.SemaphoreType.DMA((2,))],
        ),
        out_shape=jax.ShapeDtypeStruct((MOE_ROWS, D_MODEL), F32),
        compiler_params=_cparams("arbitrary"),
        name="moe_ffn",
    )(block_e, n_used, nxt, xb, b_gu.reshape(b_gu.shape[0], N_EXPERTS, 1, 2 * D_FF),
      b_dn.reshape(b_dn.shape[0], N_EXPERTS, 1, D_MODEL), w_gu, w_dn)


def _tile_rows(v, n_rows):
    mr = v.shape[0]
    if mr in (1, n_rows):
        return v
    return jnp.concatenate([v] * (n_rows // mr), axis=0)


def _combine_kernel(tile_off, final, rs_ref, rl_ref, x_ref, gf_ref, lpt_ref, wt_ref, fg_ref,
                    yb_ref, out_ref, ys_ref, sem):
    i = pl.program_id(0)
    g = (tile_off + i) * N_EXPERTS

    @pl.when(i == 0)
    def _():
        ys_ref[...] = jnp.zeros_like(ys_ref)

    def run_copy(e, lo, off, b):
        src = pl.multiple_of(rs_ref[g + e] + off, RUN_ALIGN)
        dst = pl.multiple_of(lo + off, RUN_ALIGN)
        return pltpu.make_async_copy(yb_ref.at[pl.ds(src, b)], ys_ref.at[pl.ds(dst, b)], sem)

    def start(e, lo):
        n = rl_ref[g + e]
        _for_each_piece(n, RUN_PIECES, lambda off, b: run_copy(e, lo, off, b).start())
        return lo + n

    def wait(e, lo):
        n = rl_ref[g + e]
        _for_each_piece(n, RUN_PIECES, lambda off, b: run_copy(e, lo, off, b).wait())
        return lo + n

    lax.fori_loop(0, N_EXPERTS, start, 0)
    lax.fori_loop(0, N_EXPERTS, wait, 0)

    cols = lax.broadcasted_iota(I32, (MOE_RT, MOE_CAP), 1)
    w = jnp.where(cols == lpt_ref[:, 0:1], wt_ref[:, 0:1], 0.0)
    for k in range(1, TOP_K):
        w = w + jnp.where(cols == lpt_ref[:, k:k + 1], wt_ref[:, k:k + 1], 0.0)
    w_hi, w_lo = _split_bf16(w)
    ys = ys_ref[...].astype(BF16)
    y = (jnp.dot(w_hi, ys, preferred_element_type=F32)
         + jnp.dot(w_lo, ys, preferred_element_type=F32))
    xn = x_ref[...] + _tile_rows(gf_ref[...], MOE_RT) * y
    if final:
        xn = xn * lax.rsqrt(jnp.mean(xn * xn, axis=-1, keepdims=True) + NORM_EPS) * fg_ref[...]
    out_ref[...] = xn


def _combine(tile_off, n_tiles, final, tables, x, mod, mod_map, lpos_t, wt, final_g, yb):
    pm = lambda i, *_: (i, 0)
    tm = lambda i, *_: (i + tile_off, 0)
    return pl.pallas_call(
        functools.partial(_combine_kernel, tile_off, final),
        grid_spec=pltpu.PrefetchScalarGridSpec(
            num_scalar_prefetch=len(tables),
            grid=(n_tiles,),
            in_specs=[
                pl.BlockSpec((MOE_RT, D_MODEL), pm),
                pl.BlockSpec((None, mod.shape[1], D_MODEL), mod_map),
                pl.BlockSpec((MOE_RT, TOP_K), tm),
                pl.BlockSpec((MOE_RT, TOP_K), tm),
                pl.BlockSpec((1, D_MODEL), lambda i, *_: (0, 0)),
                pl.BlockSpec(memory_space=pl.ANY),
            ],
            out_specs=pl.BlockSpec((MOE_RT, D_MODEL), pm),
            scratch_shapes=[pltpu.VMEM((MOE_CAP, D_MODEL), F32), pltpu.SemaphoreType.DMA(())],
        ),
        out_shape=jax.ShapeDtypeStruct(x.shape, F32),
        compiler_params=_cparams("arbitrary"),
        name="moe_combine",
    )(*tables, x, mod, lpos_t, wt, final_g.reshape(1, D_MODEL), yb)


def _moe(layer, final, st_p, st_s, x_p, x_s, post_p, post_s,
         w_gu, b_gu, w_dn, b_dn, final_g):
    te = jnp.concatenate([post_p[1], post_s[1]], axis=1)
    tw = jnp.concatenate([post_p[2], post_s[2]], axis=1)
    lpos, rstart, rlen, tot = _route(te)
    tot = tot[:, 0].astype(I32)
    padded = (tot + MOE_TB - 1) // MOE_TB * MOE_TB
    pends = jnp.cumsum(padded)
    n_used = (pends[-1] // MOE_TB).reshape(1)
    blk = jnp.minimum(jnp.arange(MOE_NBLK, dtype=I32), n_used[0] - 1) * MOE_TB
    block_e = jnp.minimum(jnp.sum(pends[None, :] <= blk[:, None], axis=-1),
                          N_EXPERTS - 1).astype(I32)
    ids = jnp.where(tot > 0, jnp.arange(N_EXPERTS, dtype=I32), N_EXPERTS)
    later = jnp.concatenate([lax.cummin(ids[::-1])[::-1][1:], jnp.full((1,), N_EXPERTS, I32)])
    nxt = jnp.where(later < N_EXPERTS, later, -1)[block_e]
    run_tables = (rstart[:, :, 0].reshape(-1), rlen[:, :, 0].reshape(-1))
    tail_tables = (pends - padded + tot, padded - tot)

    nt_p = N_PROMPT // MOE_RT
    nt_s = N_SAMPLE // MOE_RT
    xb = _dispatch(run_tables + tail_tables + (n_used,), post_p[0], post_s[0], lpos)
    yb = _ffn(layer, block_e, n_used, nxt, xb, w_gu, b_gu, w_dn, b_dn)
    lpos_t = lpos.T
    wt = tw.T
    tiles_per_seq = SEQ // MOE_RT
    out_p = _combine(0, nt_p, final, run_tables, x_p, st_p.mod,
                     lambda i, *_: (i // tiles_per_seq, 0, N_MOD - 1), lpos_t, wt, final_g, yb)
    out_s = _combine(nt_p, nt_s, final, run_tables, x_s, st_s.mod,
                     lambda i, *_: (0, 0, N_MOD - 1), lpos_t, wt, final_g, yb)
    return out_p, out_s


def _rope_tables(pos):
    half = RET_DK // 2
    inv_freq = ROPE_BASE ** (-jnp.arange(half, dtype=F32) / half)
    ang = pos.astype(F32)[:, None] * inv_freq[None, :]
    return jnp.cos(ang), jnp.sin(ang)


def kernel(x_prompt, x_sample, c_prompt, c_sample, state_conv, state_ret, norm_mix_g, norm_ff_g,
           w_mod, b_mod, conv_w1, conv_b1, conv_dw, conv_dw_b, conv_ln_g, conv_ln_b, conv_w2,
           conv_b2, ret_w_in, ret_w_o, router_w, router_b, moe_w_gu, moe_b_gu, moe_w_dn,
           moe_b_dn, final_g):
    mod = _adaln(jnp.concatenate([c_prompt, c_sample], axis=0), w_mod, b_mod)
    x_p = x_prompt.reshape(N_PROMPT, D_MODEL)
    x_s = jnp.swapaxes(x_sample, 0, 1).reshape(N_SAMPLE, D_MODEL)
    router_wt = jnp.swapaxes(router_w, 1, 2)

    def streams(layer):
        st_p = _Stream(N_PROMPT, TM_PROMPT, SEQ // TM_PROMPT,
                       mod[layer, :BATCH].reshape(BATCH, 1, N_MOD * D_MODEL))
        st_s = _Stream(N_SAMPLE, TM_SAMPLE, None,
                       mod[layer, BATCH:].reshape(1, DEC_BATCH, N_MOD * D_MODEL))
        return st_p, st_s

    st_p, st_s = streams(0)
    u_p = _pre_conv(st_p, x_p, norm_mix_g, conv_w1, conv_b1, 0, 0)
    u_s = _pre_conv(st_s, x_s, norm_mix_g, conv_w1, conv_b1, 0, 0)
    buf_tm = jnp.swapaxes(state_conv[0], 0, 1)
    z_s, ns_tm = _conv_sample(buf_tm, u_s.reshape(DEC_SEQ, DEC_BATCH, D_MODEL),
                              conv_dw, conv_dw_b, 0)
    post_p = _mixer_post("conv", st_p, 0, 0, x_p, u_p, norm_ff_g, router_wt, router_b,
                         conv_w2, conv_b2, conv_dw, conv_dw_b, conv_ln_g, conv_ln_b)
    post_s = _mixer_post("z", st_s, 0, 0, x_s, z_s.reshape(N_SAMPLE, D_MODEL), norm_ff_g,
                         router_wt, router_b, conv_w2, conv_b2, None, None, conv_ln_g, conv_ln_b)
    x_p, x_s = _moe(0, False, st_p, st_s, post_p[0], post_s[0], post_p[1:], post_s[1:],
                    moe_w_gu, moe_b_gu, moe_w_dn, moe_b_dn, final_g)
    conv_p = u_p.reshape(BATCH, SEQ, D_MODEL)[:, SEQ - CONV_STATE:][None]
    conv_s = jnp.swapaxes(ns_tm, 0, 1)[None]

    st_p, st_s = streams(1)
    w_in_b = ret_w_in.astype(BF16)
    cos_p, sin_p = _rope_tables(jnp.arange(SEQ, dtype=I32))
    cos_s, sin_s = _rope_tables(PAST_LEN + jnp.arange(DEC_SEQ, dtype=I32))
    half = RET_DK // 2
    tps = SEQ // TM_PROMPT
    tbl_p = pl.BlockSpec((TM_PROMPT, half), lambda i: (i % tps, 0))
    tbl_s = pl.BlockSpec((None, 1, half), lambda i: (i, 0, 0))
    q_p, k_p, v_p, sg_p = _pre_ret(st_p, x_p, norm_mix_g, w_in_b, 1, 0, cos_p, sin_p, tbl_p)
    q_s, k_s, v_s, sg_s = _pre_ret(st_s, x_s, norm_mix_g, w_in_b, 1, 0,
                                   cos_s.reshape(DEC_SEQ, 1, half),
                                   sin_s.reshape(DEC_SEQ, 1, half), tbl_s)
    lg = jnp.log1p(-jnp.exp2(-5.0 - jnp.arange(RET_HEADS, dtype=F32)))
    lg_tbl = jnp.broadcast_to(lg[:, None, None], (RET_HEADS, 8, 128))
    zero_state = jnp.zeros((1, BATCH, RET_HEADS, RET_DK, RET_DV), F32)
    gated_p, ret_p = _retention(q_p, k_p, v_p, sg_p, zero_state, 0, lg_tbl,
                                BATCH, SEQ // RET_CHUNK, RET_CHUNK, RET_CHUNK)

    def to_seq_major(a):
        a = jnp.swapaxes(a.reshape(DEC_SEQ, DEC_BATCH, -1), 0, 1)
        a = jnp.pad(a, ((0, 0), (0, RET_PAD - DEC_SEQ), (0, 0)))
        return a.reshape(DEC_BATCH * RET_PAD, -1)

    gated_s, ret_s = _retention_sample(to_seq_major(q_s), to_seq_major(k_s), to_seq_major(v_s),
                                       to_seq_major(sg_s), state_ret, 0, lg_tbl)
    gated_s = jnp.swapaxes(gated_s.reshape(DEC_BATCH, RET_PAD, -1)[:, :DEC_SEQ], 0, 1)
    gated_s = gated_s.reshape(N_SAMPLE, -1)
    post_p = _mixer_post("ret", st_p, 1, 0, x_p, gated_p, norm_ff_g, router_wt, router_b, ret_w_o)
    post_s = _mixer_post("ret", st_s, 1, 0, x_s, gated_s, norm_ff_g, router_wt, router_b, ret_w_o)
    y_p, y_s = _moe(1, True, st_p, st_s, post_p[0], post_s[0], post_p[1:], post_s[1:],
                    moe_w_gu, moe_b_gu, moe_w_dn, moe_b_dn, final_g)

    y_prompt = y_p.reshape(BATCH, SEQ, D_MODEL)
    y_sample = jnp.swapaxes(y_s.reshape(DEC_SEQ, DEC_BATCH, D_MODEL), 0, 1)
    return (y_prompt, y_sample, conv_p, conv_s, ret_p, ret_s)
```

```python
import functools

import jax
import jax.numpy as jnp
from jax import lax
from jax.experimental import pallas as pl
from jax.experimental.pallas import tpu as pltpu

F32 = jnp.float32
BF16 = jnp.bfloat16
I32 = jnp.int32

D_MODEL = 1024
SEQ = 2048
BATCH = 8
DEC_BATCH = 128
DEC_SEQ = 4
PAST_LEN = 16384
CONV_WIDTH = 31
CONV_STATE = CONV_WIDTH - 1
RET_HEADS = 4
RET_DK = 256
RET_DV = 512
ROPE_BASE = 10000.0
N_EXPERTS = 32
TOP_K = 4
D_FF = 1024
SWIGLU_LIMIT = 7.0
SWIGLU_ALPHA = 1.702
N_MOD = 6
NORM_EPS = 1e-5

N_PROMPT = BATCH * SEQ
N_SAMPLE = DEC_BATCH * DEC_SEQ
N_TOK = N_PROMPT + N_SAMPLE
N_ASSIGN = N_TOK * TOP_K

TM_PROMPT = 512
TM_SAMPLE = DEC_BATCH
HALO = 32
CONV_RB = 64
CONV_LW = 128
RET_CHUNK = 256
RET_PAD = 16
RET_SB = 8
MOE_TB = 256
MOE_RT = 256
MOE_NT = N_TOK // MOE_RT
RUN_ALIGN = 8
RUN_PIECES = (256, 128, 64, 32, 16, 8)
TAIL_PIECES = (128, 64, 32, 16, 8)
MOE_CAP = -(-(TOP_K * MOE_RT + N_EXPERTS * (RUN_ALIGN - 1)) // 256) * 256
MOE_MAX_ROWS = N_ASSIGN + MOE_NT * N_EXPERTS * (RUN_ALIGN - 1)
MOE_NBLK = -(-MOE_MAX_ROWS // MOE_TB) + N_EXPERTS
MOE_ROWS = MOE_NBLK * MOE_TB
FF_CHUNK = 256
VMEM_LIMIT = 56 * 1024 * 1024


def _cparams(*sem):
    return pltpu.CompilerParams(dimension_semantics=sem, vmem_limit_bytes=VMEM_LIMIT)


def _silu(x):
    return x * jax.nn.sigmoid(x)


def _rms_mod(x, g, sc, sh):
    y = x * lax.rsqrt(jnp.mean(x * x, axis=-1, keepdims=True) + NORM_EPS) * g
    return y * (1.0 + sc) + sh


def _split_bf16(x):
    hi = x.astype(BF16)
    lo = (x - hi.astype(F32)).astype(BF16)
    return hi, lo


def _adaln_kernel(c_ref, w_ref, b_ref, o_ref):
    a = _silu(c_ref[...]).astype(BF16)
    o_ref[...] = jnp.dot(a, w_ref[...].astype(BF16), preferred_element_type=F32) + b_ref[...]


def _adaln(c_all, w_mod, b_mod):
    depth, d, nm = w_mod.shape
    n = c_all.shape[0]
    return pl.pallas_call(
        _adaln_kernel,
        grid=(depth, nm // d),
        in_specs=[
            pl.BlockSpec((n, d), lambda l, j: (0, 0)),
            pl.BlockSpec((None, d, d), lambda l, j: (l, 0, j)),
            pl.BlockSpec((None, 1, d), lambda l, j: (l, 0, j)),
        ],
        out_specs=pl.BlockSpec((None, n, d), lambda l, j: (l, 0, j)),
        out_shape=jax.ShapeDtypeStruct((depth, n, nm), F32),
        compiler_params=_cparams("parallel", "parallel"),
        name="adaln",
    )(c_all, w_mod, b_mod.reshape(depth, 1, nm))


class _Stream:
    def __init__(self, n_rows, tm, tiles_per_seq, mod):
        self.n_rows = n_rows
        self.tm = tm
        self.n_tiles = n_rows // tm
        self.tiles_per_seq = tiles_per_seq
        self.mod = mod

    def mod_spec(self, j):
        mr = self.mod.shape[1]
        if self.tiles_per_seq is None:
            return pl.BlockSpec((None, mr, D_MODEL), lambda i: (0, 0, j))
        tps = self.tiles_per_seq
        return pl.BlockSpec((None, mr, D_MODEL), lambda i: (i // tps, 0, j))

    def row_spec(self, width, col=0):
        return pl.BlockSpec((self.tm, width), lambda i: (i, col))


def _const_spec(shape):
    nd = len(shape)
    return pl.BlockSpec(shape, lambda i: (0,) * nd)


def _layer_spec(shape, layer):
    nd = len(shape)
    return pl.BlockSpec((None,) + shape, lambda i: (layer,) + (0,) * nd)


def _pre_conv_kernel(x_ref, sh_ref, sc_ref, g_ref, w1_ref, b1_ref, u_ref, w1b_ref):
    @pl.when(pl.program_id(0) == 0)
    def _():
        w1b_ref[...] = w1_ref[...].astype(BF16)

    h = _rms_mod(x_ref[...], g_ref[...], sc_ref[...], sh_ref[...])
    ag = jnp.dot(h.astype(BF16), w1b_ref[...], preferred_element_type=F32) + b1_ref[...]
    u_ref[...] = ag[:, :D_MODEL] * jax.nn.sigmoid(ag[:, D_MODEL:])


def _pre_conv(st, x, norm_g, w1, b1, layer, j):
    c2 = w1.shape[-1]
    return pl.pallas_call(
        _pre_conv_kernel,
        grid=(st.n_tiles,),
        in_specs=[
            st.row_spec(D_MODEL),
            st.mod_spec(0), st.mod_spec(1),
            _layer_spec((1, D_MODEL), layer),
            _layer_spec((D_MODEL, c2), j),
            _layer_spec((1, c2), j),
        ],
        out_specs=st.row_spec(D_MODEL),
        out_shape=jax.ShapeDtypeStruct((st.n_rows, D_MODEL), F32),
        scratch_shapes=[pltpu.VMEM((D_MODEL, c2), BF16)],
        compiler_params=_cparams("arbitrary"),
        name="pre_conv",
    )(x, st.mod, st.mod, norm_g.reshape(-1, 1, D_MODEL), w1, b1.reshape(-1, 1, c2))


def _conv_sample_kernel(buf_ref, u_ref, dw_ref, dwb_ref, z_ref, ns_ref):
    for t in range(DEC_SEQ):
        acc = jnp.broadcast_to(dwb_ref[...], u_ref.shape[1:])
        for j in range(t, CONV_STATE):
            acc = acc + buf_ref[j] * dw_ref[j - t:j - t + 1, :]
        for s in range(t + 1):
            k = CONV_STATE + s - t
            acc = acc + u_ref[s] * dw_ref[k:k + 1, :]
        z_ref[t] = acc
    ns_ref[0:CONV_STATE - DEC_SEQ] = buf_ref[DEC_SEQ:CONV_STATE]
    ns_ref[CONV_STATE - DEC_SEQ:CONV_STATE] = u_ref[...]


def _conv_sample(buf_tm, u_tm, dw, dwb, j):
    sb = 32
    c = u_tm.shape[-1]
    return pl.pallas_call(
        _conv_sample_kernel,
        grid=(DEC_BATCH // sb,),
        in_specs=[
            pl.BlockSpec((CONV_STATE, sb, c), lambda i: (0, i, 0)),
            pl.BlockSpec((DEC_SEQ, sb, c), lambda i: (0, i, 0)),
            _layer_spec((CONV_WIDTH, c), j),
            _layer_spec((1, c), j),
        ],
        out_specs=[
            pl.BlockSpec((DEC_SEQ, sb, c), lambda i: (0, i, 0)),
            pl.BlockSpec((CONV_STATE, sb, c), lambda i: (0, i, 0)),
        ],
        out_shape=[
            jax.ShapeDtypeStruct((DEC_SEQ, DEC_BATCH, c), F32),
            jax.ShapeDtypeStruct((CONV_STATE, DEC_BATCH, c), F32),
        ],
        compiler_params=_cparams("parallel"),
        name="conv_sample",
    )(buf_tm, u_tm, dw, dwb.reshape(-1, 1, c))


def _route_topk(h2, rwt_ref, rb_ref, te_ref, tw_ref):
    h_hi, h_lo = _split_bf16(h2)
    w_hi, w_lo = _split_bf16(rwt_ref[...])
    nt = (((1,), (1,)), ((), ()))
    logits = (lax.dot_general(w_hi, h_hi, nt, preferred_element_type=F32)
              + lax.dot_general(w_hi, h_lo, nt, preferred_element_type=F32)
              + lax.dot_general(w_lo, h_hi, nt, preferred_element_type=F32)
              + rb_ref[...])
    iota_e = lax.broadcasted_iota(I32, logits.shape, 0).astype(F32)
    vals, idxs = [], []
    for _ in range(TOP_K):
        m = jnp.max(logits, axis=0, keepdims=True)
        idx = jnp.min(jnp.where(logits == m, iota_e, float(N_EXPERTS)), axis=0, keepdims=True)
        logits = jnp.where(iota_e == idx, -jnp.inf, logits)
        vals.append(m)
        idxs.append(idx)
    es = [jnp.exp(v - vals[0]) for v in vals]
    tot = es[0] + es[1] + es[2] + es[3]
    for k in range(TOP_K):
        te_ref[k:k + 1, :] = idxs[k].astype(I32)
        tw_ref[k:k + 1, :] = es[k] / tot


def _mixer_post_kernel(mode, tiles_per_seq, *refs):
    it = iter(refs)
    if mode == "conv":
        u_ref, halo_ref, dw_ref, dwb_ref = next(it), next(it), next(it), next(it)
    elif mode == "z":
        z_in_ref = next(it)
    else:
        gated_ref = next(it)
    if mode in ("conv", "z"):
        lng_ref, lnb_ref = next(it), next(it)
    w_ref, b_ref = next(it), next(it)
    x_ref, gm_ref, shf_ref, scf_ref, gff_ref, rwt_ref, rb_ref = (next(it) for _ in range(7))
    x1_ref, h2_ref, te_ref, tw_ref = (next(it) for _ in range(4))
    wb_ref = next(it)
    if mode == "conv":
        ext_ref, z_ref = next(it), next(it)

    i = pl.program_id(0)

    @pl.when(i == 0)
    def _():
        wb_ref[...] = w_ref[...].astype(BF16)

    if mode == "conv":
        tm = u_ref.shape[0]
        first = (i % tiles_per_seq) == 0
        ext_ref[0:HALO, :] = jnp.where(first, 0.0, halo_ref[...])
        ext_ref[HALO:HALO + tm, :] = u_ref[...]
        off = HALO - CONV_STATE

        def row_block(r, carry):
            r0 = pl.multiple_of(r * CONV_RB, CONV_RB)
            for lc in range(D_MODEL // CONV_LW):
                ls = slice(lc * CONV_LW, (lc + 1) * CONV_LW)
                win = ext_ref[pl.ds(r0, CONV_RB + HALO), ls]
                acc = jnp.broadcast_to(dwb_ref[:, ls], (CONV_RB, CONV_LW))
                for s in range(8):
                    ws = win if s == 0 else pltpu.roll(win, CONV_RB + HALO - s, 0)
                    for k in range(CONV_WIDTH):
                        if (off + k) % 8 == s:
                            a = off + k - s
                            acc = acc + ws[a:a + CONV_RB, :] * dw_ref[k:k + 1, ls]
                z_ref[pl.ds(r0, CONV_RB), ls] = acc
            return carry

        lax.fori_loop(0, tm // CONV_RB, row_block, 0)
        z = z_ref[...]
    elif mode == "z":
        z = z_in_ref[...]

    if mode in ("conv", "z"):
        mu = jnp.mean(z, axis=-1, keepdims=True)
        zc = z - mu
        var = jnp.mean(zc * zc, axis=-1, keepdims=True)
        zn = zc * lax.rsqrt(var + NORM_EPS) * lng_ref[...] + lnb_ref[...]
        a = _silu(zn).astype(BF16)
        y = jnp.dot(a, wb_ref[...], preferred_element_type=F32) + b_ref[...]
    else:
        y = jnp.dot(gated_ref[...], wb_ref[...], preferred_element_type=F32)

    x1 = x_ref[...] + gm_ref[...] * y
    x1_ref[...] = x1
    h2 = _rms_mod(x1, gff_ref[...], scf_ref[...], shf_ref[...])
    h2_ref[...] = h2
    _route_topk(h2, rwt_ref, rb_ref, te_ref, tw_ref)


def _mixer_post(mode, st, layer, j, x, mix_in, norm_ff_g, router_wt, router_b, w, b=None,
                dw=None, dwb=None, ln_g=None, ln_b=None):
    tm = st.tm
    k_in = w.shape[-2]
    args, specs = [], []
    if mode == "conv":
        per32 = tm // HALO
        args += [mix_in, mix_in, dw, dwb.reshape(-1, 1, D_MODEL)]
        specs += [st.row_spec(D_MODEL),
                  pl.BlockSpec((HALO, D_MODEL), lambda i: (jnp.maximum(i * per32 - 1, 0), 0)),
                  _layer_spec((CONV_WIDTH, D_MODEL), j), _layer_spec((1, D_MODEL), j)]
    elif mode == "z":
        args += [mix_in]
        specs += [st.row_spec(D_MODEL)]
    else:
        args += [mix_in]
        specs += [st.row_spec(k_in)]
    if mode in ("conv", "z"):
        args += [ln_g.reshape(-1, 1, D_MODEL), ln_b.reshape(-1, 1, D_MODEL)]
        specs += [_layer_spec((1, D_MODEL), j), _layer_spec((1, D_MODEL), j)]
    if b is None:
        b = jnp.zeros((w.shape[0], D_MODEL), F32)
    args += [w, b.reshape(-1, 1, D_MODEL)]
    specs += [_layer_spec((k_in, D_MODEL), j), _layer_spec((1, D_MODEL), j)]
    args += [x, st.mod, st.mod, st.mod, norm_ff_g.reshape(-1, 1, D_MODEL), router_wt,
             router_b.reshape(-1, N_EXPERTS, 1)]
    specs += [st.row_spec(D_MODEL), st.mod_spec(2), st.mod_spec(3), st.mod_spec(4),
              _layer_spec((1, D_MODEL), layer), _layer_spec((N_EXPERTS, D_MODEL), layer),
              _layer_spec((N_EXPERTS, 1), layer)]
    scratch = [pltpu.VMEM((k_in, D_MODEL), BF16)]
    if mode == "conv":
        scratch += [pltpu.VMEM((tm + HALO, D_MODEL), F32), pltpu.VMEM((tm, D_MODEL), F32)]
    return pl.pallas_call(
        functools.partial(_mixer_post_kernel, mode, st.tiles_per_seq),
        grid=(st.n_tiles,),
        in_specs=specs,
        out_specs=[st.row_spec(D_MODEL), st.row_spec(D_MODEL),
                   pl.BlockSpec((TOP_K, tm), lambda i: (0, i)),
                   pl.BlockSpec((TOP_K, tm), lambda i: (0, i))],
        out_shape=[jax.ShapeDtypeStruct((st.n_rows, D_MODEL), F32),
                   jax.ShapeDtypeStruct((st.n_rows, D_MODEL), F32),
                   jax.ShapeDtypeStruct((TOP_K, st.n_rows), I32),
                   jax.ShapeDtypeStruct((TOP_K, st.n_rows), F32)],
        scratch_shapes=scratch,
        compiler_params=_cparams("arbitrary"),
        name="mixer_post_" + mode,
    )(*args)


def _pre_ret_kernel(x_ref, sh_ref, sc_ref, g_ref, w_ref, cos_ref, sin_ref,
                    q_ref, k_ref, v_ref, sg_ref):
    h = _rms_mod(x_ref[...], g_ref[...], sc_ref[...], sh_ref[...]).astype(BF16)
    cos = cos_ref[...]
    sin = sin_ref[...]
    half = RET_DK // 2
    qk = RET_HEADS * RET_DK
    vd = RET_HEADS * RET_DV
    for out_ref, base, scale in ((q_ref, 0, 1.0), (k_ref, qk, RET_DK ** -0.5)):
        for hh in range(RET_HEADS):
            c0 = base + hh * RET_DK
            p = jnp.dot(h, w_ref[:, c0:c0 + RET_DK], preferred_element_type=F32)
            p1, p2 = p[:, :half], p[:, half:]
            o0 = hh * RET_DK
            out_ref[:, o0:o0 + half] = ((p1 * cos - p2 * sin) * scale).astype(BF16)
            out_ref[:, o0 + half:o0 + RET_DK] = ((p1 * sin + p2 * cos) * scale).astype(BF16)
    for hh in range(RET_HEADS):
        c0 = 2 * qk + hh * RET_DV
        v_ref[:, hh * RET_DV:(hh + 1) * RET_DV] = jnp.dot(
            h, w_ref[:, c0:c0 + RET_DV], preferred_element_type=F32).astype(BF16)
        g = jnp.dot(h, w_ref[:, vd + c0:vd + c0 + RET_DV], preferred_element_type=F32)
        sg_ref[:, hh * RET_DV:(hh + 1) * RET_DV] = _silu(g).astype(BF16)


def _pre_ret(st, x, norm_g, w_in_b, layer, j, cos, sin, tbl_spec):
    qk = RET_HEADS * RET_DK
    vd = RET_HEADS * RET_DV
    return pl.pallas_call(
        _pre_ret_kernel,
        grid=(st.n_tiles,),
        in_specs=[
            st.row_spec(D_MODEL),
            st.mod_spec(0), st.mod_spec(1),
            _layer_spec((1, D_MODEL), layer),
            _layer_spec((D_MODEL, 2 * qk + 2 * vd), j),
            tbl_spec, tbl_spec,
        ],
        out_specs=[st.row_spec(qk), st.row_spec(qk), st.row_spec(vd), st.row_spec(vd)],
        out_shape=[jax.ShapeDtypeStruct((st.n_rows, qk), BF16),
                   jax.ShapeDtypeStruct((st.n_rows, qk), BF16),
                   jax.ShapeDtypeStruct((st.n_rows, vd), BF16),
                   jax.ShapeDtypeStruct((st.n_rows, vd), BF16)],
        compiler_params=_cparams("parallel"),
        name="pre_ret",
    )(x, st.mod, st.mod, norm_g.reshape(-1, 1, D_MODEL), w_in_b, cos, sin)


def _decay_matrix(lg, ch):
    diff = (lax.broadcasted_iota(I32, (ch, ch), 0)
            - lax.broadcasted_iota(I32, (ch, ch), 1)).astype(F32)
    return jnp.where(diff >= 0.0, jnp.exp(lg * jnp.maximum(diff, 0.0)), 0.0)


def _retention_chunk(valid, lg, dec, q, k, v, sg, s_prev):
    ch = q.shape[0]
    idx = lax.broadcasted_iota(I32, (ch, 1), 0).astype(F32)
    scores = lax.dot_general(q, k, (((1,), (1,)), ((), ())), preferred_element_type=F32) * dec
    o = jnp.dot(scores.astype(BF16), v, preferred_element_type=F32)
    cross = jnp.exp(lg * (idx + 1.0))
    o = o + jnp.dot(q, s_prev.astype(BF16), preferred_element_type=F32) * cross
    kd = (k.astype(F32) * jnp.exp(lg * (valid - 1.0 - idx))).astype(BF16)
    s_new = jnp.exp(lg * float(valid)) * s_prev + lax.dot_general(
        kd, v, (((0,), (0,)), ((), ())), preferred_element_type=F32)
    mu = jnp.mean(o, axis=-1, keepdims=True)
    oc = o - mu
    var = jnp.mean(oc * oc, axis=-1, keepdims=True)
    on = oc * lax.rsqrt(var + NORM_EPS)
    return (sg.astype(F32) * on).astype(BF16), s_new


def _retention_kernel(valid, lg_ref, q_ref, k_ref, v_ref, sg_ref, s0_ref,
                      o_ref, s_out_ref, s_ref, dec_ref):
    c = pl.program_id(2)
    lg = lg_ref[0:1, 0:1]

    @pl.when(c == 0)
    def _():
        s_ref[...] = s0_ref[...]
        dec_ref[...] = _decay_matrix(lg, q_ref.shape[0])

    gated, s_new = _retention_chunk(valid, lg, dec_ref[...], q_ref[...], k_ref[...], v_ref[...],
                                    sg_ref[...], s_ref[...])
    o_ref[...] = gated
    s_ref[...] = s_new

    @pl.when(c == pl.num_programs(2) - 1)
    def _():
        s_out_ref[...] = s_new


def _retention_sample_kernel(lg_ref, q_ref, k_ref, v_ref, sg_ref, s0_ref, o_ref, s_out_ref):
    lg = lg_ref[0:1, 0:1]
    dec = _decay_matrix(lg, RET_PAD)
    for b in range(RET_SB):
        rows = slice(b * RET_PAD, (b + 1) * RET_PAD)
        gated, s_new = _retention_chunk(DEC_SEQ, lg, dec, q_ref[rows, :], k_ref[rows, :],
                                        v_ref[rows, :], sg_ref[rows, :], s0_ref[b])
        o_ref[rows, :] = gated
        s_out_ref[b] = s_new


def _retention_sample(q, k, v, sg, s0, s0_layer, lg_tbl):
    rows = RET_SB * RET_PAD
    row = lambda i, h: (i, h)
    state_block = (None, RET_SB, None, RET_DK, RET_DV)
    return pl.pallas_call(
        _retention_sample_kernel,
        grid=(DEC_BATCH // RET_SB, RET_HEADS),
        in_specs=[
            pl.BlockSpec((None, 8, 128), lambda i, h: (h, 0, 0)),
            pl.BlockSpec((rows, RET_DK), row), pl.BlockSpec((rows, RET_DK), row),
            pl.BlockSpec((rows, RET_DV), row), pl.BlockSpec((rows, RET_DV), row),
            pl.BlockSpec(state_block, lambda i, h: (s0_layer, i, h, 0, 0)),
        ],
        out_specs=[pl.BlockSpec((rows, RET_DV), row),
                   pl.BlockSpec(state_block, lambda i, h: (0, i, h, 0, 0))],
        out_shape=[jax.ShapeDtypeStruct((DEC_BATCH * RET_PAD, RET_HEADS * RET_DV), BF16),
                   jax.ShapeDtypeStruct((1, DEC_BATCH, RET_HEADS, RET_DK, RET_DV), F32)],
        compiler_params=_cparams("parallel", "parallel"),
        name="retention_sample",
    )(lg_tbl, q, k, v, sg, s0)


def _retention(q, k, v, sg, s0, s0_layer, lg_tbl, n_seq, n_chunks, ch, valid):
    n_rows = q.shape[0]
    row = lambda b, h, c: (b * n_chunks + c, h)
    state_spec = pl.BlockSpec((None, None, None, RET_DK, RET_DV),
                              lambda b, h, c: (s0_layer, b, h, 0, 0))
    out_state_spec = pl.BlockSpec((None, None, None, RET_DK, RET_DV),
                                  lambda b, h, c: (0, b, h, 0, 0))
    return pl.pallas_call(
        functools.partial(_retention_kernel, valid),
        grid=(n_seq, RET_HEADS, n_chunks),
        in_specs=[
            pl.BlockSpec((None, 8, 128), lambda b, h, c: (h, 0, 0)),
            pl.BlockSpec((ch, RET_DK), row), pl.BlockSpec((ch, RET_DK), row),
            pl.BlockSpec((ch, RET_DV), row), pl.BlockSpec((ch, RET_DV), row),
            state_spec,
        ],
        out_specs=[pl.BlockSpec((ch, RET_DV), row), out_state_spec],
        out_shape=[jax.ShapeDtypeStruct((n_rows, RET_HEADS * RET_DV), BF16),
                   jax.ShapeDtypeStruct((1, n_seq, RET_HEADS, RET_DK, RET_DV), F32)],
        scratch_shapes=[pltpu.VMEM((RET_DK, RET_DV), F32), pltpu.VMEM((ch, ch), F32)],
        compiler_params=_cparams("parallel", "parallel", "arbitrary"),
        name="retention",
    )(lg_tbl, q, k, v, sg, s0)


def _excl_prefix(col):
    r = lax.broadcasted_iota(I32, (N_EXPERTS, N_EXPERTS), 0)
    c = lax.broadcasted_iota(I32, (N_EXPERTS, N_EXPERTS), 1)
    lane = jnp.sum(jnp.where(r < c, col, 0.0), axis=0, keepdims=True)
    return jnp.sum(jnp.where(r == c, lane, 0.0), axis=1, keepdims=True)


def _ceil_to(x, m):
    return jnp.floor((x + (m - 1.0)) * (1.0 / m)) * m


def _route_kernel(te_ref, lpos_ref, rstart_ref, rlen_ref, tot_ref, tri_ref):
    rt = te_ref.shape[2]
    iota_e = lax.broadcasted_iota(I32, (N_EXPERTS, rt), 0)
    s = lax.broadcasted_iota(I32, (rt, rt), 0)
    t = lax.broadcasted_iota(I32, (rt, rt), 1)
    tri_ref[...] = (s < t).astype(BF16)

    def tile_onehots(i):
        te = te_ref[i]
        onehots = [(te[k:k + 1, :] == iota_e).astype(F32) for k in range(TOP_K)]
        cnt = jnp.sum(onehots[0] + onehots[1] + onehots[2] + onehots[3], axis=1, keepdims=True)
        return onehots, _ceil_to(cnt, RUN_ALIGN)

    tot = lax.fori_loop(0, MOE_NT, lambda i, acc: acc + tile_onehots(i)[1],
                        jnp.zeros((N_EXPERTS, 1), F32))
    tot_ref[...] = tot

    def place(i, gbase):
        onehots, run = tile_onehots(i)
        base = _excl_prefix(run)
        for k in range(TOP_K):
            oh = onehots[k]
            before = jnp.dot(oh.astype(BF16), tri_ref[...], preferred_element_type=F32)
            pos = jnp.sum(oh * (base + before), axis=0, keepdims=True)
            lpos_ref[i, k:k + 1, :] = pos.astype(I32)
            base = base + jnp.sum(oh, axis=1, keepdims=True)
        rstart_ref[i] = jnp.broadcast_to(gbase, rstart_ref.shape[1:]).astype(I32)
        rlen_ref[i] = jnp.broadcast_to(run, rlen_ref.shape[1:]).astype(I32)
        return gbase + run

    lax.fori_loop(0, MOE_NT, place, _excl_prefix(_ceil_to(tot, MOE_TB)))


def _route(te):
    full = lambda shape: pl.BlockSpec(shape, lambda i: (0,) * len(shape))
    return pl.pallas_call(
        _route_kernel,
        grid=(1,),
        in_specs=[full(te.shape)],
        out_specs=[full(te.shape), full((MOE_NT, N_EXPERTS, 128)), full((MOE_NT, N_EXPERTS, 128)),
                   full((N_EXPERTS, 1))],
        out_shape=[jax.ShapeDtypeStruct(te.shape, I32),
                   jax.ShapeDtypeStruct((MOE_NT, N_EXPERTS, 128), I32),
                   jax.ShapeDtypeStruct((MOE_NT, N_EXPERTS, 128), I32),
                   jax.ShapeDtypeStruct((N_EXPERTS, 1), F32)],
        scratch_shapes=[pltpu.VMEM((MOE_RT, MOE_RT), BF16)],
        compiler_params=_cparams("arbitrary"),
        name="moe_route",
    )(te)


def _for_each_piece(n, sizes, fn):
    off = 0
    for b in sizes:
        @pl.when((n & b) != 0)
        def _(off=off, b=b):
            fn(off, b)
        off = off + (n & b)


def _dispatch_kernel(n_prompt_tiles, rs_ref, rl_ref, tzs_ref, tzl_ref, nu_ref, hp_ref, hs_ref,
                     lpos_ref, xb_ref, xs_ref, sem, zero_ref):
    i = pl.program_id(0)
    slot = i % 2
    zero_sem = sem.at[2]

    def tail_copy(e, off, b):
        dst = pl.multiple_of(tzs_ref[e] + off, RUN_ALIGN)
        return pltpu.make_async_copy(zero_ref.at[pl.ds(0, b)], xb_ref.at[pl.ds(dst, b)], zero_sem)

    def block_copy(j):
        dst = pl.multiple_of(j * MOE_TB, MOE_TB)
        return pltpu.make_async_copy(zero_ref, xb_ref.at[pl.ds(dst, MOE_TB)], zero_sem)

    @pl.when(i == 0)
    def _():
        zero_ref[...] = jnp.zeros_like(zero_ref)

        def start(e, carry):
            _for_each_piece(tzl_ref[e], TAIL_PIECES, lambda off, b: tail_copy(e, off, b).start())
            return carry

        def wait(e, carry):
            _for_each_piece(tzl_ref[e], TAIL_PIECES, lambda off, b: tail_copy(e, off, b).wait())
            return carry

        lax.fori_loop(0, N_EXPERTS, start, 0)
        lax.fori_loop(nu_ref[0], MOE_NBLK, lambda j, c: (block_copy(j).start(), c)[1], 0)
        lax.fori_loop(0, N_EXPERTS, wait, 0)
        lax.fori_loop(nu_ref[0], MOE_NBLK, lambda j, c: (block_copy(j).wait(), c)[1], 0)

    rows = lax.broadcasted_iota(I32, (MOE_CAP, MOE_RT), 0)
    perm = jnp.where(rows == lpos_ref[0:1, :], 1.0, 0.0)
    for k in range(1, TOP_K):
        perm = perm + jnp.where(rows == lpos_ref[k:k + 1, :], 1.0, 0.0)
    h = jnp.where(i < n_prompt_tiles, hp_ref[...], hs_ref[...])
    xs_ref[slot] = jnp.dot(perm.astype(BF16), h.astype(BF16), preferred_element_type=F32)

    def run_copy(tile, e, lo, off, b):
        sl = tile % 2
        src = pl.multiple_of(lo + off, RUN_ALIGN)
        dst = pl.multiple_of(rs_ref[tile * N_EXPERTS + e] + off, RUN_ALIGN)
        return pltpu.make_async_copy(xs_ref.at[sl, pl.ds(src, b)], xb_ref.at[pl.ds(dst, b)],
                                     sem.at[sl])

    def scatter(tile, wait):
        def body(e, lo):
            n = rl_ref[tile * N_EXPERTS + e]
            if wait:
                _for_each_piece(n, RUN_PIECES, lambda off, b: run_copy(tile, e, lo, off, b).wait())
            else:
                _for_each_piece(n, RUN_PIECES, lambda off, b: run_copy(tile, e, lo, off, b).start())
            return lo + n

        lax.fori_loop(0, N_EXPERTS, body, 0)

    scatter(i, False)

    @pl.when(i > 0)
    def _():
        scatter(i - 1, True)

    @pl.when(i == pl.num_programs(0) - 1)
    def _():
        scatter(i, True)


def _dispatch(tables, h_p, h_s, lpos):
    nt_p = h_p.shape[0] // MOE_RT
    return pl.pallas_call(
        functools.partial(_dispatch_kernel, nt_p),
        grid_spec=pltpu.PrefetchScalarGridSpec(
            num_scalar_prefetch=len(tables),
            grid=(MOE_NT,),
            in_specs=[
                pl.BlockSpec((MOE_RT, D_MODEL), lambda i, *_: (jnp.minimum(i, nt_p - 1), 0)),
                pl.BlockSpec((MOE_RT, D_MODEL), lambda i, *_: (jnp.maximum(i - nt_p, 0), 0)),
                pl.BlockSpec((None, TOP_K, MOE_RT), lambda i, *_: (i, 0, 0)),
            ],
            out_specs=pl.BlockSpec(memory_space=pl.ANY),
            scratch_shapes=[pltpu.VMEM((2, MOE_CAP, D_MODEL), F32),
                            pltpu.SemaphoreType.DMA((3,)),
                            pltpu.VMEM((MOE_TB, D_MODEL), F32)],
        ),
        out_shape=jax.ShapeDtypeStruct((MOE_ROWS, D_MODEL), F32),
        compiler_params=_cparams("arbitrary"),
        name="moe_dispatch",
    )(*tables, h_p, h_s, lpos)


def _ffn_kernel(layer, be_ref, nu_ref, nxt_ref, x_ref, bgu_ref, bdn_ref, wgu_hbm, wdn_hbm, y_ref,
                wgu_f_ref, wdn_f_ref, wgu_b_ref, wdn_b_ref, act_ref, sem):
    j = pl.program_id(0)
    e = be_ref[j]
    e_prev = be_ref[jnp.maximum(j - 1, 0)]
    active = j < nu_ref[0]

    def weight_copies(ex):
        return (pltpu.make_async_copy(wgu_hbm.at[layer, ex], wgu_f_ref, sem.at[0]),
                pltpu.make_async_copy(wdn_hbm.at[layer, ex], wdn_f_ref, sem.at[1]))

    @pl.when(j == 0)
    def _():
        for cp in weight_copies(e):
            cp.start()

    @pl.when(active & ((j == 0) | (e != e_prev)))
    def _():
        for cp in weight_copies(e):
            cp.wait()
        wgu_b_ref[...] = wgu_f_ref[...].astype(BF16)
        wdn_b_ref[...] = wdn_f_ref[...].astype(BF16)
        nx = nxt_ref[j]

        @pl.when(nx >= 0)
        def _():
            for cp in weight_copies(nx):
                cp.start()

    @pl.when(active)
    def _():
        x = x_ref[...].astype(BF16)
        for c in range(D_FF // FF_CHUNK):
            gs = slice(c * FF_CHUNK, (c + 1) * FF_CHUNK)
            us = slice(D_FF + c * FF_CHUNK, D_FF + (c + 1) * FF_CHUNK)
            gate = jnp.dot(x, wgu_b_ref[:, gs], preferred_element_type=F32) + bgu_ref[:, gs]
            up = jnp.dot(x, wgu_b_ref[:, us], preferred_element_type=F32) + bgu_ref[:, us]
            gate = jnp.minimum(gate, SWIGLU_LIMIT)
            up = jnp.clip(up, -SWIGLU_LIMIT, SWIGLU_LIMIT)
            act = (up + 1.0) * (gate * jax.nn.sigmoid(SWIGLU_ALPHA * gate))
            act_ref[:, gs] = act.astype(BF16)
        y_ref[...] = jnp.dot(act_ref[...], wdn_b_ref[...],
                             preferred_element_type=F32) + bdn_ref[...]

    @pl.when(jnp.logical_not(active))
    def _():
        y_ref[...] = jnp.zeros_like(y_ref)


def _ffn(layer, block_e, n_used, nxt, xb, w_gu, b_gu, w_dn, b_dn):
    row_map = lambda j, be, nu, nx: (jnp.minimum(j, nu[0] - 1), 0)
    out_map = lambda j, be, nu, nx: (j, 0)
    exp_map = lambda j, be, nu, nx: (layer, be[j], 0, 0)
    return pl.pallas_call(
        functools.partial(_ffn_kernel, layer),
        grid_spec=pltpu.PrefetchScalarGridSpec(
            num_scalar_prefetch=3,
            grid=(MOE_NBLK,),
            in_specs=[
                pl.BlockSpec((MOE_TB, D_MODEL), row_map),
                pl.BlockSpec((None, None, 1, 2 * D_FF), exp_map),
                pl.BlockSpec((None, None, 1, D_MODEL), exp_map),
                pl.BlockSpec(memory_space=pl.ANY),
                pl.BlockSpec(memory_space=pl.ANY),
            ],
            out_specs=pl.BlockSpec((MOE_TB, D_MODEL), out_map),
            scratch_shapes=[pltpu.VMEM((D_MODEL, 2 * D_FF), F32),
                            pltpu.VMEM((D_FF, D_MODEL), F32),
                            pltpu.VMEM((D_MODEL, 2 * D_FF), BF16),
                            pltpu.VMEM((D_FF, D_MODEL), BF16),
                            pltpu.VMEM((MOE_TB, D_FF), BF16),
                            pltpu.SemaphoreType.DMA((2,))],
        ),
        out_shape=jax.ShapeDtypeStruct((MOE_ROWS, D_MODEL), F32),
        compiler_params=_cparams("arbitrary"),
        name="moe_ffn",
    )(block_e, n_used, nxt, xb, b_gu.reshape(b_gu.shape[0], N_EXPERTS, 1, 2 * D_FF),
      b_dn.reshape(b_dn.shape[0], N_EXPERTS, 1, D_MODEL), w_gu, w_dn)


def _tile_rows(v, n_rows):
    mr = v.shape[0]
    if mr in (1, n_rows):
        return v
    return jnp.concatenate([v] * (n_rows // mr), axis=0)


def _combine_kernel(tile_off, final, rs_ref, rl_ref, x_ref, gf_ref, lpt_ref, wt_ref, fg_ref,
                    yb_ref, out_ref, ys_ref, sem):
    i = pl.program_id(0)

    def run_copy(tile, e, lo, off, b):
        sl = tile % 2
        src = pl.multiple_of(rs_ref[(tile_off + tile) * N_EXPERTS + e] + off, RUN_ALIGN)
        dst = pl.multiple_of(lo + off, RUN_ALIGN)
        return pltpu.make_async_copy(yb_ref.at[pl.ds(src, b)], ys_ref.at[sl, pl.ds(dst, b)],
                                     sem.at[sl])

    def gather(tile, wait):
        def body(e, lo):
            n = rl_ref[(tile_off + tile) * N_EXPERTS + e]
            if wait:
                _for_each_piece(n, RUN_PIECES, lambda off, b: run_copy(tile, e, lo, off, b).wait())
            else:
                _for_each_piece(n, RUN_PIECES, lambda off, b: run_copy(tile, e, lo, off, b).start())
            return lo + n

        lax.fori_loop(0, N_EXPERTS, body, 0)

    @pl.when(i == 0)
    def _():
        ys_ref[...] = jnp.zeros_like(ys_ref)
        gather(0, False)

    @pl.when(i + 1 < pl.num_programs(0))
    def _():
        gather(i + 1, False)

    gather(i, True)

    cols = lax.broadcasted_iota(I32, (MOE_RT, MOE_CAP), 1)
    w = jnp.where(cols == lpt_ref[:, 0:1], wt_ref[:, 0:1], 0.0)
    for k in range(1, TOP_K):
        w = w + jnp.where(cols == lpt_ref[:, k:k + 1], wt_ref[:, k:k + 1], 0.0)
    w_hi, w_lo = _split_bf16(w)
    ys = ys_ref[i % 2].astype(BF16)
    y = (jnp.dot(w_hi, ys, preferred_element_type=F32)
         + jnp.dot(w_lo, ys, preferred_element_type=F32))
    xn = x_ref[...] + _tile_rows(gf_ref[...], MOE_RT) * y
    if final:
        xn = xn * lax.rsqrt(jnp.mean(xn * xn, axis=-1, keepdims=True) + NORM_EPS) * fg_ref[...]
    out_ref[...] = xn


def _combine(tile_off, n_tiles, final, tables, x, mod, mod_map, lpos_t, wt, final_g, yb):
    pm = lambda i, *_: (i, 0)
    tm = lambda i, *_: (i + tile_off, 0)
    return pl.pallas_call(
        functools.partial(_combine_kernel, tile_off, final),
        grid_spec=pltpu.PrefetchScalarGridSpec(
            num_scalar_prefetch=len(tables),
            grid=(n_tiles,),
            in_specs=[
                pl.BlockSpec((MOE_RT, D_MODEL), pm),
                pl.BlockSpec((None, mod.shape[1], D_MODEL), mod_map),
                pl.BlockSpec((MOE_RT, TOP_K), tm),
                pl.BlockSpec((MOE_RT, TOP_K), tm),
                pl.BlockSpec((1, D_MODEL), lambda i, *_: (0, 0)),
                pl.BlockSpec(memory_space=pl.ANY),
            ],
            out_specs=pl.BlockSpec((MOE_RT, D_MODEL), pm),
            scratch_shapes=[pltpu.VMEM((2, MOE_CAP, D_MODEL), F32),
                            pltpu.SemaphoreType.DMA((2,))],
        ),
        out_shape=jax.ShapeDtypeStruct(x.shape, F32),
        compiler_params=_cparams("arbitrary"),
        name="moe_combine",
    )(*tables, x, mod, lpos_t, wt, final_g.reshape(1, D_MODEL), yb)


def _moe(layer, final, st_p, st_s, x_p, x_s, post_p, post_s,
         w_gu, b_gu, w_dn, b_dn, final_g):
    te = jnp.concatenate([post_p[1], post_s[1]], axis=1)
    tw = jnp.concatenate([post_p[2], post_s[2]], axis=1)
    te = jnp.swapaxes(te.reshape(TOP_K, MOE_NT, MOE_RT), 0, 1)
    lpos, rstart, rlen, tot = _route(te)
    tot = tot[:, 0].astype(I32)
    padded = (tot + MOE_TB - 1) // MOE_TB * MOE_TB
    pends = jnp.cumsum(padded)
    n_used = (pends[-1] // MOE_TB).reshape(1)
    blk = jnp.minimum(jnp.arange(MOE_NBLK, dtype=I32), n_used[0] - 1) * MOE_TB
    block_e = jnp.minimum(jnp.sum(pends[None, :] <= blk[:, None], axis=-1),
                          N_EXPERTS - 1).astype(I32)
    ids = jnp.where(tot > 0, jnp.arange(N_EXPERTS, dtype=I32), N_EXPERTS)
    later = jnp.concatenate([lax.cummin(ids[::-1])[::-1][1:], jnp.full((1,), N_EXPERTS, I32)])
    nxt = jnp.where(later < N_EXPERTS, later, -1)[block_e]
    run_tables = (rstart[:, :, 0].reshape(-1), rlen[:, :, 0].reshape(-1))
    tail_tables = (pends - padded + tot, padded - tot)

    nt_p = N_PROMPT // MOE_RT
    nt_s = N_SAMPLE // MOE_RT
    xb = _dispatch(run_tables + tail_tables + (n_used,), post_p[0], post_s[0], lpos)
    yb = _ffn(layer, block_e, n_used, nxt, xb, w_gu, b_gu, w_dn, b_dn)
    lpos_t = jnp.swapaxes(lpos, 1, 2).reshape(N_TOK, TOP_K)
    wt = tw.T
    tiles_per_seq = SEQ // MOE_RT
    out_p = _combine(0, nt_p, final, run_tables, x_p, st_p.mod,
                     lambda i, *_: (i // tiles_per_seq, 0, N_MOD - 1), lpos_t, wt, final_g, yb)
    out_s = _combine(nt_p, nt_s, final, run_tables, x_s, st_s.mod,
                     lambda i, *_: (0, 0, N_MOD - 1), lpos_t, wt, final_g, yb)
    return out_p, out_s


def _rope_tables(pos):
    half = RET_DK // 2
    inv_freq = ROPE_BASE ** (-jnp.arange(half, dtype=F32) / half)
    ang = pos.astype(F32)[:, None] * inv_freq[None, :]
    return jnp.cos(ang), jnp.sin(ang)


def kernel(x_prompt, x_sample, c_prompt, c_sample, state_conv, state_ret, norm_mix_g, norm_ff_g,
           w_mod, b_mod, conv_w1, conv_b1, conv_dw, conv_dw_b, conv_ln_g, conv_ln_b, conv_w2,
           conv_b2, ret_w_in, ret_w_o, router_w, router_b, moe_w_gu, moe_b_gu, moe_w_dn,
           moe_b_dn, final_g):
    mod = _adaln(jnp.concatenate([c_prompt, c_sample], axis=0), w_mod, b_mod)
    x_p = x_prompt.reshape(N_PROMPT, D_MODEL)
    x_s = jnp.swapaxes(x_sample, 0, 1).reshape(N_SAMPLE, D_MODEL)
    router_wt = jnp.swapaxes(router_w, 1, 2)

    def streams(layer):
        st_p = _Stream(N_PROMPT, TM_PROMPT, SEQ // TM_PROMPT,
                       mod[layer, :BATCH].reshape(BATCH, 1, N_MOD * D_MODEL))
        st_s = _Stream(N_SAMPLE, TM_SAMPLE, None,
                       mod[layer, BATCH:].reshape(1, DEC_BATCH, N_MOD * D_MODEL))
        return st_p, st_s

    st_p, st_s = streams(0)
    u_p = _pre_conv(st_p, x_p, norm_mix_g, conv_w1, conv_b1, 0, 0)
    u_s = _pre_conv(st_s, x_s, norm_mix_g, conv_w1, conv_b1, 0, 0)
    buf_tm = jnp.swapaxes(state_conv[0], 0, 1)
    z_s, ns_tm = _conv_sample(buf_tm, u_s.reshape(DEC_SEQ, DEC_BATCH, D_MODEL),
                              conv_dw, conv_dw_b, 0)
    post_p = _mixer_post("conv", st_p, 0, 0, x_p, u_p, norm_ff_g, router_wt, router_b,
                         conv_w2, conv_b2, conv_dw, conv_dw_b, conv_ln_g, conv_ln_b)
    post_s = _mixer_post("z", st_s, 0, 0, x_s, z_s.reshape(N_SAMPLE, D_MODEL), norm_ff_g,
                         router_wt, router_b, conv_w2, conv_b2, None, None, conv_ln_g, conv_ln_b)
    x_p, x_s = _moe(0, False, st_p, st_s, post_p[0], post_s[0], post_p[1:], post_s[1:],
                    moe_w_gu, moe_b_gu, moe_w_dn, moe_b_dn, final_g)
    conv_p = u_p.reshape(BATCH, SEQ, D_MODEL)[:, SEQ - CONV_STATE:][None]
    conv_s = jnp.swapaxes(ns_tm, 0, 1)[None]

    st_p, st_s = streams(1)
    w_in_b = ret_w_in.astype(BF16)
    cos_p, sin_p = _rope_tables(jnp.arange(SEQ, dtype=I32))
    cos_s, sin_s = _rope_tables(PAST_LEN + jnp.arange(DEC_SEQ, dtype=I32))
    half = RET_DK // 2
    tps = SEQ // TM_PROMPT
    tbl_p = pl.BlockSpec((TM_PROMPT, half), lambda i: (i % tps, 0))
    tbl_s = pl.BlockSpec((None, 1, half), lambda i: (i, 0, 0))
    q_p, k_p, v_p, sg_p = _pre_ret(st_p, x_p, norm_mix_g, w_in_b, 1, 0, cos_p, sin_p, tbl_p)
    q_s, k_s, v_s, sg_s = _pre_ret(st_s, x_s, norm_mix_g, w_in_b, 1, 0,
                                   cos_s.reshape(DEC_SEQ, 1, half),
                                   sin_s.reshape(DEC_SEQ, 1, half), tbl_s)
    lg = jnp.log1p(-jnp.exp2(-5.0 - jnp.arange(RET_HEADS, dtype=F32)))
    lg_tbl = jnp.broadcast_to(lg[:, None, None], (RET_HEADS, 8, 128))
    zero_state = jnp.zeros((1, BATCH, RET_HEADS, RET_DK, RET_DV), F32)
    gated_p, ret_p = _retention(q_p, k_p, v_p, sg_p, zero_state, 0, lg_tbl,
                                BATCH, SEQ // RET_CHUNK, RET_CHUNK, RET_CHUNK)

    def to_seq_major(a):
        a = jnp.swapaxes(a.reshape(DEC_SEQ, DEC_BATCH, -1), 0, 1)
        a = jnp.pad(a, ((0, 0), (0, RET_PAD - DEC_SEQ), (0, 0)))
        return a.reshape(DEC_BATCH * RET_PAD, -1)

    gated_s, ret_s = _retention_sample(to_seq_major(q_s), to_seq_major(k_s), to_seq_major(v_s),
                                       to_seq_major(sg_s), state_ret, 0, lg_tbl)
    gated_s = jnp.swapaxes(gated_s.reshape(DEC_BATCH, RET_PAD, -1)[:, :DEC_SEQ], 0, 1)
    gated_s = gated_s.reshape(N_SAMPLE, -1)
    post_p = _mixer_post("ret", st_p, 1, 0, x_p, gated_p, norm_ff_g, router_wt, router_b, ret_w_o)
    post_s = _mixer_post("ret", st_s, 1, 0, x_s, gated_s, norm_ff_g, router_wt, router_b, ret_w_o)
    y_p, y_s = _moe(1, True, st_p, st_s, post_p[0], post_s[0], post_p[1:], post_s[1:],
                    moe_w_gu, moe_b_gu, moe_w_dn, moe_b_dn, final_g)

    y_prompt = y_p.reshape(BATCH, SEQ, D_MODEL)
    y_sample = jnp.swapaxes(y_s.reshape(DEC_SEQ, DEC_BATCH, D_MODEL), 0, 1)
    return (y_prompt, y_sample, conv_p, conv_s, ret_p, ret_s)
```

```python
import functools

import jax
import jax.numpy as jnp
from jax import lax
from jax.experimental import pallas as pl
from jax.experimental.pallas import tpu as pltpu

F32 = jnp.float32
BF16 = jnp.bfloat16
I32 = jnp.int32

D_MODEL = 1024
SEQ = 2048
BATCH = 8
DEC_BATCH = 128
DEC_SEQ = 4
PAST_LEN = 16384
CONV_WIDTH = 31
CONV_STATE = CONV_WIDTH - 1
RET_HEADS = 4
RET_DK = 256
RET_DV = 512
ROPE_BASE = 10000.0
N_EXPERTS = 32
TOP_K = 4
D_FF = 1024
SWIGLU_LIMIT = 7.0
SWIGLU_ALPHA = 1.702
N_MOD = 6
NORM_EPS = 1e-5

N_PROMPT = BATCH * SEQ
N_SAMPLE = DEC_BATCH * DEC_SEQ
N_TOK = N_PROMPT + N_SAMPLE
N_ASSIGN = N_TOK * TOP_K

TM_PROMPT = 512
TM_SAMPLE = DEC_BATCH
HALO = 32
CONV_RB = 64
CONV_LW = 128
RET_CHUNK = 256
RET_PAD = 16
RET_SB = 8
MOE_TB = 512
MOE_RT = 256
MOE_NT = N_TOK // MOE_RT
RUN_ALIGN = 8
MOE_CAP = -(-(TOP_K * MOE_RT + N_EXPERTS * (RUN_ALIGN - 1)) // 256) * 256
MOE_MAX_ROWS = N_ASSIGN + MOE_NT * N_EXPERTS * (RUN_ALIGN - 1)
MOE_NBLK = -(-MOE_MAX_ROWS // MOE_TB) + N_EXPERTS
MOE_ROWS = MOE_NBLK * MOE_TB
FF_CHUNK = 256
VMEM_LIMIT = 56 * 1024 * 1024


def _cparams(*sem):
    return pltpu.CompilerParams(dimension_semantics=sem, vmem_limit_bytes=VMEM_LIMIT)


def _silu(x):
    return x * jax.nn.sigmoid(x)


def _rms_mod(x, g, sc, sh):
    y = x * lax.rsqrt(jnp.mean(x * x, axis=-1, keepdims=True) + NORM_EPS) * g
    return y * (1.0 + sc) + sh


def _split_bf16(x):
    hi = x.astype(BF16)
    lo = (x - hi.astype(F32)).astype(BF16)
    return hi, lo


def _adaln_kernel(c_ref, w_ref, b_ref, o_ref):
    a = _silu(c_ref[...]).astype(BF16)
    o_ref[...] = jnp.dot(a, w_ref[...].astype(BF16), preferred_element_type=F32) + b_ref[...]


def _adaln(c_all, w_mod, b_mod):
    depth, d, nm = w_mod.shape
    n = c_all.shape[0]
    return pl.pallas_call(
        _adaln_kernel,
        grid=(depth, nm // d),
        in_specs=[
            pl.BlockSpec((n, d), lambda l, j: (0, 0)),
            pl.BlockSpec((None, d, d), lambda l, j: (l, 0, j)),
            pl.BlockSpec((None, 1, d), lambda l, j: (l, 0, j)),
        ],
        out_specs=pl.BlockSpec((None, n, d), lambda l, j: (l, 0, j)),
        out_shape=jax.ShapeDtypeStruct((depth, n, nm), F32),
        compiler_params=_cparams("parallel", "parallel"),
        name="adaln",
    )(c_all, w_mod, b_mod.reshape(depth, 1, nm))


class _Stream:
    def __init__(self, n_rows, tm, tiles_per_seq, mod):
        self.n_rows = n_rows
        self.tm = tm
        self.n_tiles = n_rows // tm
        self.tiles_per_seq = tiles_per_seq
        self.mod = mod

    def mod_spec(self, j):
        mr = self.mod.shape[1]
        if self.tiles_per_seq is None:
            return pl.BlockSpec((None, mr, D_MODEL), lambda i: (0, 0, j))
        tps = self.tiles_per_seq
        return pl.BlockSpec((None, mr, D_MODEL), lambda i: (i // tps, 0, j))

    def row_spec(self, width, col=0):
        return pl.BlockSpec((self.tm, width), lambda i: (i, col))


def _const_spec(shape):
    nd = len(shape)
    return pl.BlockSpec(shape, lambda i: (0,) * nd)


def _layer_spec(shape, layer):
    nd = len(shape)
    return pl.BlockSpec((None,) + shape, lambda i: (layer,) + (0,) * nd)


def _pre_conv_kernel(x_ref, sh_ref, sc_ref, g_ref, w1_ref, b1_ref, u_ref, w1b_ref):
    @pl.when(pl.program_id(0) == 0)
    def _():
        w1b_ref[...] = w1_ref[...].astype(BF16)

    h = _rms_mod(x_ref[...], g_ref[...], sc_ref[...], sh_ref[...])
    ag = jnp.dot(h.astype(BF16), w1b_ref[...], preferred_element_type=F32) + b1_ref[...]
    u_ref[...] = ag[:, :D_MODEL] * jax.nn.sigmoid(ag[:, D_MODEL:])


def _pre_conv(st, x, norm_g, w1, b1, layer, j):
    c2 = w1.shape[-1]
    return pl.pallas_call(
        _pre_conv_kernel,
        grid=(st.n_tiles,),
        in_specs=[
            st.row_spec(D_MODEL),
            st.mod_spec(0), st.mod_spec(1),
            _layer_spec((1, D_MODEL), layer),
            _layer_spec((D_MODEL, c2), j),
            _layer_spec((1, c2), j),
        ],
        out_specs=st.row_spec(D_MODEL),
        out_shape=jax.ShapeDtypeStruct((st.n_rows, D_MODEL), F32),
        scratch_shapes=[pltpu.VMEM((D_MODEL, c2), BF16)],
        compiler_params=_cparams("arbitrary"),
        name="pre_conv",
    )(x, st.mod, st.mod, norm_g.reshape(-1, 1, D_MODEL), w1, b1.reshape(-1, 1, c2))


def _conv_sample_kernel(buf_ref, u_ref, dw_ref, dwb_ref, z_ref, ns_ref):
    for t in range(DEC_SEQ):
        acc = jnp.broadcast_to(dwb_ref[...], u_ref.shape[1:])
        for j in range(t, CONV_STATE):
            acc = acc + buf_ref[j] * dw_ref[j - t:j - t + 1, :]
        for s in range(t + 1):
            k = CONV_STATE + s - t
            acc = acc + u_ref[s] * dw_ref[k:k + 1, :]
        z_ref[t] = acc
    ns_ref[0:CONV_STATE - DEC_SEQ] = buf_ref[DEC_SEQ:CONV_STATE]
    ns_ref[CONV_STATE - DEC_SEQ:CONV_STATE] = u_ref[...]


def _conv_sample(buf_tm, u_tm, dw, dwb, j):
    sb = 32
    c = u_tm.shape[-1]
    return pl.pallas_call(
        _conv_sample_kernel,
        grid=(DEC_BATCH // sb,),
        in_specs=[
            pl.BlockSpec((CONV_STATE, sb, c), lambda i: (0, i, 0)),
            pl.BlockSpec((DEC_SEQ, sb, c), lambda i: (0, i, 0)),
            _layer_spec((CONV_WIDTH, c), j),
            _layer_spec((1, c), j),
        ],
        out_specs=[
            pl.BlockSpec((DEC_SEQ, sb, c), lambda i: (0, i, 0)),
            pl.BlockSpec((CONV_STATE, sb, c), lambda i: (0, i, 0)),
        ],
        out_shape=[
            jax.ShapeDtypeStruct((DEC_SEQ, DEC_BATCH, c), F32),
            jax.ShapeDtypeStruct((CONV_STATE, DEC_BATCH, c), F32),
        ],
        compiler_params=_cparams("parallel"),
        name="conv_sample",
    )(buf_tm, u_tm, dw, dwb.reshape(-1, 1, c))


def _route_topk(h2, rwt_ref, rb_ref, te_ref, tw_ref):
    h_hi, h_lo = _split_bf16(h2)
    w_hi, w_lo = _split_bf16(rwt_ref[...])
    nt = (((1,), (1,)), ((), ()))
    logits = (lax.dot_general(w_hi, h_hi, nt, preferred_element_type=F32)
              + lax.dot_general(w_hi, h_lo, nt, preferred_element_type=F32)
              + lax.dot_general(w_lo, h_hi, nt, preferred_element_type=F32)
              + rb_ref[...])
    iota_e = lax.broadcasted_iota(I32, logits.shape, 0).astype(F32)
    vals, idxs = [], []
    for _ in range(TOP_K):
        m = jnp.max(logits, axis=0, keepdims=True)
        idx = jnp.min(jnp.where(logits == m, iota_e, float(N_EXPERTS)), axis=0, keepdims=True)
        logits = jnp.where(iota_e == idx, -jnp.inf, logits)
        vals.append(m)
        idxs.append(idx)
    es = [jnp.exp(v - vals[0]) for v in vals]
    tot = es[0] + es[1] + es[2] + es[3]
    for k in range(TOP_K):
        te_ref[k:k + 1, :] = idxs[k].astype(I32)
        tw_ref[k:k + 1, :] = es[k] / tot


def _mixer_post_kernel(mode, tiles_per_seq, *refs):
    it = iter(refs)
    if mode == "conv":
        u_ref, halo_ref, dw_ref, dwb_ref = next(it), next(it), next(it), next(it)
    elif mode == "z":
        z_in_ref = next(it)
    else:
        gated_ref = next(it)
    if mode in ("conv", "z"):
        lng_ref, lnb_ref = next(it), next(it)
    w_ref, b_ref = next(it), next(it)
    x_ref, gm_ref, shf_ref, scf_ref, gff_ref, rwt_ref, rb_ref = (next(it) for _ in range(7))
    x1_ref, h2_ref, te_ref, tw_ref = (next(it) for _ in range(4))
    wb_ref = next(it)
    if mode == "conv":
        ext_ref, z_ref = next(it), next(it)

    i = pl.program_id(0)

    @pl.when(i == 0)
    def _():
        wb_ref[...] = w_ref[...].astype(BF16)

    if mode == "conv":
        tm = u_ref.shape[0]
        first = (i % tiles_per_seq) == 0
        ext_ref[0:HALO, :] = jnp.where(first, 0.0, halo_ref[...])
        ext_ref[HALO:HALO + tm, :] = u_ref[...]
        off = HALO - CONV_STATE

        def row_block(r, carry):
            r0 = pl.multiple_of(r * CONV_RB, CONV_RB)
            for lc in range(D_MODEL // CONV_LW):
                ls = slice(lc * CONV_LW, (lc + 1) * CONV_LW)
                win = ext_ref[pl.ds(r0, CONV_RB + HALO), ls]
                acc = jnp.broadcast_to(dwb_ref[:, ls], (CONV_RB, CONV_LW))
                for s in range(8):
                    ws = win if s == 0 else pltpu.roll(win, CONV_RB + HALO - s, 0)
                    for k in range(CONV_WIDTH):
                        if (off + k) % 8 == s:
                            a = off + k - s
                            acc = acc + ws[a:a + CONV_RB, :] * dw_ref[k:k + 1, ls]
                z_ref[pl.ds(r0, CONV_RB), ls] = acc
            return carry

        lax.fori_loop(0, tm // CONV_RB, row_block, 0)
        z = z_ref[...]
    elif mode == "z":
        z = z_in_ref[...]

    if mode in ("conv", "z"):
        mu = jnp.mean(z, axis=-1, keepdims=True)
        zc = z - mu
        var = jnp.mean(zc * zc, axis=-1, keepdims=True)
        zn = zc * lax.rsqrt(var + NORM_EPS) * lng_ref[...] + lnb_ref[...]
        a = _silu(zn).astype(BF16)
        y = jnp.dot(a, wb_ref[...], preferred_element_type=F32) + b_ref[...]
    else:
        y = jnp.dot(gated_ref[...], wb_ref[...], preferred_element_type=F32)

    x1 = x_ref[...] + gm_ref[...] * y
    x1_ref[...] = x1
    h2 = _rms_mod(x1, gff_ref[...], scf_ref[...], shf_ref[...])
    h2_ref[...] = h2
    _route_topk(h2, rwt_ref, rb_ref, te_ref, tw_ref)


def _mixer_post(mode, st, layer, j, x, mix_in, norm_ff_g, router_wt, router_b, w, b=None,
                dw=None, dwb=None, ln_g=None, ln_b=None):
    tm = st.tm
    k_in = w.shape[-2]
    args, specs = [], []
    if mode == "conv":
        per32 = tm // HALO
        args += [mix_in, mix_in, dw, dwb.reshape(-1, 1, D_MODEL)]
        specs += [st.row_spec(D_MODEL),
                  pl.BlockSpec((HALO, D_MODEL), lambda i: (jnp.maximum(i * per32 - 1, 0), 0)),
                  _layer_spec((CONV_WIDTH, D_MODEL), j), _layer_spec((1, D_MODEL), j)]
    elif mode == "z":
        args += [mix_in]
        specs += [st.row_spec(D_MODEL)]
    else:
        args += [mix_in]
        specs += [st.row_spec(k_in)]
    if mode in ("conv", "z"):
        args += [ln_g.reshape(-1, 1, D_MODEL), ln_b.reshape(-1, 1, D_MODEL)]
        specs += [_layer_spec((1, D_MODEL), j), _layer_spec((1, D_MODEL), j)]
    if b is None:
        b = jnp.zeros((w.shape[0], D_MODEL), F32)
    args += [w, b.reshape(-1, 1, D_MODEL)]
    specs += [_layer_spec((k_in, D_MODEL), j), _layer_spec((1, D_MODEL), j)]
    args += [x, st.mod, st.mod, st.mod, norm_ff_g.reshape(-1, 1, D_MODEL), router_wt,
             router_b.reshape(-1, N_EXPERTS, 1)]
    specs += [st.row_spec(D_MODEL), st.mod_spec(2), st.mod_spec(3), st.mod_spec(4),
              _layer_spec((1, D_MODEL), layer), _layer_spec((N_EXPERTS, D_MODEL), layer),
              _layer_spec((N_EXPERTS, 1), layer)]
    scratch = [pltpu.VMEM((k_in, D_MODEL), BF16)]
    if mode == "conv":
        scratch += [pltpu.VMEM((tm + HALO, D_MODEL), F32), pltpu.VMEM((tm, D_MODEL), F32)]
    return pl.pallas_call(
        functools.partial(_mixer_post_kernel, mode, st.tiles_per_seq),
        grid=(st.n_tiles,),
        in_specs=specs,
        out_specs=[st.row_spec(D_MODEL), st.row_spec(D_MODEL),
                   pl.BlockSpec((TOP_K, tm), lambda i: (0, i)),
                   pl.BlockSpec((TOP_K, tm), lambda i: (0, i))],
        out_shape=[jax.ShapeDtypeStruct((st.n_rows, D_MODEL), F32),
                   jax.ShapeDtypeStruct((st.n_rows, D_MODEL), F32),
                   jax.ShapeDtypeStruct((TOP_K, st.n_rows), I32),
                   jax.ShapeDtypeStruct((TOP_K, st.n_rows), F32)],
        scratch_shapes=scratch,
        compiler_params=_cparams("arbitrary"),
        name="mixer_post_" + mode,
    )(*args)


def _pre_ret_kernel(x_ref, sh_ref, sc_ref, g_ref, w_ref, cos_ref, sin_ref,
                    q_ref, k_ref, v_ref, sg_ref):
    h = _rms_mod(x_ref[...], g_ref[...], sc_ref[...], sh_ref[...]).astype(BF16)
    cos = cos_ref[...]
    sin = sin_ref[...]
    half = RET_DK // 2
    qk = RET_HEADS * RET_DK
    vd = RET_HEADS * RET_DV
    for out_ref, base, scale in ((q_ref, 0, 1.0), (k_ref, qk, RET_DK ** -0.5)):
        for hh in range(RET_HEADS):
            c0 = base + hh * RET_DK
            p = jnp.dot(h, w_ref[:, c0:c0 + RET_DK], preferred_element_type=F32)
            p1, p2 = p[:, :half], p[:, half:]
            o0 = hh * RET_DK
            out_ref[:, o0:o0 + half] = ((p1 * cos - p2 * sin) * scale).astype(BF16)
            out_ref[:, o0 + half:o0 + RET_DK] = ((p1 * sin + p2 * cos) * scale).astype(BF16)
    for hh in range(RET_HEADS):
        c0 = 2 * qk + hh * RET_DV
        v_ref[:, hh * RET_DV:(hh + 1) * RET_DV] = jnp.dot(
            h, w_ref[:, c0:c0 + RET_DV], preferred_element_type=F32).astype(BF16)
        g = jnp.dot(h, w_ref[:, vd + c0:vd + c0 + RET_DV], preferred_element_type=F32)
        sg_ref[:, hh * RET_DV:(hh + 1) * RET_DV] = _silu(g).astype(BF16)


def _pre_ret(st, x, norm_g, w_in_b, layer, j, cos, sin, tbl_spec):
    qk = RET_HEADS * RET_DK
    vd = RET_HEADS * RET_DV
    return pl.pallas_call(
        _pre_ret_kernel,
        grid=(st.n_tiles,),
        in_specs=[
            st.row_spec(D_MODEL),
            st.mod_spec(0), st.mod_spec(1),
            _layer_spec((1, D_MODEL), layer),
            _layer_spec((D_MODEL, 2 * qk + 2 * vd), j),
            tbl_spec, tbl_spec,
        ],
        out_specs=[st.row_spec(qk), st.row_spec(qk), st.row_spec(vd), st.row_spec(vd)],
        out_shape=[jax.ShapeDtypeStruct((st.n_rows, qk), BF16),
                   jax.ShapeDtypeStruct((st.n_rows, qk), BF16),
                   jax.ShapeDtypeStruct((st.n_rows, vd), BF16),
                   jax.ShapeDtypeStruct((st.n_rows, vd), BF16)],
        compiler_params=_cparams("parallel"),
        name="pre_ret",
    )(x, st.mod, st.mod, norm_g.reshape(-1, 1, D_MODEL), w_in_b, cos, sin)


def _decay_matrix(lg, ch):
    diff = (lax.broadcasted_iota(I32, (ch, ch), 0)
            - lax.broadcasted_iota(I32, (ch, ch), 1)).astype(F32)
    return jnp.where(diff >= 0.0, jnp.exp(lg * jnp.maximum(diff, 0.0)), 0.0)


def _retention_chunk(valid, lg, dec, q, k, v, sg, s_prev):
    ch = q.shape[0]
    idx = lax.broadcasted_iota(I32, (ch, 1), 0).astype(F32)
    scores = lax.dot_general(q, k, (((1,), (1,)), ((), ())), preferred_element_type=F32) * dec
    o = jnp.dot(scores.astype(BF16), v, preferred_element_type=F32)
    cross = jnp.exp(lg * (idx + 1.0))
    o = o + jnp.dot(q, s_prev.astype(BF16), preferred_element_type=F32) * cross
    kd = (k.astype(F32) * jnp.exp(lg * (valid - 1.0 - idx))).astype(BF16)
    s_new = jnp.exp(lg * float(valid)) * s_prev + lax.dot_general(
        kd, v, (((0,), (0,)), ((), ())), preferred_element_type=F32)
    mu = jnp.mean(o, axis=-1, keepdims=True)
    oc = o - mu
    var = jnp.mean(oc * oc, axis=-1, keepdims=True)
    on = oc * lax.rsqrt(var + NORM_EPS)
    return (sg.astype(F32) * on).astype(BF16), s_new


def _retention_kernel(valid, lg_ref, q_ref, k_ref, v_ref, sg_ref, s0_ref,
                      o_ref, s_out_ref, s_ref, dec_ref):
    c = pl.program_id(2)
    lg = lg_ref[0:1, 0:1]

    @pl.when(c == 0)
    def _():
        s_ref[...] = s0_ref[...]
        dec_ref[...] = _decay_matrix(lg, q_ref.shape[0])

    gated, s_new = _retention_chunk(valid, lg, dec_ref[...], q_ref[...], k_ref[...], v_ref[...],
                                    sg_ref[...], s_ref[...])
    o_ref[...] = gated
    s_ref[...] = s_new

    @pl.when(c == pl.num_programs(2) - 1)
    def _():
        s_out_ref[...] = s_new


def _retention_sample_kernel(lg_ref, q_ref, k_ref, v_ref, sg_ref, s0_ref, o_ref, s_out_ref):
    lg = lg_ref[0:1, 0:1]
    dec = _decay_matrix(lg, RET_PAD)
    for b in range(RET_SB):
        rows = slice(b * RET_PAD, (b + 1) * RET_PAD)
        gated, s_new = _retention_chunk(DEC_SEQ, lg, dec, q_ref[rows, :], k_ref[rows, :],
                                        v_ref[rows, :], sg_ref[rows, :], s0_ref[b])
        o_ref[rows, :] = gated
        s_out_ref[b] = s_new


def _retention_sample(q, k, v, sg, s0, s0_layer, lg_tbl):
    rows = RET_SB * RET_PAD
    row = lambda i, h: (i, h)
    state_block = (None, RET_SB, None, RET_DK, RET_DV)
    return pl.pallas_call(
        _retention_sample_kernel,
        grid=(DEC_BATCH // RET_SB, RET_HEADS),
        in_specs=[
            pl.BlockSpec((None, 8, 128), lambda i, h: (h, 0, 0)),
            pl.BlockSpec((rows, RET_DK), row), pl.BlockSpec((rows, RET_DK), row),
            pl.BlockSpec((rows, RET_DV), row), pl.BlockSpec((rows, RET_DV), row),
            pl.BlockSpec(state_block, lambda i, h: (s0_layer, i, h, 0, 0)),
        ],
        out_specs=[pl.BlockSpec((rows, RET_DV), row),
                   pl.BlockSpec(state_block, lambda i, h: (0, i, h, 0, 0))],
        out_shape=[jax.ShapeDtypeStruct((DEC_BATCH * RET_PAD, RET_HEADS * RET_DV), BF16),
                   jax.ShapeDtypeStruct((1, DEC_BATCH, RET_HEADS, RET_DK, RET_DV), F32)],
        compiler_params=_cparams("parallel", "parallel"),
        name="retention_sample",
    )(lg_tbl, q, k, v, sg, s0)


def _retention(q, k, v, sg, s0, s0_layer, lg_tbl, n_seq, n_chunks, ch, valid):
    n_rows = q.shape[0]
    row = lambda b, h, c: (b * n_chunks + c, h)
    state_spec = pl.BlockSpec((None, None, None, RET_DK, RET_DV),
                              lambda b, h, c: (s0_layer, b, h, 0, 0))
    out_state_spec = pl.BlockSpec((None, None, None, RET_DK, RET_DV),
                                  lambda b, h, c: (0, b, h, 0, 0))
    return pl.pallas_call(
        functools.partial(_retention_kernel, valid),
        grid=(n_seq, RET_HEADS, n_chunks),
        in_specs=[
            pl.BlockSpec((None, 8, 128), lambda b, h, c: (h, 0, 0)),
            pl.BlockSpec((ch, RET_DK), row), pl.BlockSpec((ch, RET_DK), row),
            pl.BlockSpec((ch, RET_DV), row), pl.BlockSpec((ch, RET_DV), row),
            state_spec,
        ],
        out_specs=[pl.BlockSpec((ch, RET_DV), row), out_state_spec],
        out_shape=[jax.ShapeDtypeStruct((n_rows, RET_HEADS * RET_DV), BF16),
                   jax.ShapeDtypeStruct((1, n_seq, RET_HEADS, RET_DK, RET_DV), F32)],
        scratch_shapes=[pltpu.VMEM((RET_DK, RET_DV), F32), pltpu.VMEM((ch, ch), F32)],
        compiler_params=_cparams("parallel", "parallel", "arbitrary"),
        name="retention",
    )(lg_tbl, q, k, v, sg, s0)


def _excl_prefix(col):
    r = lax.broadcasted_iota(I32, (N_EXPERTS, N_EXPERTS), 0)
    c = lax.broadcasted_iota(I32, (N_EXPERTS, N_EXPERTS), 1)
    lane = jnp.sum(jnp.where(r < c, col, 0.0), axis=0, keepdims=True)
    return jnp.sum(jnp.where(r == c, lane, 0.0), axis=1, keepdims=True)


def _ceil_to(x, m):
    return jnp.floor((x + (m - 1.0)) * (1.0 / m)) * m


def _route_kernel(te_ref, lpos_ref, rstart_ref, rlen_ref, tot_ref, tri_ref):
    rt = te_ref.shape[2]
    iota_e = lax.broadcasted_iota(I32, (N_EXPERTS, rt), 0)
    s = lax.broadcasted_iota(I32, (rt, rt), 0)
    t = lax.broadcasted_iota(I32, (rt, rt), 1)
    tri_ref[...] = (s < t).astype(BF16)

    def tile_onehots(i):
        te = te_ref[i]
        onehots = [(te[k:k + 1, :] == iota_e).astype(F32) for k in range(TOP_K)]
        cnt = jnp.sum(onehots[0] + onehots[1] + onehots[2] + onehots[3], axis=1, keepdims=True)
        return onehots, _ceil_to(cnt, RUN_ALIGN)

    tot = lax.fori_loop(0, MOE_NT, lambda i, acc: acc + tile_onehots(i)[1],
                        jnp.zeros((N_EXPERTS, 1), F32))
    tot_ref[...] = tot

    def place(i, gbase):
        onehots, run = tile_onehots(i)
        base = _excl_prefix(run)
        for k in range(TOP_K):
            oh = onehots[k]
            before = jnp.dot(oh.astype(BF16), tri_ref[...], preferred_element_type=F32)
            pos = jnp.sum(oh * (base + before), axis=0, keepdims=True)
            lpos_ref[i, k:k + 1, :] = pos.astype(I32)
            base = base + jnp.sum(oh, axis=1, keepdims=True)
        rstart_ref[i] = jnp.broadcast_to(gbase, rstart_ref.shape[1:]).astype(I32)
        rlen_ref[i] = jnp.broadcast_to(run, rlen_ref.shape[1:]).astype(I32)
        return gbase + run

    lax.fori_loop(0, MOE_NT, place, _excl_prefix(_ceil_to(tot, MOE_TB)))


def _route(te):
    full = lambda shape: pl.BlockSpec(shape, lambda i: (0,) * len(shape))
    return pl.pallas_call(
        _route_kernel,
        grid=(1,),
        in_specs=[full(te.shape)],
        out_specs=[full(te.shape), full((MOE_NT, N_EXPERTS, 128)), full((MOE_NT, N_EXPERTS, 128)),
                   full((N_EXPERTS, 1))],
        out_shape=[jax.ShapeDtypeStruct(te.shape, I32),
                   jax.ShapeDtypeStruct((MOE_NT, N_EXPERTS, 128), I32),
                   jax.ShapeDtypeStruct((MOE_NT, N_EXPERTS, 128), I32),
                   jax.ShapeDtypeStruct((N_EXPERTS, 1), F32)],
        scratch_shapes=[pltpu.VMEM((MOE_RT, MOE_RT), BF16)],
        compiler_params=_cparams("arbitrary"),
        name="moe_route",
    )(te)


def _dispatch_kernel(n_prompt_tiles, rs_ref, rl_ref, tzs_ref, tzl_ref, nu_ref, hp_ref, hs_ref,
                     lpos_ref, xb_ref, xs_ref, sem, zero_ref):
    i = pl.program_id(0)
    slot = i % 2
    zero_sem = sem.at[2]

    def tail_copy(e):
        n = pl.multiple_of(tzl_ref[e], RUN_ALIGN)
        dst = pl.multiple_of(tzs_ref[e], RUN_ALIGN)
        return pltpu.make_async_copy(zero_ref.at[pl.ds(0, n)], xb_ref.at[pl.ds(dst, n)], zero_sem)

    def block_copy(j):
        dst = pl.multiple_of(j * MOE_TB, MOE_TB)
        return pltpu.make_async_copy(zero_ref, xb_ref.at[pl.ds(dst, MOE_TB)], zero_sem)

    @pl.when(i == 0)
    def _():
        zero_ref[...] = jnp.zeros_like(zero_ref)

        def start(e, carry):
            pl.when(tzl_ref[e] > 0)(lambda: tail_copy(e).start())
            return carry

        def wait(e, carry):
            pl.when(tzl_ref[e] > 0)(lambda: tail_copy(e).wait())
            return carry

        lax.fori_loop(0, N_EXPERTS, start, 0)
        lax.fori_loop(nu_ref[0], MOE_NBLK, lambda j, c: (block_copy(j).start(), c)[1], 0)
        lax.fori_loop(0, N_EXPERTS, wait, 0)
        lax.fori_loop(nu_ref[0], MOE_NBLK, lambda j, c: (block_copy(j).wait(), c)[1], 0)

    rows = lax.broadcasted_iota(I32, (MOE_CAP, MOE_RT), 0)
    perm = jnp.where(rows == lpos_ref[0:1, :], 1.0, 0.0)
    for k in range(1, TOP_K):
        perm = perm + jnp.where(rows == lpos_ref[k:k + 1, :], 1.0, 0.0)
    h = jnp.where(i < n_prompt_tiles, hp_ref[...], hs_ref[...])
    xs_ref[slot] = jnp.dot(perm.astype(BF16), h.astype(BF16), preferred_element_type=F32)

    def scatter(tile, wait):
        def body(e, lo):
            n = pl.multiple_of(rl_ref[tile * N_EXPERTS + e], RUN_ALIGN)
            sl = tile % 2
            src = pl.multiple_of(lo, RUN_ALIGN)
            dst = pl.multiple_of(rs_ref[tile * N_EXPERTS + e], RUN_ALIGN)
            cp = pltpu.make_async_copy(xs_ref.at[sl, pl.ds(src, n)], xb_ref.at[pl.ds(dst, n)],
                                       sem.at[sl])

            @pl.when(n > 0)
            def _():
                if wait:
                    cp.wait()
                else:
                    cp.start()

            return lo + n

        lax.fori_loop(0, N_EXPERTS, body, 0)

    scatter(i, False)

    @pl.when(i > 0)
    def _():
        scatter(i - 1, True)

    @pl.when(i == pl.num_programs(0) - 1)
    def _():
        scatter(i, True)


def _dispatch(tables, h_p, h_s, lpos):
    nt_p = h_p.shape[0] // MOE_RT
    return pl.pallas_call(
        functools.partial(_dispatch_kernel, nt_p),
        grid_spec=pltpu.PrefetchScalarGridSpec(
            num_scalar_prefetch=len(tables),
            grid=(MOE_NT,),
            in_specs=[
                pl.BlockSpec((MOE_RT, D_MODEL), lambda i, *_: (jnp.minimum(i, nt_p - 1), 0)),
                pl.BlockSpec((MOE_RT, D_MODEL), lambda i, *_: (jnp.maximum(i - nt_p, 0), 0)),
                pl.BlockSpec((None, TOP_K, MOE_RT), lambda i, *_: (i, 0, 0)),
            ],
            out_specs=pl.BlockSpec(memory_space=pl.ANY),
            scratch_shapes=[pltpu.VMEM((2, MOE_CAP, D_MODEL), F32),
                            pltpu.SemaphoreType.DMA((3,)),
                            pltpu.VMEM((MOE_TB, D_MODEL), F32)],
        ),
        out_shape=jax.ShapeDtypeStruct((MOE_ROWS, D_MODEL), F32),
        compiler_params=_cparams("arbitrary"),
        name="moe_dispatch",
    )(*tables, h_p, h_s, lpos)


def _ffn_kernel(layer, be_ref, nu_ref, nxt_ref, x_ref, bgu_ref, bdn_ref, wgu_hbm, wdn_hbm, y_ref,
                wgu_f_ref, wdn_f_ref, wgu_b_ref, wdn_b_ref, act_ref, sem):
    j = pl.program_id(0)
    e = be_ref[j]
    e_prev = be_ref[jnp.maximum(j - 1, 0)]
    active = j < nu_ref[0]

    def weight_copies(ex):
        return (pltpu.make_async_copy(wgu_hbm.at[layer, ex], wgu_f_ref, sem.at[0]),
                pltpu.make_async_copy(wdn_hbm.at[layer, ex], wdn_f_ref, sem.at[1]))

    @pl.when(j == 0)
    def _():
        for cp in weight_copies(e):
            cp.start()

    @pl.when(active & ((j == 0) | (e != e_prev)))
    def _():
        for cp in weight_copies(e):
            cp.wait()
        wgu_b_ref[...] = wgu_f_ref[...].astype(BF16)
        wdn_b_ref[...] = wdn_f_ref[...].astype(BF16)
        nx = nxt_ref[j]

        @pl.when(nx >= 0)
        def _():
            for cp in weight_copies(nx):
                cp.start()

    @pl.when(active)
    def _():
        x = x_ref[...].astype(BF16)
        for c in range(D_FF // FF_CHUNK):
            gs = slice(c * FF_CHUNK, (c + 1) * FF_CHUNK)
            us = slice(D_FF + c * FF_CHUNK, D_FF + (c + 1) * FF_CHUNK)
            gate = jnp.dot(x, wgu_b_ref[:, gs], preferred_element_type=F32) + bgu_ref[:, gs]
            up = jnp.dot(x, wgu_b_ref[:, us], preferred_element_type=F32) + bgu_ref[:, us]
            gate = jnp.minimum(gate, SWIGLU_LIMIT)
            up = jnp.clip(up, -SWIGLU_LIMIT, SWIGLU_LIMIT)
            act = (up + 1.0) * (gate * jax.nn.sigmoid(SWIGLU_ALPHA * gate))
            act_ref[:, gs] = act.astype(BF16)
        y_ref[...] = jnp.dot(act_ref[...], wdn_b_ref[...],
                             preferred_element_type=F32) + bdn_ref[...]

    @pl.when(jnp.logical_not(active))
    def _():
        y_ref[...] = jnp.zeros_like(y_ref)


def _ffn(layer, block_e, n_used, nxt, xb, w_gu, b_gu, w_dn, b_dn):
    row_map = lambda j, be, nu, nx: (jnp.minimum(j, nu[0] - 1), 0)
    out_map = lambda j, be, nu, nx: (j, 0)
    exp_map = lambda j, be, nu, nx: (layer, be[j], 0, 0)
    return pl.pallas_call(
        functools.partial(_ffn_kernel, layer),
        grid_spec=pltpu.PrefetchScalarGridSpec(
            num_scalar_prefetch=3,
            grid=(MOE_NBLK,),
            in_specs=[
                pl.BlockSpec((MOE_TB, D_MODEL), row_map),
                pl.BlockSpec((None, None, 1, 2 * D_FF), exp_map),
                pl.BlockSpec((None, None, 1, D_MODEL), exp_map),
                pl.BlockSpec(memory_space=pl.ANY),
                pl.BlockSpec(memory_space=pl.ANY),
            ],
            out_specs=pl.BlockSpec((MOE_TB, D_MODEL), out_map),
            scratch_shapes=[pltpu.VMEM((D_MODEL, 2 * D_FF), F32),
                            pltpu.VMEM((D_FF, D_MODEL), F32),
                            pltpu.VMEM((D_MODEL, 2 * D_FF), BF16),
                            pltpu.VMEM((D_FF, D_MODEL), BF16),
                            pltpu.VMEM((MOE_TB, D_FF), BF16),
                            pltpu.SemaphoreType.DMA((2,))],
        ),
        out_shape=jax.ShapeDtypeStruct((MOE_ROWS, D_MODEL), F32),
        compiler_params=_cparams("arbitrary"),
        name="moe_ffn",
    )(block_e, n_used, nxt, xb, b_gu.reshape(b_gu.shape[0], N_EXPERTS, 1, 2 * D_FF),
      b_dn.reshape(b_dn.shape[0], N_EXPERTS, 1, D_MODEL), w_gu, w_dn)


def _tile_rows(v, n_rows):
    mr = v.shape[0]
    if mr in (1, n_rows):
        return v
    return jnp.concatenate([v] * (n_rows // mr), axis=0)


def _combine_kernel(tile_off, final, rs_ref, rl_ref, x_ref, gf_ref, lpt_ref, wt_ref, fg_ref,
                    yb_ref, out_ref, ys_ref, sem):
    i = pl.program_id(0)

    def gather(tile, wait):
        def body(e, lo):
            n = pl.multiple_of(rl_ref[(tile_off + tile) * N_EXPERTS + e], RUN_ALIGN)
            sl = tile % 2
            src = pl.multiple_of(rs_ref[(tile_off + tile) * N_EXPERTS + e], RUN_ALIGN)
            dst = pl.multiple_of(lo, RUN_ALIGN)
            cp = pltpu.make_async_copy(yb_ref.at[pl.ds(src, n)], ys_ref.at[sl, pl.ds(dst, n)],
                                       sem.at[sl])

            @pl.when(n > 0)
            def _():
                if wait:
                    cp.wait()
                else:
                    cp.start()

            return lo + n

        lax.fori_loop(0, N_EXPERTS, body, 0)

    @pl.when(i == 0)
    def _():
        ys_ref[...] = jnp.zeros_like(ys_ref)
        gather(0, False)

    @pl.when(i + 1 < pl.num_programs(0))
    def _():
        gather(i + 1, False)

    gather(i, True)

    cols = lax.broadcasted_iota(I32, (MOE_RT, MOE_CAP), 1)
    w = jnp.where(cols == lpt_ref[:, 0:1], wt_ref[:, 0:1], 0.0)
    for k in range(1, TOP_K):
        w = w + jnp.where(cols == lpt_ref[:, k:k + 1], wt_ref[:, k:k + 1], 0.0)
    w_hi, w_lo = _split_bf16(w)
    ys = ys_ref[i % 2].astype(BF16)
    y = (jnp.dot(w_hi, ys, preferred_element_type=F32)
         + jnp.dot(w_lo, ys, preferred_element_type=F32))
    xn = x_ref[...] + _tile_rows(gf_ref[...], MOE_RT) * y
    if final:
        xn = xn * lax.rsqrt(jnp.mean(xn * xn, axis=-1, keepdims=True) + NORM_EPS) * fg_ref[...]
    out_ref[...] = xn


def _combine(tile_off, n_tiles, final, tables, x, mod, mod_map, lpos_t, wt, final_g, yb):
    pm = lambda i, *_: (i, 0)
    tm = lambda i, *_: (i + tile_off, 0)
    return pl.pallas_call(
        functools.partial(_combine_kernel, tile_off, final),
        grid_spec=pltpu.PrefetchScalarGridSpec(
            num_scalar_prefetch=len(tables),
            grid=(n_tiles,),
            in_specs=[
                pl.BlockSpec((MOE_RT, D_MODEL), pm),
                pl.BlockSpec((None, mod.shape[1], D_MODEL), mod_map),
                pl.BlockSpec((MOE_RT, TOP_K), tm),
                pl.BlockSpec((MOE_RT, TOP_K), tm),
                pl.BlockSpec((1, D_MODEL), lambda i, *_: (0, 0)),
                pl.BlockSpec(memory_space=pl.ANY),
            ],
            out_specs=pl.BlockSpec((MOE_RT, D_MODEL), pm),
            scratch_shapes=[pltpu.VMEM((2, MOE_CAP, D_MODEL), F32),
                            pltpu.SemaphoreType.DMA((2,))],
        ),
        out_shape=jax.ShapeDtypeStruct(x.shape, F32),
        compiler_params=_cparams("arbitrary"),
        name="moe_combine",
    )(*tables, x, mod, lpos_t, wt, final_g.reshape(1, D_MODEL), yb)


def _moe(layer, final, st_p, st_s, x_p, x_s, post_p, post_s,
         w_gu, b_gu, w_dn, b_dn, final_g):
    te = jnp.concatenate([post_p[1], post_s[1]], axis=1)
    tw = jnp.concatenate([post_p[2], post_s[2]], axis=1)
    te = jnp.swapaxes(te.reshape(TOP_K, MOE_NT, MOE_RT), 0, 1)
    lpos, rstart, rlen, tot = _route(te)
    tot = tot[:, 0].astype(I32)
    padded = (tot + MOE_TB - 1) // MOE_TB * MOE_TB
    pends = jnp.cumsum(padded)
    n_used = (pends[-1] // MOE_TB).reshape(1)
    blk = jnp.minimum(jnp.arange(MOE_NBLK, dtype=I32), n_used[0] - 1) * MOE_TB
    block_e = jnp.minimum(jnp.sum(pends[None, :] <= blk[:, None], axis=-1),
                          N_EXPERTS - 1).astype(I32)
    ids = jnp.where(tot > 0, jnp.arange(N_EXPERTS, dtype=I32), N_EXPERTS)
    later = jnp.concatenate([lax.cummin(ids[::-1])[::-1][1:], jnp.full((1,), N_EXPERTS, I32)])
    nxt = jnp.where(later < N_EXPERTS, later, -1)[block_e]
    run_tables = (rstart[:, :, 0].reshape(-1), rlen[:, :, 0].reshape(-1))
    tail_tables = (pends - padded + tot, padded - tot)

    nt_p = N_PROMPT // MOE_RT
    nt_s = N_SAMPLE // MOE_RT
    xb = _dispatch(run_tables + tail_tables + (n_used,), post_p[0], post_s[0], lpos)
    yb = _ffn(layer, block_e, n_used, nxt, xb, w_gu, b_gu, w_dn, b_dn)
    lpos_t = jnp.swapaxes(lpos, 1, 2).reshape(N_TOK, TOP_K)
    wt = tw.T
    tiles_per_seq = SEQ // MOE_RT
    out_p = _combine(0, nt_p, final, run_tables, x_p, st_p.mod,
                     lambda i, *_: (i // tiles_per_seq, 0, N_MOD - 1), lpos_t, wt, final_g, yb)
    out_s = _combine(nt_p, nt_s, final, run_tables, x_s, st_s.mod,
                     lambda i, *_: (0, 0, N_MOD - 1), lpos_t, wt, final_g, yb)
    return out_p, out_s


def _rope_tables(pos):
    half = RET_DK // 2
    inv_freq = ROPE_BASE ** (-jnp.arange(half, dtype=F32) / half)
    ang = pos.astype(F32)[:, None] * inv_freq[None, :]
    return jnp.cos(ang), jnp.sin(ang)


def kernel(x_prompt, x_sample, c_prompt, c_sample, state_conv, state_ret, norm_mix_g, norm_ff_g,
           w_mod, b_mod, conv_w1, conv_b1, conv_dw, conv_dw_b, conv_ln_g, conv_ln_b, conv_w2,
           conv_b2, ret_w_in, ret_w_o, router_w, router_b, moe_w_gu, moe_b_gu, moe_w_dn,
           moe_b_dn, final_g):
    mod = _adaln(jnp.concatenate([c_prompt, c_sample], axis=0), w_mod, b_mod)
    x_p = x_prompt.reshape(N_PROMPT, D_MODEL)
    x_s = jnp.swapaxes(x_sample, 0, 1).reshape(N_SAMPLE, D_MODEL)
    router_wt = jnp.swapaxes(router_w, 1, 2)

    def streams(layer):
        st_p = _Stream(N_PROMPT, TM_PROMPT, SEQ // TM_PROMPT,
                       mod[layer, :BATCH].reshape(BATCH, 1, N_MOD * D_MODEL))
        st_s = _Stream(N_SAMPLE, TM_SAMPLE, None,
                       mod[layer, BATCH:].reshape(1, DEC_BATCH, N_MOD * D_MODEL))
        return st_p, st_s

    st_p, st_s = streams(0)
    u_p = _pre_conv(st_p, x_p, norm_mix_g, conv_w1, conv_b1, 0, 0)
    u_s = _pre_conv(st_s, x_s, norm_mix_g, conv_w1, conv_b1, 0, 0)
    buf_tm = jnp.swapaxes(state_conv[0], 0, 1)
    z_s, ns_tm = _conv_sample(buf_tm, u_s.reshape(DEC_SEQ, DEC_BATCH, D_MODEL),
                              conv_dw, conv_dw_b, 0)
    post_p = _mixer_post("conv", st_p, 0, 0, x_p, u_p, norm_ff_g, router_wt, router_b,
                         conv_w2, conv_b2, conv_dw, conv_dw_b, conv_ln_g, conv_ln_b)
    post_s = _mixer_post("z", st_s, 0, 0, x_s, z_s.reshape(N_SAMPLE, D_MODEL), norm_ff_g,
                         router_wt, router_b, conv_w2, conv_b2, None, None, conv_ln_g, conv_ln_b)
    x_p, x_s = _moe(0, False, st_p, st_s, post_p[0], post_s[0], post_p[1:], post_s[1:],
                    moe_w_gu, moe_b_gu, moe_w_dn, moe_b_dn, final_g)
    conv_p = u_p.reshape(BATCH, SEQ, D_MODEL)[:, SEQ - CONV_STATE:][None]
    conv_s = jnp.swapaxes(ns_tm, 0, 1)[None]

    st_p, st_s = streams(1)
    w_in_b = ret_w_in.astype(BF16)
    cos_p, sin_p = _rope_tables(jnp.arange(SEQ, dtype=I32))
    cos_s, sin_s = _rope_tables(PAST_LEN + jnp.arange(DEC_SEQ, dtype=I32))
    half = RET_DK // 2
    tps = SEQ // TM_PROMPT
    tbl_p = pl.BlockSpec((TM_PROMPT, half), lambda i: (i % tps, 0))
    tbl_s = pl.BlockSpec((None, 1, half), lambda i: (i, 0, 0))
    q_p, k_p, v_p, sg_p = _pre_ret(st_p, x_p, norm_mix_g, w_in_b, 1, 0, cos_p, sin_p, tbl_p)
    q_s, k_s, v_s, sg_s = _pre_ret(st_s, x_s, norm_mix_g, w_in_b, 1, 0,
                                   cos_s.reshape(DEC_SEQ, 1, half),
                                   sin_s.reshape(DEC_SEQ, 1, half), tbl_s)
    lg = jnp.log1p(-jnp.exp2(-5.0 - jnp.arange(RET_HEADS, dtype=F32)))
    lg_tbl = jnp.broadcast_to(lg[:, None, None], (RET_HEADS, 8, 128))
    zero_state = jnp.zeros((1, BATCH, RET_HEADS, RET_DK, RET_DV), F32)
    gated_p, ret_p = _retention(q_p, k_p, v_p, sg_p, zero_state, 0, lg_tbl,
                                BATCH, SEQ // RET_CHUNK, RET_CHUNK, RET_CHUNK)

    def to_seq_major(a):
        a = jnp.swapaxes(a.reshape(DEC_SEQ, DEC_BATCH, -1), 0, 1)
        a = jnp.pad(a, ((0, 0), (0, RET_PAD - DEC_SEQ), (0, 0)))
        return a.reshape(DEC_BATCH * RET_PAD, -1)

    gated_s, ret_s = _retention_sample(to_seq_major(q_s), to_seq_major(k_s), to_seq_major(v_s),
                                       to_seq_major(sg_s), state_ret, 0, lg_tbl)
    gated_s = jnp.swapaxes(gated_s.reshape(DEC_BATCH, RET_PAD, -1)[:, :DEC_SEQ], 0, 1)
    gated_s = gated_s.reshape(N_SAMPLE, -1)
    post_p = _mixer_post("ret", st_p, 1, 0, x_p, gated_p, norm_ff_g, router_wt, router_b, ret_w_o)
    post_s = _mixer_post("ret", st_s, 1, 0, x_s, gated_s, norm_ff_g, router_wt, router_b, ret_w_o)
    y_p, y_s = _moe(1, True, st_p, st_s, post_p[0], post_s[0], post_p[1:], post_s[1:],
                    moe_w_gu, moe_b_gu, moe_w_dn, moe_b_dn, final_g)

    y_prompt = y_p.reshape(BATCH, SEQ, D_MODEL)
    y_sample = jnp.swapaxes(y_s.reshape(DEC_SEQ, DEC_BATCH, D_MODEL), 0, 1)
    return (y_prompt, y_sample, conv_p, conv_s, ret_p, ret_s)
```

```python
import functools

import jax
import jax.numpy as jnp
from jax import lax
from jax.experimental import pallas as pl
from jax.experimental.pallas import tpu as pltpu

F32 = jnp.float32
BF16 = jnp.bfloat16
I32 = jnp.int32

D_MODEL = 1024
SEQ = 2048
BATCH = 8
DEC_BATCH = 128
DEC_SEQ = 4
PAST_LEN = 16384
CONV_WIDTH = 31
CONV_STATE = CONV_WIDTH - 1
RET_HEADS = 4
RET_DK = 256
RET_DV = 512
ROPE_BASE = 10000.0
N_EXPERTS = 32
TOP_K = 4
D_FF = 1024
SWIGLU_LIMIT = 7.0
SWIGLU_ALPHA = 1.702
N_MOD = 6
NORM_EPS = 1e-5

N_PROMPT = BATCH * SEQ
N_SAMPLE = DEC_BATCH * DEC_SEQ
N_TOK = N_PROMPT + N_SAMPLE
N_ASSIGN = N_TOK * TOP_K

TM_PROMPT = 512
TM_SAMPLE = DEC_BATCH
HALO = 32
CONV_RB = 64
CONV_LW = 128
RET_CHUNK = 256
RET_PAD = 16
RET_SB = 8
MOE_TB = 512
MOE_RT = 256
MOE_NT = N_TOK // MOE_RT
RUN_ALIGN = 8
MOE_CAP = -(-(TOP_K * MOE_RT + N_EXPERTS * (RUN_ALIGN - 1)) // 256) * 256
MOE_MAX_ROWS = N_ASSIGN + MOE_NT * N_EXPERTS * (RUN_ALIGN - 1)
MOE_NBLK = -(-MOE_MAX_ROWS // MOE_TB) + N_EXPERTS
MOE_ROWS = MOE_NBLK * MOE_TB
FF_CHUNK = 256
VMEM_LIMIT = 56 * 1024 * 1024


def _cparams(*sem):
    return pltpu.CompilerParams(dimension_semantics=sem, vmem_limit_bytes=VMEM_LIMIT)


def _silu(x):
    return x * jax.nn.sigmoid(x)


def _rms_mod(x, g, sc, sh):
    y = x * lax.rsqrt(jnp.mean(x * x, axis=-1, keepdims=True) + NORM_EPS) * g
    return y * (1.0 + sc) + sh


def _split_bf16(x):
    hi = x.astype(BF16)
    lo = (x - hi.astype(F32)).astype(BF16)
    return hi, lo


def _adaln_kernel(c_ref, w_ref, b_ref, o_ref):
    a = _silu(c_ref[...]).astype(BF16)
    o_ref[...] = jnp.dot(a, w_ref[...].astype(BF16), preferred_element_type=F32) + b_ref[...]


def _adaln(c_all, w_mod, b_mod):
    depth, d, nm = w_mod.shape
    n = c_all.shape[0]
    return pl.pallas_call(
        _adaln_kernel,
        grid=(depth, nm // d),
        in_specs=[
            pl.BlockSpec((n, d), lambda l, j: (0, 0)),
            pl.BlockSpec((None, d, d), lambda l, j: (l, 0, j)),
            pl.BlockSpec((None, 1, d), lambda l, j: (l, 0, j)),
        ],
        out_specs=pl.BlockSpec((None, n, d), lambda l, j: (l, 0, j)),
        out_shape=jax.ShapeDtypeStruct((depth, n, nm), F32),
        compiler_params=_cparams("parallel", "parallel"),
        name="adaln",
    )(c_all, w_mod, b_mod.reshape(depth, 1, nm))


class _Stream:
    def __init__(self, n_rows, tm, tiles_per_seq, mod):
        self.n_rows = n_rows
        self.tm = tm
        self.n_tiles = n_rows // tm
        self.tiles_per_seq = tiles_per_seq
        self.mod = mod

    def mod_spec(self, j):
        mr = self.mod.shape[1]
        if self.tiles_per_seq is None:
            return pl.BlockSpec((None, mr, D_MODEL), lambda i: (0, 0, j))
        tps = self.tiles_per_seq
        return pl.BlockSpec((None, mr, D_MODEL), lambda i: (i // tps, 0, j))

    def row_spec(self, width, col=0):
        return pl.BlockSpec((self.tm, width), lambda i: (i, col))


def _const_spec(shape):
    nd = len(shape)
    return pl.BlockSpec(shape, lambda i: (0,) * nd)


def _layer_spec(shape, layer):
    nd = len(shape)
    return pl.BlockSpec((None,) + shape, lambda i: (layer,) + (0,) * nd)


def _pre_conv_kernel(x_ref, sh_ref, sc_ref, g_ref, w1_ref, b1_ref, u_ref, w1b_ref):
    @pl.when(pl.program_id(0) == 0)
    def _():
        w1b_ref[...] = w1_ref[...].astype(BF16)

    h = _rms_mod(x_ref[...], g_ref[...], sc_ref[...], sh_ref[...])
    ag = jnp.dot(h.astype(BF16), w1b_ref[...], preferred_element_type=F32) + b1_ref[...]
    u_ref[...] = ag[:, :D_MODEL] * jax.nn.sigmoid(ag[:, D_MODEL:])


def _pre_conv(st, x, norm_g, w1, b1, layer, j):
    c2 = w1.shape[-1]
    return pl.pallas_call(
        _pre_conv_kernel,
        grid=(st.n_tiles,),
        in_specs=[
            st.row_spec(D_MODEL),
            st.mod_spec(0), st.mod_spec(1),
            _layer_spec((1, D_MODEL), layer),
            _layer_spec((D_MODEL, c2), j),
            _layer_spec((1, c2), j),
        ],
        out_specs=st.row_spec(D_MODEL),
        out_shape=jax.ShapeDtypeStruct((st.n_rows, D_MODEL), F32),
        scratch_shapes=[pltpu.VMEM((D_MODEL, c2), BF16)],
        compiler_params=_cparams("arbitrary"),
        name="pre_conv",
    )(x, st.mod, st.mod, norm_g.reshape(-1, 1, D_MODEL), w1, b1.reshape(-1, 1, c2))


def _conv_sample_kernel(buf_ref, u_ref, dw_ref, dwb_ref, z_ref, ns_ref):
    for t in range(DEC_SEQ):
        acc = jnp.broadcast_to(dwb_ref[...], u_ref.shape[1:])
        for j in range(t, CONV_STATE):
            acc = acc + buf_ref[j] * dw_ref[j - t:j - t + 1, :]
        for s in range(t + 1):
            k = CONV_STATE + s - t
            acc = acc + u_ref[s] * dw_ref[k:k + 1, :]
        z_ref[t] = acc
    ns_ref[0:CONV_STATE - DEC_SEQ] = buf_ref[DEC_SEQ:CONV_STATE]
    ns_ref[CONV_STATE - DEC_SEQ:CONV_STATE] = u_ref[...]


def _conv_sample(buf_tm, u_tm, dw, dwb, j):
    sb = 32
    c = u_tm.shape[-1]
    return pl.pallas_call(
        _conv_sample_kernel,
        grid=(DEC_BATCH // sb,),
        in_specs=[
            pl.BlockSpec((CONV_STATE, sb, c), lambda i: (0, i, 0)),
            pl.BlockSpec((DEC_SEQ, sb, c), lambda i: (0, i, 0)),
            _layer_spec((CONV_WIDTH, c), j),
            _layer_spec((1, c), j),
        ],
        out_specs=[
            pl.BlockSpec((DEC_SEQ, sb, c), lambda i: (0, i, 0)),
            pl.BlockSpec((CONV_STATE, sb, c), lambda i: (0, i, 0)),
        ],
        out_shape=[
            jax.ShapeDtypeStruct((DEC_SEQ, DEC_BATCH, c), F32),
            jax.ShapeDtypeStruct((CONV_STATE, DEC_BATCH, c), F32),
        ],
        compiler_params=_cparams("parallel"),
        name="conv_sample",
    )(buf_tm, u_tm, dw, dwb.reshape(-1, 1, c))


def _route_topk(h2, rwt_ref, rb_ref, te_ref, tw_ref):
    h_hi, h_lo = _split_bf16(h2)
    w_hi, w_lo = _split_bf16(rwt_ref[...])
    nt = (((1,), (1,)), ((), ()))
    logits = (lax.dot_general(w_hi, h_hi, nt, preferred_element_type=F32)
              + lax.dot_general(w_hi, h_lo, nt, preferred_element_type=F32)
              + lax.dot_general(w_lo, h_hi, nt, preferred_element_type=F32)
              + rb_ref[...])
    iota_e = lax.broadcasted_iota(I32, logits.shape, 0).astype(F32)
    vals, idxs = [], []
    for _ in range(TOP_K):
        m = jnp.max(logits, axis=0, keepdims=True)
        idx = jnp.min(jnp.where(logits == m, iota_e, float(N_EXPERTS)), axis=0, keepdims=True)
        logits = jnp.where(iota_e == idx, -jnp.inf, logits)
        vals.append(m)
        idxs.append(idx)
    es = [jnp.exp(v - vals[0]) for v in vals]
    tot = es[0] + es[1] + es[2] + es[3]
    for k in range(TOP_K):
        te_ref[k:k + 1, :] = idxs[k].astype(I32)
        tw_ref[k:k + 1, :] = es[k] / tot


def _mixer_post_kernel(mode, tiles_per_seq, *refs):
    it = iter(refs)
    if mode == "conv":
        u_ref, halo_ref, dw_ref, dwb_ref = next(it), next(it), next(it), next(it)
    elif mode == "z":
        z_in_ref = next(it)
    else:
        gated_ref = next(it)
    if mode in ("conv", "z"):
        lng_ref, lnb_ref = next(it), next(it)
    w_ref, b_ref = next(it), next(it)
    x_ref, gm_ref, shf_ref, scf_ref, gff_ref, rwt_ref, rb_ref = (next(it) for _ in range(7))
    x1_ref, h2_ref, te_ref, tw_ref = (next(it) for _ in range(4))
    wb_ref = next(it)
    if mode == "conv":
        ext_ref, z_ref = next(it), next(it)

    i = pl.program_id(0)

    @pl.when(i == 0)
    def _():
        wb_ref[...] = w_ref[...].astype(BF16)

    if mode == "conv":
        tm = u_ref.shape[0]
        first = (i % tiles_per_seq) == 0
        ext_ref[0:HALO, :] = jnp.where(first, 0.0, halo_ref[...])
        ext_ref[HALO:HALO + tm, :] = u_ref[...]
        off = HALO - CONV_STATE

        def row_block(r, carry):
            r0 = pl.multiple_of(r * CONV_RB, CONV_RB)
            for lc in range(D_MODEL // CONV_LW):
                ls = slice(lc * CONV_LW, (lc + 1) * CONV_LW)
                win = ext_ref[pl.ds(r0, CONV_RB + HALO), ls]
                acc = jnp.broadcast_to(dwb_ref[:, ls], (CONV_RB, CONV_LW))
                for s in range(8):
                    ws = win if s == 0 else pltpu.roll(win, CONV_RB + HALO - s, 0)
                    for k in range(CONV_WIDTH):
                        if (off + k) % 8 == s:
                            a = off + k - s
                            acc = acc + ws[a:a + CONV_RB, :] * dw_ref[k:k + 1, ls]
                z_ref[pl.ds(r0, CONV_RB), ls] = acc
            return carry

        lax.fori_loop(0, tm // CONV_RB, row_block, 0)
        z = z_ref[...]
    elif mode == "z":
        z = z_in_ref[...]

    if mode in ("conv", "z"):
        mu = jnp.mean(z, axis=-1, keepdims=True)
        zc = z - mu
        var = jnp.mean(zc * zc, axis=-1, keepdims=True)
        zn = zc * lax.rsqrt(var + NORM_EPS) * lng_ref[...] + lnb_ref[...]
        a = _silu(zn).astype(BF16)
        y = jnp.dot(a, wb_ref[...], preferred_element_type=F32) + b_ref[...]
    else:
        y = jnp.dot(gated_ref[...], wb_ref[...], preferred_element_type=F32)

    x1 = x_ref[...] + gm_ref[...] * y
    x1_ref[...] = x1
    h2 = _rms_mod(x1, gff_ref[...], scf_ref[...], shf_ref[...])
    h2_ref[...] = h2
    _route_topk(h2, rwt_ref, rb_ref, te_ref, tw_ref)


def _mixer_post(mode, st, layer, j, x, mix_in, norm_ff_g, router_wt, router_b, w, b=None,
                dw=None, dwb=None, ln_g=None, ln_b=None):
    tm = st.tm
    k_in = w.shape[-2]
    args, specs = [], []
    if mode == "conv":
        per32 = tm // HALO
        args += [mix_in, mix_in, dw, dwb.reshape(-1, 1, D_MODEL)]
        specs += [st.row_spec(D_MODEL),
                  pl.BlockSpec((HALO, D_MODEL), lambda i: (jnp.maximum(i * per32 - 1, 0), 0)),
                  _layer_spec((CONV_WIDTH, D_MODEL), j), _layer_spec((1, D_MODEL), j)]
    elif mode == "z":
        args += [mix_in]
        specs += [st.row_spec(D_MODEL)]
    else:
        args += [mix_in]
        specs += [st.row_spec(k_in)]
    if mode in ("conv", "z"):
        args += [ln_g.reshape(-1, 1, D_MODEL), ln_b.reshape(-1, 1, D_MODEL)]
        specs += [_layer_spec((1, D_MODEL), j), _layer_spec((1, D_MODEL), j)]
    if b is None:
        b = jnp.zeros((w.shape[0], D_MODEL), F32)
    args += [w, b.reshape(-1, 1, D_MODEL)]
    specs += [_layer_spec((k_in, D_MODEL), j), _layer_spec((1, D_MODEL), j)]
    args += [x, st.mod, st.mod, st.mod, norm_ff_g.reshape(-1, 1, D_MODEL), router_wt,
             router_b.reshape(-1, N_EXPERTS, 1)]
    specs += [st.row_spec(D_MODEL), st.mod_spec(2), st.mod_spec(3), st.mod_spec(4),
              _layer_spec((1, D_MODEL), layer), _layer_spec((N_EXPERTS, D_MODEL), layer),
              _layer_spec((N_EXPERTS, 1), layer)]
    scratch = [pltpu.VMEM((k_in, D_MODEL), BF16)]
    if mode == "conv":
        scratch += [pltpu.VMEM((tm + HALO, D_MODEL), F32), pltpu.VMEM((tm, D_MODEL), F32)]
    return pl.pallas_call(
        functools.partial(_mixer_post_kernel, mode, st.tiles_per_seq),
        grid=(st.n_tiles,),
        in_specs=specs,
        out_specs=[st.row_spec(D_MODEL), st.row_spec(D_MODEL),
                   pl.BlockSpec((TOP_K, tm), lambda i: (0, i)),
                   pl.BlockSpec((TOP_K, tm), lambda i: (0, i))],
        out_shape=[jax.ShapeDtypeStruct((st.n_rows, D_MODEL), F32),
                   jax.ShapeDtypeStruct((st.n_rows, D_MODEL), F32),
                   jax.ShapeDtypeStruct((TOP_K, st.n_rows), I32),
                   jax.ShapeDtypeStruct((TOP_K, st.n_rows), F32)],
        scratch_shapes=scratch,
        compiler_params=_cparams("arbitrary"),
        name="mixer_post_" + mode,
    )(*args)


def _pre_ret_kernel(x_ref, sh_ref, sc_ref, g_ref, w_ref, cos_ref, sin_ref,
                    q_ref, k_ref, v_ref, sg_ref):
    h = _rms_mod(x_ref[...], g_ref[...], sc_ref[...], sh_ref[...]).astype(BF16)
    cos = cos_ref[...]
    sin = sin_ref[...]
    half = RET_DK // 2
    qk = RET_HEADS * RET_DK
    vd = RET_HEADS * RET_DV
    for out_ref, base, scale in ((q_ref, 0, 1.0), (k_ref, qk, RET_DK ** -0.5)):
        for hh in range(RET_HEADS):
            c0 = base + hh * RET_DK
            p = jnp.dot(h, w_ref[:, c0:c0 + RET_DK], preferred_element_type=F32)
            p1, p2 = p[:, :half], p[:, half:]
            o0 = hh * RET_DK
            out_ref[:, o0:o0 + half] = ((p1 * cos - p2 * sin) * scale).astype(BF16)
            out_ref[:, o0 + half:o0 + RET_DK] = ((p1 * sin + p2 * cos) * scale).astype(BF16)
    for hh in range(RET_HEADS):
        c0 = 2 * qk + hh * RET_DV
        v_ref[:, hh * RET_DV:(hh + 1) * RET_DV] = jnp.dot(
            h, w_ref[:, c0:c0 + RET_DV], preferred_element_type=F32).astype(BF16)
        g = jnp.dot(h, w_ref[:, vd + c0:vd + c0 + RET_DV], preferred_element_type=F32)
        sg_ref[:, hh * RET_DV:(hh + 1) * RET_DV] = _silu(g).astype(BF16)


def _pre_ret(st, x, norm_g, w_in_b, layer, j, cos, sin, tbl_spec):
    qk = RET_HEADS * RET_DK
    vd = RET_HEADS * RET_DV
    return pl.pallas_call(
        _pre_ret_kernel,
        grid=(st.n_tiles,),
        in_specs=[
            st.row_spec(D_MODEL),
            st.mod_spec(0), st.mod_spec(1),
            _layer_spec((1, D_MODEL), layer),
            _layer_spec((D_MODEL, 2 * qk + 2 * vd), j),
            tbl_spec, tbl_spec,
        ],
        out_specs=[st.row_spec(qk), st.row_spec(qk), st.row_spec(vd), st.row_spec(vd)],
        out_shape=[jax.ShapeDtypeStruct((st.n_rows, qk), BF16),
                   jax.ShapeDtypeStruct((st.n_rows, qk), BF16),
                   jax.ShapeDtypeStruct((st.n_rows, vd), BF16),
                   jax.ShapeDtypeStruct((st.n_rows, vd), BF16)],
        compiler_params=_cparams("parallel"),
        name="pre_ret",
    )(x, st.mod, st.mod, norm_g.reshape(-1, 1, D_MODEL), w_in_b, cos, sin)


def _decay_matrix(lg, ch):
    diff = (lax.broadcasted_iota(I32, (ch, ch), 0)
            - lax.broadcasted_iota(I32, (ch, ch), 1)).astype(F32)
    return jnp.where(diff >= 0.0, jnp.exp(lg * jnp.maximum(diff, 0.0)), 0.0)


def _retention_chunk(valid, lg, dec, q, k, v, sg, s_prev):
    ch = q.shape[0]
    idx = lax.broadcasted_iota(I32, (ch, 1), 0).astype(F32)
    scores = lax.dot_general(q, k, (((1,), (1,)), ((), ())), preferred_element_type=F32) * dec
    o = jnp.dot(scores.astype(BF16), v, preferred_element_type=F32)
    cross = jnp.exp(lg * (idx + 1.0))
    o = o + jnp.dot(q, s_prev.astype(BF16), preferred_element_type=F32) * cross
    kd = (k.astype(F32) * jnp.exp(lg * (valid - 1.0 - idx))).astype(BF16)
    s_new = jnp.exp(lg * float(valid)) * s_prev + lax.dot_general(
        kd, v, (((0,), (0,)), ((), ())), preferred_element_type=F32)
    mu = jnp.mean(o, axis=-1, keepdims=True)
    oc = o - mu
    var = jnp.mean(oc * oc, axis=-1, keepdims=True)
    on = oc * lax.rsqrt(var + NORM_EPS)
    return (sg.astype(F32) * on).astype(BF16), s_new


def _retention_kernel(valid, lg_ref, q_ref, k_ref, v_ref, sg_ref, s0_ref,
                      o_ref, s_out_ref, s_ref, dec_ref):
    c = pl.program_id(1)
    ch = q_ref.shape[0]

    @pl.when(c == 0)
    def _():
        s_ref[...] = s0_ref[...]
        for h in range(RET_HEADS):
            dec_ref[h] = _decay_matrix(lg_ref[h, 0:1, 0:1], ch)

    for h in range(RET_HEADS):
        ks = slice(h * RET_DK, (h + 1) * RET_DK)
        vs = slice(h * RET_DV, (h + 1) * RET_DV)
        gated, s_new = _retention_chunk(valid, lg_ref[h, 0:1, 0:1], dec_ref[h], q_ref[:, ks],
                                        k_ref[:, ks], v_ref[:, vs], sg_ref[:, vs], s_ref[h])
        o_ref[:, vs] = gated
        s_ref[h] = s_new

    @pl.when(c == pl.num_programs(1) - 1)
    def _():
        s_out_ref[...] = s_ref[...]


def _retention_sample_kernel(lg_ref, q_ref, k_ref, v_ref, sg_ref, s0_ref, o_ref, s_out_ref):
    lg = lg_ref[0:1, 0:1]
    dec = _decay_matrix(lg, RET_PAD)
    for b in range(RET_SB):
        rows = slice(b * RET_PAD, (b + 1) * RET_PAD)
        gated, s_new = _retention_chunk(DEC_SEQ, lg, dec, q_ref[rows, :], k_ref[rows, :],
                                        v_ref[rows, :], sg_ref[rows, :], s0_ref[b])
        o_ref[rows, :] = gated
        s_out_ref[b] = s_new


def _retention_sample(q, k, v, sg, s0, s0_layer, lg_tbl):
    rows = RET_SB * RET_PAD
    row = lambda i, h: (i, h)
    state_block = (None, RET_SB, None, RET_DK, RET_DV)
    return pl.pallas_call(
        _retention_sample_kernel,
        grid=(DEC_BATCH // RET_SB, RET_HEADS),
        in_specs=[
            pl.BlockSpec((None, 8, 128), lambda i, h: (h, 0, 0)),
            pl.BlockSpec((rows, RET_DK), row), pl.BlockSpec((rows, RET_DK), row),
            pl.BlockSpec((rows, RET_DV), row), pl.BlockSpec((rows, RET_DV), row),
            pl.BlockSpec(state_block, lambda i, h: (s0_layer, i, h, 0, 0)),
        ],
        out_specs=[pl.BlockSpec((rows, RET_DV), row),
                   pl.BlockSpec(state_block, lambda i, h: (0, i, h, 0, 0))],
        out_shape=[jax.ShapeDtypeStruct((DEC_BATCH * RET_PAD, RET_HEADS * RET_DV), BF16),
                   jax.ShapeDtypeStruct((1, DEC_BATCH, RET_HEADS, RET_DK, RET_DV), F32)],
        compiler_params=_cparams("parallel", "parallel"),
        name="retention_sample",
    )(lg_tbl, q, k, v, sg, s0)


def _retention(q, k, v, sg, s0, s0_layer, lg_tbl, n_seq, n_chunks, ch, valid):
    n_rows = q.shape[0]
    qk = RET_HEADS * RET_DK
    vd = RET_HEADS * RET_DV
    row = lambda b, c: (b * n_chunks + c, 0)
    state_block = (None, None, RET_HEADS, RET_DK, RET_DV)
    return pl.pallas_call(
        functools.partial(_retention_kernel, valid),
        grid=(n_seq, n_chunks),
        in_specs=[
            pl.BlockSpec((RET_HEADS, 8, 128), lambda b, c: (0, 0, 0)),
            pl.BlockSpec((ch, qk), row), pl.BlockSpec((ch, qk), row),
            pl.BlockSpec((ch, vd), row), pl.BlockSpec((ch, vd), row),
            pl.BlockSpec(state_block, lambda b, c: (s0_layer, b, 0, 0, 0)),
        ],
        out_specs=[pl.BlockSpec((ch, vd), row),
                   pl.BlockSpec(state_block, lambda b, c: (0, b, 0, 0, 0))],
        out_shape=[jax.ShapeDtypeStruct((n_rows, vd), BF16),
                   jax.ShapeDtypeStruct((1, n_seq, RET_HEADS, RET_DK, RET_DV), F32)],
        scratch_shapes=[pltpu.VMEM((RET_HEADS, RET_DK, RET_DV), F32),
                        pltpu.VMEM((RET_HEADS, ch, ch), F32)],
        compiler_params=_cparams("parallel", "arbitrary"),
        name="retention",
    )(lg_tbl, q, k, v, sg, s0)


def _excl_prefix(col):
    r = lax.broadcasted_iota(I32, (N_EXPERTS, N_EXPERTS), 0)
    c = lax.broadcasted_iota(I32, (N_EXPERTS, N_EXPERTS), 1)
    lane = jnp.sum(jnp.where(r < c, col, 0.0), axis=0, keepdims=True)
    return jnp.sum(jnp.where(r == c, lane, 0.0), axis=1, keepdims=True)


def _ceil_to(x, m):
    return jnp.floor((x + (m - 1.0)) * (1.0 / m)) * m


def _route_kernel(te_ref, lpos_ref, rstart_ref, rlen_ref, tot_ref, tri_ref):
    rt = te_ref.shape[2]
    iota_e = lax.broadcasted_iota(I32, (N_EXPERTS, rt), 0)
    s = lax.broadcasted_iota(I32, (rt, rt), 0)
    t = lax.broadcasted_iota(I32, (rt, rt), 1)
    tri_ref[...] = (s < t).astype(BF16)

    def tile_onehots(i):
        te = te_ref[i]
        onehots = [(te[k:k + 1, :] == iota_e).astype(F32) for k in range(TOP_K)]
        cnt = jnp.sum(onehots[0] + onehots[1] + onehots[2] + onehots[3], axis=1, keepdims=True)
        return onehots, _ceil_to(cnt, RUN_ALIGN)

    tot = lax.fori_loop(0, MOE_NT, lambda i, acc: acc + tile_onehots(i)[1],
                        jnp.zeros((N_EXPERTS, 1), F32))
    tot_ref[...] = tot

    def place(i, gbase):
        onehots, run = tile_onehots(i)
        base = _excl_prefix(run)
        for k in range(TOP_K):
            oh = onehots[k]
            before = jnp.dot(oh.astype(BF16), tri_ref[...], preferred_element_type=F32)
            pos = jnp.sum(oh * (base + before), axis=0, keepdims=True)
            lpos_ref[i, k:k + 1, :] = pos.astype(I32)
            base = base + jnp.sum(oh, axis=1, keepdims=True)
        rstart_ref[i] = jnp.broadcast_to(gbase, rstart_ref.shape[1:]).astype(I32)
        rlen_ref[i] = jnp.broadcast_to(run, rlen_ref.shape[1:]).astype(I32)
        return gbase + run

    lax.fori_loop(0, MOE_NT, place, _excl_prefix(_ceil_to(tot, MOE_TB)))


def _route(te):
    full = lambda shape: pl.BlockSpec(shape, lambda i: (0,) * len(shape))
    return pl.pallas_call(
        _route_kernel,
        grid=(1,),
        in_specs=[full(te.shape)],
        out_specs=[full(te.shape), full((MOE_NT, N_EXPERTS, 128)), full((MOE_NT, N_EXPERTS, 128)),
                   full((N_EXPERTS, 1))],
        out_shape=[jax.ShapeDtypeStruct(te.shape, I32),
                   jax.ShapeDtypeStruct((MOE_NT, N_EXPERTS, 128), I32),
                   jax.ShapeDtypeStruct((MOE_NT, N_EXPERTS, 128), I32),
                   jax.ShapeDtypeStruct((N_EXPERTS, 1), F32)],
        scratch_shapes=[pltpu.VMEM((MOE_RT, MOE_RT), BF16)],
        compiler_params=_cparams("arbitrary"),
        name="moe_route",
    )(te)


def _dispatch_kernel(n_prompt_tiles, rs_ref, rl_ref, tzs_ref, tzl_ref, nu_ref, hp_ref, hs_ref,
                     lpos_ref, xb_ref, xs_ref, sem, zero_ref):
    i = pl.program_id(0)
    slot = i % 2
    zero_sem = sem.at[2]

    def tail_copy(e):
        n = pl.multiple_of(tzl_ref[e], RUN_ALIGN)
        dst = pl.multiple_of(tzs_ref[e], RUN_ALIGN)
        return pltpu.make_async_copy(zero_ref.at[pl.ds(0, n)], xb_ref.at[pl.ds(dst, n)], zero_sem)

    def block_copy(j):
        dst = pl.multiple_of(j * MOE_TB, MOE_TB)
        return pltpu.make_async_copy(zero_ref, xb_ref.at[pl.ds(dst, MOE_TB)], zero_sem)

    @pl.when(i == 0)
    def _():
        zero_ref[...] = jnp.zeros_like(zero_ref)

        def start(e, carry):
            pl.when(tzl_ref[e] > 0)(lambda: tail_copy(e).start())
            return carry

        def wait(e, carry):
            pl.when(tzl_ref[e] > 0)(lambda: tail_copy(e).wait())
            return carry

        lax.fori_loop(0, N_EXPERTS, start, 0)
        lax.fori_loop(nu_ref[0], MOE_NBLK, lambda j, c: (block_copy(j).start(), c)[1], 0)
        lax.fori_loop(0, N_EXPERTS, wait, 0)
        lax.fori_loop(nu_ref[0], MOE_NBLK, lambda j, c: (block_copy(j).wait(), c)[1], 0)

    rows = lax.broadcasted_iota(I32, (MOE_CAP, MOE_RT), 0)
    perm = jnp.where(rows == lpos_ref[0:1, :], 1.0, 0.0)
    for k in range(1, TOP_K):
        perm = perm + jnp.where(rows == lpos_ref[k:k + 1, :], 1.0, 0.0)
    h = jnp.where(i < n_prompt_tiles, hp_ref[...], hs_ref[...])
    xs_ref[slot] = jnp.dot(perm.astype(BF16), h.astype(BF16), preferred_element_type=F32)

    def scatter(tile, wait):
        def body(e, lo):
            n = pl.multiple_of(rl_ref[tile * N_EXPERTS + e], RUN_ALIGN)
            sl = tile % 2
            src = pl.multiple_of(lo, RUN_ALIGN)
            dst = pl.multiple_of(rs_ref[tile * N_EXPERTS + e], RUN_ALIGN)
            cp = pltpu.make_async_copy(xs_ref.at[sl, pl.ds(src, n)], xb_ref.at[pl.ds(dst, n)],
                                       sem.at[sl])

            @pl.when(n > 0)
            def _():
                if wait:
                    cp.wait()
                else:
                    cp.start()

            return lo + n

        lax.fori_loop(0, N_EXPERTS, body, 0)

    scatter(i, False)

    @pl.when(i > 0)
    def _():
        scatter(i - 1, True)

    @pl.when(i == pl.num_programs(0) - 1)
    def _():
        scatter(i, True)


def _dispatch(tables, h_p, h_s, lpos):
    nt_p = h_p.shape[0] // MOE_RT
    return pl.pallas_call(
        functools.partial(_dispatch_kernel, nt_p),
        grid_spec=pltpu.PrefetchScalarGridSpec(
            num_scalar_prefetch=len(tables),
            grid=(MOE_NT,),
            in_specs=[
                pl.BlockSpec((MOE_RT, D_MODEL), lambda i, *_: (jnp.minimum(i, nt_p - 1), 0)),
                pl.BlockSpec((MOE_RT, D_MODEL), lambda i, *_: (jnp.maximum(i - nt_p, 0), 0)),
                pl.BlockSpec((None, TOP_K, MOE_RT), lambda i, *_: (i, 0, 0)),
            ],
            out_specs=pl.BlockSpec(memory_space=pl.ANY),
            scratch_shapes=[pltpu.VMEM((2, MOE_CAP, D_MODEL), F32),
                            pltpu.SemaphoreType.DMA((3,)),
                            pltpu.VMEM((MOE_TB, D_MODEL), F32)],
        ),
        out_shape=jax.ShapeDtypeStruct((MOE_ROWS, D_MODEL), F32),
        compiler_params=_cparams("arbitrary"),
        name="moe_dispatch",
    )(*tables, h_p, h_s, lpos)


def _ffn_kernel(layer, be_ref, nu_ref, nxt_ref, x_ref, bgu_ref, bdn_ref, wgu_hbm, wdn_hbm, y_ref,
                wgu_f_ref, wdn_f_ref, wgu_b_ref, wdn_b_ref, act_ref, sem):
    j = pl.program_id(0)
    e = be_ref[j]
    e_prev = be_ref[jnp.maximum(j - 1, 0)]
    active = j < nu_ref[0]

    def weight_copies(ex):
        return (pltpu.make_async_copy(wgu_hbm.at[layer, ex], wgu_f_ref, sem.at[0]),
                pltpu.make_async_copy(wdn_hbm.at[layer, ex], wdn_f_ref, sem.at[1]))

    @pl.when(j == 0)
    def _():
        for cp in weight_copies(e):
            cp.start()

    @pl.when(active & ((j == 0) | (e != e_prev)))
    def _():
        for cp in weight_copies(e):
            cp.wait()
        wgu_b_ref[...] = wgu_f_ref[...].astype(BF16)
        wdn_b_ref[...] = wdn_f_ref[...].astype(BF16)
        nx = nxt_ref[j]

        @pl.when(nx >= 0)
        def _():
            for cp in weight_copies(nx):
                cp.start()

    @pl.when(active)
    def _():
        x = x_ref[...].astype(BF16)
        for c in range(D_FF // FF_CHUNK):
            gs = slice(c * FF_CHUNK, (c + 1) * FF_CHUNK)
            us = slice(D_FF + c * FF_CHUNK, D_FF + (c + 1) * FF_CHUNK)
            gate = jnp.dot(x, wgu_b_ref[:, gs], preferred_element_type=F32) + bgu_ref[:, gs]
            up = jnp.dot(x, wgu_b_ref[:, us], preferred_element_type=F32) + bgu_ref[:, us]
            gate = jnp.minimum(gate, SWIGLU_LIMIT)
            up = jnp.clip(up, -SWIGLU_LIMIT, SWIGLU_LIMIT)
            act = (up + 1.0) * (gate * jax.nn.sigmoid(SWIGLU_ALPHA * gate))
            act_ref[:, gs] = act.astype(BF16)
        y_ref[...] = jnp.dot(act_ref[...], wdn_b_ref[...],
                             preferred_element_type=F32) + bdn_ref[...]

    @pl.when(jnp.logical_not(active))
    def _():
        y_ref[...] = jnp.zeros_like(y_ref)


def _ffn(layer, block_e, n_used, nxt, xb, w_gu, b_gu, w_dn, b_dn):
    row_map = lambda j, be, nu, nx: (jnp.minimum(j, nu[0] - 1), 0)
    out_map = lambda j, be, nu, nx: (j, 0)
    exp_map = lambda j, be, nu, nx: (layer, be[j], 0, 0)
    return pl.pallas_call(
        functools.partial(_ffn_kernel, layer),
        grid_spec=pltpu.PrefetchScalarGridSpec(
            num_scalar_prefetch=3,
            grid=(MOE_NBLK,),
            in_specs=[
                pl.BlockSpec((MOE_TB, D_MODEL), row_map),
                pl.BlockSpec((None, None, 1, 2 * D_FF), exp_map),
                pl.BlockSpec((None, None, 1, D_MODEL), exp_map),
                pl.BlockSpec(memory_space=pl.ANY),
                pl.BlockSpec(memory_space=pl.ANY),
            ],
            out_specs=pl.BlockSpec((MOE_TB, D_MODEL), out_map),
            scratch_shapes=[pltpu.VMEM((D_MODEL, 2 * D_FF), F32),
                            pltpu.VMEM((D_FF, D_MODEL), F32),
                            pltpu.VMEM((D_MODEL, 2 * D_FF), BF16),
                            pltpu.VMEM((D_FF, D_MODEL), BF16),
                            pltpu.VMEM((MOE_TB, D_FF), BF16),
                            pltpu.SemaphoreType.DMA((2,))],
        ),
        out_shape=jax.ShapeDtypeStruct((MOE_ROWS, D_MODEL), F32),
        compiler_params=_cparams("arbitrary"),
        name="moe_ffn",
    )(block_e, n_used, nxt, xb, b_gu.reshape(b_gu.shape[0], N_EXPERTS, 1, 2 * D_FF),
      b_dn.reshape(b_dn.shape[0], N_EXPERTS, 1, D_MODEL), w_gu, w_dn)


def _tile_rows(v, n_rows):
    mr = v.shape[0]
    if mr in (1, n_rows):
        return v
    return jnp.concatenate([v] * (n_rows // mr), axis=0)


def _combine_kernel(tile_off, final, rs_ref, rl_ref, x_ref, gf_ref, lpt_ref, wt_ref, fg_ref,
                    yb_ref, out_ref, ys_ref, sem):
    i = pl.program_id(0)

    def gather(tile, wait):
        def body(e, lo):
            n = pl.multiple_of(rl_ref[(tile_off + tile) * N_EXPERTS + e], RUN_ALIGN)
            sl = tile % 2
            src = pl.multiple_of(rs_ref[(tile_off + tile) * N_EXPERTS + e], RUN_ALIGN)
            dst = pl.multiple_of(lo, RUN_ALIGN)
            cp = pltpu.make_async_copy(yb_ref.at[pl.ds(src, n)], ys_ref.at[sl, pl.ds(dst, n)],
                                       sem.at[sl])

            @pl.when(n > 0)
            def _():
                if wait:
                    cp.wait()
                else:
                    cp.start()

            return lo + n

        lax.fori_loop(0, N_EXPERTS, body, 0)

    @pl.when(i == 0)
    def _():
        ys_ref[...] = jnp.zeros_like(ys_ref)
        gather(0, False)

    @pl.when(i + 1 < pl.num_programs(0))
    def _():
        gather(i + 1, False)

    gather(i, True)

    cols = lax.broadcasted_iota(I32, (MOE_RT, MOE_CAP), 1)
    w = jnp.where(cols == lpt_ref[:, 0:1], wt_ref[:, 0:1], 0.0)
    for k in range(1, TOP_K):
        w = w + jnp.where(cols == lpt_ref[:, k:k + 1], wt_ref[:, k:k + 1], 0.0)
    ys = ys_ref[i % 2].astype(BF16)
    y = jnp.dot(w.astype(BF16), ys, preferred_element_type=F32)
    xn = x_ref[...] + _tile_rows(gf_ref[...], MOE_RT) * y
    if final:
        xn = xn * lax.rsqrt(jnp.mean(xn * xn, axis=-1, keepdims=True) + NORM_EPS) * fg_ref[...]
    out_ref[...] = xn


def _combine(tile_off, n_tiles, final, tables, x, mod, mod_map, lpos_t, wt, final_g, yb):
    pm = lambda i, *_: (i, 0)
    tm = lambda i, *_: (i + tile_off, 0)
    return pl.pallas_call(
        functools.partial(_combine_kernel, tile_off, final),
        grid_spec=pltpu.PrefetchScalarGridSpec(
            num_scalar_prefetch=len(tables),
            grid=(n_tiles,),
            in_specs=[
                pl.BlockSpec((MOE_RT, D_MODEL), pm),
                pl.BlockSpec((None, mod.shape[1], D_MODEL), mod_map),
                pl.BlockSpec((MOE_RT, TOP_K), tm),
                pl.BlockSpec((MOE_RT, TOP_K), tm),
                pl.BlockSpec((1, D_MODEL), lambda i, *_: (0, 0)),
                pl.BlockSpec(memory_space=pl.ANY),
            ],
            out_specs=pl.BlockSpec((MOE_RT, D_MODEL), pm),
            scratch_shapes=[pltpu.VMEM((2, MOE_CAP, D_MODEL), F32),
                            pltpu.SemaphoreType.DMA((2,))],
        ),
        out_shape=jax.ShapeDtypeStruct(x.shape, F32),
        compiler_params=_cparams("arbitrary"),
        name="moe_combine",
    )(*tables, x, mod, lpos_t, wt, final_g.reshape(1, D_MODEL), yb)


def _moe(layer, final, st_p, st_s, x_p, x_s, post_p, post_s,
         w_gu, b_gu, w_dn, b_dn, final_g):
    te = jnp.concatenate([post_p[1], post_s[1]], axis=1)
    tw = jnp.concatenate([post_p[2], post_s[2]], axis=1)
    te = jnp.swapaxes(te.reshape(TOP_K, MOE_NT, MOE_RT), 0, 1)
    lpos, rstart, rlen, tot = _route(te)
    tot = tot[:, 0].astype(I32)
    padded = (tot + MOE_TB - 1) // MOE_TB * MOE_TB
    pends = jnp.cumsum(padded)
    n_used = (pends[-1] // MOE_TB).reshape(1)
    blk = jnp.minimum(jnp.arange(MOE_NBLK, dtype=I32), n_used[0] - 1) * MOE_TB
    block_e = jnp.minimum(jnp.sum(pends[None, :] <= blk[:, None], axis=-1),
                          N_EXPERTS - 1).astype(I32)
    ids = jnp.where(tot > 0, jnp.arange(N_EXPERTS, dtype=I32), N_EXPERTS)
    later = jnp.concatenate([lax.cummin(ids[::-1])[::-1][1:], jnp.full((1,), N_EXPERTS, I32)])
    nxt = jnp.where(later < N_EXPERTS, later, -1)[block_e]
    run_tables = (rstart[:, :, 0].reshape(-1), rlen[:, :, 0].reshape(-1))
    tail_tables = (pends - padded + tot, padded - tot)

    nt_p = N_PROMPT // MOE_RT
    nt_s = N_SAMPLE // MOE_RT
    xb = _dispatch(run_tables + tail_tables + (n_used,), post_p[0], post_s[0], lpos)
    yb = _ffn(layer, block_e, n_used, nxt, xb, w_gu, b_gu, w_dn, b_dn)
    lpos_t = jnp.swapaxes(lpos, 1, 2).reshape(N_TOK, TOP_K)
    wt = tw.T
    tiles_per_seq = SEQ // MOE_RT
    out_p = _combine(0, nt_p, final, run_tables, x_p, st_p.mod,
                     lambda i, *_: (i // tiles_per_seq, 0, N_MOD - 1), lpos_t, wt, final_g, yb)
    out_s = _combine(nt_p, nt_s, final, run_tables, x_s, st_s.mod,
                     lambda i, *_: (0, 0, N_MOD - 1), lpos_t, wt, final_g, yb)
    return out_p, out_s


def _rope_tables(pos):
    half = RET_DK // 2
    inv_freq = ROPE_BASE ** (-jnp.arange(half, dtype=F32) / half)
    ang = pos.astype(F32)[:, None] * inv_freq[None, :]
    return jnp.cos(ang), jnp.sin(ang)


def kernel(x_prompt, x_sample, c_prompt, c_sample, state_conv, state_ret, norm_mix_g, norm_ff_g,
           w_mod, b_mod, conv_w1, conv_b1, conv_dw, conv_dw_b, conv_ln_g, conv_ln_b, conv_w2,
           conv_b2, ret_w_in, ret_w_o, router_w, router_b, moe_w_gu, moe_b_gu, moe_w_dn,
           moe_b_dn, final_g):
    mod = _adaln(jnp.concatenate([c_prompt, c_sample], axis=0), w_mod, b_mod)
    x_p = x_prompt.reshape(N_PROMPT, D_MODEL)
    x_s = jnp.swapaxes(x_sample, 0, 1).reshape(N_SAMPLE, D_MODEL)
    router_wt = jnp.swapaxes(router_w, 1, 2)

    def streams(layer):
        st_p = _Stream(N_PROMPT, TM_PROMPT, SEQ // TM_PROMPT,
                       mod[layer, :BATCH].reshape(BATCH, 1, N_MOD * D_MODEL))
        st_s = _Stream(N_SAMPLE, TM_SAMPLE, None,
                       mod[layer, BATCH:].reshape(1, DEC_BATCH, N_MOD * D_MODEL))
        return st_p, st_s

    st_p, st_s = streams(0)
    u_p = _pre_conv(st_p, x_p, norm_mix_g, conv_w1, conv_b1, 0, 0)
    u_s = _pre_conv(st_s, x_s, norm_mix_g, conv_w1, conv_b1, 0, 0)
    buf_tm = jnp.swapaxes(state_conv[0], 0, 1)
    z_s, ns_tm = _conv_sample(buf_tm, u_s.reshape(DEC_SEQ, DEC_BATCH, D_MODEL),
                              conv_dw, conv_dw_b, 0)
    post_p = _mixer_post("conv", st_p, 0, 0, x_p, u_p, norm_ff_g, router_wt, router_b,
                         conv_w2, conv_b2, conv_dw, conv_dw_b, conv_ln_g, conv_ln_b)
    post_s = _mixer_post("z", st_s, 0, 0, x_s, z_s.reshape(N_SAMPLE, D_MODEL), norm_ff_g,
                         router_wt, router_b, conv_w2, conv_b2, None, None, conv_ln_g, conv_ln_b)
    x_p, x_s = _moe(0, False, st_p, st_s, post_p[0], post_s[0], post_p[1:], post_s[1:],
                    moe_w_gu, moe_b_gu, moe_w_dn, moe_b_dn, final_g)
    conv_p = u_p.reshape(BATCH, SEQ, D_MODEL)[:, SEQ - CONV_STATE:][None]
    conv_s = jnp.swapaxes(ns_tm, 0, 1)[None]

    st_p, st_s = streams(1)
    w_in_b = ret_w_in.astype(BF16)
    cos_p, sin_p = _rope_tables(jnp.arange(SEQ, dtype=I32))
    cos_s, sin_s = _rope_tables(PAST_LEN + jnp.arange(DEC_SEQ, dtype=I32))
    half = RET_DK // 2
    tps = SEQ // TM_PROMPT
    tbl_p = pl.BlockSpec((TM_PROMPT, half), lambda i: (i % tps, 0))
    tbl_s = pl.BlockSpec((None, 1, half), lambda i: (i, 0, 0))
    q_p, k_p, v_p, sg_p = _pre_ret(st_p, x_p, norm_mix_g, w_in_b, 1, 0, cos_p, sin_p, tbl_p)
    q_s, k_s, v_s, sg_s = _pre_ret(st_s, x_s, norm_mix_g, w_in_b, 1, 0,
                                   cos_s.reshape(DEC_SEQ, 1, half),
                                   sin_s.reshape(DEC_SEQ, 1, half), tbl_s)
    lg = jnp.log1p(-jnp.exp2(-5.0 - jnp.arange(RET_HEADS, dtype=F32)))
    lg_tbl = jnp.broadcast_to(lg[:, None, None], (RET_HEADS, 8, 128))
    zero_state = jnp.zeros((1, BATCH, RET_HEADS, RET_DK, RET_DV), F32)
    gated_p, ret_p = _retention(q_p, k_p, v_p, sg_p, zero_state, 0, lg_tbl,
                                BATCH, SEQ // RET_CHUNK, RET_CHUNK, RET_CHUNK)

    def to_seq_major(a):
        a = jnp.swapaxes(a.reshape(DEC_SEQ, DEC_BATCH, -1), 0, 1)
        a = jnp.pad(a, ((0, 0), (0, RET_PAD - DEC_SEQ), (0, 0)))
        return a.reshape(DEC_BATCH * RET_PAD, -1)

    gated_s, ret_s = _retention_sample(to_seq_major(q_s), to_seq_major(k_s), to_seq_major(v_s),
                                       to_seq_major(sg_s), state_ret, 0, lg_tbl)
    gated_s = jnp.swapaxes(gated_s.reshape(DEC_BATCH, RET_PAD, -1)[:, :DEC_SEQ], 0, 1)
    gated_s = gated_s.reshape(N_SAMPLE, -1)
    post_p = _mixer_post("ret", st_p, 1, 0, x_p, gated_p, norm_ff_g, router_wt, router_b, ret_w_o)
    post_s = _mixer_post("ret", st_s, 1, 0, x_s, gated_s, norm_ff_g, router_wt, router_b, ret_w_o)
    y_p, y_s = _moe(1, True, st_p, st_s, post_p[0], post_s[0], post_p[1:], post_s[1:],
                    moe_w_gu, moe_b_gu, moe_w_dn, moe_b_dn, final_g)

    y_prompt = y_p.reshape(BATCH, SEQ, D_MODEL)
    y_sample = jnp.swapaxes(y_s.reshape(DEC_SEQ, DEC_BATCH, D_MODEL), 0, 1)
    return (y_prompt, y_sample, conv_p, conv_s, ret_p, ret_s)
```

```python
import functools

import jax
import jax.numpy as jnp
from jax import lax
from jax.experimental import pallas as pl
from jax.experimental.pallas import tpu as pltpu

F32 = jnp.float32
BF16 = jnp.bfloat16
I32 = jnp.int32
U32 = jnp.uint32

D_MODEL = 1024
SEQ = 2048
BATCH = 8
DEC_BATCH = 128
DEC_SEQ = 4
PAST_LEN = 16384
CONV_WIDTH = 31
CONV_STATE = CONV_WIDTH - 1
RET_HEADS = 4
RET_DK = 256
RET_DV = 512
ROPE_BASE = 10000.0
N_EXPERTS = 32
TOP_K = 4
D_FF = 1024
SWIGLU_LIMIT = 7.0
SWIGLU_ALPHA = 1.702
N_MOD = 6
NORM_EPS = 1e-5

N_PROMPT = BATCH * SEQ
N_SAMPLE = DEC_BATCH * DEC_SEQ
N_TOK = N_PROMPT + N_SAMPLE
N_ASSIGN = N_TOK * TOP_K

TM_PROMPT = 512
TM_SAMPLE = DEC_BATCH
HALO = 32
CONV_RB = 64
CONV_LW = 128
RET_CHUNK = 256
RET_PAD = 16
RET_SB = 8
MOE_TB = 512
MOE_RT = 256
MOE_NT = N_TOK // MOE_RT
RUN_ALIGN = 8
MOE_CAP = -(-(TOP_K * MOE_RT + N_EXPERTS * (RUN_ALIGN - 1)) // 256) * 256
MOE_MAX_ROWS = N_ASSIGN + MOE_NT * N_EXPERTS * (RUN_ALIGN - 1)
MOE_NBLK = -(-MOE_MAX_ROWS // MOE_TB) + N_EXPERTS
MOE_ROWS = MOE_NBLK * MOE_TB
FF_CHUNK = 256
VMEM_LIMIT = 56 * 1024 * 1024


def _cparams(*sem):
    return pltpu.CompilerParams(dimension_semantics=sem, vmem_limit_bytes=VMEM_LIMIT)


def _silu(x):
    return x * jax.nn.sigmoid(x)


def _rms_mod(x, g, sc, sh):
    y = x * lax.rsqrt(jnp.mean(x * x, axis=-1, keepdims=True) + NORM_EPS) * g
    return y * (1.0 + sc) + sh


def _split_bf16(x):
    hi = x.astype(BF16)
    lo = (x - hi.astype(F32)).astype(BF16)
    return hi, lo


def _pack_halves(x):
    c = x.shape[1] // 2
    lo = lax.bitcast_convert_type(x[:, :c], U32)
    hi = lax.bitcast_convert_type(x[:, c:], U32)
    return (lo >> 16) | (hi & jnp.uint32(0xFFFF0000))


def _unpack_halves(p):
    lo = lax.bitcast_convert_type(p << 16, F32)
    hi = lax.bitcast_convert_type(p & jnp.uint32(0xFFFF0000), F32)
    return jnp.concatenate([lo.astype(BF16), hi.astype(BF16)], axis=1)


def _adaln_kernel(c_ref, w_ref, b_ref, o_ref):
    a = _silu(c_ref[...]).astype(BF16)
    o_ref[...] = jnp.dot(a, w_ref[...].astype(BF16), preferred_element_type=F32) + b_ref[...]


def _adaln(c_all, w_mod, b_mod):
    depth, d, nm = w_mod.shape
    n = c_all.shape[0]
    return pl.pallas_call(
        _adaln_kernel,
        grid=(depth, nm // d),
        in_specs=[
            pl.BlockSpec((n, d), lambda l, j: (0, 0)),
            pl.BlockSpec((None, d, d), lambda l, j: (l, 0, j)),
            pl.BlockSpec((None, 1, d), lambda l, j: (l, 0, j)),
        ],
        out_specs=pl.BlockSpec((None, n, d), lambda l, j: (l, 0, j)),
        out_shape=jax.ShapeDtypeStruct((depth, n, nm), F32),
        compiler_params=_cparams("parallel", "parallel"),
        name="adaln",
    )(c_all, w_mod, b_mod.reshape(depth, 1, nm))


class _Stream:
    def __init__(self, n_rows, tm, tiles_per_seq, mod):
        self.n_rows = n_rows
        self.tm = tm
        self.n_tiles = n_rows // tm
        self.tiles_per_seq = tiles_per_seq
        self.mod = mod

    def mod_spec(self, j):
        mr = self.mod.shape[1]
        if self.tiles_per_seq is None:
            return pl.BlockSpec((None, mr, D_MODEL), lambda i: (0, 0, j))
        tps = self.tiles_per_seq
        return pl.BlockSpec((None, mr, D_MODEL), lambda i: (i // tps, 0, j))

    def row_spec(self, width, col=0):
        return pl.BlockSpec((self.tm, width), lambda i: (i, col))


def _const_spec(shape):
    nd = len(shape)
    return pl.BlockSpec(shape, lambda i: (0,) * nd)


def _layer_spec(shape, layer):
    nd = len(shape)
    return pl.BlockSpec((None,) + shape, lambda i: (layer,) + (0,) * nd)


def _pre_conv_kernel(x_ref, sh_ref, sc_ref, g_ref, w1_ref, b1_ref, u_ref, w1b_ref):
    @pl.when(pl.program_id(0) == 0)
    def _():
        w1b_ref[...] = w1_ref[...].astype(BF16)

    h = _rms_mod(x_ref[...], g_ref[...], sc_ref[...], sh_ref[...])
    ag = jnp.dot(h.astype(BF16), w1b_ref[...], preferred_element_type=F32) + b1_ref[...]
    u_ref[...] = ag[:, :D_MODEL] * jax.nn.sigmoid(ag[:, D_MODEL:])


def _pre_conv(st, x, norm_g, w1, b1, layer, j):
    c2 = w1.shape[-1]
    return pl.pallas_call(
        _pre_conv_kernel,
        grid=(st.n_tiles,),
        in_specs=[
            st.row_spec(D_MODEL),
            st.mod_spec(0), st.mod_spec(1),
            _layer_spec((1, D_MODEL), layer),
            _layer_spec((D_MODEL, c2), j),
            _layer_spec((1, c2), j),
        ],
        out_specs=st.row_spec(D_MODEL),
        out_shape=jax.ShapeDtypeStruct((st.n_rows, D_MODEL), F32),
        scratch_shapes=[pltpu.VMEM((D_MODEL, c2), BF16)],
        compiler_params=_cparams("arbitrary"),
        name="pre_conv",
    )(x, st.mod, st.mod, norm_g.reshape(-1, 1, D_MODEL), w1, b1.reshape(-1, 1, c2))


def _conv_sample_kernel(buf_ref, u_ref, dw_ref, dwb_ref, z_ref, ns_ref):
    for t in range(DEC_SEQ):
        acc = jnp.broadcast_to(dwb_ref[...], u_ref.shape[1:])
        for j in range(t, CONV_STATE):
            acc = acc + buf_ref[j] * dw_ref[j - t:j - t + 1, :]
        for s in range(t + 1):
            k = CONV_STATE + s - t
            acc = acc + u_ref[s] * dw_ref[k:k + 1, :]
        z_ref[t] = acc
    ns_ref[0:CONV_STATE - DEC_SEQ] = buf_ref[DEC_SEQ:CONV_STATE]
    ns_ref[CONV_STATE - DEC_SEQ:CONV_STATE] = u_ref[...]


def _conv_sample(buf_tm, u_tm, dw, dwb, j):
    sb = 32
    c = u_tm.shape[-1]
    return pl.pallas_call(
        _conv_sample_kernel,
        grid=(DEC_BATCH // sb,),
        in_specs=[
            pl.BlockSpec((CONV_STATE, sb, c), lambda i: (0, i, 0)),
            pl.BlockSpec((DEC_SEQ, sb, c), lambda i: (0, i, 0)),
            _layer_spec((CONV_WIDTH, c), j),
            _layer_spec((1, c), j),
        ],
        out_specs=[
            pl.BlockSpec((DEC_SEQ, sb, c), lambda i: (0, i, 0)),
            pl.BlockSpec((CONV_STATE, sb, c), lambda i: (0, i, 0)),
        ],
        out_shape=[
            jax.ShapeDtypeStruct((DEC_SEQ, DEC_BATCH, c), F32),
            jax.ShapeDtypeStruct((CONV_STATE, DEC_BATCH, c), F32),
        ],
        compiler_params=_cparams("parallel"),
        name="conv_sample",
    )(buf_tm, u_tm, dw, dwb.reshape(-1, 1, c))


def _route_topk(h2, rwt_ref, rb_ref, te_ref, tw_ref):
    h_hi, h_lo = _split_bf16(h2)
    w_hi, w_lo = _split_bf16(rwt_ref[...])
    nt = (((1,), (1,)), ((), ()))
    logits = (lax.dot_general(w_hi, h_hi, nt, preferred_element_type=F32)
              + lax.dot_general(w_hi, h_lo, nt, preferred_element_type=F32)
              + lax.dot_general(w_lo, h_hi, nt, preferred_element_type=F32)
              + rb_ref[...])
    iota_e = lax.broadcasted_iota(I32, logits.shape, 0).astype(F32)
    vals, idxs = [], []
    for _ in range(TOP_K):
        m = jnp.max(logits, axis=0, keepdims=True)
        idx = jnp.min(jnp.where(logits == m, iota_e, float(N_EXPERTS)), axis=0, keepdims=True)
        logits = jnp.where(iota_e == idx, -jnp.inf, logits)
        vals.append(m)
        idxs.append(idx)
    es = [jnp.exp(v - vals[0]) for v in vals]
    tot = es[0] + es[1] + es[2] + es[3]
    for k in range(TOP_K):
        te_ref[k:k + 1, :] = idxs[k].astype(I32)
        tw_ref[k:k + 1, :] = es[k] / tot


def _mixer_post_kernel(mode, tiles_per_seq, *refs):
    it = iter(refs)
    if mode == "conv":
        u_ref, halo_ref, dw_ref, dwb_ref = next(it), next(it), next(it), next(it)
    elif mode == "z":
        z_in_ref = next(it)
    else:
        gated_ref = next(it)
    if mode in ("conv", "z"):
        lng_ref, lnb_ref = next(it), next(it)
    w_ref, b_ref = next(it), next(it)
    x_ref, gm_ref, shf_ref, scf_ref, gff_ref, rwt_ref, rb_ref = (next(it) for _ in range(7))
    x1_ref, h2_ref, te_ref, tw_ref = (next(it) for _ in range(4))
    wb_ref = next(it)
    if mode == "conv":
        ext_ref, z_ref = next(it), next(it)

    i = pl.program_id(0)

    @pl.when(i == 0)
    def _():
        wb_ref[...] = w_ref[...].astype(BF16)

    if mode == "conv":
        tm = u_ref.shape[0]
        first = (i % tiles_per_seq) == 0
        ext_ref[0:HALO, :] = jnp.where(first, 0.0, halo_ref[...])
        ext_ref[HALO:HALO + tm, :] = u_ref[...]
        off = HALO - CONV_STATE

        def row_block(r, carry):
            r0 = pl.multiple_of(r * CONV_RB, CONV_RB)
            for lc in range(D_MODEL // CONV_LW):
                ls = slice(lc * CONV_LW, (lc + 1) * CONV_LW)
                win = ext_ref[pl.ds(r0, CONV_RB + HALO), ls]
                acc = jnp.broadcast_to(dwb_ref[:, ls], (CONV_RB, CONV_LW))
                for s in range(8):
                    ws = win if s == 0 else pltpu.roll(win, CONV_RB + HALO - s, 0)
                    for k in range(CONV_WIDTH):
                        if (off + k) % 8 == s:
                            a = off + k - s
                            acc = acc + ws[a:a + CONV_RB, :] * dw_ref[k:k + 1, ls]
                z_ref[pl.ds(r0, CONV_RB), ls] = acc
            return carry

        lax.fori_loop(0, tm // CONV_RB, row_block, 0)
        z = z_ref[...]
    elif mode == "z":
        z = z_in_ref[...]

    if mode in ("conv", "z"):
        mu = jnp.mean(z, axis=-1, keepdims=True)
        zc = z - mu
        var = jnp.mean(zc * zc, axis=-1, keepdims=True)
        zn = zc * lax.rsqrt(var + NORM_EPS) * lng_ref[...] + lnb_ref[...]
        a = _silu(zn).astype(BF16)
        y = jnp.dot(a, wb_ref[...], preferred_element_type=F32) + b_ref[...]
    else:
        y = jnp.dot(gated_ref[...], wb_ref[...], preferred_element_type=F32)

    x1 = x_ref[...] + gm_ref[...] * y
    x1_ref[...] = x1
    h2 = _rms_mod(x1, gff_ref[...], scf_ref[...], shf_ref[...])
    h2_ref[...] = h2
    _route_topk(h2, rwt_ref, rb_ref, te_ref, tw_ref)


def _mixer_post(mode, st, layer, j, x, mix_in, norm_ff_g, router_wt, router_b, w, b=None,
                dw=None, dwb=None, ln_g=None, ln_b=None):
    tm = st.tm
    k_in = w.shape[-2]
    args, specs = [], []
    if mode == "conv":
        per32 = tm // HALO
        args += [mix_in, mix_in, dw, dwb.reshape(-1, 1, D_MODEL)]
        specs += [st.row_spec(D_MODEL),
                  pl.BlockSpec((HALO, D_MODEL), lambda i: (jnp.maximum(i * per32 - 1, 0), 0)),
                  _layer_spec((CONV_WIDTH, D_MODEL), j), _layer_spec((1, D_MODEL), j)]
    elif mode == "z":
        args += [mix_in]
        specs += [st.row_spec(D_MODEL)]
    else:
        args += [mix_in]
        specs += [st.row_spec(k_in)]
    if mode in ("conv", "z"):
        args += [ln_g.reshape(-1, 1, D_MODEL), ln_b.reshape(-1, 1, D_MODEL)]
        specs += [_layer_spec((1, D_MODEL), j), _layer_spec((1, D_MODEL), j)]
    if b is None:
        b = jnp.zeros((w.shape[0], D_MODEL), F32)
    args += [w, b.reshape(-1, 1, D_MODEL)]
    specs += [_layer_spec((k_in, D_MODEL), j), _layer_spec((1, D_MODEL), j)]
    args += [x, st.mod, st.mod, st.mod, norm_ff_g.reshape(-1, 1, D_MODEL), router_wt,
             router_b.reshape(-1, N_EXPERTS, 1)]
    specs += [st.row_spec(D_MODEL), st.mod_spec(2), st.mod_spec(3), st.mod_spec(4),
              _layer_spec((1, D_MODEL), layer), _layer_spec((N_EXPERTS, D_MODEL), layer),
              _layer_spec((N_EXPERTS, 1), layer)]
    scratch = [pltpu.VMEM((k_in, D_MODEL), BF16)]
    if mode == "conv":
        scratch += [pltpu.VMEM((tm + HALO, D_MODEL), F32), pltpu.VMEM((tm, D_MODEL), F32)]
    return pl.pallas_call(
        functools.partial(_mixer_post_kernel, mode, st.tiles_per_seq),
        grid=(st.n_tiles,),
        in_specs=specs,
        out_specs=[st.row_spec(D_MODEL), st.row_spec(D_MODEL),
                   pl.BlockSpec((TOP_K, tm), lambda i: (0, i)),
                   pl.BlockSpec((TOP_K, tm), lambda i: (0, i))],
        out_shape=[jax.ShapeDtypeStruct((st.n_rows, D_MODEL), F32),
                   jax.ShapeDtypeStruct((st.n_rows, D_MODEL), F32),
                   jax.ShapeDtypeStruct((TOP_K, st.n_rows), I32),
                   jax.ShapeDtypeStruct((TOP_K, st.n_rows), F32)],
        scratch_shapes=scratch,
        compiler_params=_cparams("arbitrary"),
        name="mixer_post_" + mode,
    )(*args)


def _pre_ret_kernel(x_ref, sh_ref, sc_ref, g_ref, w_ref, cos_ref, sin_ref,
                    q_ref, k_ref, v_ref, sg_ref):
    h = _rms_mod(x_ref[...], g_ref[...], sc_ref[...], sh_ref[...]).astype(BF16)
    cos = cos_ref[...]
    sin = sin_ref[...]
    half = RET_DK // 2
    qk = RET_HEADS * RET_DK
    vd = RET_HEADS * RET_DV
    for out_ref, base, scale in ((q_ref, 0, 1.0), (k_ref, qk, RET_DK ** -0.5)):
        for hh in range(RET_HEADS):
            c0 = base + hh * RET_DK
            p = jnp.dot(h, w_ref[:, c0:c0 + RET_DK], preferred_element_type=F32)
            p1, p2 = p[:, :half], p[:, half:]
            o0 = hh * RET_DK
            out_ref[:, o0:o0 + half] = ((p1 * cos - p2 * sin) * scale).astype(BF16)
            out_ref[:, o0 + half:o0 + RET_DK] = ((p1 * sin + p2 * cos) * scale).astype(BF16)
    for hh in range(RET_HEADS):
        c0 = 2 * qk + hh * RET_DV
        v_ref[:, hh * RET_DV:(hh + 1) * RET_DV] = jnp.dot(
            h, w_ref[:, c0:c0 + RET_DV], preferred_element_type=F32).astype(BF16)
        g = jnp.dot(h, w_ref[:, vd + c0:vd + c0 + RET_DV], preferred_element_type=F32)
        sg_ref[:, hh * RET_DV:(hh + 1) * RET_DV] = _silu(g).astype(BF16)


def _pre_ret(st, x, norm_g, w_in_b, layer, j, cos, sin, tbl_spec):
    qk = RET_HEADS * RET_DK
    vd = RET_HEADS * RET_DV
    return pl.pallas_call(
        _pre_ret_kernel,
        grid=(st.n_tiles,),
        in_specs=[
            st.row_spec(D_MODEL),
            st.mod_spec(0), st.mod_spec(1),
            _layer_spec((1, D_MODEL), layer),
            _layer_spec((D_MODEL, 2 * qk + 2 * vd), j),
            tbl_spec, tbl_spec,
        ],
        out_specs=[st.row_spec(qk), st.row_spec(qk), st.row_spec(vd), st.row_spec(vd)],
        out_shape=[jax.ShapeDtypeStruct((st.n_rows, qk), BF16),
                   jax.ShapeDtypeStruct((st.n_rows, qk), BF16),
                   jax.ShapeDtypeStruct((st.n_rows, vd), BF16),
                   jax.ShapeDtypeStruct((st.n_rows, vd), BF16)],
        compiler_params=_cparams("parallel"),
        name="pre_ret",
    )(x, st.mod, st.mod, norm_g.reshape(-1, 1, D_MODEL), w_in_b, cos, sin)


def _decay_matrix(lg, ch):
    diff = (lax.broadcasted_iota(I32, (ch, ch), 0)
            - lax.broadcasted_iota(I32, (ch, ch), 1)).astype(F32)
    return jnp.where(diff >= 0.0, jnp.exp(lg * jnp.maximum(diff, 0.0)), 0.0)


def _retention_chunk(valid, lg, dec, q, k, v, sg, s_prev):
    ch = q.shape[0]
    idx = lax.broadcasted_iota(I32, (ch, 1), 0).astype(F32)
    scores = lax.dot_general(q, k, (((1,), (1,)), ((), ())), preferred_element_type=F32) * dec
    o = jnp.dot(scores.astype(BF16), v, preferred_element_type=F32)
    cross = jnp.exp(lg * (idx + 1.0))
    o = o + jnp.dot(q, s_prev.astype(BF16), preferred_element_type=F32) * cross
    kd = (k.astype(F32) * jnp.exp(lg * (valid - 1.0 - idx))).astype(BF16)
    s_new = jnp.exp(lg * float(valid)) * s_prev + lax.dot_general(
        kd, v, (((0,), (0,)), ((), ())), preferred_element_type=F32)
    mu = jnp.mean(o, axis=-1, keepdims=True)
    oc = o - mu
    var = jnp.mean(oc * oc, axis=-1, keepdims=True)
    on = oc * lax.rsqrt(var + NORM_EPS)
    return (sg.astype(F32) * on).astype(BF16), s_new


def _retention_kernel(valid, lg_ref, q_ref, k_ref, v_ref, sg_ref, o_ref, s_out_ref, s_ref, dec_ref):
    c = pl.program_id(1)
    ch = q_ref.shape[0]

    @pl.when(c == 0)
    def _():
        s_ref[...] = jnp.zeros_like(s_ref)
        for h in range(RET_HEADS):
            dec_ref[h] = _decay_matrix(lg_ref[h, 0:1, 0:1], ch)

    for h in range(RET_HEADS):
        ks = slice(h * RET_DK, (h + 1) * RET_DK)
        vs = slice(h * RET_DV, (h + 1) * RET_DV)
        gated, s_new = _retention_chunk(valid, lg_ref[h, 0:1, 0:1], dec_ref[h], q_ref[:, ks],
                                        k_ref[:, ks], v_ref[:, vs], sg_ref[:, vs], s_ref[h])
        o_ref[:, vs] = gated
        s_ref[h] = s_new

    @pl.when(c == pl.num_programs(1) - 1)
    def _():
        s_out_ref[...] = s_ref[...]


def _retention_sample_kernel(lg_ref, q_ref, k_ref, v_ref, sg_ref, s0_ref, o_ref, s_out_ref):
    lg = lg_ref[0:1, 0:1]
    dec = _decay_matrix(lg, RET_PAD)
    for b in range(RET_SB):
        rows = slice(b * RET_PAD, (b + 1) * RET_PAD)
        gated, s_new = _retention_chunk(DEC_SEQ, lg, dec, q_ref[rows, :], k_ref[rows, :],
                                        v_ref[rows, :], sg_ref[rows, :], s0_ref[b])
        o_ref[rows, :] = gated
        s_out_ref[b] = s_new


def _retention_sample(q, k, v, sg, s0, s0_layer, lg_tbl):
    rows = RET_SB * RET_PAD
    row = lambda i, h: (i, h)
    state_block = (None, RET_SB, None, RET_DK, RET_DV)
    return pl.pallas_call(
        _retention_sample_kernel,
        grid=(DEC_BATCH // RET_SB, RET_HEADS),
        in_specs=[
            pl.BlockSpec((None, 8, 128), lambda i, h: (h, 0, 0)),
            pl.BlockSpec((rows, RET_DK), row), pl.BlockSpec((rows, RET_DK), row),
            pl.BlockSpec((rows, RET_DV), row), pl.BlockSpec((rows, RET_DV), row),
            pl.BlockSpec(state_block, lambda i, h: (s0_layer, i, h, 0, 0)),
        ],
        out_specs=[pl.BlockSpec((rows, RET_DV), row),
                   pl.BlockSpec(state_block, lambda i, h: (0, i, h, 0, 0))],
        out_shape=[jax.ShapeDtypeStruct((DEC_BATCH * RET_PAD, RET_HEADS * RET_DV), BF16),
                   jax.ShapeDtypeStruct((1, DEC_BATCH, RET_HEADS, RET_DK, RET_DV), F32)],
        compiler_params=_cparams("parallel", "parallel"),
        name="retention_sample",
    )(lg_tbl, q, k, v, sg, s0)


def _retention(q, k, v, sg, lg_tbl, n_seq, n_chunks, ch, valid):
    n_rows = q.shape[0]
    qk = RET_HEADS * RET_DK
    vd = RET_HEADS * RET_DV
    row = lambda b, c: (b * n_chunks + c, 0)
    state_block = (None, None, RET_HEADS, RET_DK, RET_DV)
    return pl.pallas_call(
        functools.partial(_retention_kernel, valid),
        grid=(n_seq, n_chunks),
        in_specs=[
            pl.BlockSpec((RET_HEADS, 8, 128), lambda b, c: (0, 0, 0)),
            pl.BlockSpec((ch, qk), row), pl.BlockSpec((ch, qk), row),
            pl.BlockSpec((ch, vd), row), pl.BlockSpec((ch, vd), row),
        ],
        out_specs=[pl.BlockSpec((ch, vd), row),
                   pl.BlockSpec(state_block, lambda b, c: (0, b, 0, 0, 0))],
        out_shape=[jax.ShapeDtypeStruct((n_rows, vd), BF16),
                   jax.ShapeDtypeStruct((1, n_seq, RET_HEADS, RET_DK, RET_DV), F32)],
        scratch_shapes=[pltpu.VMEM((RET_HEADS, RET_DK, RET_DV), F32),
                        pltpu.VMEM((RET_HEADS, ch, ch), F32)],
        compiler_params=_cparams("parallel", "arbitrary"),
        name="retention",
    )(lg_tbl, q, k, v, sg)


def _excl_prefix(col):
    r = lax.broadcasted_iota(I32, (N_EXPERTS, N_EXPERTS), 0)
    c = lax.broadcasted_iota(I32, (N_EXPERTS, N_EXPERTS), 1)
    lane = jnp.sum(jnp.where(r < c, col, 0.0), axis=0, keepdims=True)
    return jnp.sum(jnp.where(r == c, lane, 0.0), axis=1, keepdims=True)


def _ceil_to(x, m):
    return jnp.floor((x + (m - 1.0)) * (1.0 / m)) * m


def _route_kernel(te_ref, lpos_ref, rstart_ref, rlen_ref, tot_ref, tri_ref):
    rt = te_ref.shape[2]
    iota_e = lax.broadcasted_iota(I32, (N_EXPERTS, rt), 0)
    s = lax.broadcasted_iota(I32, (rt, rt), 0)
    t = lax.broadcasted_iota(I32, (rt, rt), 1)
    tri_ref[...] = (s < t).astype(BF16)

    def tile_onehots(i):
        te = te_ref[i]
        onehots = [(te[k:k + 1, :] == iota_e).astype(F32) for k in range(TOP_K)]
        cnt = jnp.sum(onehots[0] + onehots[1] + onehots[2] + onehots[3], axis=1, keepdims=True)
        return onehots, _ceil_to(cnt, RUN_ALIGN)

    tot = lax.fori_loop(0, MOE_NT, lambda i, acc: acc + tile_onehots(i)[1],
                        jnp.zeros((N_EXPERTS, 1), F32))
    tot_ref[...] = tot

    def place(i, gbase):
        onehots, run = tile_onehots(i)
        base = _excl_prefix(run)
        for k in range(TOP_K):
            oh = onehots[k]
            before = jnp.dot(oh.astype(BF16), tri_ref[...], preferred_element_type=F32)
            pos = jnp.sum(oh * (base + before), axis=0, keepdims=True)
            lpos_ref[i, k:k + 1, :] = pos.astype(I32)
            base = base + jnp.sum(oh, axis=1, keepdims=True)
        rstart_ref[i] = jnp.broadcast_to(gbase, rstart_ref.shape[1:]).astype(I32)
        rlen_ref[i] = jnp.broadcast_to(run, rlen_ref.shape[1:]).astype(I32)
        return gbase + run

    lax.fori_loop(0, MOE_NT, place, _excl_prefix(_ceil_to(tot, MOE_TB)))


def _route(te):
    full = lambda shape: pl.BlockSpec(shape, lambda i: (0,) * len(shape))
    return pl.pallas_call(
        _route_kernel,
        grid=(1,),
        in_specs=[full(te.shape)],
        out_specs=[full(te.shape), full((MOE_NT, N_EXPERTS, 128)), full((MOE_NT, N_EXPERTS, 128)),
                   full((N_EXPERTS, 1))],
        out_shape=[jax.ShapeDtypeStruct(te.shape, I32),
                   jax.ShapeDtypeStruct((MOE_NT, N_EXPERTS, 128), I32),
                   jax.ShapeDtypeStruct((MOE_NT, N_EXPERTS, 128), I32),
                   jax.ShapeDtypeStruct((N_EXPERTS, 1), F32)],
        scratch_shapes=[pltpu.VMEM((MOE_RT, MOE_RT), BF16)],
        compiler_params=_cparams("arbitrary"),
        name="moe_route",
    )(te)


def _dispatch_kernel(n_prompt_tiles, rs_ref, rl_ref, tzs_ref, tzl_ref, nu_ref, hp_ref, hs_ref,
                     lpos_ref, xb_ref, xs_ref, sem, zero_ref):
    i = pl.program_id(0)
    slot = i % 2
    zero_sem = sem.at[2]

    def tail_copy(e):
        n = pl.multiple_of(tzl_ref[e], RUN_ALIGN)
        dst = pl.multiple_of(tzs_ref[e], RUN_ALIGN)
        return pltpu.make_async_copy(zero_ref.at[pl.ds(0, n)], xb_ref.at[pl.ds(dst, n)], zero_sem)

    def block_copy(j):
        dst = pl.multiple_of(j * MOE_TB, MOE_TB)
        return pltpu.make_async_copy(zero_ref, xb_ref.at[pl.ds(dst, MOE_TB)], zero_sem)

    @pl.when(i == 0)
    def _():
        zero_ref[...] = jnp.zeros_like(zero_ref)

        def start(e, carry):
            pl.when(tzl_ref[e] > 0)(lambda: tail_copy(e).start())
            return carry

        def wait(e, carry):
            pl.when(tzl_ref[e] > 0)(lambda: tail_copy(e).wait())
            return carry

        lax.fori_loop(0, N_EXPERTS, start, 0)
        lax.fori_loop(nu_ref[0], MOE_NBLK, lambda j, c: (block_copy(j).start(), c)[1], 0)
        lax.fori_loop(0, N_EXPERTS, wait, 0)
        lax.fori_loop(nu_ref[0], MOE_NBLK, lambda j, c: (block_copy(j).wait(), c)[1], 0)

    rows = lax.broadcasted_iota(I32, (MOE_CAP, MOE_RT), 0)
    perm = jnp.where(rows == lpos_ref[0:1, :], 1.0, 0.0)
    for k in range(1, TOP_K):
        perm = perm + jnp.where(rows == lpos_ref[k:k + 1, :], 1.0, 0.0)
    h = jnp.where(i < n_prompt_tiles, hp_ref[...], hs_ref[...])
    xs_ref[slot] = _pack_halves(
        jnp.dot(perm.astype(BF16), h.astype(BF16), preferred_element_type=F32))

    def scatter(tile, wait):
        def body(e, lo):
            n = pl.multiple_of(rl_ref[tile * N_EXPERTS + e], RUN_ALIGN)
            sl = tile % 2
            src = pl.multiple_of(lo, RUN_ALIGN)
            dst = pl.multiple_of(rs_ref[tile * N_EXPERTS + e], RUN_ALIGN)
            cp = pltpu.make_async_copy(xs_ref.at[sl, pl.ds(src, n)], xb_ref.at[pl.ds(dst, n)],
                                       sem.at[sl])

            @pl.when(n > 0)
            def _():
                if wait:
                    cp.wait()
                else:
                    cp.start()

            return lo + n

        lax.fori_loop(0, N_EXPERTS, body, 0)

    scatter(i, False)

    @pl.when(i > 0)
    def _():
        scatter(i - 1, True)

    @pl.when(i == pl.num_programs(0) - 1)
    def _():
        scatter(i, True)


def _dispatch(tables, h_p, h_s, lpos):
    nt_p = h_p.shape[0] // MOE_RT
    return pl.pallas_call(
        functools.partial(_dispatch_kernel, nt_p),
        grid_spec=pltpu.PrefetchScalarGridSpec(
            num_scalar_prefetch=len(tables),
            grid=(MOE_NT,),
            in_specs=[
                pl.BlockSpec((MOE_RT, D_MODEL), lambda i, *_: (jnp.minimum(i, nt_p - 1), 0)),
                pl.BlockSpec((MOE_RT, D_MODEL), lambda i, *_: (jnp.maximum(i - nt_p, 0), 0)),
                pl.BlockSpec((None, TOP_K, MOE_RT), lambda i, *_: (i, 0, 0)),
            ],
            out_specs=pl.BlockSpec(memory_space=pl.ANY),
            scratch_shapes=[pltpu.VMEM((2, MOE_CAP, D_MODEL // 2), U32),
                            pltpu.SemaphoreType.DMA((3,)),
                            pltpu.VMEM((MOE_TB, D_MODEL // 2), U32)],
        ),
        out_shape=jax.ShapeDtypeStruct((MOE_ROWS, D_MODEL // 2), U32),
        compiler_params=_cparams("arbitrary"),
        name="moe_dispatch",
    )(*tables, h_p, h_s, lpos)


def _ffn_kernel(layer, be_ref, nu_ref, nxt_ref, x_ref, bgu_ref, bdn_ref, wgu_hbm, wdn_hbm, y_ref,
                wgu_f_ref, wdn_f_ref, wgu_b_ref, wdn_b_ref, act_ref, sem):
    j = pl.program_id(0)
    e = be_ref[j]
    e_prev = be_ref[jnp.maximum(j - 1, 0)]
    active = j < nu_ref[0]

    def weight_copies(ex):
        return (pltpu.make_async_copy(wgu_hbm.at[layer, ex], wgu_f_ref, sem.at[0]),
                pltpu.make_async_copy(wdn_hbm.at[layer, ex], wdn_f_ref, sem.at[1]))

    @pl.when(j == 0)
    def _():
        for cp in weight_copies(e):
            cp.start()

    @pl.when(active & ((j == 0) | (e != e_prev)))
    def _():
        for cp in weight_copies(e):
            cp.wait()
        wgu_b_ref[...] = wgu_f_ref[...].astype(BF16)
        wdn_b_ref[...] = wdn_f_ref[...].astype(BF16)
        nx = nxt_ref[j]

        @pl.when(nx >= 0)
        def _():
            for cp in weight_copies(nx):
                cp.start()

    @pl.when(active)
    def _():
        x = _unpack_halves(x_ref[...])
        for c in range(D_FF // FF_CHUNK):
            gs = slice(c * FF_CHUNK, (c + 1) * FF_CHUNK)
            us = slice(D_FF + c * FF_CHUNK, D_FF + (c + 1) * FF_CHUNK)
            gate = jnp.dot(x, wgu_b_ref[:, gs], preferred_element_type=F32) + bgu_ref[:, gs]
            up = jnp.dot(x, wgu_b_ref[:, us], preferred_element_type=F32) + bgu_ref[:, us]
            gate = jnp.minimum(gate, SWIGLU_LIMIT)
            up = jnp.clip(up, -SWIGLU_LIMIT, SWIGLU_LIMIT)
            act = (up + 1.0) * (gate * jax.nn.sigmoid(SWIGLU_ALPHA * gate))
            act_ref[:, gs] = act.astype(BF16)
        y = jnp.dot(act_ref[...], wdn_b_ref[...], preferred_element_type=F32) + bdn_ref[...]
        y_ref[...] = _pack_halves(y.astype(BF16).astype(F32))

    @pl.when(jnp.logical_not(active))
    def _():
        y_ref[...] = jnp.zeros_like(y_ref)


def _ffn(layer, block_e, n_used, nxt, xb, w_gu, b_gu, w_dn, b_dn):
    row_map = lambda j, be, nu, nx: (jnp.minimum(j, nu[0] - 1), 0)
    out_map = lambda j, be, nu, nx: (j, 0)
    exp_map = lambda j, be, nu, nx: (layer, be[j], 0, 0)
    return pl.pallas_call(
        functools.partial(_ffn_kernel, layer),
        grid_spec=pltpu.PrefetchScalarGridSpec(
            num_scalar_prefetch=3,
            grid=(MOE_NBLK,),
            in_specs=[
                pl.BlockSpec((MOE_TB, D_MODEL // 2), row_map),
                pl.BlockSpec((None, None, 1, 2 * D_FF), exp_map),
                pl.BlockSpec((None, None, 1, D_MODEL), exp_map),
                pl.BlockSpec(memory_space=pl.ANY),
                pl.BlockSpec(memory_space=pl.ANY),
            ],
            out_specs=pl.BlockSpec((MOE_TB, D_MODEL // 2), out_map),
            scratch_shapes=[pltpu.VMEM((D_MODEL, 2 * D_FF), F32),
                            pltpu.VMEM((D_FF, D_MODEL), F32),
                            pltpu.VMEM((D_MODEL, 2 * D_FF), BF16),
                            pltpu.VMEM((D_FF, D_MODEL), BF16),
                            pltpu.VMEM((MOE_TB, D_FF), BF16),
                            pltpu.SemaphoreType.DMA((2,))],
        ),
        out_shape=jax.ShapeDtypeStruct((MOE_ROWS, D_MODEL // 2), U32),
        compiler_params=_cparams("arbitrary"),
        name="moe_ffn",
    )(block_e, n_used, nxt, xb, b_gu.reshape(b_gu.shape[0], N_EXPERTS, 1, 2 * D_FF),
      b_dn.reshape(b_dn.shape[0], N_EXPERTS, 1, D_MODEL), w_gu, w_dn)


def _tile_rows(v, n_rows):
    mr = v.shape[0]
    if mr in (1, n_rows):
        return v
    return jnp.concatenate([v] * (n_rows // mr), axis=0)


def _combine_kernel(tile_off, final, rs_ref, rl_ref, x_ref, gf_ref, lpt_ref, wt_ref, fg_ref,
                    yb_ref, out_ref, ys_ref, sem):
    i = pl.program_id(0)

    def gather(tile, wait):
        def body(e, lo):
            n = pl.multiple_of(rl_ref[(tile_off + tile) * N_EXPERTS + e], RUN_ALIGN)
            sl = tile % 2
            src = pl.multiple_of(rs_ref[(tile_off + tile) * N_EXPERTS + e], RUN_ALIGN)
            dst = pl.multiple_of(lo, RUN_ALIGN)
            cp = pltpu.make_async_copy(yb_ref.at[pl.ds(src, n)], ys_ref.at[sl, pl.ds(dst, n)],
                                       sem.at[sl])

            @pl.when(n > 0)
            def _():
                if wait:
                    cp.wait()
                else:
                    cp.start()

            return lo + n

        lax.fori_loop(0, N_EXPERTS, body, 0)

    @pl.when(i == 0)
    def _():
        ys_ref[...] = jnp.zeros_like(ys_ref)
        gather(0, False)

    @pl.when(i + 1 < pl.num_programs(0))
    def _():
        gather(i + 1, False)

    gather(i, True)

    cols = lax.broadcasted_iota(I32, (MOE_RT, MOE_CAP), 1)
    w = jnp.where(cols == lpt_ref[:, 0:1], wt_ref[:, 0:1], 0.0)
    for k in range(1, TOP_K):
        w = w + jnp.where(cols == lpt_ref[:, k:k + 1], wt_ref[:, k:k + 1], 0.0)
    ys = _unpack_halves(ys_ref[i % 2])
    y = jnp.dot(w.astype(BF16), ys, preferred_element_type=F32)
    xn = x_ref[...] + _tile_rows(gf_ref[...], MOE_RT) * y
    if final:
        xn = xn * lax.rsqrt(jnp.mean(xn * xn, axis=-1, keepdims=True) + NORM_EPS) * fg_ref[...]
    out_ref[...] = xn


def _combine(tile_off, n_tiles, final, tables, x, mod, mod_map, lpos_t, wt, final_g, yb):
    pm = lambda i, *_: (i, 0)
    tm = lambda i, *_: (i + tile_off, 0)
    return pl.pallas_call(
        functools.partial(_combine_kernel, tile_off, final),
        grid_spec=pltpu.PrefetchScalarGridSpec(
            num_scalar_prefetch=len(tables),
            grid=(n_tiles,),
            in_specs=[
                pl.BlockSpec((MOE_RT, D_MODEL), pm),
                pl.BlockSpec((None, mod.shape[1], D_MODEL), mod_map),
                pl.BlockSpec((MOE_RT, TOP_K), tm),
                pl.BlockSpec((MOE_RT, TOP_K), tm),
                pl.BlockSpec((1, D_MODEL), lambda i, *_: (0, 0)),
                pl.BlockSpec(memory_space=pl.ANY),
            ],
            out_specs=pl.BlockSpec((MOE_RT, D_MODEL), pm),
            scratch_shapes=[pltpu.VMEM((2, MOE_CAP, D_MODEL // 2), U32),
                            pltpu.SemaphoreType.DMA((2,))],
        ),
        out_shape=jax.ShapeDtypeStruct(x.shape, F32),
        compiler_params=_cparams("arbitrary"),
        name="moe_combine",
    )(*tables, x, mod, lpos_t, wt, final_g.reshape(1, D_MODEL), yb)


def _moe(layer, final, st_p, st_s, x_p, x_s, post_p, post_s,
         w_gu, b_gu, w_dn, b_dn, final_g):
    te = jnp.concatenate([post_p[1], post_s[1]], axis=1)
    tw = jnp.concatenate([post_p[2], post_s[2]], axis=1)
    te = jnp.swapaxes(te.reshape(TOP_K, MOE_NT, MOE_RT), 0, 1)
    lpos, rstart, rlen, tot = _route(te)
    tot = tot[:, 0].astype(I32)
    padded = (tot + MOE_TB - 1) // MOE_TB * MOE_TB
    pends = jnp.cumsum(padded)
    n_used = (pends[-1] // MOE_TB).reshape(1)
    blk = jnp.minimum(jnp.arange(MOE_NBLK, dtype=I32), n_used[0] - 1) * MOE_TB
    block_e = jnp.minimum(jnp.sum(pends[None, :] <= blk[:, None], axis=-1),
                          N_EXPERTS - 1).astype(I32)
    ids = jnp.where(tot > 0, jnp.arange(N_EXPERTS, dtype=I32), N_EXPERTS)
    later = jnp.concatenate([lax.cummin(ids[::-1])[::-1][1:], jnp.full((1,), N_EXPERTS, I32)])
    nxt = jnp.where(later < N_EXPERTS, later, -1)[block_e]
    run_tables = (rstart[:, :, 0].reshape(-1), rlen[:, :, 0].reshape(-1))
    tail_tables = (pends - padded + tot, padded - tot)

    nt_p = N_PROMPT // MOE_RT
    nt_s = N_SAMPLE // MOE_RT
    xb = _dispatch(run_tables + tail_tables + (n_used,), post_p[0], post_s[0], lpos)
    yb = _ffn(layer, block_e, n_used, nxt, xb, w_gu, b_gu, w_dn, b_dn)
    lpos_t = jnp.swapaxes(lpos, 1, 2).reshape(N_TOK, TOP_K)
    wt = tw.T
    tiles_per_seq = SEQ // MOE_RT
    out_p = _combine(0, nt_p, final, run_tables, x_p, st_p.mod,
                     lambda i, *_: (i // tiles_per_seq, 0, N_MOD - 1), lpos_t, wt, final_g, yb)
    out_s = _combine(nt_p, nt_s, final, run_tables, x_s, st_s.mod,
                     lambda i, *_: (0, 0, N_MOD - 1), lpos_t, wt, final_g, yb)
    return out_p, out_s


def _rope_tables(pos):
    half = RET_DK // 2
    inv_freq = ROPE_BASE ** (-jnp.arange(half, dtype=F32) / half)
    ang = pos.astype(F32)[:, None] * inv_freq[None, :]
    return jnp.cos(ang), jnp.sin(ang)


def kernel(x_prompt, x_sample, c_prompt, c_sample, state_conv, state_ret, norm_mix_g, norm_ff_g,
           w_mod, b_mod, conv_w1, conv_b1, conv_dw, conv_dw_b, conv_ln_g, conv_ln_b, conv_w2,
           conv_b2, ret_w_in, ret_w_o, router_w, router_b, moe_w_gu, moe_b_gu, moe_w_dn,
           moe_b_dn, final_g):
    mod = _adaln(jnp.concatenate([c_prompt, c_sample], axis=0), w_mod, b_mod)
    x_p = x_prompt.reshape(N_PROMPT, D_MODEL)
    x_s = jnp.swapaxes(x_sample, 0, 1).reshape(N_SAMPLE, D_MODEL)
    router_wt = jnp.swapaxes(router_w, 1, 2)

    def streams(layer):
        st_p = _Stream(N_PROMPT, TM_PROMPT, SEQ // TM_PROMPT,
                       mod[layer, :BATCH].reshape(BATCH, 1, N_MOD * D_MODEL))
        st_s = _Stream(N_SAMPLE, TM_SAMPLE, None,
                       mod[layer, BATCH:].reshape(1, DEC_BATCH, N_MOD * D_MODEL))
        return st_p, st_s

    st_p, st_s = streams(0)
    u_p = _pre_conv(st_p, x_p, norm_mix_g, conv_w1, conv_b1, 0, 0)
    u_s = _pre_conv(st_s, x_s, norm_mix_g, conv_w1, conv_b1, 0, 0)
    buf_tm = jnp.swapaxes(state_conv[0], 0, 1)
    z_s, ns_tm = _conv_sample(buf_tm, u_s.reshape(DEC_SEQ, DEC_BATCH, D_MODEL),
                              conv_dw, conv_dw_b, 0)
    post_p = _mixer_post("conv", st_p, 0, 0, x_p, u_p, norm_ff_g, router_wt, router_b,
                         conv_w2, conv_b2, conv_dw, conv_dw_b, conv_ln_g, conv_ln_b)
    post_s = _mixer_post("z", st_s, 0, 0, x_s, z_s.reshape(N_SAMPLE, D_MODEL), norm_ff_g,
                         router_wt, router_b, conv_w2, conv_b2, None, None, conv_ln_g, conv_ln_b)
    x_p, x_s = _moe(0, False, st_p, st_s, post_p[0], post_s[0], post_p[1:], post_s[1:],
                    moe_w_gu, moe_b_gu, moe_w_dn, moe_b_dn, final_g)
    conv_p = u_p.reshape(BATCH, SEQ, D_MODEL)[:, SEQ - CONV_STATE:][None]
    conv_s = jnp.swapaxes(ns_tm, 0, 1)[None]

    st_p, st_s = streams(1)
    w_in_b = ret_w_in.astype(BF16)
    cos_p, sin_p = _rope_tables(jnp.arange(SEQ, dtype=I32))
    cos_s, sin_s = _rope_tables(PAST_LEN + jnp.arange(DEC_SEQ, dtype=I32))
    half = RET_DK // 2
    tps = SEQ // TM_PROMPT
    tbl_p = pl.BlockSpec((TM_PROMPT, half), lambda i: (i % tps, 0))
    tbl_s = pl.BlockSpec((None, 1, half), lambda i: (i, 0, 0))
    q_p, k_p, v_p, sg_p = _pre_ret(st_p, x_p, norm_mix_g, w_in_b, 1, 0, cos_p, sin_p, tbl_p)
    q_s, k_s, v_s, sg_s = _pre_ret(st_s, x_s, norm_mix_g, w_in_b, 1, 0,
                                   cos_s.reshape(DEC_SEQ, 1, half),
                                   sin_s.reshape(DEC_SEQ, 1, half), tbl_s)
    lg = jnp.log1p(-jnp.exp2(-5.0 - jnp.arange(RET_HEADS, dtype=F32)))
    lg_tbl = jnp.broadcast_to(lg[:, None, None], (RET_HEADS, 8, 128))
    gated_p, ret_p = _retention(q_p, k_p, v_p, sg_p, lg_tbl,
                                BATCH, SEQ // RET_CHUNK, RET_CHUNK, RET_CHUNK)

    def to_seq_major(a):
        a = jnp.swapaxes(a.reshape(DEC_SEQ, DEC_BATCH, -1), 0, 1)
        a = jnp.pad(a, ((0, 0), (0, RET_PAD - DEC_SEQ), (0, 0)))
        return a.reshape(DEC_BATCH * RET_PAD, -1)

    gated_s, ret_s = _retention_sample(to_seq_major(q_s), to_seq_major(k_s), to_seq_major(v_s),
                                       to_seq_major(sg_s), state_ret, 0, lg_tbl)
    gated_s = jnp.swapaxes(gated_s.reshape(DEC_BATCH, RET_PAD, -1)[:, :DEC_SEQ], 0, 1)
    gated_s = gated_s.reshape(N_SAMPLE, -1)
    post_p = _mixer_post("ret", st_p, 1, 0, x_p, gated_p, norm_ff_g, router_wt, router_b, ret_w_o)
    post_s = _mixer_post("ret", st_s, 1, 0, x_s, gated_s, norm_ff_g, router_wt, router_b, ret_w_o)
    y_p, y_s = _moe(1, True, st_p, st_s, post_p[0], post_s[0], post_p[1:], post_s[1:],
                    moe_w_gu, moe_b_gu, moe_w_dn, moe_b_dn, final_g)

    y_prompt = y_p.reshape(BATCH, SEQ, D_MODEL)
    y_sample = jnp.swapaxes(y_s.reshape(DEC_SEQ, DEC_BATCH, D_MODEL), 0, 1)
    return (y_prompt, y_sample, conv_p, conv_s, ret_p, ret_s)
```

```python
import functools

import jax
import jax.numpy as jnp
from jax import lax
from jax.experimental import pallas as pl
from jax.experimental.pallas import tpu as pltpu

F32 = jnp.float32
BF16 = jnp.bfloat16
I32 = jnp.int32
U32 = jnp.uint32

D_MODEL = 1024
SEQ = 2048
BATCH = 8
DEC_BATCH = 128
DEC_SEQ = 4
PAST_LEN = 16384
CONV_WIDTH = 31
CONV_STATE = CONV_WIDTH - 1
RET_HEADS = 4
RET_DK = 256
RET_DV = 512
ROPE_BASE = 10000.0
N_EXPERTS = 32
TOP_K = 4
D_FF = 1024
SWIGLU_LIMIT = 7.0
SWIGLU_ALPHA = 1.702
N_MOD = 6
NORM_EPS = 1e-5

N_PROMPT = BATCH * SEQ
N_SAMPLE = DEC_BATCH * DEC_SEQ
N_TOK = N_PROMPT + N_SAMPLE
N_ASSIGN = N_TOK * TOP_K

TM_PROMPT = 512
TM_SAMPLE = DEC_BATCH
HALO = 32
CONV_RB = 64
CONV_LW = 128
RET_CHUNK = 256
RET_PAD = 16
RET_SB = 8
MOE_TB = 512
MOE_RT = 256
MOE_NT = N_TOK // MOE_RT
RUN_ALIGN = 8
MOE_CAP = -(-(TOP_K * MOE_RT + N_EXPERTS * (RUN_ALIGN - 1)) // 256) * 256
MOE_MAX_ROWS = N_ASSIGN + MOE_NT * N_EXPERTS * (RUN_ALIGN - 1)
MOE_NBLK = -(-MOE_MAX_ROWS // MOE_TB) + N_EXPERTS
MOE_ROWS = MOE_NBLK * MOE_TB
FF_CHUNK = 256
VMEM_LIMIT = 56 * 1024 * 1024


def _cparams(*sem):
    return pltpu.CompilerParams(dimension_semantics=sem, vmem_limit_bytes=VMEM_LIMIT)


def _silu(x):
    return x * jax.nn.sigmoid(x)


def _rms_mod(x, g, sc, sh):
    y = x * lax.rsqrt(jnp.mean(x * x, axis=-1, keepdims=True) + NORM_EPS) * g
    return y * (1.0 + sc) + sh


def _split_bf16(x):
    hi = x.astype(BF16)
    lo = (x - hi.astype(F32)).astype(BF16)
    return hi, lo


def _pack_halves(x):
    c = x.shape[1] // 2
    lo = lax.bitcast_convert_type(x[:, :c], U32)
    hi = lax.bitcast_convert_type(x[:, c:], U32)
    return (lo >> 16) | (hi & jnp.uint32(0xFFFF0000))


def _unpack_halves(p):
    lo = lax.bitcast_convert_type(p << 16, F32)
    hi = lax.bitcast_convert_type(p & jnp.uint32(0xFFFF0000), F32)
    return jnp.concatenate([lo.astype(BF16), hi.astype(BF16)], axis=1)


def _adaln_kernel(c_ref, w_ref, b_ref, o_ref):
    a = _silu(c_ref[...]).astype(BF16)
    o_ref[...] = jnp.dot(a, w_ref[...].astype(BF16), preferred_element_type=F32) + b_ref[...]


def _adaln(c_all, w_mod, b_mod):
    depth, d, nm = w_mod.shape
    n = c_all.shape[0]
    return pl.pallas_call(
        _adaln_kernel,
        grid=(depth, nm // d),
        in_specs=[
            pl.BlockSpec((n, d), lambda l, j: (0, 0)),
            pl.BlockSpec((None, d, d), lambda l, j: (l, 0, j)),
            pl.BlockSpec((None, 1, d), lambda l, j: (l, 0, j)),
        ],
        out_specs=pl.BlockSpec((None, n, d), lambda l, j: (l, 0, j)),
        out_shape=jax.ShapeDtypeStruct((depth, n, nm), F32),
        compiler_params=_cparams("parallel", "parallel"),
        name="adaln",
    )(c_all, w_mod, b_mod.reshape(depth, 1, nm))


class _Stream:
    def __init__(self, n_rows, tm, tiles_per_seq, mod):
        self.n_rows = n_rows
        self.tm = tm
        self.n_tiles = n_rows // tm
        self.tiles_per_seq = tiles_per_seq
        self.mod = mod

    def mod_spec(self, j):
        mr = self.mod.shape[1]
        if self.tiles_per_seq is None:
            return pl.BlockSpec((None, mr, D_MODEL), lambda i: (0, 0, j))
        tps = self.tiles_per_seq
        return pl.BlockSpec((None, mr, D_MODEL), lambda i: (i // tps, 0, j))

    def row_spec(self, width, col=0):
        return pl.BlockSpec((self.tm, width), lambda i: (i, col))


def _const_spec(shape):
    nd = len(shape)
    return pl.BlockSpec(shape, lambda i: (0,) * nd)


def _layer_spec(shape, layer):
    nd = len(shape)
    return pl.BlockSpec((None,) + shape, lambda i: (layer,) + (0,) * nd)


def _pre_conv_kernel(x_ref, sh_ref, sc_ref, g_ref, w1_ref, b1_ref, u_ref, w1b_ref):
    @pl.when(pl.program_id(0) == 0)
    def _():
        w1b_ref[...] = w1_ref[...].astype(BF16)

    h = _rms_mod(x_ref[...], g_ref[...], sc_ref[...], sh_ref[...])
    ag = jnp.dot(h.astype(BF16), w1b_ref[...], preferred_element_type=F32) + b1_ref[...]
    u_ref[...] = ag[:, :D_MODEL] * jax.nn.sigmoid(ag[:, D_MODEL:])


def _pre_conv(st, x, norm_g, w1, b1, layer, j):
    c2 = w1.shape[-1]
    return pl.pallas_call(
        _pre_conv_kernel,
        grid=(st.n_tiles,),
        in_specs=[
            st.row_spec(D_MODEL),
            st.mod_spec(0), st.mod_spec(1),
            _layer_spec((1, D_MODEL), layer),
            _layer_spec((D_MODEL, c2), j),
            _layer_spec((1, c2), j),
        ],
        out_specs=st.row_spec(D_MODEL),
        out_shape=jax.ShapeDtypeStruct((st.n_rows, D_MODEL), F32),
        scratch_shapes=[pltpu.VMEM((D_MODEL, c2), BF16)],
        compiler_params=_cparams("arbitrary"),
        name="pre_conv",
    )(x, st.mod, st.mod, norm_g.reshape(-1, 1, D_MODEL), w1, b1.reshape(-1, 1, c2))


def _conv_sample_kernel(buf_ref, u_ref, dw_ref, dwb_ref, z_ref, ns_ref):
    for t in range(DEC_SEQ):
        acc = jnp.broadcast_to(dwb_ref[...], u_ref.shape[1:])
        for j in range(t, CONV_STATE):
            acc = acc + buf_ref[j] * dw_ref[j - t:j - t + 1, :]
        for s in range(t + 1):
            k = CONV_STATE + s - t
            acc = acc + u_ref[s] * dw_ref[k:k + 1, :]
        z_ref[t] = acc
    ns_ref[0:CONV_STATE - DEC_SEQ] = buf_ref[DEC_SEQ:CONV_STATE]
    ns_ref[CONV_STATE - DEC_SEQ:CONV_STATE] = u_ref[...]


def _conv_sample(buf_tm, u_tm, dw, dwb, j):
    sb = 32
    c = u_tm.shape[-1]
    return pl.pallas_call(
        _conv_sample_kernel,
        grid=(DEC_BATCH // sb,),
        in_specs=[
            pl.BlockSpec((CONV_STATE, sb, c), lambda i: (0, i, 0)),
            pl.BlockSpec((DEC_SEQ, sb, c), lambda i: (0, i, 0)),
            _layer_spec((CONV_WIDTH, c), j),
            _layer_spec((1, c), j),
        ],
        out_specs=[
            pl.BlockSpec((DEC_SEQ, sb, c), lambda i: (0, i, 0)),
            pl.BlockSpec((CONV_STATE, sb, c), lambda i: (0, i, 0)),
        ],
        out_shape=[
            jax.ShapeDtypeStruct((DEC_SEQ, DEC_BATCH, c), F32),
            jax.ShapeDtypeStruct((CONV_STATE, DEC_BATCH, c), F32),
        ],
        compiler_params=_cparams("parallel"),
        name="conv_sample",
    )(buf_tm, u_tm, dw, dwb.reshape(-1, 1, c))


def _route_topk(h2, rwt_ref, rb_ref, te_ref, tw_ref):
    h_hi, h_lo = _split_bf16(h2)
    w_hi, w_lo = _split_bf16(rwt_ref[...])
    nt = (((1,), (1,)), ((), ()))
    logits = (lax.dot_general(w_hi, h_hi, nt, preferred_element_type=F32)
              + lax.dot_general(w_hi, h_lo, nt, preferred_element_type=F32)
              + lax.dot_general(w_lo, h_hi, nt, preferred_element_type=F32)
              + rb_ref[...])
    iota_e = lax.broadcasted_iota(I32, logits.shape, 0).astype(F32)
    vals, idxs = [], []
    for _ in range(TOP_K):
        m = jnp.max(logits, axis=0, keepdims=True)
        idx = jnp.min(jnp.where(logits == m, iota_e, float(N_EXPERTS)), axis=0, keepdims=True)
        logits = jnp.where(iota_e == idx, -jnp.inf, logits)
        vals.append(m)
        idxs.append(idx)
    es = [jnp.exp(v - vals[0]) for v in vals]
    tot = es[0] + es[1] + es[2] + es[3]
    for k in range(TOP_K):
        te_ref[k:k + 1, :] = idxs[k].astype(I32)
        tw_ref[k:k + 1, :] = es[k] / tot


def _mixer_post_kernel(mode, tiles_per_seq, *refs):
    it = iter(refs)
    if mode == "conv":
        u_ref, halo_ref, dw_ref, dwb_ref = next(it), next(it), next(it), next(it)
    elif mode == "z":
        z_in_ref = next(it)
    else:
        gated_ref = next(it)
    if mode in ("conv", "z"):
        lng_ref, lnb_ref = next(it), next(it)
    w_ref, b_ref = next(it), next(it)
    x_ref, gm_ref, shf_ref, scf_ref, gff_ref, rwt_ref, rb_ref = (next(it) for _ in range(7))
    x1_ref, h2_ref, te_ref, tw_ref = (next(it) for _ in range(4))
    wb_ref = next(it)
    if mode == "conv":
        ext_ref, z_ref = next(it), next(it)

    i = pl.program_id(0)

    @pl.when(i == 0)
    def _():
        wb_ref[...] = w_ref[...].astype(BF16)

    if mode == "conv":
        tm = u_ref.shape[0]
        first = (i % tiles_per_seq) == 0
        ext_ref[0:HALO, :] = jnp.where(first, 0.0, halo_ref[...])
        ext_ref[HALO:HALO + tm, :] = u_ref[...]
        off = HALO - CONV_STATE

        def row_block(r, carry):
            r0 = pl.multiple_of(r * CONV_RB, CONV_RB)
            for lc in range(D_MODEL // CONV_LW):
                ls = slice(lc * CONV_LW, (lc + 1) * CONV_LW)
                win = ext_ref[pl.ds(r0, CONV_RB + HALO), ls]
                acc = jnp.broadcast_to(dwb_ref[:, ls], (CONV_RB, CONV_LW))
                for s in range(8):
                    ws = win if s == 0 else pltpu.roll(win, CONV_RB + HALO - s, 0)
                    for k in range(CONV_WIDTH):
                        if (off + k) % 8 == s:
                            a = off + k - s
                            acc = acc + ws[a:a + CONV_RB, :] * dw_ref[k:k + 1, ls]
                z_ref[pl.ds(r0, CONV_RB), ls] = acc
            return carry

        lax.fori_loop(0, tm // CONV_RB, row_block, 0)
        z = z_ref[...]
    elif mode == "z":
        z = z_in_ref[...]

    if mode in ("conv", "z"):
        mu = jnp.mean(z, axis=-1, keepdims=True)
        zc = z - mu
        var = jnp.mean(zc * zc, axis=-1, keepdims=True)
        zn = zc * lax.rsqrt(var + NORM_EPS) * lng_ref[...] + lnb_ref[...]
        a = _silu(zn).astype(BF16)
        y = jnp.dot(a, wb_ref[...], preferred_element_type=F32) + b_ref[...]
    else:
        y = jnp.dot(gated_ref[...], wb_ref[...], preferred_element_type=F32)

    x1 = x_ref[...] + gm_ref[...] * y
    x1_ref[...] = x1
    h2 = _rms_mod(x1, gff_ref[...], scf_ref[...], shf_ref[...])
    h2_ref[...] = h2
    _route_topk(h2, rwt_ref, rb_ref, te_ref, tw_ref)


def _mixer_post(mode, st, layer, j, x, mix_in, norm_ff_g, router_wt, router_b, w, b=None,
                dw=None, dwb=None, ln_g=None, ln_b=None):
    tm = st.tm
    k_in = w.shape[-2]
    args, specs = [], []
    if mode == "conv":
        per32 = tm // HALO
        args += [mix_in, mix_in, dw, dwb.reshape(-1, 1, D_MODEL)]
        specs += [st.row_spec(D_MODEL),
                  pl.BlockSpec((HALO, D_MODEL), lambda i: (jnp.maximum(i * per32 - 1, 0), 0)),
                  _layer_spec((CONV_WIDTH, D_MODEL), j), _layer_spec((1, D_MODEL), j)]
    elif mode == "z":
        args += [mix_in]
        specs += [st.row_spec(D_MODEL)]
    else:
        args += [mix_in]
        specs += [st.row_spec(k_in)]
    if mode in ("conv", "z"):
        args += [ln_g.reshape(-1, 1, D_MODEL), ln_b.reshape(-1, 1, D_MODEL)]
        specs += [_layer_spec((1, D_MODEL), j), _layer_spec((1, D_MODEL), j)]
    if b is None:
        b = jnp.zeros((w.shape[0], D_MODEL), F32)
    args += [w, b.reshape(-1, 1, D_MODEL)]
    specs += [_layer_spec((k_in, D_MODEL), j), _layer_spec((1, D_MODEL), j)]
    args += [x, st.mod, st.mod, st.mod, norm_ff_g.reshape(-1, 1, D_MODEL), router_wt,
             router_b.reshape(-1, N_EXPERTS, 1)]
    specs += [st.row_spec(D_MODEL), st.mod_spec(2), st.mod_spec(3), st.mod_spec(4),
              _layer_spec((1, D_MODEL), layer), _layer_spec((N_EXPERTS, D_MODEL), layer),
              _layer_spec((N_EXPERTS, 1), layer)]
    scratch = [pltpu.VMEM((k_in, D_MODEL), BF16)]
    if mode == "conv":
        scratch += [pltpu.VMEM((tm + HALO, D_MODEL), F32), pltpu.VMEM((tm, D_MODEL), F32)]
    return pl.pallas_call(
        functools.partial(_mixer_post_kernel, mode, st.tiles_per_seq),
        grid=(st.n_tiles,),
        in_specs=specs,
        out_specs=[st.row_spec(D_MODEL), st.row_spec(D_MODEL),
                   pl.BlockSpec((TOP_K, tm), lambda i: (0, i)),
                   pl.BlockSpec((TOP_K, tm), lambda i: (0, i))],
        out_shape=[jax.ShapeDtypeStruct((st.n_rows, D_MODEL), F32),
                   jax.ShapeDtypeStruct((st.n_rows, D_MODEL), F32),
                   jax.ShapeDtypeStruct((TOP_K, st.n_rows), I32),
                   jax.ShapeDtypeStruct((TOP_K, st.n_rows), F32)],
        scratch_shapes=scratch,
        compiler_params=_cparams("arbitrary"),
        name="mixer_post_" + mode,
    )(*args)


def _pre_ret_kernel(x_ref, sh_ref, sc_ref, g_ref, w_ref, cos_ref, sin_ref,
                    q_ref, k_ref, v_ref, sg_ref):
    h = _rms_mod(x_ref[...], g_ref[...], sc_ref[...], sh_ref[...]).astype(BF16)
    cos = cos_ref[...]
    sin = sin_ref[...]
    half = RET_DK // 2
    qk = RET_HEADS * RET_DK
    vd = RET_HEADS * RET_DV
    for out_ref, base, scale in ((q_ref, 0, 1.0), (k_ref, qk, RET_DK ** -0.5)):
        for hh in range(RET_HEADS):
            c0 = base + hh * RET_DK
            p = jnp.dot(h, w_ref[:, c0:c0 + RET_DK], preferred_element_type=F32)
            p1, p2 = p[:, :half], p[:, half:]
            o0 = hh * RET_DK
            out_ref[:, o0:o0 + half] = ((p1 * cos - p2 * sin) * scale).astype(BF16)
            out_ref[:, o0 + half:o0 + RET_DK] = ((p1 * sin + p2 * cos) * scale).astype(BF16)
    for hh in range(RET_HEADS):
        c0 = 2 * qk + hh * RET_DV
        v_ref[:, hh * RET_DV:(hh + 1) * RET_DV] = jnp.dot(
            h, w_ref[:, c0:c0 + RET_DV], preferred_element_type=F32).astype(BF16)
        g = jnp.dot(h, w_ref[:, vd + c0:vd + c0 + RET_DV], preferred_element_type=F32)
        sg_ref[:, hh * RET_DV:(hh + 1) * RET_DV] = _silu(g).astype(BF16)


def _pre_ret(st, x, norm_g, w_in_b, layer, j, cos, sin, tbl_spec):
    qk = RET_HEADS * RET_DK
    vd = RET_HEADS * RET_DV
    return pl.pallas_call(
        _pre_ret_kernel,
        grid=(st.n_tiles,),
        in_specs=[
            st.row_spec(D_MODEL),
            st.mod_spec(0), st.mod_spec(1),
            _layer_spec((1, D_MODEL), layer),
            _layer_spec((D_MODEL, 2 * qk + 2 * vd), j),
            tbl_spec, tbl_spec,
        ],
        out_specs=[st.row_spec(qk), st.row_spec(qk), st.row_spec(vd), st.row_spec(vd)],
        out_shape=[jax.ShapeDtypeStruct((st.n_rows, qk), BF16),
                   jax.ShapeDtypeStruct((st.n_rows, qk), BF16),
                   jax.ShapeDtypeStruct((st.n_rows, vd), BF16),
                   jax.ShapeDtypeStruct((st.n_rows, vd), BF16)],
        compiler_params=_cparams("parallel"),
        name="pre_ret",
    )(x, st.mod, st.mod, norm_g.reshape(-1, 1, D_MODEL), w_in_b, cos, sin)


def _decay_matrix(lg, ch):
    diff = (lax.broadcasted_iota(I32, (ch, ch), 0)
            - lax.broadcasted_iota(I32, (ch, ch), 1)).astype(F32)
    return jnp.where(diff >= 0.0, jnp.exp(lg * jnp.maximum(diff, 0.0)), 0.0)


def _retention_chunk(valid, lg, dec, q, k, v, sg, s_prev):
    ch = q.shape[0]
    idx = lax.broadcasted_iota(I32, (ch, 1), 0).astype(F32)
    scores = lax.dot_general(q, k, (((1,), (1,)), ((), ())), preferred_element_type=F32) * dec
    o = jnp.dot(scores.astype(BF16), v, preferred_element_type=F32)
    cross = jnp.exp(lg * (idx + 1.0))
    o = o + jnp.dot(q, s_prev.astype(BF16), preferred_element_type=F32) * cross
    kd = (k.astype(F32) * jnp.exp(lg * (valid - 1.0 - idx))).astype(BF16)
    s_new = jnp.exp(lg * float(valid)) * s_prev + lax.dot_general(
        kd, v, (((0,), (0,)), ((), ())), preferred_element_type=F32)
    mu = jnp.mean(o, axis=-1, keepdims=True)
    oc = o - mu
    var = jnp.mean(oc * oc, axis=-1, keepdims=True)
    on = oc * lax.rsqrt(var + NORM_EPS)
    return (sg.astype(F32) * on).astype(BF16), s_new


def _retention_kernel(valid, lg_ref, q_ref, k_ref, v_ref, sg_ref, o_ref, s_out_ref, s_ref, dec_ref):
    c = pl.program_id(1)
    ch = q_ref.shape[0]

    @pl.when(c == 0)
    def _():
        s_ref[...] = jnp.zeros_like(s_ref)
        for h in range(RET_HEADS):
            dec_ref[h] = _decay_matrix(lg_ref[h, 0:1, 0:1], ch)

    for h in range(RET_HEADS):
        ks = slice(h * RET_DK, (h + 1) * RET_DK)
        vs = slice(h * RET_DV, (h + 1) * RET_DV)
        gated, s_new = _retention_chunk(valid, lg_ref[h, 0:1, 0:1], dec_ref[h], q_ref[:, ks],
                                        k_ref[:, ks], v_ref[:, vs], sg_ref[:, vs], s_ref[h])
        o_ref[:, vs] = gated
        s_ref[h] = s_new

    @pl.when(c == pl.num_programs(1) - 1)
    def _():
        s_out_ref[...] = s_ref[...]


def _retention_sample_kernel(lg_ref, q_ref, k_ref, v_ref, sg_ref, s0_ref, o_ref, s_out_ref):
    lg = lg_ref[0:1, 0:1]
    dec = _decay_matrix(lg, RET_PAD)
    for b in range(RET_SB):
        rows = slice(b * RET_PAD, (b + 1) * RET_PAD)
        gated, s_new = _retention_chunk(DEC_SEQ, lg, dec, q_ref[rows, :], k_ref[rows, :],
                                        v_ref[rows, :], sg_ref[rows, :], s0_ref[b])
        o_ref[rows, :] = gated
        s_out_ref[b] = s_new


def _retention_sample(q, k, v, sg, s0, s0_layer, lg_tbl):
    rows = RET_SB * RET_PAD
    row = lambda i, h: (i, h)
    state_block = (None, RET_SB, None, RET_DK, RET_DV)
    return pl.pallas_call(
        _retention_sample_kernel,
        grid=(DEC_BATCH // RET_SB, RET_HEADS),
        in_specs=[
            pl.BlockSpec((None, 8, 128), lambda i, h: (h, 0, 0)),
            pl.BlockSpec((rows, RET_DK), row), pl.BlockSpec((rows, RET_DK), row),
            pl.BlockSpec((rows, RET_DV), row), pl.BlockSpec((rows, RET_DV), row),
            pl.BlockSpec(state_block, lambda i, h: (s0_layer, i, h, 0, 0)),
        ],
        out_specs=[pl.BlockSpec((rows, RET_DV), row),
                   pl.BlockSpec(state_block, lambda i, h: (0, i, h, 0, 0))],
        out_shape=[jax.ShapeDtypeStruct((DEC_BATCH * RET_PAD, RET_HEADS * RET_DV), BF16),
                   jax.ShapeDtypeStruct((1, DEC_BATCH, RET_HEADS, RET_DK, RET_DV), F32)],
        compiler_params=_cparams("parallel", "parallel"),
        name="retention_sample",
    )(lg_tbl, q, k, v, sg, s0)


def _retention(q, k, v, sg, lg_tbl, n_seq, n_chunks, ch, valid):
    n_rows = q.shape[0]
    qk = RET_HEADS * RET_DK
    vd = RET_HEADS * RET_DV
    row = lambda b, c: (b * n_chunks + c, 0)
    state_block = (None, None, RET_HEADS, RET_DK, RET_DV)
    return pl.pallas_call(
        functools.partial(_retention_kernel, valid),
        grid=(n_seq, n_chunks),
        in_specs=[
            pl.BlockSpec((RET_HEADS, 8, 128), lambda b, c: (0, 0, 0)),
            pl.BlockSpec((ch, qk), row), pl.BlockSpec((ch, qk), row),
            pl.BlockSpec((ch, vd), row), pl.BlockSpec((ch, vd), row),
        ],
        out_specs=[pl.BlockSpec((ch, vd), row),
                   pl.BlockSpec(state_block, lambda b, c: (0, b, 0, 0, 0))],
        out_shape=[jax.ShapeDtypeStruct((n_rows, vd), BF16),
                   jax.ShapeDtypeStruct((1, n_seq, RET_HEADS, RET_DK, RET_DV), F32)],
        scratch_shapes=[pltpu.VMEM((RET_HEADS, RET_DK, RET_DV), F32),
                        pltpu.VMEM((RET_HEADS, ch, ch), F32)],
        compiler_params=_cparams("parallel", "arbitrary"),
        name="retention",
    )(lg_tbl, q, k, v, sg)


def _excl_prefix(col):
    r = lax.broadcasted_iota(I32, (N_EXPERTS, N_EXPERTS), 0)
    c = lax.broadcasted_iota(I32, (N_EXPERTS, N_EXPERTS), 1)
    lane = jnp.sum(jnp.where(r < c, col, 0.0), axis=0, keepdims=True)
    return jnp.sum(jnp.where(r == c, lane, 0.0), axis=1, keepdims=True)


def _ceil_to(x, m):
    return jnp.floor((x + (m - 1.0)) * (1.0 / m)) * m


def _route_kernel(te_ref, lpos_ref, rstart_ref, rlen_ref, tot_ref, tri_ref):
    rt = te_ref.shape[2]
    iota_e = lax.broadcasted_iota(I32, (N_EXPERTS, rt), 0)
    s = lax.broadcasted_iota(I32, (rt, rt), 0)
    t = lax.broadcasted_iota(I32, (rt, rt), 1)
    tri_ref[...] = (s < t).astype(BF16)

    def tile_onehots(i):
        te = te_ref[i]
        onehots = [(te[k:k + 1, :] == iota_e).astype(F32) for k in range(TOP_K)]
        cnt = jnp.sum(onehots[0] + onehots[1] + onehots[2] + onehots[3], axis=1, keepdims=True)
        return onehots, _ceil_to(cnt, RUN_ALIGN)

    tot = lax.fori_loop(0, MOE_NT, lambda i, acc: acc + tile_onehots(i)[1],
                        jnp.zeros((N_EXPERTS, 1), F32))
    tot_ref[...] = tot

    def place(i, gbase):
        onehots, run = tile_onehots(i)
        base = _excl_prefix(run)
        for k in range(TOP_K):
            oh = onehots[k]
            before = jnp.dot(oh.astype(BF16), tri_ref[...], preferred_element_type=F32)
            pos = jnp.sum(oh * (base + before), axis=0, keepdims=True)
            lpos_ref[i, k:k + 1, :] = pos.astype(I32)
            base = base + jnp.sum(oh, axis=1, keepdims=True)
        rstart_ref[i] = jnp.broadcast_to(gbase, rstart_ref.shape[1:]).astype(I32)
        rlen_ref[i] = jnp.broadcast_to(run, rlen_ref.shape[1:]).astype(I32)
        return gbase + run

    lax.fori_loop(0, MOE_NT, place, _excl_prefix(_ceil_to(tot, MOE_TB)))


def _route(te):
    full = lambda shape: pl.BlockSpec(shape, lambda i: (0,) * len(shape))
    return pl.pallas_call(
        _route_kernel,
        grid=(1,),
        in_specs=[full(te.shape)],
        out_specs=[full(te.shape), full((MOE_NT, N_EXPERTS, 128)), full((MOE_NT, N_EXPERTS, 128)),
                   full((N_EXPERTS, 1))],
        out_shape=[jax.ShapeDtypeStruct(te.shape, I32),
                   jax.ShapeDtypeStruct((MOE_NT, N_EXPERTS, 128), I32),
                   jax.ShapeDtypeStruct((MOE_NT, N_EXPERTS, 128), I32),
                   jax.ShapeDtypeStruct((N_EXPERTS, 1), F32)],
        scratch_shapes=[pltpu.VMEM((MOE_RT, MOE_RT), BF16)],
        compiler_params=_cparams("arbitrary"),
        name="moe_route",
    )(te)


def _dispatch_kernel(n_prompt_tiles, rs_ref, rl_ref, tzs_ref, tzl_ref, nu_ref, hp_ref, hs_ref,
                     lpos_ref, xb_ref, xs_ref, sem, zero_ref):
    i = pl.program_id(0)
    slot = i % 2
    zero_sem = sem.at[2]

    def tail_copy(e):
        n = pl.multiple_of(tzl_ref[e], RUN_ALIGN)
        dst = pl.multiple_of(tzs_ref[e], RUN_ALIGN)
        return pltpu.make_async_copy(zero_ref.at[pl.ds(0, n)], xb_ref.at[pl.ds(dst, n)], zero_sem)

    def block_copy(j):
        dst = pl.multiple_of(j * MOE_TB, MOE_TB)
        return pltpu.make_async_copy(zero_ref, xb_ref.at[pl.ds(dst, MOE_TB)], zero_sem)

    @pl.when(i == 0)
    def _():
        zero_ref[...] = jnp.zeros_like(zero_ref)

        def start(e, carry):
            pl.when(tzl_ref[e] > 0)(lambda: tail_copy(e).start())
            return carry

        def wait(e, carry):
            pl.when(tzl_ref[e] > 0)(lambda: tail_copy(e).wait())
            return carry

        lax.fori_loop(0, N_EXPERTS, start, 0)
        lax.fori_loop(nu_ref[0], MOE_NBLK, lambda j, c: (block_copy(j).start(), c)[1], 0)
        lax.fori_loop(0, N_EXPERTS, wait, 0)
        lax.fori_loop(nu_ref[0], MOE_NBLK, lambda j, c: (block_copy(j).wait(), c)[1], 0)

    rows = lax.broadcasted_iota(I32, (MOE_CAP, MOE_RT), 0)
    perm = jnp.where(rows == lpos_ref[0:1, :], 1.0, 0.0)
    for k in range(1, TOP_K):
        perm = perm + jnp.where(rows == lpos_ref[k:k + 1, :], 1.0, 0.0)
    h = jnp.where(i < n_prompt_tiles, hp_ref[...], hs_ref[...])
    xs_ref[slot] = _pack_halves(
        jnp.dot(perm.astype(BF16), h.astype(BF16), preferred_element_type=F32))

    def scatter(tile, wait):
        def body(e, lo):
            n = pl.multiple_of(rl_ref[tile * N_EXPERTS + e], RUN_ALIGN)
            sl = tile % 2
            src = pl.multiple_of(lo, RUN_ALIGN)
            dst = pl.multiple_of(rs_ref[tile * N_EXPERTS + e], RUN_ALIGN)
            cp = pltpu.make_async_copy(xs_ref.at[sl, pl.ds(src, n)], xb_ref.at[pl.ds(dst, n)],
                                       sem.at[sl])

            @pl.when(n > 0)
            def _():
                if wait:
                    cp.wait()
                else:
                    cp.start()

            return lo + n

        lax.fori_loop(0, N_EXPERTS, body, 0)

    scatter(i, False)

    @pl.when(i > 0)
    def _():
        scatter(i - 1, True)

    @pl.when(i == pl.num_programs(0) - 1)
    def _():
        scatter(i, True)


def _dispatch(tables, h_p, h_s, lpos):
    nt_p = h_p.shape[0] // MOE_RT
    return pl.pallas_call(
        functools.partial(_dispatch_kernel, nt_p),
        grid_spec=pltpu.PrefetchScalarGridSpec(
            num_scalar_prefetch=len(tables),
            grid=(MOE_NT,),
            in_specs=[
                pl.BlockSpec((MOE_RT, D_MODEL), lambda i, *_: (jnp.minimum(i, nt_p - 1), 0)),
                pl.BlockSpec((MOE_RT, D_MODEL), lambda i, *_: (jnp.maximum(i - nt_p, 0), 0)),
                pl.BlockSpec((None, TOP_K, MOE_RT), lambda i, *_: (i, 0, 0)),
            ],
            out_specs=pl.BlockSpec(memory_space=pl.ANY),
            scratch_shapes=[pltpu.VMEM((2, MOE_CAP, D_MODEL // 2), U32),
                            pltpu.SemaphoreType.DMA((3,)),
                            pltpu.VMEM((MOE_TB, D_MODEL // 2), U32)],
        ),
        out_shape=jax.ShapeDtypeStruct((MOE_ROWS, D_MODEL // 2), U32),
        compiler_params=_cparams("arbitrary"),
        name="moe_dispatch",
    )(*tables, h_p, h_s, lpos)


def _ffn_kernel(layer, be_ref, nu_ref, nxt_ref, bgu_ref, bdn_ref, xb_hbm, wgu_hbm, wdn_hbm,
                yb_hbm, wgu_f_ref, wdn_f_ref, wgu_b_ref, wdn_b_ref, act_ref, xbuf_ref, ybuf_ref,
                wsem, xsem, ysem):
    n_used = nu_ref[0]

    def weight_copies(ex):
        return (pltpu.make_async_copy(wgu_hbm.at[layer, ex], wgu_f_ref, wsem.at[0]),
                pltpu.make_async_copy(wdn_hbm.at[layer, ex], wdn_f_ref, wsem.at[1]))

    def block_rows(j):
        return pl.ds(pl.multiple_of(j * MOE_TB, MOE_TB), MOE_TB)

    def x_copy(j):
        return pltpu.make_async_copy(xb_hbm.at[block_rows(j)], xbuf_ref.at[j % 2], xsem.at[j % 2])

    def y_copy(j):
        return pltpu.make_async_copy(ybuf_ref.at[j % 2], yb_hbm.at[block_rows(j)], ysem.at[j % 2])

    for cp in weight_copies(be_ref[0]):
        cp.start()
    x_copy(0).start()

    def block(j, carry):
        e = be_ref[j]
        e_prev = be_ref[jnp.maximum(j - 1, 0)]

        @pl.when(j + 1 < n_used)
        def _():
            x_copy(j + 1).start()

        @pl.when((j == 0) | (e != e_prev))
        def _():
            for cp in weight_copies(e):
                cp.wait()
            wgu_b_ref[...] = wgu_f_ref[...].astype(BF16)
            wdn_b_ref[...] = wdn_f_ref[...].astype(BF16)
            nx = nxt_ref[j]

            @pl.when(nx >= 0)
            def _():
                for cp in weight_copies(nx):
                    cp.start()

        x_copy(j).wait()

        @pl.when(j >= 2)
        def _():
            y_copy(j - 2).wait()

        x = _unpack_halves(xbuf_ref[j % 2])
        bgu = bgu_ref[e]
        for c in range(D_FF // FF_CHUNK):
            gs = slice(c * FF_CHUNK, (c + 1) * FF_CHUNK)
            us = slice(D_FF + c * FF_CHUNK, D_FF + (c + 1) * FF_CHUNK)
            gate = jnp.dot(x, wgu_b_ref[:, gs], preferred_element_type=F32) + bgu[:, gs]
            up = jnp.dot(x, wgu_b_ref[:, us], preferred_element_type=F32) + bgu[:, us]
            gate = jnp.minimum(gate, SWIGLU_LIMIT)
            up = jnp.clip(up, -SWIGLU_LIMIT, SWIGLU_LIMIT)
            act = (up + 1.0) * (gate * jax.nn.sigmoid(SWIGLU_ALPHA * gate))
            act_ref[:, gs] = act.astype(BF16)
        y = jnp.dot(act_ref[...], wdn_b_ref[...], preferred_element_type=F32) + bdn_ref[e]
        ybuf_ref[j % 2] = _pack_halves(y.astype(BF16).astype(F32))
        y_copy(j).start()
        return carry

    lax.fori_loop(0, n_used, block, 0)

    @pl.when(n_used >= 2)
    def _():
        y_copy(n_used - 2).wait()

    y_copy(n_used - 1).wait()

    ybuf_ref[0] = jnp.zeros_like(ybuf_ref[0])

    def zero_copy(j):
        return pltpu.make_async_copy(ybuf_ref.at[0], yb_hbm.at[block_rows(j)], ysem.at[0])

    lax.fori_loop(n_used, MOE_NBLK, lambda j, c: (zero_copy(j).start(), c)[1], 0)
    lax.fori_loop(n_used, MOE_NBLK, lambda j, c: (zero_copy(j).wait(), c)[1], 0)


def _ffn(layer, block_e, n_used, nxt, xb, w_gu, b_gu, w_dn, b_dn):
    bias_map = lambda i, be, nu, nx: (layer, 0, 0, 0)
    return pl.pallas_call(
        functools.partial(_ffn_kernel, layer),
        grid_spec=pltpu.PrefetchScalarGridSpec(
            num_scalar_prefetch=3,
            grid=(1,),
            in_specs=[
                pl.BlockSpec((None, N_EXPERTS, 1, 2 * D_FF), bias_map),
                pl.BlockSpec((None, N_EXPERTS, 1, D_MODEL), bias_map),
                pl.BlockSpec(memory_space=pl.ANY),
                pl.BlockSpec(memory_space=pl.ANY),
                pl.BlockSpec(memory_space=pl.ANY),
            ],
            out_specs=pl.BlockSpec(memory_space=pl.ANY),
            scratch_shapes=[pltpu.VMEM((D_MODEL, 2 * D_FF), F32),
                            pltpu.VMEM((D_FF, D_MODEL), F32),
                            pltpu.VMEM((D_MODEL, 2 * D_FF), BF16),
                            pltpu.VMEM((D_FF, D_MODEL), BF16),
                            pltpu.VMEM((MOE_TB, D_FF), BF16),
                            pltpu.VMEM((2, MOE_TB, D_MODEL // 2), U32),
                            pltpu.VMEM((2, MOE_TB, D_MODEL // 2), U32),
                            pltpu.SemaphoreType.DMA((2,)),
                            pltpu.SemaphoreType.DMA((2,)),
                            pltpu.SemaphoreType.DMA((2,))],
        ),
        out_shape=jax.ShapeDtypeStruct((MOE_ROWS, D_MODEL // 2), U32),
        compiler_params=_cparams("arbitrary"),
        name="moe_ffn",
    )(block_e, n_used, nxt, b_gu.reshape(b_gu.shape[0], N_EXPERTS, 1, 2 * D_FF),
      b_dn.reshape(b_dn.shape[0], N_EXPERTS, 1, D_MODEL), xb, w_gu, w_dn)


def _tile_rows(v, n_rows):
    mr = v.shape[0]
    if mr in (1, n_rows):
        return v
    return jnp.concatenate([v] * (n_rows // mr), axis=0)


def _combine_kernel(tile_off, final, rs_ref, rl_ref, x_ref, gf_ref, lpt_ref, wt_ref, fg_ref,
                    yb_ref, out_ref, ys_ref, sem):
    i = pl.program_id(0)

    def gather(tile, wait):
        def body(e, lo):
            n = pl.multiple_of(rl_ref[(tile_off + tile) * N_EXPERTS + e], RUN_ALIGN)
            sl = tile % 2
            src = pl.multiple_of(rs_ref[(tile_off + tile) * N_EXPERTS + e], RUN_ALIGN)
            dst = pl.multiple_of(lo, RUN_ALIGN)
            cp = pltpu.make_async_copy(yb_ref.at[pl.ds(src, n)], ys_ref.at[sl, pl.ds(dst, n)],
                                       sem.at[sl])

            @pl.when(n > 0)
            def _():
                if wait:
                    cp.wait()
                else:
                    cp.start()

            return lo + n

        lax.fori_loop(0, N_EXPERTS, body, 0)

    @pl.when(i == 0)
    def _():
        ys_ref[...] = jnp.zeros_like(ys_ref)
        gather(0, False)

    @pl.when(i + 1 < pl.num_programs(0))
    def _():
        gather(i + 1, False)

    gather(i, True)

    cols = lax.broadcasted_iota(I32, (MOE_RT, MOE_CAP), 1)
    w = jnp.where(cols == lpt_ref[:, 0:1], wt_ref[:, 0:1], 0.0)
    for k in range(1, TOP_K):
        w = w + jnp.where(cols == lpt_ref[:, k:k + 1], wt_ref[:, k:k + 1], 0.0)
    ys = _unpack_halves(ys_ref[i % 2])
    y = jnp.dot(w.astype(BF16), ys, preferred_element_type=F32)
    xn = x_ref[...] + _tile_rows(gf_ref[...], MOE_RT) * y
    if final:
        xn = xn * lax.rsqrt(jnp.mean(xn * xn, axis=-1, keepdims=True) + NORM_EPS) * fg_ref[...]
    out_ref[...] = xn


def _combine(tile_off, n_tiles, final, tables, x, mod, mod_map, lpos_t, wt, final_g, yb):
    pm = lambda i, *_: (i, 0)
    tm = lambda i, *_: (i + tile_off, 0)
    return pl.pallas_call(
        functools.partial(_combine_kernel, tile_off, final),
        grid_spec=pltpu.PrefetchScalarGridSpec(
            num_scalar_prefetch=len(tables),
            grid=(n_tiles,),
            in_specs=[
                pl.BlockSpec((MOE_RT, D_MODEL), pm),
                pl.BlockSpec((None, mod.shape[1], D_MODEL), mod_map),
                pl.BlockSpec((MOE_RT, TOP_K), tm),
                pl.BlockSpec((MOE_RT, TOP_K), tm),
                pl.BlockSpec((1, D_MODEL), lambda i, *_: (0, 0)),
                pl.BlockSpec(memory_space=pl.ANY),
            ],
            out_specs=pl.BlockSpec((MOE_RT, D_MODEL), pm),
            scratch_shapes=[pltpu.VMEM((2, MOE_CAP, D_MODEL // 2), U32),
                            pltpu.SemaphoreType.DMA((2,))],
        ),
        out_shape=jax.ShapeDtypeStruct(x.shape, F32),
        compiler_params=_cparams("arbitrary"),
        name="moe_combine",
    )(*tables, x, mod, lpos_t, wt, final_g.reshape(1, D_MODEL), yb)


def _moe(layer, final, st_p, st_s, x_p, x_s, post_p, post_s,
         w_gu, b_gu, w_dn, b_dn, final_g):
    te = jnp.concatenate([post_p[1], post_s[1]], axis=1)
    tw = jnp.concatenate([post_p[2], post_s[2]], axis=1)
    te = jnp.swapaxes(te.reshape(TOP_K, MOE_NT, MOE_RT), 0, 1)
    lpos, rstart, rlen, tot = _route(te)
    tot = tot[:, 0].astype(I32)
    padded = (tot + MOE_TB - 1) // MOE_TB * MOE_TB
    pends = jnp.cumsum(padded)
    n_used = (pends[-1] // MOE_TB).reshape(1)
    blk = jnp.minimum(jnp.arange(MOE_NBLK, dtype=I32), n_used[0] - 1) * MOE_TB
    block_e = jnp.minimum(jnp.sum(pends[None, :] <= blk[:, None], axis=-1),
                          N_EXPERTS - 1).astype(I32)
    ids = jnp.where(tot > 0, jnp.arange(N_EXPERTS, dtype=I32), N_EXPERTS)
    later = jnp.concatenate([lax.cummin(ids[::-1])[::-1][1:], jnp.full((1,), N_EXPERTS, I32)])
    nxt = jnp.where(later < N_EXPERTS, later, -1)[block_e]
    run_tables = (rstart[:, :, 0].reshape(-1), rlen[:, :, 0].reshape(-1))
    tail_tables = (pends - padded + tot, padded - tot)

    nt_p = N_PROMPT // MOE_RT
    nt_s = N_SAMPLE // MOE_RT
    xb = _dispatch(run_tables + tail_tables + (n_used,), post_p[0], post_s[0], lpos)
    yb = _ffn(layer, block_e, n_used, nxt, xb, w_gu, b_gu, w_dn, b_dn)
    lpos_t = jnp.swapaxes(lpos, 1, 2).reshape(N_TOK, TOP_K)
    wt = tw.T
    tiles_per_seq = SEQ // MOE_RT
    out_p = _combine(0, nt_p, final, run_tables, x_p, st_p.mod,
                     lambda i, *_: (i // tiles_per_seq, 0, N_MOD - 1), lpos_t, wt, final_g, yb)
    out_s = _combine(nt_p, nt_s, final, run_tables, x_s, st_s.mod,
                     lambda i, *_: (0, 0, N_MOD - 1), lpos_t, wt, final_g, yb)
    return out_p, out_s


def _rope_tables(pos):
    half = RET_DK // 2
    inv_freq = ROPE_BASE ** (-jnp.arange(half, dtype=F32) / half)
    ang = pos.astype(F32)[:, None] * inv_freq[None, :]
    return jnp.cos(ang), jnp.sin(ang)


def kernel(x_prompt, x_sample, c_prompt, c_sample, state_conv, state_ret, norm_mix_g, norm_ff_g,
           w_mod, b_mod, conv_w1, conv_b1, conv_dw, conv_dw_b, conv_ln_g, conv_ln_b, conv_w2,
           conv_b2, ret_w_in, ret_w_o, router_w, router_b, moe_w_gu, moe_b_gu, moe_w_dn,
           moe_b_dn, final_g):
    mod = _adaln(jnp.concatenate([c_prompt, c_sample], axis=0), w_mod, b_mod)
    x_p = x_prompt.reshape(N_PROMPT, D_MODEL)
    x_s = jnp.swapaxes(x_sample, 0, 1).reshape(N_SAMPLE, D_MODEL)
    router_wt = jnp.swapaxes(router_w, 1, 2)

    def streams(layer):
        st_p = _Stream(N_PROMPT, TM_PROMPT, SEQ // TM_PROMPT,
                       mod[layer, :BATCH].reshape(BATCH, 1, N_MOD * D_MODEL))
        st_s = _Stream(N_SAMPLE, TM_SAMPLE, None,
                       mod[layer, BATCH:].reshape(1, DEC_BATCH, N_MOD * D_MODEL))
        return st_p, st_s

    st_p, st_s = streams(0)
    u_p = _pre_conv(st_p, x_p, norm_mix_g, conv_w1, conv_b1, 0, 0)
    u_s = _pre_conv(st_s, x_s, norm_mix_g, conv_w1, conv_b1, 0, 0)
    buf_tm = jnp.swapaxes(state_conv[0], 0, 1)
    z_s, ns_tm = _conv_sample(buf_tm, u_s.reshape(DEC_SEQ, DEC_BATCH, D_MODEL),
                              conv_dw, conv_dw_b, 0)
    post_p = _mixer_post("conv", st_p, 0, 0, x_p, u_p, norm_ff_g, router_wt, router_b,
                         conv_w2, conv_b2, conv_dw, conv_dw_b, conv_ln_g, conv_ln_b)
    post_s = _mixer_post("z", st_s, 0, 0, x_s, z_s.reshape(N_SAMPLE, D_MODEL), norm_ff_g,
                         router_wt, router_b, conv_w2, conv_b2, None, None, conv_ln_g, conv_ln_b)
    x_p, x_s = _moe(0, False, st_p, st_s, post_p[0], post_s[0], post_p[1:], post_s[1:],
                    moe_w_gu, moe_b_gu, moe_w_dn, moe_b_dn, final_g)
    conv_p = u_p.reshape(BATCH, SEQ, D_MODEL)[:, SEQ - CONV_STATE:][None]
    conv_s = jnp.swapaxes(ns_tm, 0, 1)[None]

    st_p, st_s = streams(1)
    w_in_b = ret_w_in.astype(BF16)
    cos_p, sin_p = _rope_tables(jnp.arange(SEQ, dtype=I32))
    cos_s, sin_s = _rope_tables(PAST_LEN + jnp.arange(DEC_SEQ, dtype=I32))
    half = RET_DK // 2
    tps = SEQ // TM_PROMPT
    tbl_p = pl.BlockSpec((TM_PROMPT, half), lambda i: (i % tps, 0))
    tbl_s = pl.BlockSpec((None, 1, half), lambda i: (i, 0, 0))
    q_p, k_p, v_p, sg_p = _pre_ret(st_p, x_p, norm_mix_g, w_in_b, 1, 0, cos_p, sin_p, tbl_p)
    q_s, k_s, v_s, sg_s = _pre_ret(st_s, x_s, norm_mix_g, w_in_b, 1, 0,
                                   cos_s.reshape(DEC_SEQ, 1, half),
                                   sin_s.reshape(DEC_SEQ, 1, half), tbl_s)
    lg = jnp.log1p(-jnp.exp2(-5.0 - jnp.arange(RET_HEADS, dtype=F32)))
    lg_tbl = jnp.broadcast_to(lg[:, None, None], (RET_HEADS, 8, 128))
    gated_p, ret_p = _retention(q_p, k_p, v_p, sg_p, lg_tbl,
                                BATCH, SEQ // RET_CHUNK, RET_CHUNK, RET_CHUNK)

    def to_seq_major(a):
        a = jnp.swapaxes(a.reshape(DEC_SEQ, DEC_BATCH, -1), 0, 1)
        a = jnp.pad(a, ((0, 0), (0, RET_PAD - DEC_SEQ), (0, 0)))
        return a.reshape(DEC_BATCH * RET_PAD, -1)

    gated_s, ret_s = _retention_sample(to_seq_major(q_s), to_seq_major(k_s), to_seq_major(v_s),
                                       to_seq_major(sg_s), state_ret, 0, lg_tbl)
    gated_s = jnp.swapaxes(gated_s.reshape(DEC_BATCH, RET_PAD, -1)[:, :DEC_SEQ], 0, 1)
    gated_s = gated_s.reshape(N_SAMPLE, -1)
    post_p = _mixer_post("ret", st_p, 1, 0, x_p, gated_p, norm_ff_g, router_wt, router_b, ret_w_o)
    post_s = _mixer_post("ret", st_s, 1, 0, x_s, gated_s, norm_ff_g, router_wt, router_b, ret_w_o)
    y_p, y_s = _moe(1, True, st_p, st_s, post_p[0], post_s[0], post_p[1:], post_s[1:],
                    moe_w_gu, moe_b_gu, moe_w_dn, moe_b_dn, final_g)

    y_prompt = y_p.reshape(BATCH, SEQ, D_MODEL)
    y_sample = jnp.swapaxes(y_s.reshape(DEC_SEQ, DEC_BATCH, D_MODEL), 0, 1)
    return (y_prompt, y_sample, conv_p, conv_s, ret_p, ret_s)
```

```python
import functools

import jax
import jax.numpy as jnp
from jax import lax
from jax.experimental import pallas as pl
from jax.experimental.pallas import tpu as pltpu

F32 = jnp.float32
BF16 = jnp.bfloat16
I32 = jnp.int32
U32 = jnp.uint32

D_MODEL = 1024
SEQ = 2048
BATCH = 8
DEC_BATCH = 128
DEC_SEQ = 4
PAST_LEN = 16384
CONV_WIDTH = 31
CONV_STATE = CONV_WIDTH - 1
RET_HEADS = 4
RET_DK = 256
RET_DV = 512
ROPE_BASE = 10000.0
N_EXPERTS = 32
TOP_K = 4
D_FF = 1024
SWIGLU_LIMIT = 7.0
SWIGLU_ALPHA = 1.702
N_MOD = 6
NORM_EPS = 1e-5

N_PROMPT = BATCH * SEQ
N_SAMPLE = DEC_BATCH * DEC_SEQ
N_TOK = N_PROMPT + N_SAMPLE
N_ASSIGN = N_TOK * TOP_K

TM_PROMPT = 512
TM_SAMPLE = DEC_BATCH
HALO = 32
CONV_RB = 64
CONV_LW = 128
RET_CHUNK = 256
RET_PAD = 16
RET_SB = 16
MOE_TB = 512
MOE_RT = 256
MOE_NT = N_TOK // MOE_RT
RUN_ALIGN = 8
MOE_CAP = -(-(TOP_K * MOE_RT + N_EXPERTS * (RUN_ALIGN - 1)) // 256) * 256
MOE_MAX_ROWS = N_ASSIGN + MOE_NT * N_EXPERTS * (RUN_ALIGN - 1)
MOE_NBLK = -(-MOE_MAX_ROWS // MOE_TB) + N_EXPERTS
MOE_ROWS = MOE_NBLK * MOE_TB
FF_CHUNK = 256
VMEM_LIMIT = 56 * 1024 * 1024


def _cparams(*sem):
    return pltpu.CompilerParams(dimension_semantics=sem, vmem_limit_bytes=VMEM_LIMIT)


def _silu(x):
    return x * jax.nn.sigmoid(x)


def _rms_mod(x, g, sc, sh):
    y = x * lax.rsqrt(jnp.mean(x * x, axis=-1, keepdims=True) + NORM_EPS) * g
    return y * (1.0 + sc) + sh


def _split_bf16(x):
    hi = x.astype(BF16)
    lo = (x - hi.astype(F32)).astype(BF16)
    return hi, lo


def _pack_halves(x):
    c = x.shape[1] // 2
    lo = lax.bitcast_convert_type(x[:, :c], U32)
    hi = lax.bitcast_convert_type(x[:, c:], U32)
    return (lo >> 16) | (hi & jnp.uint32(0xFFFF0000))


def _unpack_halves(p):
    lo = lax.bitcast_convert_type(p << 16, F32)
    hi = lax.bitcast_convert_type(p & jnp.uint32(0xFFFF0000), F32)
    return jnp.concatenate([lo.astype(BF16), hi.astype(BF16)], axis=1)


def _adaln_kernel(c_ref, w_ref, b_ref, o_ref):
    a = _silu(c_ref[...]).astype(BF16)
    o_ref[...] = jnp.dot(a, w_ref[...].astype(BF16), preferred_element_type=F32) + b_ref[...]


def _adaln(c_all, w_mod, b_mod):
    depth, d, nm = w_mod.shape
    n = c_all.shape[0]
    return pl.pallas_call(
        _adaln_kernel,
        grid=(depth, nm // d),
        in_specs=[
            pl.BlockSpec((n, d), lambda l, j: (0, 0)),
            pl.BlockSpec((None, d, d), lambda l, j: (l, 0, j)),
            pl.BlockSpec((None, 1, d), lambda l, j: (l, 0, j)),
        ],
        out_specs=pl.BlockSpec((None, n, d), lambda l, j: (l, 0, j)),
        out_shape=jax.ShapeDtypeStruct((depth, n, nm), F32),
        compiler_params=_cparams("parallel", "parallel"),
        name="adaln",
    )(c_all, w_mod, b_mod.reshape(depth, 1, nm))


class _Stream:
    def __init__(self, n_rows, tm, tiles_per_seq, mod):
        self.n_rows = n_rows
        self.tm = tm
        self.n_tiles = n_rows // tm
        self.tiles_per_seq = tiles_per_seq
        self.mod = mod

    def mod_spec(self, j):
        mr = self.mod.shape[1]
        if self.tiles_per_seq is None:
            return pl.BlockSpec((None, mr, D_MODEL), lambda i: (0, 0, j))
        tps = self.tiles_per_seq
        return pl.BlockSpec((None, mr, D_MODEL), lambda i: (i // tps, 0, j))

    def row_spec(self, width, col=0):
        return pl.BlockSpec((self.tm, width), lambda i: (i, col))


def _const_spec(shape):
    nd = len(shape)
    return pl.BlockSpec(shape, lambda i: (0,) * nd)


def _layer_spec(shape, layer):
    nd = len(shape)
    return pl.BlockSpec((None,) + shape, lambda i: (layer,) + (0,) * nd)


def _pre_conv_kernel(x_ref, sh_ref, sc_ref, g_ref, w1_ref, b1_ref, u_ref, w1b_ref):
    @pl.when(pl.program_id(0) == 0)
    def _():
        w1b_ref[...] = w1_ref[...].astype(BF16)

    h = _rms_mod(x_ref[...], g_ref[...], sc_ref[...], sh_ref[...])
    ag = jnp.dot(h.astype(BF16), w1b_ref[...], preferred_element_type=F32) + b1_ref[...]
    u_ref[...] = ag[:, :D_MODEL] * jax.nn.sigmoid(ag[:, D_MODEL:])


def _pre_conv(st, x, norm_g, w1, b1, layer, j):
    c2 = w1.shape[-1]
    return pl.pallas_call(
        _pre_conv_kernel,
        grid=(st.n_tiles,),
        in_specs=[
            st.row_spec(D_MODEL),
            st.mod_spec(0), st.mod_spec(1),
            _layer_spec((1, D_MODEL), layer),
            _layer_spec((D_MODEL, c2), j),
            _layer_spec((1, c2), j),
        ],
        out_specs=st.row_spec(D_MODEL),
        out_shape=jax.ShapeDtypeStruct((st.n_rows, D_MODEL), F32),
        scratch_shapes=[pltpu.VMEM((D_MODEL, c2), BF16)],
        compiler_params=_cparams("arbitrary"),
        name="pre_conv",
    )(x, st.mod, st.mod, norm_g.reshape(-1, 1, D_MODEL), w1, b1.reshape(-1, 1, c2))


def _conv_sample_kernel(buf_ref, u_ref, dw_ref, dwb_ref, z_ref, ns_ref):
    for t in range(DEC_SEQ):
        acc = jnp.broadcast_to(dwb_ref[...], u_ref.shape[1:])
        for j in range(t, CONV_STATE):
            acc = acc + buf_ref[j] * dw_ref[j - t:j - t + 1, :]
        for s in range(t + 1):
            k = CONV_STATE + s - t
            acc = acc + u_ref[s] * dw_ref[k:k + 1, :]
        z_ref[t] = acc
    ns_ref[0:CONV_STATE - DEC_SEQ] = buf_ref[DEC_SEQ:CONV_STATE]
    ns_ref[CONV_STATE - DEC_SEQ:CONV_STATE] = u_ref[...]


def _conv_sample(buf_tm, u_tm, dw, dwb, j):
    sb = 32
    c = u_tm.shape[-1]
    return pl.pallas_call(
        _conv_sample_kernel,
        grid=(DEC_BATCH // sb,),
        in_specs=[
            pl.BlockSpec((CONV_STATE, sb, c), lambda i: (0, i, 0)),
            pl.BlockSpec((DEC_SEQ, sb, c), lambda i: (0, i, 0)),
            _layer_spec((CONV_WIDTH, c), j),
            _layer_spec((1, c), j),
        ],
        out_specs=[
            pl.BlockSpec((DEC_SEQ, sb, c), lambda i: (0, i, 0)),
            pl.BlockSpec((CONV_STATE, sb, c), lambda i: (0, i, 0)),
        ],
        out_shape=[
            jax.ShapeDtypeStruct((DEC_SEQ, DEC_BATCH, c), F32),
            jax.ShapeDtypeStruct((CONV_STATE, DEC_BATCH, c), F32),
        ],
        compiler_params=_cparams("parallel"),
        name="conv_sample",
    )(buf_tm, u_tm, dw, dwb.reshape(-1, 1, c))


def _route_topk(h2, rwt_ref, rb_ref, te_ref, tw_ref):
    h_hi, h_lo = _split_bf16(h2)
    w_hi, w_lo = _split_bf16(rwt_ref[...])
    nt = (((1,), (1,)), ((), ()))
    logits = (lax.dot_general(w_hi, h_hi, nt, preferred_element_type=F32)
              + lax.dot_general(w_hi, h_lo, nt, preferred_element_type=F32)
              + lax.dot_general(w_lo, h_hi, nt, preferred_element_type=F32)
              + rb_ref[...])
    iota_e = lax.broadcasted_iota(I32, logits.shape, 0).astype(F32)
    vals, idxs = [], []
    for _ in range(TOP_K):
        m = jnp.max(logits, axis=0, keepdims=True)
        idx = jnp.min(jnp.where(logits == m, iota_e, float(N_EXPERTS)), axis=0, keepdims=True)
        logits = jnp.where(iota_e == idx, -jnp.inf, logits)
        vals.append(m)
        idxs.append(idx)
    es = [jnp.exp(v - vals[0]) for v in vals]
    tot = es[0] + es[1] + es[2] + es[3]
    for k in range(TOP_K):
        te_ref[k:k + 1, :] = idxs[k].astype(I32)
        tw_ref[k:k + 1, :] = es[k] / tot


def _mixer_post_kernel(mode, tiles_per_seq, *refs):
    it = iter(refs)
    if mode == "conv":
        u_ref, halo_ref, dw_ref, dwb_ref = next(it), next(it), next(it), next(it)
    elif mode == "z":
        z_in_ref = next(it)
    else:
        gated_ref = next(it)
    if mode in ("conv", "z"):
        lng_ref, lnb_ref = next(it), next(it)
    w_ref, b_ref = next(it), next(it)
    x_ref, gm_ref, shf_ref, scf_ref, gff_ref, rwt_ref, rb_ref = (next(it) for _ in range(7))
    x1_ref, h2_ref, te_ref, tw_ref = (next(it) for _ in range(4))
    wb_ref = next(it)
    if mode == "conv":
        ext_ref, z_ref = next(it), next(it)

    i = pl.program_id(0)

    @pl.when(i == 0)
    def _():
        wb_ref[...] = w_ref[...].astype(BF16)

    if mode == "conv":
        tm = u_ref.shape[0]
        first = (i % tiles_per_seq) == 0
        ext_ref[0:HALO, :] = jnp.where(first, 0.0, halo_ref[...])
        ext_ref[HALO:HALO + tm, :] = u_ref[...]
        off = HALO - CONV_STATE

        def row_block(r, carry):
            r0 = pl.multiple_of(r * CONV_RB, CONV_RB)
            for lc in range(D_MODEL // CONV_LW):
                ls = slice(lc * CONV_LW, (lc + 1) * CONV_LW)
                win = ext_ref[pl.ds(r0, CONV_RB + HALO), ls]
                acc = jnp.broadcast_to(dwb_ref[:, ls], (CONV_RB, CONV_LW))
                for s in range(8):
                    ws = win if s == 0 else pltpu.roll(win, CONV_RB + HALO - s, 0)
                    for k in range(CONV_WIDTH):
                        if (off + k) % 8 == s:
                            a = off + k - s
                            acc = acc + ws[a:a + CONV_RB, :] * dw_ref[k:k + 1, ls]
                z_ref[pl.ds(r0, CONV_RB), ls] = acc
            return carry

        lax.fori_loop(0, tm // CONV_RB, row_block, 0)
        z = z_ref[...]
    elif mode == "z":
        z = z_in_ref[...]

    if mode in ("conv", "z"):
        mu = jnp.mean(z, axis=-1, keepdims=True)
        zc = z - mu
        var = jnp.mean(zc * zc, axis=-1, keepdims=True)
        zn = zc * lax.rsqrt(var + NORM_EPS) * lng_ref[...] + lnb_ref[...]
        a = _silu(zn).astype(BF16)
        y = jnp.dot(a, wb_ref[...], preferred_element_type=F32) + b_ref[...]
    else:
        y = jnp.dot(gated_ref[...], wb_ref[...], preferred_element_type=F32)

    x1 = x_ref[...] + gm_ref[...] * y
    x1_ref[...] = x1
    h2 = _rms_mod(x1, gff_ref[...], scf_ref[...], shf_ref[...])
    h2_ref[...] = h2
    _route_topk(h2, rwt_ref, rb_ref, te_ref, tw_ref)


def _mixer_post(mode, st, layer, j, x, mix_in, norm_ff_g, router_wt, router_b, w, b=None,
                dw=None, dwb=None, ln_g=None, ln_b=None):
    tm = st.tm
    k_in = w.shape[-2]
    args, specs = [], []
    if mode == "conv":
        per32 = tm // HALO
        args += [mix_in, mix_in, dw, dwb.reshape(-1, 1, D_MODEL)]
        specs += [st.row_spec(D_MODEL),
                  pl.BlockSpec((HALO, D_MODEL), lambda i: (jnp.maximum(i * per32 - 1, 0), 0)),
                  _layer_spec((CONV_WIDTH, D_MODEL), j), _layer_spec((1, D_MODEL), j)]
    elif mode == "z":
        args += [mix_in]
        specs += [st.row_spec(D_MODEL)]
    else:
        args += [mix_in]
        specs += [st.row_spec(k_in)]
    if mode in ("conv", "z"):
        args += [ln_g.reshape(-1, 1, D_MODEL), ln_b.reshape(-1, 1, D_MODEL)]
        specs += [_layer_spec((1, D_MODEL), j), _layer_spec((1, D_MODEL), j)]
    if b is None:
        b = jnp.zeros((w.shape[0], D_MODEL), F32)
    args += [w, b.reshape(-1, 1, D_MODEL)]
    specs += [_layer_spec((k_in, D_MODEL), j), _layer_spec((1, D_MODEL), j)]
    args += [x, st.mod, st.mod, st.mod, norm_ff_g.reshape(-1, 1, D_MODEL), router_wt,
             router_b.reshape(-1, N_EXPERTS, 1)]
    specs += [st.row_spec(D_MODEL), st.mod_spec(2), st.mod_spec(3), st.mod_spec(4),
              _layer_spec((1, D_MODEL), layer), _layer_spec((N_EXPERTS, D_MODEL), layer),
              _layer_spec((N_EXPERTS, 1), layer)]
    scratch = [pltpu.VMEM((k_in, D_MODEL), BF16)]
    if mode == "conv":
        scratch += [pltpu.VMEM((tm + HALO, D_MODEL), F32), pltpu.VMEM((tm, D_MODEL), F32)]
    return pl.pallas_call(
        functools.partial(_mixer_post_kernel, mode, st.tiles_per_seq),
        grid=(st.n_tiles,),
        in_specs=specs,
        out_specs=[st.row_spec(D_MODEL), st.row_spec(D_MODEL),
                   pl.BlockSpec((TOP_K, tm), lambda i: (0, i)),
                   pl.BlockSpec((TOP_K, tm), lambda i: (0, i))],
        out_shape=[jax.ShapeDtypeStruct((st.n_rows, D_MODEL), F32),
                   jax.ShapeDtypeStruct((st.n_rows, D_MODEL), F32),
                   jax.ShapeDtypeStruct((TOP_K, st.n_rows), I32),
                   jax.ShapeDtypeStruct((TOP_K, st.n_rows), F32)],
        scratch_shapes=scratch,
        compiler_params=_cparams("arbitrary"),
        name="mixer_post_" + mode,
    )(*args)


def _pre_ret_kernel(x_ref, sh_ref, sc_ref, g_ref, w_ref, cos_ref, sin_ref,
                    q_ref, k_ref, v_ref, sg_ref):
    h = _rms_mod(x_ref[...], g_ref[...], sc_ref[...], sh_ref[...]).astype(BF16)
    cos = cos_ref[...]
    sin = sin_ref[...]
    half = RET_DK // 2
    qk = RET_HEADS * RET_DK
    vd = RET_HEADS * RET_DV
    for out_ref, base, scale in ((q_ref, 0, 1.0), (k_ref, qk, RET_DK ** -0.5)):
        for hh in range(RET_HEADS):
            c0 = base + hh * RET_DK
            p = jnp.dot(h, w_ref[:, c0:c0 + RET_DK], preferred_element_type=F32)
            p1, p2 = p[:, :half], p[:, half:]
            o0 = hh * RET_DK
            out_ref[:, o0:o0 + half] = ((p1 * cos - p2 * sin) * scale).astype(BF16)
            out_ref[:, o0 + half:o0 + RET_DK] = ((p1 * sin + p2 * cos) * scale).astype(BF16)
    for hh in range(RET_HEADS):
        c0 = 2 * qk + hh * RET_DV
        v_ref[:, hh * RET_DV:(hh + 1) * RET_DV] = jnp.dot(
            h, w_ref[:, c0:c0 + RET_DV], preferred_element_type=F32).astype(BF16)
        g = jnp.dot(h, w_ref[:, vd + c0:vd + c0 + RET_DV], preferred_element_type=F32)
        sg_ref[:, hh * RET_DV:(hh + 1) * RET_DV] = _silu(g).astype(BF16)


def _pre_ret(st, x, norm_g, w_in_b, layer, j, cos, sin, tbl_spec):
    qk = RET_HEADS * RET_DK
    vd = RET_HEADS * RET_DV
    return pl.pallas_call(
        _pre_ret_kernel,
        grid=(st.n_tiles,),
        in_specs=[
            st.row_spec(D_MODEL),
            st.mod_spec(0), st.mod_spec(1),
            _layer_spec((1, D_MODEL), layer),
            _layer_spec((D_MODEL, 2 * qk + 2 * vd), j),
            tbl_spec, tbl_spec,
        ],
        out_specs=[st.row_spec(qk), st.row_spec(qk), st.row_spec(vd), st.row_spec(vd)],
        out_shape=[jax.ShapeDtypeStruct((st.n_rows, qk), BF16),
                   jax.ShapeDtypeStruct((st.n_rows, qk), BF16),
                   jax.ShapeDtypeStruct((st.n_rows, vd), BF16),
                   jax.ShapeDtypeStruct((st.n_rows, vd), BF16)],
        compiler_params=_cparams("parallel"),
        name="pre_ret",
    )(x, st.mod, st.mod, norm_g.reshape(-1, 1, D_MODEL), w_in_b, cos, sin)


def _decay_matrix(lg, ch):
    diff = (lax.broadcasted_iota(I32, (ch, ch), 0)
            - lax.broadcasted_iota(I32, (ch, ch), 1)).astype(F32)
    return jnp.where(diff >= 0.0, jnp.exp(lg * jnp.maximum(diff, 0.0)), 0.0)


def _retention_chunk(valid, lg, dec, q, k, v, sg, s_prev):
    ch = q.shape[0]
    idx = lax.broadcasted_iota(I32, (ch, 1), 0).astype(F32)
    scores = lax.dot_general(q, k, (((1,), (1,)), ((), ())), preferred_element_type=F32) * dec
    o = jnp.dot(scores.astype(BF16), v, preferred_element_type=F32)
    cross = jnp.exp(lg * (idx + 1.0))
    o = o + jnp.dot(q, s_prev.astype(BF16), preferred_element_type=F32) * cross
    kd = (k.astype(F32) * jnp.exp(lg * (valid - 1.0 - idx))).astype(BF16)
    s_new = jnp.exp(lg * float(valid)) * s_prev + lax.dot_general(
        kd, v, (((0,), (0,)), ((), ())), preferred_element_type=F32)
    mu = jnp.mean(o, axis=-1, keepdims=True)
    oc = o - mu
    var = jnp.mean(oc * oc, axis=-1, keepdims=True)
    on = oc * lax.rsqrt(var + NORM_EPS)
    return (sg.astype(F32) * on).astype(BF16), s_new


def _retention_kernel(valid, lg_ref, q_ref, k_ref, v_ref, sg_ref, o_ref, s_out_ref, s_ref, dec_ref):
    c = pl.program_id(1)
    ch = q_ref.shape[0]

    @pl.when(c == 0)
    def _():
        s_ref[...] = jnp.zeros_like(s_ref)
        for h in range(RET_HEADS):
            dec_ref[h] = _decay_matrix(lg_ref[h, 0:1, 0:1], ch)

    for h in range(RET_HEADS):
        ks = slice(h * RET_DK, (h + 1) * RET_DK)
        vs = slice(h * RET_DV, (h + 1) * RET_DV)
        gated, s_new = _retention_chunk(valid, lg_ref[h, 0:1, 0:1], dec_ref[h], q_ref[:, ks],
                                        k_ref[:, ks], v_ref[:, vs], sg_ref[:, vs], s_ref[h])
        o_ref[:, vs] = gated
        s_ref[h] = s_new

    @pl.when(c == pl.num_programs(1) - 1)
    def _():
        s_out_ref[...] = s_ref[...]


def _retention_sample_kernel(lg_ref, q_ref, k_ref, v_ref, sg_ref, s0_ref, o_ref, s_out_ref):
    lg = lg_ref[0:1, 0:1]
    dec = _decay_matrix(lg, RET_PAD)
    for b in range(RET_SB):
        rows = slice(b * RET_PAD, (b + 1) * RET_PAD)
        gated, s_new = _retention_chunk(DEC_SEQ, lg, dec, q_ref[rows, :], k_ref[rows, :],
                                        v_ref[rows, :], sg_ref[rows, :], s0_ref[b])
        o_ref[rows, :] = gated
        s_out_ref[b] = s_new


def _retention_sample(qkvg, s0, s0_layer, lg_tbl):
    rows = RET_SB * RET_PAD
    row = lambda i, h: (i, h)
    nh = RET_HEADS
    state_block = (None, RET_SB, None, RET_DK, RET_DV)
    q = k = v = sg = qkvg
    return pl.pallas_call(
        _retention_sample_kernel,
        grid=(DEC_BATCH // RET_SB, RET_HEADS),
        in_specs=[
            pl.BlockSpec((None, 8, 128), lambda i, h: (h, 0, 0)),
            pl.BlockSpec((rows, RET_DK), row),
            pl.BlockSpec((rows, RET_DK), lambda i, h: (i, nh + h)),
            pl.BlockSpec((rows, RET_DV), lambda i, h: (i, nh + h)),
            pl.BlockSpec((rows, RET_DV), lambda i, h: (i, 2 * nh + h)),
            pl.BlockSpec(state_block, lambda i, h: (s0_layer, i, h, 0, 0)),
        ],
        out_specs=[pl.BlockSpec((rows, RET_DV), row),
                   pl.BlockSpec(state_block, lambda i, h: (0, i, h, 0, 0))],
        out_shape=[jax.ShapeDtypeStruct((DEC_BATCH * RET_PAD, RET_HEADS * RET_DV), BF16),
                   jax.ShapeDtypeStruct((1, DEC_BATCH, RET_HEADS, RET_DK, RET_DV), F32)],
        compiler_params=_cparams("parallel", "parallel"),
        name="retention_sample",
    )(lg_tbl, q, k, v, sg, s0)


def _retention(q, k, v, sg, lg_tbl, n_seq, n_chunks, ch, valid):
    n_rows = q.shape[0]
    qk = RET_HEADS * RET_DK
    vd = RET_HEADS * RET_DV
    row = lambda b, c: (b * n_chunks + c, 0)
    state_block = (None, None, RET_HEADS, RET_DK, RET_DV)
    return pl.pallas_call(
        functools.partial(_retention_kernel, valid),
        grid=(n_seq, n_chunks),
        in_specs=[
            pl.BlockSpec((RET_HEADS, 8, 128), lambda b, c: (0, 0, 0)),
            pl.BlockSpec((ch, qk), row), pl.BlockSpec((ch, qk), row),
            pl.BlockSpec((ch, vd), row), pl.BlockSpec((ch, vd), row),
        ],
        out_specs=[pl.BlockSpec((ch, vd), row),
                   pl.BlockSpec(state_block, lambda b, c: (0, b, 0, 0, 0))],
        out_shape=[jax.ShapeDtypeStruct((n_rows, vd), BF16),
                   jax.ShapeDtypeStruct((1, n_seq, RET_HEADS, RET_DK, RET_DV), F32)],
        scratch_shapes=[pltpu.VMEM((RET_HEADS, RET_DK, RET_DV), F32),
                        pltpu.VMEM((RET_HEADS, ch, ch), F32)],
        compiler_params=_cparams("parallel", "arbitrary"),
        name="retention",
    )(lg_tbl, q, k, v, sg)


def _excl_prefix(col):
    r = lax.broadcasted_iota(I32, (N_EXPERTS, N_EXPERTS), 0)
    c = lax.broadcasted_iota(I32, (N_EXPERTS, N_EXPERTS), 1)
    lane = jnp.sum(jnp.where(r < c, col, 0.0), axis=0, keepdims=True)
    return jnp.sum(jnp.where(r == c, lane, 0.0), axis=1, keepdims=True)


def _ceil_to(x, m):
    return jnp.floor((x + (m - 1.0)) * (1.0 / m)) * m


def _route_kernel(te_ref, lpos_ref, rstart_ref, rlen_ref, tot_ref, tri_ref):
    rt = te_ref.shape[2]
    iota_e = lax.broadcasted_iota(I32, (N_EXPERTS, rt), 0)
    s = lax.broadcasted_iota(I32, (rt, rt), 0)
    t = lax.broadcasted_iota(I32, (rt, rt), 1)
    tri_ref[...] = (s < t).astype(BF16)

    def tile_onehots(i):
        te = te_ref[i]
        onehots = [(te[k:k + 1, :] == iota_e).astype(F32) for k in range(TOP_K)]
        cnt = jnp.sum(onehots[0] + onehots[1] + onehots[2] + onehots[3], axis=1, keepdims=True)
        return onehots, _ceil_to(cnt, RUN_ALIGN)

    tot = lax.fori_loop(0, MOE_NT, lambda i, acc: acc + tile_onehots(i)[1],
                        jnp.zeros((N_EXPERTS, 1), F32))
    tot_ref[...] = tot

    def place(i, gbase):
        onehots, run = tile_onehots(i)
        base = _excl_prefix(run)
        for k in range(TOP_K):
            oh = onehots[k]
            before = jnp.dot(oh.astype(BF16), tri_ref[...], preferred_element_type=F32)
            pos = jnp.sum(oh * (base + before), axis=0, keepdims=True)
            lpos_ref[i, k:k + 1, :] = pos.astype(I32)
            base = base + jnp.sum(oh, axis=1, keepdims=True)
        rstart_ref[i] = jnp.broadcast_to(gbase, rstart_ref.shape[1:]).astype(I32)
        rlen_ref[i] = jnp.broadcast_to(run, rlen_ref.shape[1:]).astype(I32)
        return gbase + run

    lax.fori_loop(0, MOE_NT, place, _excl_prefix(_ceil_to(tot, MOE_TB)))


def _route(te):
    full = lambda shape: pl.BlockSpec(shape, lambda i: (0,) * len(shape))
    return pl.pallas_call(
        _route_kernel,
        grid=(1,),
        in_specs=[full(te.shape)],
        out_specs=[full(te.shape), full((MOE_NT, N_EXPERTS, 128)), full((MOE_NT, N_EXPERTS, 128)),
                   full((N_EXPERTS, 1))],
        out_shape=[jax.ShapeDtypeStruct(te.shape, I32),
                   jax.ShapeDtypeStruct((MOE_NT, N_EXPERTS, 128), I32),
                   jax.ShapeDtypeStruct((MOE_NT, N_EXPERTS, 128), I32),
                   jax.ShapeDtypeStruct((N_EXPERTS, 1), F32)],
        scratch_shapes=[pltpu.VMEM((MOE_RT, MOE_RT), BF16)],
        compiler_params=_cparams("arbitrary"),
        name="moe_route",
    )(te)


def _dispatch_kernel(n_prompt_tiles, rs_ref, rl_ref, tzs_ref, tzl_ref, nu_ref, hp_ref, hs_ref,
                     lpos_ref, xb_ref, xs_ref, sem, zero_ref):
    i = pl.program_id(0)
    slot = i % 2
    zero_sem = sem.at[2]

    def tail_copy(e):
        n = pl.multiple_of(tzl_ref[e], RUN_ALIGN)
        dst = pl.multiple_of(tzs_ref[e], RUN_ALIGN)
        return pltpu.make_async_copy(zero_ref.at[pl.ds(0, n)], xb_ref.at[pl.ds(dst, n)], zero_sem)

    def block_copy(j):
        dst = pl.multiple_of(j * MOE_TB, MOE_TB)
        return pltpu.make_async_copy(zero_ref, xb_ref.at[pl.ds(dst, MOE_TB)], zero_sem)

    @pl.when(i == 0)
    def _():
        zero_ref[...] = jnp.zeros_like(zero_ref)

        def start(e, carry):
            pl.when(tzl_ref[e] > 0)(lambda: tail_copy(e).start())
            return carry

        def wait(e, carry):
            pl.when(tzl_ref[e] > 0)(lambda: tail_copy(e).wait())
            return carry

        lax.fori_loop(0, N_EXPERTS, start, 0)
        lax.fori_loop(nu_ref[0], MOE_NBLK, lambda j, c: (block_copy(j).start(), c)[1], 0)
        lax.fori_loop(0, N_EXPERTS, wait, 0)
        lax.fori_loop(nu_ref[0], MOE_NBLK, lambda j, c: (block_copy(j).wait(), c)[1], 0)

    rows = lax.broadcasted_iota(I32, (MOE_CAP, MOE_RT), 0)
    perm = jnp.where(rows == lpos_ref[0:1, :], 1.0, 0.0)
    for k in range(1, TOP_K):
        perm = perm + jnp.where(rows == lpos_ref[k:k + 1, :], 1.0, 0.0)
    h = jnp.where(i < n_prompt_tiles, hp_ref[...], hs_ref[...])
    xs_ref[slot] = _pack_halves(
        jnp.dot(perm.astype(BF16), h.astype(BF16), preferred_element_type=F32))

    def scatter(tile, wait):
        def body(e, lo):
            n = pl.multiple_of(rl_ref[tile * N_EXPERTS + e], RUN_ALIGN)
            sl = tile % 2
            src = pl.multiple_of(lo, RUN_ALIGN)
            dst = pl.multiple_of(rs_ref[tile * N_EXPERTS + e], RUN_ALIGN)
            cp = pltpu.make_async_copy(xs_ref.at[sl, pl.ds(src, n)], xb_ref.at[pl.ds(dst, n)],
                                       sem.at[sl])

            @pl.when(n > 0)
            def _():
                if wait:
                    cp.wait()
                else:
                    cp.start()

            return lo + n

        lax.fori_loop(0, N_EXPERTS, body, 0)

    scatter(i, False)

    @pl.when(i > 0)
    def _():
        scatter(i - 1, True)

    @pl.when(i == pl.num_programs(0) - 1)
    def _():
        scatter(i, True)


def _dispatch(tables, h_p, h_s, lpos):
    nt_p = h_p.shape[0] // MOE_RT
    return pl.pallas_call(
        functools.partial(_dispatch_kernel, nt_p),
        grid_spec=pltpu.PrefetchScalarGridSpec(
            num_scalar_prefetch=len(tables),
            grid=(MOE_NT,),
            in_specs=[
                pl.BlockSpec((MOE_RT, D_MODEL), lambda i, *_: (jnp.minimum(i, nt_p - 1), 0)),
                pl.BlockSpec((MOE_RT, D_MODEL), lambda i, *_: (jnp.maximum(i - nt_p, 0), 0)),
                pl.BlockSpec((None, TOP_K, MOE_RT), lambda i, *_: (i, 0, 0)),
            ],
            out_specs=pl.BlockSpec(memory_space=pl.ANY),
            scratch_shapes=[pltpu.VMEM((2, MOE_CAP, D_MODEL // 2), U32),
                            pltpu.SemaphoreType.DMA((3,)),
                            pltpu.VMEM((MOE_TB, D_MODEL // 2), U32)],
        ),
        out_shape=jax.ShapeDtypeStruct((MOE_ROWS, D_MODEL // 2), U32),
        compiler_params=_cparams("arbitrary"),
        name="moe_dispatch",
    )(*tables, h_p, h_s, lpos)


def _ffn_kernel(layer, be_ref, nu_ref, nxt_ref, bgu_ref, bdn_ref, xb_hbm, wgu_hbm, wdn_hbm,
                yb_hbm, wgu_f_ref, wdn_f_ref, wgu_b_ref, wdn_b_ref, act_ref, xbuf_ref, ybuf_ref,
                wsem, xsem, ysem):
    n_used = nu_ref[0]

    def weight_copies(ex):
        return (pltpu.make_async_copy(wgu_hbm.at[layer, ex], wgu_f_ref, wsem.at[0]),
                pltpu.make_async_copy(wdn_hbm.at[layer, ex], wdn_f_ref, wsem.at[1]))

    def block_rows(j):
        return pl.ds(pl.multiple_of(j * MOE_TB, MOE_TB), MOE_TB)

    def x_copy(j):
        return pltpu.make_async_copy(xb_hbm.at[block_rows(j)], xbuf_ref.at[j % 2], xsem.at[j % 2])

    def y_copy(j):
        return pltpu.make_async_copy(ybuf_ref.at[j % 2], yb_hbm.at[block_rows(j)], ysem.at[j % 2])

    for cp in weight_copies(be_ref[0]):
        cp.start()
    x_copy(0).start()

    def block(j, carry):
        e = be_ref[j]
        e_prev = be_ref[jnp.maximum(j - 1, 0)]

        @pl.when(j + 1 < n_used)
        def _():
            x_copy(j + 1).start()

        @pl.when((j == 0) | (e != e_prev))
        def _():
            for cp in weight_copies(e):
                cp.wait()
            wgu_b_ref[...] = wgu_f_ref[...].astype(BF16)
            wdn_b_ref[...] = wdn_f_ref[...].astype(BF16)
            nx = nxt_ref[j]

            @pl.when(nx >= 0)
            def _():
                for cp in weight_copies(nx):
                    cp.start()

        x_copy(j).wait()

        @pl.when(j >= 2)
        def _():
            y_copy(j - 2).wait()

        x = _unpack_halves(xbuf_ref[j % 2])
        bgu = bgu_ref[e]
        for c in range(D_FF // FF_CHUNK):
            gs = slice(c * FF_CHUNK, (c + 1) * FF_CHUNK)
            us = slice(D_FF + c * FF_CHUNK, D_FF + (c + 1) * FF_CHUNK)
            gate = jnp.dot(x, wgu_b_ref[:, gs], preferred_element_type=F32) + bgu[:, gs]
            up = jnp.dot(x, wgu_b_ref[:, us], preferred_element_type=F32) + bgu[:, us]
            gate = jnp.minimum(gate, SWIGLU_LIMIT)
            up = jnp.clip(up, -SWIGLU_LIMIT, SWIGLU_LIMIT)
            act = (up + 1.0) * (gate * jax.nn.sigmoid(SWIGLU_ALPHA * gate))
            act_ref[:, gs] = act.astype(BF16)
        y = jnp.dot(act_ref[...], wdn_b_ref[...], preferred_element_type=F32) + bdn_ref[e]
        ybuf_ref[j % 2] = _pack_halves(y.astype(BF16).astype(F32))
        y_copy(j).start()
        return carry

    lax.fori_loop(0, n_used, block, 0)

    @pl.when(n_used >= 2)
    def _():
        y_copy(n_used - 2).wait()

    y_copy(n_used - 1).wait()

    ybuf_ref[0] = jnp.zeros_like(ybuf_ref[0])

    def zero_copy(j):
        return pltpu.make_async_copy(ybuf_ref.at[0], yb_hbm.at[block_rows(j)], ysem.at[0])

    lax.fori_loop(n_used, MOE_NBLK, lambda j, c: (zero_copy(j).start(), c)[1], 0)
    lax.fori_loop(n_used, MOE_NBLK, lambda j, c: (zero_copy(j).wait(), c)[1], 0)


def _ffn(layer, block_e, n_used, nxt, xb, w_gu, b_gu, w_dn, b_dn):
    bias_map = lambda i, be, nu, nx: (layer, 0, 0, 0)
    return pl.pallas_call(
        functools.partial(_ffn_kernel, layer),
        grid_spec=pltpu.PrefetchScalarGridSpec(
            num_scalar_prefetch=3,
            grid=(1,),
            in_specs=[
                pl.BlockSpec((None, N_EXPERTS, 1, 2 * D_FF), bias_map),
                pl.BlockSpec((None, N_EXPERTS, 1, D_MODEL), bias_map),
                pl.BlockSpec(memory_space=pl.ANY),
                pl.BlockSpec(memory_space=pl.ANY),
                pl.BlockSpec(memory_space=pl.ANY),
            ],
            out_specs=pl.BlockSpec(memory_space=pl.ANY),
            scratch_shapes=[pltpu.VMEM((D_MODEL, 2 * D_FF), F32),
                            pltpu.VMEM((D_FF, D_MODEL), F32),
                            pltpu.VMEM((D_MODEL, 2 * D_FF), BF16),
                            pltpu.VMEM((D_FF, D_MODEL), BF16),
                            pltpu.VMEM((MOE_TB, D_FF), BF16),
                            pltpu.VMEM((2, MOE_TB, D_MODEL // 2), U32),
                            pltpu.VMEM((2, MOE_TB, D_MODEL // 2), U32),
                            pltpu.SemaphoreType.DMA((2,)),
                            pltpu.SemaphoreType.DMA((2,)),
                            pltpu.SemaphoreType.DMA((2,))],
        ),
        out_shape=jax.ShapeDtypeStruct((MOE_ROWS, D_MODEL // 2), U32),
        compiler_params=_cparams("arbitrary"),
        name="moe_ffn",
    )(block_e, n_used, nxt, b_gu.reshape(b_gu.shape[0], N_EXPERTS, 1, 2 * D_FF),
      b_dn.reshape(b_dn.shape[0], N_EXPERTS, 1, D_MODEL), xb, w_gu, w_dn)


def _tile_rows(v, n_rows):
    mr = v.shape[0]
    if mr in (1, n_rows):
        return v
    return jnp.concatenate([v] * (n_rows // mr), axis=0)


def _combine_kernel(tile_off, final, rs_ref, rl_ref, x_ref, gf_ref, lpt_ref, wt_ref, fg_ref,
                    yb_ref, out_ref, ys_ref, sem):
    i = pl.program_id(0)

    def gather(tile, wait):
        def body(e, lo):
            n = pl.multiple_of(rl_ref[(tile_off + tile) * N_EXPERTS + e], RUN_ALIGN)
            sl = tile % 2
            src = pl.multiple_of(rs_ref[(tile_off + tile) * N_EXPERTS + e], RUN_ALIGN)
            dst = pl.multiple_of(lo, RUN_ALIGN)
            cp = pltpu.make_async_copy(yb_ref.at[pl.ds(src, n)], ys_ref.at[sl, pl.ds(dst, n)],
                                       sem.at[sl])

            @pl.when(n > 0)
            def _():
                if wait:
                    cp.wait()
                else:
                    cp.start()

            return lo + n

        lax.fori_loop(0, N_EXPERTS, body, 0)

    @pl.when(i == 0)
    def _():
        ys_ref[...] = jnp.zeros_like(ys_ref)
        gather(0, False)

    @pl.when(i + 1 < pl.num_programs(0))
    def _():
        gather(i + 1, False)

    gather(i, True)

    cols = lax.broadcasted_iota(I32, (MOE_RT, MOE_CAP), 1)
    w = jnp.where(cols == lpt_ref[:, 0:1], wt_ref[:, 0:1], 0.0)
    for k in range(1, TOP_K):
        w = w + jnp.where(cols == lpt_ref[:, k:k + 1], wt_ref[:, k:k + 1], 0.0)
    ys = _unpack_halves(ys_ref[i % 2])
    y = jnp.dot(w.astype(BF16), ys, preferred_element_type=F32)
    xn = x_ref[...] + _tile_rows(gf_ref[...], MOE_RT) * y
    if final:
        xn = xn * lax.rsqrt(jnp.mean(xn * xn, axis=-1, keepdims=True) + NORM_EPS) * fg_ref[...]
    out_ref[...] = xn


def _combine(tile_off, n_tiles, final, tables, x, mod, mod_map, lpos_t, wt, final_g, yb):
    pm = lambda i, *_: (i, 0)
    tm = lambda i, *_: (i + tile_off, 0)
    return pl.pallas_call(
        functools.partial(_combine_kernel, tile_off, final),
        grid_spec=pltpu.PrefetchScalarGridSpec(
            num_scalar_prefetch=len(tables),
            grid=(n_tiles,),
            in_specs=[
                pl.BlockSpec((MOE_RT, D_MODEL), pm),
                pl.BlockSpec((None, mod.shape[1], D_MODEL), mod_map),
                pl.BlockSpec((MOE_RT, TOP_K), tm),
                pl.BlockSpec((MOE_RT, TOP_K), tm),
                pl.BlockSpec((1, D_MODEL), lambda i, *_: (0, 0)),
                pl.BlockSpec(memory_space=pl.ANY),
            ],
            out_specs=pl.BlockSpec((MOE_RT, D_MODEL), pm),
            scratch_shapes=[pltpu.VMEM((2, MOE_CAP, D_MODEL // 2), U32),
                            pltpu.SemaphoreType.DMA((2,))],
        ),
        out_shape=jax.ShapeDtypeStruct(x.shape, F32),
        compiler_params=_cparams("arbitrary"),
        name="moe_combine",
    )(*tables, x, mod, lpos_t, wt, final_g.reshape(1, D_MODEL), yb)


def _moe(layer, final, st_p, st_s, x_p, x_s, post_p, post_s,
         w_gu, b_gu, w_dn, b_dn, final_g):
    te = jnp.concatenate([post_p[1], post_s[1]], axis=1)
    tw = jnp.concatenate([post_p[2], post_s[2]], axis=1)
    te = jnp.swapaxes(te.reshape(TOP_K, MOE_NT, MOE_RT), 0, 1)
    lpos, rstart, rlen, tot = _route(te)
    tot = tot[:, 0].astype(I32)
    padded = (tot + MOE_TB - 1) // MOE_TB * MOE_TB
    pends = jnp.cumsum(padded)
    n_used = (pends[-1] // MOE_TB).reshape(1)
    blk = jnp.minimum(jnp.arange(MOE_NBLK, dtype=I32), n_used[0] - 1) * MOE_TB
    block_e = jnp.minimum(jnp.sum(pends[None, :] <= blk[:, None], axis=-1),
                          N_EXPERTS - 1).astype(I32)
    ids = jnp.where(tot > 0, jnp.arange(N_EXPERTS, dtype=I32), N_EXPERTS)
    later = jnp.concatenate([lax.cummin(ids[::-1])[::-1][1:], jnp.full((1,), N_EXPERTS, I32)])
    nxt_of = jnp.where(later < N_EXPERTS, later, -1)
    nxt = jnp.sum(jnp.where(block_e[:, None] == jnp.arange(N_EXPERTS, dtype=I32)[None, :],
                            nxt_of[None, :], 0), axis=-1)
    run_tables = (rstart[:, :, 0].reshape(-1), rlen[:, :, 0].reshape(-1))
    tail_tables = (pends - padded + tot, padded - tot)

    nt_p = N_PROMPT // MOE_RT
    nt_s = N_SAMPLE // MOE_RT
    xb = _dispatch(run_tables + tail_tables + (n_used,), post_p[0], post_s[0], lpos)
    yb = _ffn(layer, block_e, n_used, nxt, xb, w_gu, b_gu, w_dn, b_dn)
    lpos_t = jnp.swapaxes(lpos, 1, 2).reshape(N_TOK, TOP_K)
    wt = tw.T
    tiles_per_seq = SEQ // MOE_RT
    out_p = _combine(0, nt_p, final, run_tables, x_p, st_p.mod,
                     lambda i, *_: (i // tiles_per_seq, 0, N_MOD - 1), lpos_t, wt, final_g, yb)
    out_s = _combine(nt_p, nt_s, final, run_tables, x_s, st_s.mod,
                     lambda i, *_: (0, 0, N_MOD - 1), lpos_t, wt, final_g, yb)
    return out_p, out_s


def _rope_tables(pos):
    half = RET_DK // 2
    inv_freq = ROPE_BASE ** (-jnp.arange(half, dtype=F32) / half)
    ang = pos.astype(F32)[:, None] * inv_freq[None, :]
    return jnp.cos(ang), jnp.sin(ang)


def kernel(x_prompt, x_sample, c_prompt, c_sample, state_conv, state_ret, norm_mix_g, norm_ff_g,
           w_mod, b_mod, conv_w1, conv_b1, conv_dw, conv_dw_b, conv_ln_g, conv_ln_b, conv_w2,
           conv_b2, ret_w_in, ret_w_o, router_w, router_b, moe_w_gu, moe_b_gu, moe_w_dn,
           moe_b_dn, final_g):
    mod = _adaln(jnp.concatenate([c_prompt, c_sample], axis=0), w_mod, b_mod)
    x_p = x_prompt.reshape(N_PROMPT, D_MODEL)
    x_s = jnp.swapaxes(x_sample, 0, 1).reshape(N_SAMPLE, D_MODEL)
    router_wt = jnp.swapaxes(router_w, 1, 2)

    def streams(layer):
        st_p = _Stream(N_PROMPT, TM_PROMPT, SEQ // TM_PROMPT,
                       mod[layer, :BATCH].reshape(BATCH, 1, N_MOD * D_MODEL))
        st_s = _Stream(N_SAMPLE, TM_SAMPLE, None,
                       mod[layer, BATCH:].reshape(1, DEC_BATCH, N_MOD * D_MODEL))
        return st_p, st_s

    st_p, st_s = streams(0)
    u_p = _pre_conv(st_p, x_p, norm_mix_g, conv_w1, conv_b1, 0, 0)
    u_s = _pre_conv(st_s, x_s, norm_mix_g, conv_w1, conv_b1, 0, 0)
    buf_tm = jnp.swapaxes(state_conv[0], 0, 1)
    z_s, ns_tm = _conv_sample(buf_tm, u_s.reshape(DEC_SEQ, DEC_BATCH, D_MODEL),
                              conv_dw, conv_dw_b, 0)
    post_p = _mixer_post("conv", st_p, 0, 0, x_p, u_p, norm_ff_g, router_wt, router_b,
                         conv_w2, conv_b2, conv_dw, conv_dw_b, conv_ln_g, conv_ln_b)
    post_s = _mixer_post("z", st_s, 0, 0, x_s, z_s.reshape(N_SAMPLE, D_MODEL), norm_ff_g,
                         router_wt, router_b, conv_w2, conv_b2, None, None, conv_ln_g, conv_ln_b)
    x_p, x_s = _moe(0, False, st_p, st_s, post_p[0], post_s[0], post_p[1:], post_s[1:],
                    moe_w_gu, moe_b_gu, moe_w_dn, moe_b_dn, final_g)
    conv_p = u_p.reshape(BATCH, SEQ, D_MODEL)[:, SEQ - CONV_STATE:][None]
    conv_s = jnp.swapaxes(ns_tm, 0, 1)[None]

    st_p, st_s = streams(1)
    w_in_b = ret_w_in.astype(BF16)
    cos_p, sin_p = _rope_tables(jnp.arange(SEQ, dtype=I32))
    cos_s, sin_s = _rope_tables(PAST_LEN + jnp.arange(DEC_SEQ, dtype=I32))
    half = RET_DK // 2
    tps = SEQ // TM_PROMPT
    tbl_p = pl.BlockSpec((TM_PROMPT, half), lambda i: (i % tps, 0))
    tbl_s = pl.BlockSpec((None, 1, half), lambda i: (i, 0, 0))
    q_p, k_p, v_p, sg_p = _pre_ret(st_p, x_p, norm_mix_g, w_in_b, 1, 0, cos_p, sin_p, tbl_p)
    q_s, k_s, v_s, sg_s = _pre_ret(st_s, x_s, norm_mix_g, w_in_b, 1, 0,
                                   cos_s.reshape(DEC_SEQ, 1, half),
                                   sin_s.reshape(DEC_SEQ, 1, half), tbl_s)
    lg = jnp.log1p(-jnp.exp2(-5.0 - jnp.arange(RET_HEADS, dtype=F32)))
    lg_tbl = jnp.broadcast_to(lg[:, None, None], (RET_HEADS, 8, 128))
    gated_p, ret_p = _retention(q_p, k_p, v_p, sg_p, lg_tbl,
                                BATCH, SEQ // RET_CHUNK, RET_CHUNK, RET_CHUNK)

    def to_seq_major(a):
        a = jnp.swapaxes(a.reshape(DEC_SEQ, DEC_BATCH, -1), 0, 1)
        a = jnp.pad(a, ((0, 0), (0, RET_PAD - DEC_SEQ), (0, 0)))
        return a.reshape(DEC_BATCH * RET_PAD, -1)

    qkvg_s = to_seq_major(jnp.concatenate([q_s, k_s, v_s, sg_s], axis=1))
    gated_s, ret_s = _retention_sample(qkvg_s, state_ret, 0, lg_tbl)
    gated_s = jnp.swapaxes(gated_s.reshape(DEC_BATCH, RET_PAD, -1)[:, :DEC_SEQ], 0, 1)
    gated_s = gated_s.reshape(N_SAMPLE, -1)
    post_p = _mixer_post("ret", st_p, 1, 0, x_p, gated_p, norm_ff_g, router_wt, router_b, ret_w_o)
    post_s = _mixer_post("ret", st_s, 1, 0, x_s, gated_s, norm_ff_g, router_wt, router_b, ret_w_o)
    y_p, y_s = _moe(1, True, st_p, st_s, post_p[0], post_s[0], post_p[1:], post_s[1:],
                    moe_w_gu, moe_b_gu, moe_w_dn, moe_b_dn, final_g)

    y_prompt = y_p.reshape(BATCH, SEQ, D_MODEL)
    y_sample = jnp.swapaxes(y_s.reshape(DEC_SEQ, DEC_BATCH, D_MODEL), 0, 1)
    return (y_prompt, y_sample, conv_p, conv_s, ret_p, ret_s)
```

```python
import functools

import jax
import jax.numpy as jnp
from jax import lax
from jax.experimental import pallas as pl
from jax.experimental.pallas import tpu as pltpu

F32 = jnp.float32
BF16 = jnp.bfloat16
I32 = jnp.int32
U32 = jnp.uint32

D_MODEL = 1024
SEQ = 2048
BATCH = 8
DEC_BATCH = 128
DEC_SEQ = 4
PAST_LEN = 16384
CONV_WIDTH = 31
CONV_STATE = CONV_WIDTH - 1
RET_HEADS = 4
RET_DK = 256
RET_DV = 512
ROPE_BASE = 10000.0
N_EXPERTS = 32
TOP_K = 4
D_FF = 1024
SWIGLU_LIMIT = 7.0
SWIGLU_ALPHA = 1.702
N_MOD = 6
NORM_EPS = 1e-5

N_PROMPT = BATCH * SEQ
N_SAMPLE = DEC_BATCH * DEC_SEQ
N_TOK = N_PROMPT + N_SAMPLE
N_ASSIGN = N_TOK * TOP_K

TM_PROMPT = 512
TM_SAMPLE = DEC_BATCH
HALO = 32
CONV_RB = 64
CONV_LW = 128
RET_CHUNK = 256
RET_PAD = 16
RET_SB = 16
MOE_TB = 512
MOE_RT = 256
MOE_NT = N_TOK // MOE_RT
RUN_ALIGN = 8
MOE_CAP = -(-(TOP_K * MOE_RT + N_EXPERTS * (RUN_ALIGN - 1)) // 256) * 256
MOE_MAX_ROWS = N_ASSIGN + MOE_NT * N_EXPERTS * (RUN_ALIGN - 1)
MOE_NBLK = -(-MOE_MAX_ROWS // MOE_TB) + N_EXPERTS
MOE_ROWS = MOE_NBLK * MOE_TB
FF_CHUNK = 256
VMEM_LIMIT = 56 * 1024 * 1024


def _cparams(*sem):
    return pltpu.CompilerParams(dimension_semantics=sem, vmem_limit_bytes=VMEM_LIMIT)


def _silu(x):
    return x * jax.nn.sigmoid(x)


def _rms_mod(x, g, sc, sh):
    y = x * lax.rsqrt(jnp.mean(x * x, axis=-1, keepdims=True) + NORM_EPS) * g
    return y * (1.0 + sc) + sh


def _split_bf16(x):
    hi = x.astype(BF16)
    lo = (x - hi.astype(F32)).astype(BF16)
    return hi, lo


def _pack_halves(x):
    c = x.shape[1] // 2
    lo = lax.bitcast_convert_type(x[:, :c], U32)
    hi = lax.bitcast_convert_type(x[:, c:], U32)
    return (lo >> 16) | (hi & jnp.uint32(0xFFFF0000))


def _unpack_halves(p):
    lo = lax.bitcast_convert_type(p << 16, F32)
    hi = lax.bitcast_convert_type(p & jnp.uint32(0xFFFF0000), F32)
    return jnp.concatenate([lo.astype(BF16), hi.astype(BF16)], axis=1)


def _adaln_kernel(c_ref, w_ref, b_ref, o_ref):
    a = _silu(c_ref[...]).astype(BF16)
    o_ref[...] = jnp.dot(a, w_ref[...].astype(BF16), preferred_element_type=F32) + b_ref[...]


def _adaln(c_all, w_mod, b_mod):
    depth, d, nm = w_mod.shape
    n = c_all.shape[0]
    return pl.pallas_call(
        _adaln_kernel,
        grid=(depth, nm // d),
        in_specs=[
            pl.BlockSpec((n, d), lambda l, j: (0, 0)),
            pl.BlockSpec((None, d, d), lambda l, j: (l, 0, j)),
            pl.BlockSpec((None, 1, d), lambda l, j: (l, 0, j)),
        ],
        out_specs=pl.BlockSpec((None, n, d), lambda l, j: (l, 0, j)),
        out_shape=jax.ShapeDtypeStruct((depth, n, nm), F32),
        compiler_params=_cparams("parallel", "parallel"),
        name="adaln",
    )(c_all, w_mod, b_mod.reshape(depth, 1, nm))


class _Stream:
    def __init__(self, n_rows, tm, tiles_per_seq, mod):
        self.n_rows = n_rows
        self.tm = tm
        self.n_tiles = n_rows // tm
        self.tiles_per_seq = tiles_per_seq
        self.mod = mod

    def mod_spec(self, j):
        mr = self.mod.shape[1]
        if self.tiles_per_seq is None:
            return pl.BlockSpec((None, mr, D_MODEL), lambda i: (0, 0, j))
        tps = self.tiles_per_seq
        return pl.BlockSpec((None, mr, D_MODEL), lambda i: (i // tps, 0, j))

    def row_spec(self, width, col=0):
        return pl.BlockSpec((self.tm, width), lambda i: (i, col))


def _const_spec(shape):
    nd = len(shape)
    return pl.BlockSpec(shape, lambda i: (0,) * nd)


def _layer_spec(shape, layer):
    nd = len(shape)
    return pl.BlockSpec((None,) + shape, lambda i: (layer,) + (0,) * nd)


def _pre_conv_kernel(x_ref, sh_ref, sc_ref, g_ref, w1_ref, b1_ref, u_ref, w1b_ref):
    @pl.when(pl.program_id(0) == 0)
    def _():
        w1b_ref[...] = w1_ref[...].astype(BF16)

    h = _rms_mod(x_ref[...], g_ref[...], sc_ref[...], sh_ref[...])
    ag = jnp.dot(h.astype(BF16), w1b_ref[...], preferred_element_type=F32) + b1_ref[...]
    u_ref[...] = ag[:, :D_MODEL] * jax.nn.sigmoid(ag[:, D_MODEL:])


def _pre_conv(st, x, norm_g, w1, b1, layer, j):
    c2 = w1.shape[-1]
    return pl.pallas_call(
        _pre_conv_kernel,
        grid=(st.n_tiles,),
        in_specs=[
            st.row_spec(D_MODEL),
            st.mod_spec(0), st.mod_spec(1),
            _layer_spec((1, D_MODEL), layer),
            _layer_spec((D_MODEL, c2), j),
            _layer_spec((1, c2), j),
        ],
        out_specs=st.row_spec(D_MODEL),
        out_shape=jax.ShapeDtypeStruct((st.n_rows, D_MODEL), F32),
        scratch_shapes=[pltpu.VMEM((D_MODEL, c2), BF16)],
        compiler_params=_cparams("arbitrary"),
        name="pre_conv",
    )(x, st.mod, st.mod, norm_g.reshape(-1, 1, D_MODEL), w1, b1.reshape(-1, 1, c2))


def _conv_sample_kernel(buf_ref, u_ref, dw_ref, dwb_ref, z_ref, ns_ref):
    for t in range(DEC_SEQ):
        acc = jnp.broadcast_to(dwb_ref[...], u_ref.shape[1:])
        for j in range(t, CONV_STATE):
            acc = acc + buf_ref[j] * dw_ref[j - t:j - t + 1, :]
        for s in range(t + 1):
            k = CONV_STATE + s - t
            acc = acc + u_ref[s] * dw_ref[k:k + 1, :]
        z_ref[t] = acc
    ns_ref[0:CONV_STATE - DEC_SEQ] = buf_ref[DEC_SEQ:CONV_STATE]
    ns_ref[CONV_STATE - DEC_SEQ:CONV_STATE] = u_ref[...]


def _conv_sample(buf_tm, u_tm, dw, dwb, j):
    sb = 32
    c = u_tm.shape[-1]
    return pl.pallas_call(
        _conv_sample_kernel,
        grid=(DEC_BATCH // sb,),
        in_specs=[
            pl.BlockSpec((CONV_STATE, sb, c), lambda i: (0, i, 0)),
            pl.BlockSpec((DEC_SEQ, sb, c), lambda i: (0, i, 0)),
            _layer_spec((CONV_WIDTH, c), j),
            _layer_spec((1, c), j),
        ],
        out_specs=[
            pl.BlockSpec((DEC_SEQ, sb, c), lambda i: (0, i, 0)),
            pl.BlockSpec((CONV_STATE, sb, c), lambda i: (0, i, 0)),
        ],
        out_shape=[
            jax.ShapeDtypeStruct((DEC_SEQ, DEC_BATCH, c), F32),
            jax.ShapeDtypeStruct((CONV_STATE, DEC_BATCH, c), F32),
        ],
        compiler_params=_cparams("parallel"),
        name="conv_sample",
    )(buf_tm, u_tm, dw, dwb.reshape(-1, 1, c))


def _route_topk(h2, rwt_ref, rb_ref, te_ref, tw_ref):
    h_hi, h_lo = _split_bf16(h2)
    w_hi, w_lo = _split_bf16(rwt_ref[...])
    nt = (((1,), (1,)), ((), ()))
    logits = (lax.dot_general(w_hi, h_hi, nt, preferred_element_type=F32)
              + lax.dot_general(w_hi, h_lo, nt, preferred_element_type=F32)
              + lax.dot_general(w_lo, h_hi, nt, preferred_element_type=F32)
              + rb_ref[...])
    iota_e = lax.broadcasted_iota(I32, logits.shape, 0).astype(F32)
    vals, idxs = [], []
    for _ in range(TOP_K):
        m = jnp.max(logits, axis=0, keepdims=True)
        idx = jnp.min(jnp.where(logits == m, iota_e, float(N_EXPERTS)), axis=0, keepdims=True)
        logits = jnp.where(iota_e == idx, -jnp.inf, logits)
        vals.append(m)
        idxs.append(idx)
    es = [jnp.exp(v - vals[0]) for v in vals]
    tot = es[0] + es[1] + es[2] + es[3]
    for k in range(TOP_K):
        te_ref[k:k + 1, :] = idxs[k].astype(I32)
        tw_ref[k:k + 1, :] = es[k] / tot


def _mixer_post_kernel(mode, tiles_per_seq, *refs):
    it = iter(refs)
    if mode == "conv":
        u_ref, halo_ref, dw_ref, dwb_ref = next(it), next(it), next(it), next(it)
    elif mode == "z":
        z_in_ref = next(it)
    else:
        gated_ref = next(it)
    if mode in ("conv", "z"):
        lng_ref, lnb_ref = next(it), next(it)
    w_ref, b_ref = next(it), next(it)
    x_ref, gm_ref, shf_ref, scf_ref, gff_ref, rwt_ref, rb_ref = (next(it) for _ in range(7))
    x1_ref, h2_ref, te_ref, tw_ref = (next(it) for _ in range(4))
    wb_ref = next(it)
    if mode == "conv":
        ext_ref, z_ref = next(it), next(it)

    i = pl.program_id(0)

    @pl.when(i == 0)
    def _():
        wb_ref[...] = w_ref[...].astype(BF16)

    if mode == "conv":
        tm = u_ref.shape[0]
        first = (i % tiles_per_seq) == 0
        ext_ref[0:HALO, :] = jnp.where(first, 0.0, halo_ref[...])
        ext_ref[HALO:HALO + tm, :] = u_ref[...]
        off = HALO - CONV_STATE

        def row_block(r, carry):
            r0 = pl.multiple_of(r * CONV_RB, CONV_RB)
            for lc in range(D_MODEL // CONV_LW):
                ls = slice(lc * CONV_LW, (lc + 1) * CONV_LW)
                win = ext_ref[pl.ds(r0, CONV_RB + HALO), ls]
                acc = jnp.broadcast_to(dwb_ref[:, ls], (CONV_RB, CONV_LW))
                for s in range(8):
                    ws = win if s == 0 else pltpu.roll(win, CONV_RB + HALO - s, 0)
                    for k in range(CONV_WIDTH):
                        if (off + k) % 8 == s:
                            a = off + k - s
                            acc = acc + ws[a:a + CONV_RB, :] * dw_ref[k:k + 1, ls]
                z_ref[pl.ds(r0, CONV_RB), ls] = acc
            return carry

        lax.fori_loop(0, tm // CONV_RB, row_block, 0)
        z = z_ref[...]
    elif mode == "z":
        z = z_in_ref[...]

    if mode in ("conv", "z"):
        mu = jnp.mean(z, axis=-1, keepdims=True)
        zc = z - mu
        var = jnp.mean(zc * zc, axis=-1, keepdims=True)
        zn = zc * lax.rsqrt(var + NORM_EPS) * lng_ref[...] + lnb_ref[...]
        a = _silu(zn).astype(BF16)
        y = jnp.dot(a, wb_ref[...], preferred_element_type=F32) + b_ref[...]
    else:
        y = jnp.dot(gated_ref[...], wb_ref[...], preferred_element_type=F32)

    x1 = x_ref[...] + gm_ref[...] * y
    x1_ref[...] = x1
    h2 = _rms_mod(x1, gff_ref[...], scf_ref[...], shf_ref[...])
    h2_ref[...] = h2
    _route_topk(h2, rwt_ref, rb_ref, te_ref, tw_ref)


def _mixer_post(mode, st, layer, j, x, mix_in, norm_ff_g, router_wt, router_b, w, b=None,
                dw=None, dwb=None, ln_g=None, ln_b=None):
    tm = st.tm
    k_in = w.shape[-2]
    args, specs = [], []
    if mode == "conv":
        per32 = tm // HALO
        args += [mix_in, mix_in, dw, dwb.reshape(-1, 1, D_MODEL)]
        specs += [st.row_spec(D_MODEL),
                  pl.BlockSpec((HALO, D_MODEL), lambda i: (jnp.maximum(i * per32 - 1, 0), 0)),
                  _layer_spec((CONV_WIDTH, D_MODEL), j), _layer_spec((1, D_MODEL), j)]
    elif mode == "z":
        args += [mix_in]
        specs += [st.row_spec(D_MODEL)]
    else:
        args += [mix_in]
        specs += [st.row_spec(k_in)]
    if mode in ("conv", "z"):
        args += [ln_g.reshape(-1, 1, D_MODEL), ln_b.reshape(-1, 1, D_MODEL)]
        specs += [_layer_spec((1, D_MODEL), j), _layer_spec((1, D_MODEL), j)]
    if b is None:
        b = jnp.zeros((w.shape[0], D_MODEL), F32)
    args += [w, b.reshape(-1, 1, D_MODEL)]
    specs += [_layer_spec((k_in, D_MODEL), j), _layer_spec((1, D_MODEL), j)]
    args += [x, st.mod, st.mod, st.mod, norm_ff_g.reshape(-1, 1, D_MODEL), router_wt,
             router_b.reshape(-1, N_EXPERTS, 1)]
    specs += [st.row_spec(D_MODEL), st.mod_spec(2), st.mod_spec(3), st.mod_spec(4),
              _layer_spec((1, D_MODEL), layer), _layer_spec((N_EXPERTS, D_MODEL), layer),
              _layer_spec((N_EXPERTS, 1), layer)]
    scratch = [pltpu.VMEM((k_in, D_MODEL), BF16)]
    if mode == "conv":
        scratch += [pltpu.VMEM((tm + HALO, D_MODEL), F32), pltpu.VMEM((tm, D_MODEL), F32)]
    return pl.pallas_call(
        functools.partial(_mixer_post_kernel, mode, st.tiles_per_seq),
        grid=(st.n_tiles,),
        in_specs=specs,
        out_specs=[st.row_spec(D_MODEL), st.row_spec(D_MODEL),
                   pl.BlockSpec((TOP_K, tm), lambda i: (0, i)),
                   pl.BlockSpec((TOP_K, tm), lambda i: (0, i))],
        out_shape=[jax.ShapeDtypeStruct((st.n_rows, D_MODEL), F32),
                   jax.ShapeDtypeStruct((st.n_rows, D_MODEL), F32),
                   jax.ShapeDtypeStruct((TOP_K, st.n_rows), I32),
                   jax.ShapeDtypeStruct((TOP_K, st.n_rows), F32)],
        scratch_shapes=scratch,
        compiler_params=_cparams("arbitrary"),
        name="mixer_post_" + mode,
    )(*args)


def _pre_ret_kernel(x_ref, sh_ref, sc_ref, g_ref, w_ref, cos_ref, sin_ref,
                    q_ref, k_ref, v_ref, sg_ref):
    h = _rms_mod(x_ref[...], g_ref[...], sc_ref[...], sh_ref[...]).astype(BF16)
    cos = cos_ref[...]
    sin = sin_ref[...]
    half = RET_DK // 2
    qk = RET_HEADS * RET_DK
    vd = RET_HEADS * RET_DV
    for out_ref, base, scale in ((q_ref, 0, 1.0), (k_ref, qk, RET_DK ** -0.5)):
        for hh in range(RET_HEADS):
            c0 = base + hh * RET_DK
            p = jnp.dot(h, w_ref[:, c0:c0 + RET_DK], preferred_element_type=F32)
            p1, p2 = p[:, :half], p[:, half:]
            o0 = hh * RET_DK
            out_ref[:, o0:o0 + half] = ((p1 * cos - p2 * sin) * scale).astype(BF16)
            out_ref[:, o0 + half:o0 + RET_DK] = ((p1 * sin + p2 * cos) * scale).astype(BF16)
    for hh in range(RET_HEADS):
        c0 = 2 * qk + hh * RET_DV
        v_ref[:, hh * RET_DV:(hh + 1) * RET_DV] = jnp.dot(
            h, w_ref[:, c0:c0 + RET_DV], preferred_element_type=F32).astype(BF16)
        g = jnp.dot(h, w_ref[:, vd + c0:vd + c0 + RET_DV], preferred_element_type=F32)
        sg_ref[:, hh * RET_DV:(hh + 1) * RET_DV] = _silu(g).astype(BF16)


def _pre_ret(st, x, norm_g, w_in_b, layer, j, cos, sin, tbl_spec):
    qk = RET_HEADS * RET_DK
    vd = RET_HEADS * RET_DV
    return pl.pallas_call(
        _pre_ret_kernel,
        grid=(st.n_tiles,),
        in_specs=[
            st.row_spec(D_MODEL),
            st.mod_spec(0), st.mod_spec(1),
            _layer_spec((1, D_MODEL), layer),
            _layer_spec((D_MODEL, 2 * qk + 2 * vd), j),
            tbl_spec, tbl_spec,
        ],
        out_specs=[st.row_spec(qk), st.row_spec(qk), st.row_spec(vd), st.row_spec(vd)],
        out_shape=[jax.ShapeDtypeStruct((st.n_rows, qk), BF16),
                   jax.ShapeDtypeStruct((st.n_rows, qk), BF16),
                   jax.ShapeDtypeStruct((st.n_rows, vd), BF16),
                   jax.ShapeDtypeStruct((st.n_rows, vd), BF16)],
        compiler_params=_cparams("parallel"),
        name="pre_ret",
    )(x, st.mod, st.mod, norm_g.reshape(-1, 1, D_MODEL), w_in_b, cos, sin)


def _decay_matrix(lg, ch):
    diff = (lax.broadcasted_iota(I32, (ch, ch), 0)
            - lax.broadcasted_iota(I32, (ch, ch), 1)).astype(F32)
    return jnp.where(diff >= 0.0, jnp.exp(lg * jnp.maximum(diff, 0.0)), 0.0)


def _retention_chunk(valid, lg, dec, q, k, v, sg, s_prev):
    ch = q.shape[0]
    idx = lax.broadcasted_iota(I32, (ch, 1), 0).astype(F32)
    scores = lax.dot_general(q, k, (((1,), (1,)), ((), ())), preferred_element_type=F32) * dec
    o = jnp.dot(scores.astype(BF16), v, preferred_element_type=F32)
    cross = jnp.exp(lg * (idx + 1.0))
    o = o + jnp.dot(q, s_prev.astype(BF16), preferred_element_type=F32) * cross
    kd = (k.astype(F32) * jnp.exp(lg * (valid - 1.0 - idx))).astype(BF16)
    s_new = jnp.exp(lg * float(valid)) * s_prev + lax.dot_general(
        kd, v, (((0,), (0,)), ((), ())), preferred_element_type=F32)
    mu = jnp.mean(o, axis=-1, keepdims=True)
    oc = o - mu
    var = jnp.mean(oc * oc, axis=-1, keepdims=True)
    on = oc * lax.rsqrt(var + NORM_EPS)
    return (sg.astype(F32) * on).astype(BF16), s_new


def _retention_kernel(valid, lg_ref, q_ref, k_ref, v_ref, sg_ref, o_ref, s_out_ref, s_ref, dec_ref):
    c = pl.program_id(1)
    ch = q_ref.shape[0]

    @pl.when(c == 0)
    def _():
        s_ref[...] = jnp.zeros_like(s_ref)
        for h in range(RET_HEADS):
            dec_ref[h] = _decay_matrix(lg_ref[h, 0:1, 0:1], ch)

    for h in range(RET_HEADS):
        ks = slice(h * RET_DK, (h + 1) * RET_DK)
        vs = slice(h * RET_DV, (h + 1) * RET_DV)
        gated, s_new = _retention_chunk(valid, lg_ref[h, 0:1, 0:1], dec_ref[h], q_ref[:, ks],
                                        k_ref[:, ks], v_ref[:, vs], sg_ref[:, vs], s_ref[h])
        o_ref[:, vs] = gated
        s_ref[h] = s_new

    @pl.when(c == pl.num_programs(1) - 1)
    def _():
        s_out_ref[...] = s_ref[...]


def _retention_sample_kernel(lg_ref, q_ref, k_ref, v_ref, sg_ref, s0_ref, o_ref, s_out_ref):
    lg = lg_ref[0:1, 0:1]
    dec = _decay_matrix(lg, RET_PAD)
    for b in range(RET_SB):
        rows = slice(b * RET_PAD, (b + 1) * RET_PAD)
        gated, s_new = _retention_chunk(DEC_SEQ, lg, dec, q_ref[rows, :], k_ref[rows, :],
                                        v_ref[rows, :], sg_ref[rows, :], s0_ref[b])
        o_ref[rows, :] = gated
        s_out_ref[b] = s_new


def _retention_sample(qkvg, s0, s0_layer, lg_tbl):
    rows = RET_SB * RET_PAD
    row = lambda i, h: (i, h)
    nh = RET_HEADS
    state_block = (None, RET_SB, None, RET_DK, RET_DV)
    q = k = v = sg = qkvg
    return pl.pallas_call(
        _retention_sample_kernel,
        grid=(DEC_BATCH // RET_SB, RET_HEADS),
        in_specs=[
            pl.BlockSpec((None, 8, 128), lambda i, h: (h, 0, 0)),
            pl.BlockSpec((rows, RET_DK), row),
            pl.BlockSpec((rows, RET_DK), lambda i, h: (i, nh + h)),
            pl.BlockSpec((rows, RET_DV), lambda i, h: (i, nh + h)),
            pl.BlockSpec((rows, RET_DV), lambda i, h: (i, 2 * nh + h)),
            pl.BlockSpec(state_block, lambda i, h: (s0_layer, i, h, 0, 0)),
        ],
        out_specs=[pl.BlockSpec((rows, RET_DV), row),
                   pl.BlockSpec(state_block, lambda i, h: (0, i, h, 0, 0))],
        out_shape=[jax.ShapeDtypeStruct((DEC_BATCH * RET_PAD, RET_HEADS * RET_DV), BF16),
                   jax.ShapeDtypeStruct((1, DEC_BATCH, RET_HEADS, RET_DK, RET_DV), F32)],
        compiler_params=_cparams("parallel", "parallel"),
        name="retention_sample",
    )(lg_tbl, q, k, v, sg, s0)


def _retention(q, k, v, sg, lg_tbl, n_seq, n_chunks, ch, valid):
    n_rows = q.shape[0]
    qk = RET_HEADS * RET_DK
    vd = RET_HEADS * RET_DV
    row = lambda b, c: (b * n_chunks + c, 0)
    state_block = (None, None, RET_HEADS, RET_DK, RET_DV)
    return pl.pallas_call(
        functools.partial(_retention_kernel, valid),
        grid=(n_seq, n_chunks),
        in_specs=[
            pl.BlockSpec((RET_HEADS, 8, 128), lambda b, c: (0, 0, 0)),
            pl.BlockSpec((ch, qk), row), pl.BlockSpec((ch, qk), row),
            pl.BlockSpec((ch, vd), row), pl.BlockSpec((ch, vd), row),
        ],
        out_specs=[pl.BlockSpec((ch, vd), row),
                   pl.BlockSpec(state_block, lambda b, c: (0, b, 0, 0, 0))],
        out_shape=[jax.ShapeDtypeStruct((n_rows, vd), BF16),
                   jax.ShapeDtypeStruct((1, n_seq, RET_HEADS, RET_DK, RET_DV), F32)],
        scratch_shapes=[pltpu.VMEM((RET_HEADS, RET_DK, RET_DV), F32),
                        pltpu.VMEM((RET_HEADS, ch, ch), F32)],
        compiler_params=_cparams("parallel", "arbitrary"),
        name="retention",
    )(lg_tbl, q, k, v, sg)


def _excl_prefix(col):
    r = lax.broadcasted_iota(I32, (N_EXPERTS, N_EXPERTS), 0)
    c = lax.broadcasted_iota(I32, (N_EXPERTS, N_EXPERTS), 1)
    lane = jnp.sum(jnp.where(r < c, col, 0.0), axis=0, keepdims=True)
    return jnp.sum(jnp.where(r == c, lane, 0.0), axis=1, keepdims=True)


def _ceil_to(x, m):
    return jnp.floor((x + (m - 1.0)) * (1.0 / m)) * m


def _route_kernel(te_ref, lpos_ref, rstart_ref, rlen_ref, tot_ref, tri_ref):
    rt = te_ref.shape[2]
    iota_e = lax.broadcasted_iota(I32, (N_EXPERTS, rt), 0)
    s = lax.broadcasted_iota(I32, (rt, rt), 0)
    t = lax.broadcasted_iota(I32, (rt, rt), 1)
    tri_ref[...] = (s < t).astype(BF16)

    def tile_onehots(i):
        te = te_ref[i]
        onehots = [(te[k:k + 1, :] == iota_e).astype(F32) for k in range(TOP_K)]
        cnt = jnp.sum(onehots[0] + onehots[1] + onehots[2] + onehots[3], axis=1, keepdims=True)
        return onehots, _ceil_to(cnt, RUN_ALIGN)

    tot = lax.fori_loop(0, MOE_NT, lambda i, acc: acc + tile_onehots(i)[1],
                        jnp.zeros((N_EXPERTS, 1), F32))
    tot_ref[...] = tot

    def place(i, gbase):
        onehots, run = tile_onehots(i)
        base = _excl_prefix(run)
        for k in range(TOP_K):
            oh = onehots[k]
            before = jnp.dot(oh.astype(BF16), tri_ref[...], preferred_element_type=F32)
            pos = jnp.sum(oh * (base + before), axis=0, keepdims=True)
            lpos_ref[i, k:k + 1, :] = pos.astype(I32)
            base = base + jnp.sum(oh, axis=1, keepdims=True)
        rstart_ref[i] = jnp.broadcast_to(gbase, rstart_ref.shape[1:]).astype(I32)
        rlen_ref[i] = jnp.broadcast_to(run, rlen_ref.shape[1:]).astype(I32)
        return gbase + run

    lax.fori_loop(0, MOE_NT, place, _excl_prefix(_ceil_to(tot, MOE_TB)))


def _route(te):
    full = lambda shape: pl.BlockSpec(shape, lambda i: (0,) * len(shape))
    return pl.pallas_call(
        _route_kernel,
        grid=(1,),
        in_specs=[full(te.shape)],
        out_specs=[full(te.shape), full((MOE_NT, N_EXPERTS, 128)), full((MOE_NT, N_EXPERTS, 128)),
                   full((N_EXPERTS, 1))],
        out_shape=[jax.ShapeDtypeStruct(te.shape, I32),
                   jax.ShapeDtypeStruct((MOE_NT, N_EXPERTS, 128), I32),
                   jax.ShapeDtypeStruct((MOE_NT, N_EXPERTS, 128), I32),
                   jax.ShapeDtypeStruct((N_EXPERTS, 1), F32)],
        scratch_shapes=[pltpu.VMEM((MOE_RT, MOE_RT), BF16)],
        compiler_params=_cparams("arbitrary"),
        name="moe_route",
    )(te)


def _dispatch_kernel(n_prompt_tiles, rs_ref, rl_ref, tzs_ref, tzl_ref, nu_ref, hp_ref, hs_ref,
                     lpos_ref, xb_ref, xs_ref, sem, zero_ref):
    i = pl.program_id(0)
    slot = i % 2
    zero_sem = sem.at[2]

    def tail_copy(e):
        n = pl.multiple_of(tzl_ref[e], RUN_ALIGN)
        dst = pl.multiple_of(tzs_ref[e], RUN_ALIGN)
        return pltpu.make_async_copy(zero_ref.at[pl.ds(0, n)], xb_ref.at[pl.ds(dst, n)], zero_sem)

    def block_copy(j):
        dst = pl.multiple_of(j * MOE_TB, MOE_TB)
        return pltpu.make_async_copy(zero_ref, xb_ref.at[pl.ds(dst, MOE_TB)], zero_sem)

    @pl.when(i == 0)
    def _():
        zero_ref[...] = jnp.zeros_like(zero_ref)

        def start(e, carry):
            pl.when(tzl_ref[e] > 0)(lambda: tail_copy(e).start())
            return carry

        def wait(e, carry):
            pl.when(tzl_ref[e] > 0)(lambda: tail_copy(e).wait())
            return carry

        lax.fori_loop(0, N_EXPERTS, start, 0)
        lax.fori_loop(nu_ref[0], MOE_NBLK, lambda j, c: (block_copy(j).start(), c)[1], 0)
        lax.fori_loop(0, N_EXPERTS, wait, 0)
        lax.fori_loop(nu_ref[0], MOE_NBLK, lambda j, c: (block_copy(j).wait(), c)[1], 0)

    rows = lax.broadcasted_iota(I32, (MOE_CAP, MOE_RT), 0)
    perm = jnp.where(rows == lpos_ref[0:1, :], 1.0, 0.0)
    for k in range(1, TOP_K):
        perm = perm + jnp.where(rows == lpos_ref[k:k + 1, :], 1.0, 0.0)
    h = jnp.where(i < n_prompt_tiles, hp_ref[...], hs_ref[...])
    xs_ref[slot] = _pack_halves(
        jnp.dot(perm.astype(BF16), h.astype(BF16), preferred_element_type=F32))

    def scatter(tile, wait):
        def body(e, lo):
            n = pl.multiple_of(rl_ref[tile * N_EXPERTS + e], RUN_ALIGN)
            sl = tile % 2
            src = pl.multiple_of(lo, RUN_ALIGN)
            dst = pl.multiple_of(rs_ref[tile * N_EXPERTS + e], RUN_ALIGN)
            cp = pltpu.make_async_copy(xs_ref.at[sl, pl.ds(src, n)], xb_ref.at[pl.ds(dst, n)],
                                       sem.at[sl])

            @pl.when(n > 0)
            def _():
                if wait:
                    cp.wait()
                else:
                    cp.start()

            return lo + n

        lax.fori_loop(0, N_EXPERTS, body, 0)

    scatter(i, False)

    @pl.when(i > 0)
    def _():
        scatter(i - 1, True)

    @pl.when(i == pl.num_programs(0) - 1)
    def _():
        scatter(i, True)


def _dispatch(tables, h_p, h_s, lpos):
    nt_p = h_p.shape[0] // MOE_RT
    return pl.pallas_call(
        functools.partial(_dispatch_kernel, nt_p),
        grid_spec=pltpu.PrefetchScalarGridSpec(
            num_scalar_prefetch=len(tables),
            grid=(MOE_NT,),
            in_specs=[
                pl.BlockSpec((MOE_RT, D_MODEL), lambda i, *_: (jnp.minimum(i, nt_p - 1), 0)),
                pl.BlockSpec((MOE_RT, D_MODEL), lambda i, *_: (jnp.maximum(i - nt_p, 0), 0)),
                pl.BlockSpec((None, TOP_K, MOE_RT), lambda i, *_: (i, 0, 0)),
            ],
            out_specs=pl.BlockSpec(memory_space=pl.ANY),
            scratch_shapes=[pltpu.VMEM((2, MOE_CAP, D_MODEL // 2), U32),
                            pltpu.SemaphoreType.DMA((3,)),
                            pltpu.VMEM((MOE_TB, D_MODEL // 2), U32)],
        ),
        out_shape=jax.ShapeDtypeStruct((MOE_ROWS, D_MODEL // 2), U32),
        compiler_params=_cparams("arbitrary"),
        name="moe_dispatch",
    )(*tables, h_p, h_s, lpos)


def _ffn_kernel(layer, be_ref, nu_ref, nxt_ref, half_ref, bgu_ref, bdn_ref, xb_hbm, wgu_hbm,
                wdn_hbm, yb_hbm, wgu_f_ref, wdn_f_ref, wgu_b_ref, wdn_b_ref, act_ref, xbuf_ref,
                ybuf_ref, wsem, xsem, ysem):
    n_used = nu_ref[0]
    ybuf_ref[...] = jnp.zeros_like(ybuf_ref)

    def weight_copies(ex):
        return (pltpu.make_async_copy(wgu_hbm.at[layer, ex], wgu_f_ref, wsem.at[0]),
                pltpu.make_async_copy(wdn_hbm.at[layer, ex], wdn_f_ref, wsem.at[1]))

    def block_rows(j):
        return pl.ds(pl.multiple_of(j * MOE_TB, MOE_TB), MOE_TB)

    def x_copy(j):
        return pltpu.make_async_copy(xb_hbm.at[block_rows(j)], xbuf_ref.at[j % 2], xsem.at[j % 2])

    def y_copy(j):
        return pltpu.make_async_copy(ybuf_ref.at[j % 2], yb_hbm.at[block_rows(j)], ysem.at[j % 2])

    for cp in weight_copies(be_ref[0]):
        cp.start()
    x_copy(0).start()

    def block(j, carry):
        e = be_ref[j]
        e_prev = be_ref[jnp.maximum(j - 1, 0)]

        @pl.when(j + 1 < n_used)
        def _():
            x_copy(j + 1).start()

        @pl.when((j == 0) | (e != e_prev))
        def _():
            for cp in weight_copies(e):
                cp.wait()
            wgu_b_ref[...] = wgu_f_ref[...].astype(BF16)
            wdn_b_ref[...] = wdn_f_ref[...].astype(BF16)
            nx = nxt_ref[j]

            @pl.when(nx >= 0)
            def _():
                for cp in weight_copies(nx):
                    cp.start()

        x_copy(j).wait()

        @pl.when(j >= 2)
        def _():
            y_copy(j - 2).wait()

        def expert_ffn(m):
            x = _unpack_halves(xbuf_ref[j % 2, 0:m, :])
            bgu = bgu_ref[e]
            for c in range(D_FF // FF_CHUNK):
                gs = slice(c * FF_CHUNK, (c + 1) * FF_CHUNK)
                us = slice(D_FF + c * FF_CHUNK, D_FF + (c + 1) * FF_CHUNK)
                gate = jnp.dot(x, wgu_b_ref[:, gs], preferred_element_type=F32) + bgu[:, gs]
                up = jnp.dot(x, wgu_b_ref[:, us], preferred_element_type=F32) + bgu[:, us]
                gate = jnp.minimum(gate, SWIGLU_LIMIT)
                up = jnp.clip(up, -SWIGLU_LIMIT, SWIGLU_LIMIT)
                act = (up + 1.0) * (gate * jax.nn.sigmoid(SWIGLU_ALPHA * gate))
                act_ref[0:m, gs] = act.astype(BF16)
            y = jnp.dot(act_ref[0:m, :], wdn_b_ref[...], preferred_element_type=F32) + bdn_ref[e]
            ybuf_ref[j % 2, 0:m, :] = _pack_halves(y.astype(BF16).astype(F32))

        pl.when(half_ref[j] == 0)(lambda: expert_ffn(MOE_TB))
        pl.when(half_ref[j] != 0)(lambda: expert_ffn(MOE_TB // 2))
        y_copy(j).start()
        return carry

    lax.fori_loop(0, n_used, block, 0)

    @pl.when(n_used >= 2)
    def _():
        y_copy(n_used - 2).wait()

    y_copy(n_used - 1).wait()

    ybuf_ref[0] = jnp.zeros_like(ybuf_ref[0])

    def zero_copy(j):
        return pltpu.make_async_copy(ybuf_ref.at[0], yb_hbm.at[block_rows(j)], ysem.at[0])

    lax.fori_loop(n_used, MOE_NBLK, lambda j, c: (zero_copy(j).start(), c)[1], 0)
    lax.fori_loop(n_used, MOE_NBLK, lambda j, c: (zero_copy(j).wait(), c)[1], 0)


def _ffn(layer, block_e, n_used, nxt, half, xb, w_gu, b_gu, w_dn, b_dn):
    bias_map = lambda i, *_: (layer, 0, 0, 0)
    return pl.pallas_call(
        functools.partial(_ffn_kernel, layer),
        grid_spec=pltpu.PrefetchScalarGridSpec(
            num_scalar_prefetch=4,
            grid=(1,),
            in_specs=[
                pl.BlockSpec((None, N_EXPERTS, 1, 2 * D_FF), bias_map),
                pl.BlockSpec((None, N_EXPERTS, 1, D_MODEL), bias_map),
                pl.BlockSpec(memory_space=pl.ANY),
                pl.BlockSpec(memory_space=pl.ANY),
                pl.BlockSpec(memory_space=pl.ANY),
            ],
            out_specs=pl.BlockSpec(memory_space=pl.ANY),
            scratch_shapes=[pltpu.VMEM((D_MODEL, 2 * D_FF), F32),
                            pltpu.VMEM((D_FF, D_MODEL), F32),
                            pltpu.VMEM((D_MODEL, 2 * D_FF), BF16),
                            pltpu.VMEM((D_FF, D_MODEL), BF16),
                            pltpu.VMEM((MOE_TB, D_FF), BF16),
                            pltpu.VMEM((2, MOE_TB, D_MODEL // 2), U32),
                            pltpu.VMEM((2, MOE_TB, D_MODEL // 2), U32),
                            pltpu.SemaphoreType.DMA((2,)),
                            pltpu.SemaphoreType.DMA((2,)),
                            pltpu.SemaphoreType.DMA((2,))],
        ),
        out_shape=jax.ShapeDtypeStruct((MOE_ROWS, D_MODEL // 2), U32),
        compiler_params=_cparams("arbitrary"),
        name="moe_ffn",
    )(block_e, n_used, nxt, half, b_gu.reshape(b_gu.shape[0], N_EXPERTS, 1, 2 * D_FF),
      b_dn.reshape(b_dn.shape[0], N_EXPERTS, 1, D_MODEL), xb, w_gu, w_dn)


def _tile_rows(v, n_rows):
    mr = v.shape[0]
    if mr in (1, n_rows):
        return v
    return jnp.concatenate([v] * (n_rows // mr), axis=0)


def _combine_kernel(tile_off, final, rs_ref, rl_ref, x_ref, gf_ref, lpt_ref, wt_ref, fg_ref,
                    yb_ref, out_ref, ys_ref, sem):
    i = pl.program_id(0)

    def gather(tile, wait):
        def body(e, lo):
            n = pl.multiple_of(rl_ref[(tile_off + tile) * N_EXPERTS + e], RUN_ALIGN)
            sl = tile % 2
            src = pl.multiple_of(rs_ref[(tile_off + tile) * N_EXPERTS + e], RUN_ALIGN)
            dst = pl.multiple_of(lo, RUN_ALIGN)
            cp = pltpu.make_async_copy(yb_ref.at[pl.ds(src, n)], ys_ref.at[sl, pl.ds(dst, n)],
                                       sem.at[sl])

            @pl.when(n > 0)
            def _():
                if wait:
                    cp.wait()
                else:
                    cp.start()

            return lo + n

        lax.fori_loop(0, N_EXPERTS, body, 0)

    @pl.when(i == 0)
    def _():
        ys_ref[...] = jnp.zeros_like(ys_ref)
        gather(0, False)

    @pl.when(i + 1 < pl.num_programs(0))
    def _():
        gather(i + 1, False)

    gather(i, True)

    cols = lax.broadcasted_iota(I32, (MOE_RT, MOE_CAP), 1)
    w = jnp.where(cols == lpt_ref[:, 0:1], wt_ref[:, 0:1], 0.0)
    for k in range(1, TOP_K):
        w = w + jnp.where(cols == lpt_ref[:, k:k + 1], wt_ref[:, k:k + 1], 0.0)
    ys = _unpack_halves(ys_ref[i % 2])
    y = jnp.dot(w.astype(BF16), ys, preferred_element_type=F32)
    xn = x_ref[...] + _tile_rows(gf_ref[...], MOE_RT) * y
    if final:
        xn = xn * lax.rsqrt(jnp.mean(xn * xn, axis=-1, keepdims=True) + NORM_EPS) * fg_ref[...]
    out_ref[...] = xn


def _combine(tile_off, n_tiles, final, tables, x, mod, mod_map, lpos_t, wt, final_g, yb):
    pm = lambda i, *_: (i, 0)
    tm = lambda i, *_: (i + tile_off, 0)
    return pl.pallas_call(
        functools.partial(_combine_kernel, tile_off, final),
        grid_spec=pltpu.PrefetchScalarGridSpec(
            num_scalar_prefetch=len(tables),
            grid=(n_tiles,),
            in_specs=[
                pl.BlockSpec((MOE_RT, D_MODEL), pm),
                pl.BlockSpec((None, mod.shape[1], D_MODEL), mod_map),
                pl.BlockSpec((MOE_RT, TOP_K), tm),
                pl.BlockSpec((MOE_RT, TOP_K), tm),
                pl.BlockSpec((1, D_MODEL), lambda i, *_: (0, 0)),
                pl.BlockSpec(memory_space=pl.ANY),
            ],
            out_specs=pl.BlockSpec((MOE_RT, D_MODEL), pm),
            scratch_shapes=[pltpu.VMEM((2, MOE_CAP, D_MODEL // 2), U32),
                            pltpu.SemaphoreType.DMA((2,))],
        ),
        out_shape=jax.ShapeDtypeStruct(x.shape, F32),
        compiler_params=_cparams("arbitrary"),
        name="moe_combine",
    )(*tables, x, mod, lpos_t, wt, final_g.reshape(1, D_MODEL), yb)


def _moe(layer, final, st_p, st_s, x_p, x_s, post_p, post_s,
         w_gu, b_gu, w_dn, b_dn, final_g):
    te = jnp.concatenate([post_p[1], post_s[1]], axis=1)
    tw = jnp.concatenate([post_p[2], post_s[2]], axis=1)
    te = jnp.swapaxes(te.reshape(TOP_K, MOE_NT, MOE_RT), 0, 1)
    lpos, rstart, rlen, tot = _route(te)
    tot = tot[:, 0].astype(I32)
    padded = (tot + MOE_TB - 1) // MOE_TB * MOE_TB
    pends = jnp.cumsum(padded)
    n_used = (pends[-1] // MOE_TB).reshape(1)
    blk = jnp.minimum(jnp.arange(MOE_NBLK, dtype=I32), n_used[0] - 1) * MOE_TB
    block_e = jnp.minimum(jnp.sum(pends[None, :] <= blk[:, None], axis=-1),
                          N_EXPERTS - 1).astype(I32)
    ids = jnp.where(tot > 0, jnp.arange(N_EXPERTS, dtype=I32), N_EXPERTS)
    later = jnp.concatenate([lax.cummin(ids[::-1])[::-1][1:], jnp.full((1,), N_EXPERTS, I32)])
    nxt_of = jnp.where(later < N_EXPERTS, later, -1)
    of_block = block_e[:, None] == jnp.arange(N_EXPERTS, dtype=I32)[None, :]
    nxt = jnp.sum(jnp.where(of_block, nxt_of[None, :], 0), axis=-1)
    rows_end = jnp.sum(jnp.where(of_block, (pends - padded + tot)[None, :], 0), axis=-1)
    half = (rows_end - jnp.arange(MOE_NBLK, dtype=I32) * MOE_TB <= MOE_TB // 2).astype(I32)
    run_tables = (rstart[:, :, 0].reshape(-1), rlen[:, :, 0].reshape(-1))
    tail_tables = (pends - padded + tot, padded - tot)

    nt_p = N_PROMPT // MOE_RT
    nt_s = N_SAMPLE // MOE_RT
    xb = _dispatch(run_tables + tail_tables + (n_used,), post_p[0], post_s[0], lpos)
    yb = _ffn(layer, block_e, n_used, nxt, half, xb, w_gu, b_gu, w_dn, b_dn)
    lpos_t = jnp.swapaxes(lpos, 1, 2).reshape(N_TOK, TOP_K)
    wt = tw.T
    tiles_per_seq = SEQ // MOE_RT
    out_p = _combine(0, nt_p, final, run_tables, x_p, st_p.mod,
                     lambda i, *_: (i // tiles_per_seq, 0, N_MOD - 1), lpos_t, wt, final_g, yb)
    out_s = _combine(nt_p, nt_s, final, run_tables, x_s, st_s.mod,
                     lambda i, *_: (0, 0, N_MOD - 1), lpos_t, wt, final_g, yb)
    return out_p, out_s


def _rope_tables(pos):
    half = RET_DK // 2
    inv_freq = ROPE_BASE ** (-jnp.arange(half, dtype=F32) / half)
    ang = pos.astype(F32)[:, None] * inv_freq[None, :]
    return jnp.cos(ang), jnp.sin(ang)


def kernel(x_prompt, x_sample, c_prompt, c_sample, state_conv, state_ret, norm_mix_g, norm_ff_g,
           w_mod, b_mod, conv_w1, conv_b1, conv_dw, conv_dw_b, conv_ln_g, conv_ln_b, conv_w2,
           conv_b2, ret_w_in, ret_w_o, router_w, router_b, moe_w_gu, moe_b_gu, moe_w_dn,
           moe_b_dn, final_g):
    mod = _adaln(jnp.concatenate([c_prompt, c_sample], axis=0), w_mod, b_mod)
    x_p = x_prompt.reshape(N_PROMPT, D_MODEL)
    x_s = jnp.swapaxes(x_sample, 0, 1).reshape(N_SAMPLE, D_MODEL)
    router_wt = jnp.swapaxes(router_w, 1, 2)

    def streams(layer):
        st_p = _Stream(N_PROMPT, TM_PROMPT, SEQ // TM_PROMPT,
                       mod[layer, :BATCH].reshape(BATCH, 1, N_MOD * D_MODEL))
        st_s = _Stream(N_SAMPLE, TM_SAMPLE, None,
                       mod[layer, BATCH:].reshape(1, DEC_BATCH, N_MOD * D_MODEL))
        return st_p, st_s

    st_p, st_s = streams(0)
    u_p = _pre_conv(st_p, x_p, norm_mix_g, conv_w1, conv_b1, 0, 0)
    u_s = _pre_conv(st_s, x_s, norm_mix_g, conv_w1, conv_b1, 0, 0)
    buf_tm = jnp.swapaxes(state_conv[0], 0, 1)
    z_s, ns_tm = _conv_sample(buf_tm, u_s.reshape(DEC_SEQ, DEC_BATCH, D_MODEL),
                              conv_dw, conv_dw_b, 0)
    post_p = _mixer_post("conv", st_p, 0, 0, x_p, u_p, norm_ff_g, router_wt, router_b,
                         conv_w2, conv_b2, conv_dw, conv_dw_b, conv_ln_g, conv_ln_b)
    post_s = _mixer_post("z", st_s, 0, 0, x_s, z_s.reshape(N_SAMPLE, D_MODEL), norm_ff_g,
                         router_wt, router_b, conv_w2, conv_b2, None, None, conv_ln_g, conv_ln_b)
    x_p, x_s = _moe(0, False, st_p, st_s, post_p[0], post_s[0], post_p[1:], post_s[1:],
                    moe_w_gu, moe_b_gu, moe_w_dn, moe_b_dn, final_g)
    conv_p = u_p.reshape(BATCH, SEQ, D_MODEL)[:, SEQ - CONV_STATE:][None]
    conv_s = jnp.swapaxes(ns_tm, 0, 1)[None]

    st_p, st_s = streams(1)
    w_in_b = ret_w_in.astype(BF16)
    cos_p, sin_p = _rope_tables(jnp.arange(SEQ, dtype=I32))
    cos_s, sin_s = _rope_tables(PAST_LEN + jnp.arange(DEC_SEQ, dtype=I32))
    half = RET_DK // 2
    tps = SEQ // TM_PROMPT
    tbl_p = pl.BlockSpec((TM_PROMPT, half), lambda i: (i % tps, 0))
    tbl_s = pl.BlockSpec((None, 1, half), lambda i: (i, 0, 0))
    q_p, k_p, v_p, sg_p = _pre_ret(st_p, x_p, norm_mix_g, w_in_b, 1, 0, cos_p, sin_p, tbl_p)
    q_s, k_s, v_s, sg_s = _pre_ret(st_s, x_s, norm_mix_g, w_in_b, 1, 0,
                                   cos_s.reshape(DEC_SEQ, 1, half),
                                   sin_s.reshape(DEC_SEQ, 1, half), tbl_s)
    lg = jnp.log1p(-jnp.exp2(-5.0 - jnp.arange(RET_HEADS, dtype=F32)))
    lg_tbl = jnp.broadcast_to(lg[:, None, None], (RET_HEADS, 8, 128))
    gated_p, ret_p = _retention(q_p, k_p, v_p, sg_p, lg_tbl,
                                BATCH, SEQ // RET_CHUNK, RET_CHUNK, RET_CHUNK)

    def to_seq_major(a):
        a = jnp.swapaxes(a.reshape(DEC_SEQ, DEC_BATCH, -1), 0, 1)
        a = jnp.pad(a, ((0, 0), (0, RET_PAD - DEC_SEQ), (0, 0)))
        return a.reshape(DEC_BATCH * RET_PAD, -1)

    qkvg_s = to_seq_major(jnp.concatenate([q_s, k_s, v_s, sg_s], axis=1))
    gated_s, ret_s = _retention_sample(qkvg_s, state_ret, 0, lg_tbl)
    gated_s = jnp.swapaxes(gated_s.reshape(DEC_BATCH, RET_PAD, -1)[:, :DEC_SEQ], 0, 1)
    gated_s = gated_s.reshape(N_SAMPLE, -1)
    post_p = _mixer_post("ret", st_p, 1, 0, x_p, gated_p, norm_ff_g, router_wt, router_b, ret_w_o)
    post_s = _mixer_post("ret", st_s, 1, 0, x_s, gated_s, norm_ff_g, router_wt, router_b, ret_w_o)
    y_p, y_s = _moe(1, True, st_p, st_s, post_p[0], post_s[0], post_p[1:], post_s[1:],
                    moe_w_gu, moe_b_gu, moe_w_dn, moe_b_dn, final_g)

    y_prompt = y_p.reshape(BATCH, SEQ, D_MODEL)
    y_sample = jnp.swapaxes(y_s.reshape(DEC_SEQ, DEC_BATCH, D_MODEL), 0, 1)
    return (y_prompt, y_sample, conv_p, conv_s, ret_p, ret_s)
```

```python
import functools

import jax
import jax.numpy as jnp
from jax import lax
from jax.experimental import pallas as pl
from jax.experimental.pallas import tpu as pltpu

F32 = jnp.float32
BF16 = jnp.bfloat16
I32 = jnp.int32
U32 = jnp.uint32

D_MODEL = 1024
SEQ = 2048
BATCH = 8
DEC_BATCH = 128
DEC_SEQ = 4
PAST_LEN = 16384
CONV_WIDTH = 31
CONV_STATE = CONV_WIDTH - 1
RET_HEADS = 4
RET_DK = 256
RET_DV = 512
ROPE_BASE = 10000.0
N_EXPERTS = 32
TOP_K = 4
D_FF = 1024
SWIGLU_LIMIT = 7.0
SWIGLU_ALPHA = 1.702
N_MOD = 6
NORM_EPS = 1e-5

N_PROMPT = BATCH * SEQ
N_SAMPLE = DEC_BATCH * DEC_SEQ
N_TOK = N_PROMPT + N_SAMPLE
N_ASSIGN = N_TOK * TOP_K

TM_PROMPT = 512
TM_SAMPLE = DEC_BATCH
HALO = 32
CONV_RB = 64
CONV_LW = 128
RET_CHUNK = 256
RET_PAD = 16
RET_SS = DEC_BATCH // (BATCH * (SEQ // RET_CHUNK))
MOE_TB = 512
MOE_RT = 256
MOE_NT = N_TOK // MOE_RT
RUN_ALIGN = 8
MOE_CAP = -(-(TOP_K * MOE_RT + N_EXPERTS * (RUN_ALIGN - 1)) // 256) * 256
MOE_MAX_ROWS = N_ASSIGN + MOE_NT * N_EXPERTS * (RUN_ALIGN - 1)
MOE_NBLK = -(-MOE_MAX_ROWS // MOE_TB) + N_EXPERTS
MOE_ROWS = MOE_NBLK * MOE_TB
FF_CHUNK = 256
VMEM_LIMIT = 56 * 1024 * 1024


def _cparams(*sem):
    return pltpu.CompilerParams(dimension_semantics=sem, vmem_limit_bytes=VMEM_LIMIT)


def _silu(x):
    return x * jax.nn.sigmoid(x)


def _rms_mod(x, g, sc, sh):
    y = x * lax.rsqrt(jnp.mean(x * x, axis=-1, keepdims=True) + NORM_EPS) * g
    return y * (1.0 + sc) + sh


def _split_bf16(x):
    hi = x.astype(BF16)
    lo = (x - hi.astype(F32)).astype(BF16)
    return hi, lo


def _pack_halves(x):
    c = x.shape[1] // 2
    lo = lax.bitcast_convert_type(x[:, :c], U32)
    hi = lax.bitcast_convert_type(x[:, c:], U32)
    return (lo >> 16) | (hi & jnp.uint32(0xFFFF0000))


def _unpack_halves(p):
    lo = lax.bitcast_convert_type(p << 16, F32)
    hi = lax.bitcast_convert_type(p & jnp.uint32(0xFFFF0000), F32)
    return jnp.concatenate([lo.astype(BF16), hi.astype(BF16)], axis=1)


def _adaln_kernel(c_ref, w_ref, b_ref, o_ref):
    a = _silu(c_ref[...]).astype(BF16)
    o_ref[...] = jnp.dot(a, w_ref[...].astype(BF16), preferred_element_type=F32) + b_ref[...]


def _adaln(c_all, w_mod, b_mod):
    depth, d, nm = w_mod.shape
    n = c_all.shape[0]
    return pl.pallas_call(
        _adaln_kernel,
        grid=(depth, nm // d),
        in_specs=[
            pl.BlockSpec((n, d), lambda l, j: (0, 0)),
            pl.BlockSpec((None, d, d), lambda l, j: (l, 0, j)),
            pl.BlockSpec((None, 1, d), lambda l, j: (l, 0, j)),
        ],
        out_specs=pl.BlockSpec((None, n, d), lambda l, j: (l, 0, j)),
        out_shape=jax.ShapeDtypeStruct((depth, n, nm), F32),
        compiler_params=_cparams("parallel", "parallel"),
        name="adaln",
    )(c_all, w_mod, b_mod.reshape(depth, 1, nm))


class _Stream:
    def __init__(self, n_rows, tm, tiles_per_seq, mod):
        self.n_rows = n_rows
        self.tm = tm
        self.n_tiles = n_rows // tm
        self.tiles_per_seq = tiles_per_seq
        self.mod = mod

    def mod_spec(self, j):
        mr = self.mod.shape[1]
        if self.tiles_per_seq is None:
            return pl.BlockSpec((None, mr, D_MODEL), lambda i: (0, 0, j))
        tps = self.tiles_per_seq
        return pl.BlockSpec((None, mr, D_MODEL), lambda i: (i // tps, 0, j))

    def row_spec(self, width, col=0):
        return pl.BlockSpec((self.tm, width), lambda i: (i, col))


def _const_spec(shape):
    nd = len(shape)
    return pl.BlockSpec(shape, lambda i: (0,) * nd)


def _layer_spec(shape, layer):
    nd = len(shape)
    return pl.BlockSpec((None,) + shape, lambda i: (layer,) + (0,) * nd)


def _pre_conv_kernel(x_ref, sh_ref, sc_ref, g_ref, w1_ref, b1_ref, u_ref, w1b_ref):
    @pl.when(pl.program_id(0) == 0)
    def _():
        w1b_ref[...] = w1_ref[...].astype(BF16)

    h = _rms_mod(x_ref[...], g_ref[...], sc_ref[...], sh_ref[...])
    ag = jnp.dot(h.astype(BF16), w1b_ref[...], preferred_element_type=F32) + b1_ref[...]
    u_ref[...] = ag[:, :D_MODEL] * jax.nn.sigmoid(ag[:, D_MODEL:])


def _pre_conv(st, x, norm_g, w1, b1, layer, j):
    c2 = w1.shape[-1]
    return pl.pallas_call(
        _pre_conv_kernel,
        grid=(st.n_tiles,),
        in_specs=[
            st.row_spec(D_MODEL),
            st.mod_spec(0), st.mod_spec(1),
            _layer_spec((1, D_MODEL), layer),
            _layer_spec((D_MODEL, c2), j),
            _layer_spec((1, c2), j),
        ],
        out_specs=st.row_spec(D_MODEL),
        out_shape=jax.ShapeDtypeStruct((st.n_rows, D_MODEL), F32),
        scratch_shapes=[pltpu.VMEM((D_MODEL, c2), BF16)],
        compiler_params=_cparams("arbitrary"),
        name="pre_conv",
    )(x, st.mod, st.mod, norm_g.reshape(-1, 1, D_MODEL), w1, b1.reshape(-1, 1, c2))


def _conv_sample_kernel(buf_ref, u_ref, dw_ref, dwb_ref, z_ref, ns_ref):
    for t in range(DEC_SEQ):
        acc = jnp.broadcast_to(dwb_ref[...], u_ref.shape[1:])
        for j in range(t, CONV_STATE):
            acc = acc + buf_ref[j] * dw_ref[j - t:j - t + 1, :]
        for s in range(t + 1):
            k = CONV_STATE + s - t
            acc = acc + u_ref[s] * dw_ref[k:k + 1, :]
        z_ref[t] = acc
    ns_ref[0:CONV_STATE - DEC_SEQ] = buf_ref[DEC_SEQ:CONV_STATE]
    ns_ref[CONV_STATE - DEC_SEQ:CONV_STATE] = u_ref[...]


def _conv_sample(buf_tm, u_tm, dw, dwb, j):
    sb = 32
    c = u_tm.shape[-1]
    return pl.pallas_call(
        _conv_sample_kernel,
        grid=(DEC_BATCH // sb,),
        in_specs=[
            pl.BlockSpec((CONV_STATE, sb, c), lambda i: (0, i, 0)),
            pl.BlockSpec((DEC_SEQ, sb, c), lambda i: (0, i, 0)),
            _layer_spec((CONV_WIDTH, c), j),
            _layer_spec((1, c), j),
        ],
        out_specs=[
            pl.BlockSpec((DEC_SEQ, sb, c), lambda i: (0, i, 0)),
            pl.BlockSpec((CONV_STATE, sb, c), lambda i: (0, i, 0)),
        ],
        out_shape=[
            jax.ShapeDtypeStruct((DEC_SEQ, DEC_BATCH, c), F32),
            jax.ShapeDtypeStruct((CONV_STATE, DEC_BATCH, c), F32),
        ],
        compiler_params=_cparams("parallel"),
        name="conv_sample",
    )(buf_tm, u_tm, dw, dwb.reshape(-1, 1, c))


def _route_topk(h2, rwt_ref, rb_ref, te_ref, tw_ref):
    h_hi, h_lo = _split_bf16(h2)
    w_hi, w_lo = _split_bf16(rwt_ref[...])
    nt = (((1,), (1,)), ((), ()))
    logits = (lax.dot_general(w_hi, h_hi, nt, preferred_element_type=F32)
              + lax.dot_general(w_hi, h_lo, nt, preferred_element_type=F32)
              + lax.dot_general(w_lo, h_hi, nt, preferred_element_type=F32)
              + rb_ref[...])
    iota_e = lax.broadcasted_iota(I32, logits.shape, 0).astype(F32)
    vals, idxs = [], []
    for _ in range(TOP_K):
        m = jnp.max(logits, axis=0, keepdims=True)
        idx = jnp.min(jnp.where(logits == m, iota_e, float(N_EXPERTS)), axis=0, keepdims=True)
        logits = jnp.where(iota_e == idx, -jnp.inf, logits)
        vals.append(m)
        idxs.append(idx)
    es = [jnp.exp(v - vals[0]) for v in vals]
    tot = es[0] + es[1] + es[2] + es[3]
    for k in range(TOP_K):
        te_ref[k:k + 1, :] = idxs[k].astype(I32)
        tw_ref[k:k + 1, :] = es[k] / tot


def _mixer_post_kernel(mode, tiles_per_seq, *refs):
    it = iter(refs)
    if mode == "conv":
        u_ref, halo_ref, dw_ref, dwb_ref = next(it), next(it), next(it), next(it)
    elif mode == "z":
        z_in_ref = next(it)
    else:
        gated_ref = next(it)
    if mode in ("conv", "z"):
        lng_ref, lnb_ref = next(it), next(it)
    w_ref, b_ref = next(it), next(it)
    x_ref, gm_ref, shf_ref, scf_ref, gff_ref, rwt_ref, rb_ref = (next(it) for _ in range(7))
    x1_ref, h2_ref, te_ref, tw_ref = (next(it) for _ in range(4))
    wb_ref = next(it)
    if mode == "conv":
        ext_ref, z_ref = next(it), next(it)

    i = pl.program_id(0)

    @pl.when(i == 0)
    def _():
        wb_ref[...] = w_ref[...].astype(BF16)

    if mode == "conv":
        tm = u_ref.shape[0]
        first = (i % tiles_per_seq) == 0
        ext_ref[0:HALO, :] = jnp.where(first, 0.0, halo_ref[...])
        ext_ref[HALO:HALO + tm, :] = u_ref[...]
        off = HALO - CONV_STATE

        def row_block(r, carry):
            r0 = pl.multiple_of(r * CONV_RB, CONV_RB)
            for lc in range(D_MODEL // CONV_LW):
                ls = slice(lc * CONV_LW, (lc + 1) * CONV_LW)
                win = ext_ref[pl.ds(r0, CONV_RB + HALO), ls]
                acc = jnp.broadcast_to(dwb_ref[:, ls], (CONV_RB, CONV_LW))
                for s in range(8):
                    ws = win if s == 0 else pltpu.roll(win, CONV_RB + HALO - s, 0)
                    for k in range(CONV_WIDTH):
                        if (off + k) % 8 == s:
                            a = off + k - s
                            acc = acc + ws[a:a + CONV_RB, :] * dw_ref[k:k + 1, ls]
                z_ref[pl.ds(r0, CONV_RB), ls] = acc
            return carry

        lax.fori_loop(0, tm // CONV_RB, row_block, 0)
        z = z_ref[...]
    elif mode == "z":
        z = z_in_ref[...]

    if mode in ("conv", "z"):
        mu = jnp.mean(z, axis=-1, keepdims=True)
        zc = z - mu
        var = jnp.mean(zc * zc, axis=-1, keepdims=True)
        zn = zc * lax.rsqrt(var + NORM_EPS) * lng_ref[...] + lnb_ref[...]
        a = _silu(zn).astype(BF16)
        y = jnp.dot(a, wb_ref[...], preferred_element_type=F32) + b_ref[...]
    else:
        y = jnp.dot(gated_ref[...], wb_ref[...], preferred_element_type=F32)

    x1 = x_ref[...] + gm_ref[...] * y
    x1_ref[...] = x1
    h2 = _rms_mod(x1, gff_ref[...], scf_ref[...], shf_ref[...])
    h2_ref[...] = h2
    _route_topk(h2, rwt_ref, rb_ref, te_ref, tw_ref)


def _mixer_post(mode, st, layer, j, x, mix_in, norm_ff_g, router_wt, router_b, w, b=None,
                dw=None, dwb=None, ln_g=None, ln_b=None):
    tm = st.tm
    k_in = w.shape[-2]
    args, specs = [], []
    if mode == "conv":
        per32 = tm // HALO
        args += [mix_in, mix_in, dw, dwb.reshape(-1, 1, D_MODEL)]
        specs += [st.row_spec(D_MODEL),
                  pl.BlockSpec((HALO, D_MODEL), lambda i: (jnp.maximum(i * per32 - 1, 0), 0)),
                  _layer_spec((CONV_WIDTH, D_MODEL), j), _layer_spec((1, D_MODEL), j)]
    elif mode == "z":
        args += [mix_in]
        specs += [st.row_spec(D_MODEL)]
    else:
        args += [mix_in]
        specs += [st.row_spec(k_in)]
    if mode in ("conv", "z"):
        args += [ln_g.reshape(-1, 1, D_MODEL), ln_b.reshape(-1, 1, D_MODEL)]
        specs += [_layer_spec((1, D_MODEL), j), _layer_spec((1, D_MODEL), j)]
    if b is None:
        b = jnp.zeros((w.shape[0], D_MODEL), F32)
    args += [w, b.reshape(-1, 1, D_MODEL)]
    specs += [_layer_spec((k_in, D_MODEL), j), _layer_spec((1, D_MODEL), j)]
    args += [x, st.mod, st.mod, st.mod, norm_ff_g.reshape(-1, 1, D_MODEL), router_wt,
             router_b.reshape(-1, N_EXPERTS, 1)]
    specs += [st.row_spec(D_MODEL), st.mod_spec(2), st.mod_spec(3), st.mod_spec(4),
              _layer_spec((1, D_MODEL), layer), _layer_spec((N_EXPERTS, D_MODEL), layer),
              _layer_spec((N_EXPERTS, 1), layer)]
    scratch = [pltpu.VMEM((k_in, D_MODEL), BF16)]
    if mode == "conv":
        scratch += [pltpu.VMEM((tm + HALO, D_MODEL), F32), pltpu.VMEM((tm, D_MODEL), F32)]
    return pl.pallas_call(
        functools.partial(_mixer_post_kernel, mode, st.tiles_per_seq),
        grid=(st.n_tiles,),
        in_specs=specs,
        out_specs=[st.row_spec(D_MODEL), st.row_spec(D_MODEL),
                   pl.BlockSpec((TOP_K, tm), lambda i: (0, i)),
                   pl.BlockSpec((TOP_K, tm), lambda i: (0, i))],
        out_shape=[jax.ShapeDtypeStruct((st.n_rows, D_MODEL), F32),
                   jax.ShapeDtypeStruct((st.n_rows, D_MODEL), F32),
                   jax.ShapeDtypeStruct((TOP_K, st.n_rows), I32),
                   jax.ShapeDtypeStruct((TOP_K, st.n_rows), F32)],
        scratch_shapes=scratch,
        compiler_params=_cparams("arbitrary"),
        name="mixer_post_" + mode,
    )(*args)


def _pre_ret_kernel(x_ref, sh_ref, sc_ref, g_ref, w_ref, cos_ref, sin_ref,
                    q_ref, k_ref, v_ref, sg_ref):
    h = _rms_mod(x_ref[...], g_ref[...], sc_ref[...], sh_ref[...]).astype(BF16)
    cos = cos_ref[...]
    sin = sin_ref[...]
    half = RET_DK // 2
    qk = RET_HEADS * RET_DK
    vd = RET_HEADS * RET_DV
    for out_ref, base, scale in ((q_ref, 0, 1.0), (k_ref, qk, RET_DK ** -0.5)):
        for hh in range(RET_HEADS):
            c0 = base + hh * RET_DK
            p = jnp.dot(h, w_ref[:, c0:c0 + RET_DK], preferred_element_type=F32)
            p1, p2 = p[:, :half], p[:, half:]
            o0 = hh * RET_DK
            out_ref[:, o0:o0 + half] = ((p1 * cos - p2 * sin) * scale).astype(BF16)
            out_ref[:, o0 + half:o0 + RET_DK] = ((p1 * sin + p2 * cos) * scale).astype(BF16)
    for hh in range(RET_HEADS):
        c0 = 2 * qk + hh * RET_DV
        v_ref[:, hh * RET_DV:(hh + 1) * RET_DV] = jnp.dot(
            h, w_ref[:, c0:c0 + RET_DV], preferred_element_type=F32).astype(BF16)
        g = jnp.dot(h, w_ref[:, vd + c0:vd + c0 + RET_DV], preferred_element_type=F32)
        sg_ref[:, hh * RET_DV:(hh + 1) * RET_DV] = _silu(g).astype(BF16)


def _pre_ret(st, x, norm_g, w_in_b, layer, j, cos, sin, tbl_spec):
    qk = RET_HEADS * RET_DK
    vd = RET_HEADS * RET_DV
    return pl.pallas_call(
        _pre_ret_kernel,
        grid=(st.n_tiles,),
        in_specs=[
            st.row_spec(D_MODEL),
            st.mod_spec(0), st.mod_spec(1),
            _layer_spec((1, D_MODEL), layer),
            _layer_spec((D_MODEL, 2 * qk + 2 * vd), j),
            tbl_spec, tbl_spec,
        ],
        out_specs=[st.row_spec(qk), st.row_spec(qk), st.row_spec(vd), st.row_spec(vd)],
        out_shape=[jax.ShapeDtypeStruct((st.n_rows, qk), BF16),
                   jax.ShapeDtypeStruct((st.n_rows, qk), BF16),
                   jax.ShapeDtypeStruct((st.n_rows, vd), BF16),
                   jax.ShapeDtypeStruct((st.n_rows, vd), BF16)],
        compiler_params=_cparams("parallel"),
        name="pre_ret",
    )(x, st.mod, st.mod, norm_g.reshape(-1, 1, D_MODEL), w_in_b, cos, sin)


def _decay_matrix(lg, ch):
    diff = (lax.broadcasted_iota(I32, (ch, ch), 0)
            - lax.broadcasted_iota(I32, (ch, ch), 1)).astype(F32)
    return jnp.where(diff >= 0.0, jnp.exp(lg * jnp.maximum(diff, 0.0)), 0.0)


def _retention_chunk(valid, lg, dec, q, k, v, sg, s_prev):
    ch = q.shape[0]
    idx = lax.broadcasted_iota(I32, (ch, 1), 0).astype(F32)
    scores = lax.dot_general(q, k, (((1,), (1,)), ((), ())), preferred_element_type=F32) * dec
    o = jnp.dot(scores.astype(BF16), v, preferred_element_type=F32)
    cross = jnp.exp(lg * (idx + 1.0))
    o = o + jnp.dot(q, s_prev.astype(BF16), preferred_element_type=F32) * cross
    kd = (k.astype(F32) * jnp.exp(lg * (valid - 1.0 - idx))).astype(BF16)
    s_new = jnp.exp(lg * float(valid)) * s_prev + lax.dot_general(
        kd, v, (((0,), (0,)), ((), ())), preferred_element_type=F32)
    mu = jnp.mean(o, axis=-1, keepdims=True)
    oc = o - mu
    var = jnp.mean(oc * oc, axis=-1, keepdims=True)
    on = oc * lax.rsqrt(var + NORM_EPS)
    return (sg.astype(F32) * on).astype(BF16), s_new


def _retention_kernel(lg_ref, q_ref, k_ref, v_ref, sg_ref, qs_ref, ks_ref, vs_ref, sgs_ref, s0_ref,
                      o_ref, s_out_ref, os_ref, ss_out_ref, s_ref, dec_ref, decs_ref):
    i = pl.program_id(0)
    n_chunks = SEQ // RET_CHUNK
    c = i % n_chunks

    @pl.when(i == 0)
    def _():
        for h in range(RET_HEADS):
            dec_ref[h] = _decay_matrix(lg_ref[h, 0:1, 0:1], RET_CHUNK)
            decs_ref[h] = _decay_matrix(lg_ref[h, 0:1, 0:1], RET_PAD)

    @pl.when(c == 0)
    def _():
        s_ref[...] = jnp.zeros_like(s_ref)

    for h in range(RET_HEADS):
        ks = slice(h * RET_DK, (h + 1) * RET_DK)
        vs = slice(h * RET_DV, (h + 1) * RET_DV)
        lg = lg_ref[h, 0:1, 0:1]
        gated, s_new = _retention_chunk(RET_CHUNK, lg, dec_ref[h], q_ref[:, ks], k_ref[:, ks],
                                        v_ref[:, vs], sg_ref[:, vs], s_ref[h])
        o_ref[:, vs] = gated
        s_ref[h] = s_new
        for b in range(RET_SS):
            rows = slice(b * RET_PAD, (b + 1) * RET_PAD)
            gated, s_new = _retention_chunk(DEC_SEQ, lg, decs_ref[h], qs_ref[rows, ks],
                                            ks_ref[rows, ks], vs_ref[rows, vs], sgs_ref[rows, vs],
                                            s0_ref[b, h])
            os_ref[rows, vs] = gated
            ss_out_ref[b, h] = s_new

    @pl.when(c == n_chunks - 1)
    def _():
        s_out_ref[...] = s_ref[...]


def _retention(q, k, v, sg, qkvg_s, s0, s0_layer, lg_tbl):
    qk = RET_HEADS * RET_DK
    vd = RET_HEADS * RET_DV
    n_chunks = SEQ // RET_CHUNK
    rows_s = RET_SS * RET_PAD
    col = lambda j: (lambda i: (i, j))
    state_p = (None, None, RET_HEADS, RET_DK, RET_DV)
    state_s = (None, RET_SS, RET_HEADS, RET_DK, RET_DV)
    return pl.pallas_call(
        _retention_kernel,
        grid=(BATCH * n_chunks,),
        in_specs=[
            pl.BlockSpec((RET_HEADS, 8, 128), lambda i: (0, 0, 0)),
            pl.BlockSpec((RET_CHUNK, qk), col(0)), pl.BlockSpec((RET_CHUNK, qk), col(0)),
            pl.BlockSpec((RET_CHUNK, vd), col(0)), pl.BlockSpec((RET_CHUNK, vd), col(0)),
            pl.BlockSpec((rows_s, qk), col(0)), pl.BlockSpec((rows_s, qk), col(1)),
            pl.BlockSpec((rows_s, vd), col(1)), pl.BlockSpec((rows_s, vd), col(2)),
            pl.BlockSpec(state_s, lambda i: (s0_layer, i, 0, 0, 0)),
        ],
        out_specs=[pl.BlockSpec((RET_CHUNK, vd), col(0)),
                   pl.BlockSpec(state_p, lambda i: (0, i // n_chunks, 0, 0, 0)),
                   pl.BlockSpec((rows_s, vd), col(0)),
                   pl.BlockSpec(state_s, lambda i: (0, i, 0, 0, 0))],
        out_shape=[jax.ShapeDtypeStruct((N_PROMPT, vd), BF16),
                   jax.ShapeDtypeStruct((1, BATCH, RET_HEADS, RET_DK, RET_DV), F32),
                   jax.ShapeDtypeStruct((DEC_BATCH * RET_PAD, vd), BF16),
                   jax.ShapeDtypeStruct((1, DEC_BATCH, RET_HEADS, RET_DK, RET_DV), F32)],
        scratch_shapes=[pltpu.VMEM((RET_HEADS, RET_DK, RET_DV), F32),
                        pltpu.VMEM((RET_HEADS, RET_CHUNK, RET_CHUNK), F32),
                        pltpu.VMEM((RET_HEADS, RET_PAD, RET_PAD), F32)],
        compiler_params=_cparams("arbitrary"),
        name="retention",
    )(lg_tbl, q, k, v, sg, qkvg_s, qkvg_s, qkvg_s, qkvg_s, s0)


def _excl_prefix(col):
    r = lax.broadcasted_iota(I32, (N_EXPERTS, N_EXPERTS), 0)
    c = lax.broadcasted_iota(I32, (N_EXPERTS, N_EXPERTS), 1)
    lane = jnp.sum(jnp.where(r < c, col, 0.0), axis=0, keepdims=True)
    return jnp.sum(jnp.where(r == c, lane, 0.0), axis=1, keepdims=True)


def _ceil_to(x, m):
    return jnp.floor((x + (m - 1.0)) * (1.0 / m)) * m


def _route_kernel(te_ref, lpos_ref, rstart_ref, rlen_ref, tot_ref, tri_ref):
    rt = te_ref.shape[2]
    iota_e = lax.broadcasted_iota(I32, (N_EXPERTS, rt), 0)
    s = lax.broadcasted_iota(I32, (rt, rt), 0)
    t = lax.broadcasted_iota(I32, (rt, rt), 1)
    tri_ref[...] = (s < t).astype(BF16)

    def tile_onehots(i):
        te = te_ref[i]
        onehots = [(te[k:k + 1, :] == iota_e).astype(F32) for k in range(TOP_K)]
        cnt = jnp.sum(onehots[0] + onehots[1] + onehots[2] + onehots[3], axis=1, keepdims=True)
        return onehots, _ceil_to(cnt, RUN_ALIGN)

    tot = lax.fori_loop(0, MOE_NT, lambda i, acc: acc + tile_onehots(i)[1],
                        jnp.zeros((N_EXPERTS, 1), F32))
    tot_ref[...] = tot

    def place(i, gbase):
        onehots, run = tile_onehots(i)
        base = _excl_prefix(run)
        for k in range(TOP_K):
            oh = onehots[k]
            before = jnp.dot(oh.astype(BF16), tri_ref[...], preferred_element_type=F32)
            pos = jnp.sum(oh * (base + before), axis=0, keepdims=True)
            lpos_ref[i, k:k + 1, :] = pos.astype(I32)
            base = base + jnp.sum(oh, axis=1, keepdims=True)
        rstart_ref[i] = jnp.broadcast_to(gbase, rstart_ref.shape[1:]).astype(I32)
        rlen_ref[i] = jnp.broadcast_to(run, rlen_ref.shape[1:]).astype(I32)
        return gbase + run

    lax.fori_loop(0, MOE_NT, place, _excl_prefix(_ceil_to(tot, MOE_TB)))


def _route(te):
    full = lambda shape: pl.BlockSpec(shape, lambda i: (0,) * len(shape))
    return pl.pallas_call(
        _route_kernel,
        grid=(1,),
        in_specs=[full(te.shape)],
        out_specs=[full(te.shape), full((MOE_NT, N_EXPERTS, 128)), full((MOE_NT, N_EXPERTS, 128)),
                   full((N_EXPERTS, 1))],
        out_shape=[jax.ShapeDtypeStruct(te.shape, I32),
                   jax.ShapeDtypeStruct((MOE_NT, N_EXPERTS, 128), I32),
                   jax.ShapeDtypeStruct((MOE_NT, N_EXPERTS, 128), I32),
                   jax.ShapeDtypeStruct((N_EXPERTS, 1), F32)],
        scratch_shapes=[pltpu.VMEM((MOE_RT, MOE_RT), BF16)],
        compiler_params=_cparams("arbitrary"),
        name="moe_route",
    )(te)


def _dispatch_kernel(n_prompt_tiles, rs_ref, rl_ref, tzs_ref, tzl_ref, nu_ref, hp_ref, hs_ref,
                     lpos_ref, xb_ref, xs_ref, sem, zero_ref):
    i = pl.program_id(0)
    slot = i % 2
    zero_sem = sem.at[2]

    def tail_copy(e):
        n = pl.multiple_of(tzl_ref[e], RUN_ALIGN)
        dst = pl.multiple_of(tzs_ref[e], RUN_ALIGN)
        return pltpu.make_async_copy(zero_ref.at[pl.ds(0, n)], xb_ref.at[pl.ds(dst, n)], zero_sem)

    def block_copy(j):
        dst = pl.multiple_of(j * MOE_TB, MOE_TB)
        return pltpu.make_async_copy(zero_ref, xb_ref.at[pl.ds(dst, MOE_TB)], zero_sem)

    @pl.when(i == 0)
    def _():
        zero_ref[...] = jnp.zeros_like(zero_ref)

        def start(e, carry):
            pl.when(tzl_ref[e] > 0)(lambda: tail_copy(e).start())
            return carry

        def wait(e, carry):
            pl.when(tzl_ref[e] > 0)(lambda: tail_copy(e).wait())
            return carry

        lax.fori_loop(0, N_EXPERTS, start, 0)
        lax.fori_loop(nu_ref[0], MOE_NBLK, lambda j, c: (block_copy(j).start(), c)[1], 0)
        lax.fori_loop(0, N_EXPERTS, wait, 0)
        lax.fori_loop(nu_ref[0], MOE_NBLK, lambda j, c: (block_copy(j).wait(), c)[1], 0)

    rows = lax.broadcasted_iota(I32, (MOE_CAP, MOE_RT), 0)
    perm = jnp.where(rows == lpos_ref[0:1, :], 1.0, 0.0)
    for k in range(1, TOP_K):
        perm = perm + jnp.where(rows == lpos_ref[k:k + 1, :], 1.0, 0.0)
    h = jnp.where(i < n_prompt_tiles, hp_ref[...], hs_ref[...])
    xs_ref[slot] = _pack_halves(
        jnp.dot(perm.astype(BF16), h.astype(BF16), preferred_element_type=F32))

    def scatter(tile, wait):
        def body(e, lo):
            n = pl.multiple_of(rl_ref[tile * N_EXPERTS + e], RUN_ALIGN)
            sl = tile % 2
            src = pl.multiple_of(lo, RUN_ALIGN)
            dst = pl.multiple_of(rs_ref[tile * N_EXPERTS + e], RUN_ALIGN)
            cp = pltpu.make_async_copy(xs_ref.at[sl, pl.ds(src, n)], xb_ref.at[pl.ds(dst, n)],
                                       sem.at[sl])

            @pl.when(n > 0)
            def _():
                if wait:
                    cp.wait()
                else:
                    cp.start()

            return lo + n

        lax.fori_loop(0, N_EXPERTS, body, 0)

    scatter(i, False)

    @pl.when(i > 0)
    def _():
        scatter(i - 1, True)

    @pl.when(i == pl.num_programs(0) - 1)
    def _():
        scatter(i, True)


def _dispatch(tables, h_p, h_s, lpos):
    nt_p = h_p.shape[0] // MOE_RT
    return pl.pallas_call(
        functools.partial(_dispatch_kernel, nt_p),
        grid_spec=pltpu.PrefetchScalarGridSpec(
            num_scalar_prefetch=len(tables),
            grid=(MOE_NT,),
            in_specs=[
                pl.BlockSpec((MOE_RT, D_MODEL), lambda i, *_: (jnp.minimum(i, nt_p - 1), 0)),
                pl.BlockSpec((MOE_RT, D_MODEL), lambda i, *_: (jnp.maximum(i - nt_p, 0), 0)),
                pl.BlockSpec((None, TOP_K, MOE_RT), lambda i, *_: (i, 0, 0)),
            ],
            out_specs=pl.BlockSpec(memory_space=pl.ANY),
            scratch_shapes=[pltpu.VMEM((2, MOE_CAP, D_MODEL // 2), U32),
                            pltpu.SemaphoreType.DMA((3,)),
                            pltpu.VMEM((MOE_TB, D_MODEL // 2), U32)],
        ),
        out_shape=jax.ShapeDtypeStruct((MOE_ROWS, D_MODEL // 2), U32),
        compiler_params=_cparams("arbitrary"),
        name="moe_dispatch",
    )(*tables, h_p, h_s, lpos)


def _ffn_kernel(layer, be_ref, nu_ref, nxt_ref, half_ref, bgu_ref, bdn_ref, xb_hbm, wgu_hbm,
                wdn_hbm, yb_hbm, wgu_f_ref, wdn_f_ref, wgu_b_ref, wdn_b_ref, act_ref, xbuf_ref,
                ybuf_ref, wsem, xsem, ysem):
    n_used = nu_ref[0]
    ybuf_ref[...] = jnp.zeros_like(ybuf_ref)

    def weight_copies(ex):
        return (pltpu.make_async_copy(wgu_hbm.at[layer, ex], wgu_f_ref, wsem.at[0]),
                pltpu.make_async_copy(wdn_hbm.at[layer, ex], wdn_f_ref, wsem.at[1]))

    def block_rows(j):
        return pl.ds(pl.multiple_of(j * MOE_TB, MOE_TB), MOE_TB)

    def x_copy(j):
        return pltpu.make_async_copy(xb_hbm.at[block_rows(j)], xbuf_ref.at[j % 2], xsem.at[j % 2])

    def y_copy(j):
        return pltpu.make_async_copy(ybuf_ref.at[j % 2], yb_hbm.at[block_rows(j)], ysem.at[j % 2])

    for cp in weight_copies(be_ref[0]):
        cp.start()
    x_copy(0).start()

    def block(j, carry):
        e = be_ref[j]
        e_prev = be_ref[jnp.maximum(j - 1, 0)]

        @pl.when(j + 1 < n_used)
        def _():
            x_copy(j + 1).start()

        @pl.when((j == 0) | (e != e_prev))
        def _():
            for cp in weight_copies(e):
                cp.wait()
            wgu_b_ref[...] = wgu_f_ref[...].astype(BF16)
            wdn_b_ref[...] = wdn_f_ref[...].astype(BF16)
            nx = nxt_ref[j]

            @pl.when(nx >= 0)
            def _():
                for cp in weight_copies(nx):
                    cp.start()

        x_copy(j).wait()

        @pl.when(j >= 2)
        def _():
            y_copy(j - 2).wait()

        def expert_ffn(m):
            x = _unpack_halves(xbuf_ref[j % 2, 0:m, :])
            bgu = bgu_ref[e]
            for c in range(D_FF // FF_CHUNK):
                gs = slice(c * FF_CHUNK, (c + 1) * FF_CHUNK)
                us = slice(D_FF + c * FF_CHUNK, D_FF + (c + 1) * FF_CHUNK)
                gate = jnp.dot(x, wgu_b_ref[:, gs], preferred_element_type=F32) + bgu[:, gs]
                up = jnp.dot(x, wgu_b_ref[:, us], preferred_element_type=F32) + bgu[:, us]
                gate = jnp.minimum(gate, SWIGLU_LIMIT)
                up = jnp.clip(up, -SWIGLU_LIMIT, SWIGLU_LIMIT)
                act = (up + 1.0) * (gate * jax.nn.sigmoid(SWIGLU_ALPHA * gate))
                act_ref[0:m, gs] = act.astype(BF16)
            y = jnp.dot(act_ref[0:m, :], wdn_b_ref[...], preferred_element_type=F32) + bdn_ref[e]
            ybuf_ref[j % 2, 0:m, :] = _pack_halves(y.astype(BF16).astype(F32))

        pl.when(half_ref[j] == 0)(lambda: expert_ffn(MOE_TB))
        pl.when(half_ref[j] != 0)(lambda: expert_ffn(MOE_TB // 2))
        y_copy(j).start()
        return carry

    lax.fori_loop(0, n_used, block, 0)

    @pl.when(n_used >= 2)
    def _():
        y_copy(n_used - 2).wait()

    y_copy(n_used - 1).wait()

    ybuf_ref[0] = jnp.zeros_like(ybuf_ref[0])

    def zero_copy(j):
        return pltpu.make_async_copy(ybuf_ref.at[0], yb_hbm.at[block_rows(j)], ysem.at[0])

    lax.fori_loop(n_used, MOE_NBLK, lambda j, c: (zero_copy(j).start(), c)[1], 0)
    lax.fori_loop(n_used, MOE_NBLK, lambda j, c: (zero_copy(j).wait(), c)[1], 0)


def _ffn(layer, block_e, n_used, nxt, half, xb, w_gu, b_gu, w_dn, b_dn):
    bias_map = lambda i, *_: (layer, 0, 0, 0)
    return pl.pallas_call(
        functools.partial(_ffn_kernel, layer),
        grid_spec=pltpu.PrefetchScalarGridSpec(
            num_scalar_prefetch=4,
            grid=(1,),
            in_specs=[
                pl.BlockSpec((None, N_EXPERTS, 1, 2 * D_FF), bias_map),
                pl.BlockSpec((None, N_EXPERTS, 1, D_MODEL), bias_map),
                pl.BlockSpec(memory_space=pl.ANY),
                pl.BlockSpec(memory_space=pl.ANY),
                pl.BlockSpec(memory_space=pl.ANY),
            ],
            out_specs=pl.BlockSpec(memory_space=pl.ANY),
            scratch_shapes=[pltpu.VMEM((D_MODEL, 2 * D_FF), F32),
                            pltpu.VMEM((D_FF, D_MODEL), F32),
                            pltpu.VMEM((D_MODEL, 2 * D_FF), BF16),
                            pltpu.VMEM((D_FF, D_MODEL), BF16),
                            pltpu.VMEM((MOE_TB, D_FF), BF16),
                            pltpu.VMEM((2, MOE_TB, D_MODEL // 2), U32),
                            pltpu.VMEM((2, MOE_TB, D_MODEL // 2), U32),
                            pltpu.SemaphoreType.DMA((2,)),
                            pltpu.SemaphoreType.DMA((2,)),
                            pltpu.SemaphoreType.DMA((2,))],
        ),
        out_shape=jax.ShapeDtypeStruct((MOE_ROWS, D_MODEL // 2), U32),
        compiler_params=_cparams("arbitrary"),
        name="moe_ffn",
    )(block_e, n_used, nxt, half, b_gu.reshape(b_gu.shape[0], N_EXPERTS, 1, 2 * D_FF),
      b_dn.reshape(b_dn.shape[0], N_EXPERTS, 1, D_MODEL), xb, w_gu, w_dn)


def _tile_rows(v, n_rows):
    mr = v.shape[0]
    if mr in (1, n_rows):
        return v
    return jnp.concatenate([v] * (n_rows // mr), axis=0)


def _combine_kernel(tile_off, final, rs_ref, rl_ref, x_ref, gf_ref, lpt_ref, wt_ref, fg_ref,
                    yb_ref, out_ref, ys_ref, sem):
    i = pl.program_id(0)

    def gather(tile, wait):
        def body(e, lo):
            n = pl.multiple_of(rl_ref[(tile_off + tile) * N_EXPERTS + e], RUN_ALIGN)
            sl = tile % 2
            src = pl.multiple_of(rs_ref[(tile_off + tile) * N_EXPERTS + e], RUN_ALIGN)
            dst = pl.multiple_of(lo, RUN_ALIGN)
            cp = pltpu.make_async_copy(yb_ref.at[pl.ds(src, n)], ys_ref.at[sl, pl.ds(dst, n)],
                                       sem.at[sl])

            @pl.when(n > 0)
            def _():
                if wait:
                    cp.wait()
                else:
                    cp.start()

            return lo + n

        lax.fori_loop(0, N_EXPERTS, body, 0)

    @pl.when(i == 0)
    def _():
        ys_ref[...] = jnp.zeros_like(ys_ref)
        gather(0, False)

    @pl.when(i + 1 < pl.num_programs(0))
    def _():
        gather(i + 1, False)

    gather(i, True)

    cols = lax.broadcasted_iota(I32, (MOE_RT, MOE_CAP), 1)
    w = jnp.where(cols == lpt_ref[:, 0:1], wt_ref[:, 0:1], 0.0)
    for k in range(1, TOP_K):
        w = w + jnp.where(cols == lpt_ref[:, k:k + 1], wt_ref[:, k:k + 1], 0.0)
    ys = _unpack_halves(ys_ref[i % 2])
    y = jnp.dot(w.astype(BF16), ys, preferred_element_type=F32)
    xn = x_ref[...] + _tile_rows(gf_ref[...], MOE_RT) * y
    if final:
        xn = xn * lax.rsqrt(jnp.mean(xn * xn, axis=-1, keepdims=True) + NORM_EPS) * fg_ref[...]
    out_ref[...] = xn


def _combine(tile_off, n_tiles, final, tables, x, mod, mod_map, lpos_t, wt, final_g, yb):
    pm = lambda i, *_: (i, 0)
    tm = lambda i, *_: (i + tile_off, 0)
    return pl.pallas_call(
        functools.partial(_combine_kernel, tile_off, final),
        grid_spec=pltpu.PrefetchScalarGridSpec(
            num_scalar_prefetch=len(tables),
            grid=(n_tiles,),
            in_specs=[
                pl.BlockSpec((MOE_RT, D_MODEL), pm),
                pl.BlockSpec((None, mod.shape[1], D_MODEL), mod_map),
                pl.BlockSpec((MOE_RT, TOP_K), tm),
                pl.BlockSpec((MOE_RT, TOP_K), tm),
                pl.BlockSpec((1, D_MODEL), lambda i, *_: (0, 0)),
                pl.BlockSpec(memory_space=pl.ANY),
            ],
            out_specs=pl.BlockSpec((MOE_RT, D_MODEL), pm),
            scratch_shapes=[pltpu.VMEM((2, MOE_CAP, D_MODEL // 2), U32),
                            pltpu.SemaphoreType.DMA((2,))],
        ),
        out_shape=jax.ShapeDtypeStruct(x.shape, F32),
        compiler_params=_cparams("arbitrary"),
        name="moe_combine",
    )(*tables, x, mod, lpos_t, wt, final_g.reshape(1, D_MODEL), yb)


def _moe(layer, final, st_p, st_s, x_p, x_s, post_p, post_s,
         w_gu, b_gu, w_dn, b_dn, final_g):
    te = jnp.concatenate([post_p[1], post_s[1]], axis=1)
    tw = jnp.concatenate([post_p[2], post_s[2]], axis=1)
    te = jnp.swapaxes(te.reshape(TOP_K, MOE_NT, MOE_RT), 0, 1)
    lpos, rstart, rlen, tot = _route(te)
    tot = tot[:, 0].astype(I32)
    padded = (tot + MOE_TB - 1) // MOE_TB * MOE_TB
    pends = jnp.cumsum(padded)
    n_used = (pends[-1] // MOE_TB).reshape(1)
    blk = jnp.minimum(jnp.arange(MOE_NBLK, dtype=I32), n_used[0] - 1) * MOE_TB
    block_e = jnp.minimum(jnp.sum(pends[None, :] <= blk[:, None], axis=-1),
                          N_EXPERTS - 1).astype(I32)
    ids = jnp.where(tot > 0, jnp.arange(N_EXPERTS, dtype=I32), N_EXPERTS)
    later = jnp.concatenate([lax.cummin(ids[::-1])[::-1][1:], jnp.full((1,), N_EXPERTS, I32)])
    nxt_of = jnp.where(later < N_EXPERTS, later, -1)
    of_block = block_e[:, None] == jnp.arange(N_EXPERTS, dtype=I32)[None, :]
    nxt = jnp.sum(jnp.where(of_block, nxt_of[None, :], 0), axis=-1)
    rows_end = jnp.sum(jnp.where(of_block, (pends - padded + tot)[None, :], 0), axis=-1)
    half = (rows_end - jnp.arange(MOE_NBLK, dtype=I32) * MOE_TB <= MOE_TB // 2).astype(I32)
    run_tables = (rstart[:, :, 0].reshape(-1), rlen[:, :, 0].reshape(-1))
    tail_tables = (pends - padded + tot, padded - tot)

    nt_p = N_PROMPT // MOE_RT
    nt_s = N_SAMPLE // MOE_RT
    xb = _dispatch(run_tables + tail_tables + (n_used,), post_p[0], post_s[0], lpos)
    yb = _ffn(layer, block_e, n_used, nxt, half, xb, w_gu, b_gu, w_dn, b_dn)
    lpos_t = jnp.swapaxes(lpos, 1, 2).reshape(N_TOK, TOP_K)
    wt = tw.T
    tiles_per_seq = SEQ // MOE_RT
    out_p = _combine(0, nt_p, final, run_tables, x_p, st_p.mod,
                     lambda i, *_: (i // tiles_per_seq, 0, N_MOD - 1), lpos_t, wt, final_g, yb)
    out_s = _combine(nt_p, nt_s, final, run_tables, x_s, st_s.mod,
                     lambda i, *_: (0, 0, N_MOD - 1), lpos_t, wt, final_g, yb)
    return out_p, out_s


def _rope_tables(pos):
    half = RET_DK // 2
    inv_freq = ROPE_BASE ** (-jnp.arange(half, dtype=F32) / half)
    ang = pos.astype(F32)[:, None] * inv_freq[None, :]
    return jnp.cos(ang), jnp.sin(ang)


def kernel(x_prompt, x_sample, c_prompt, c_sample, state_conv, state_ret, norm_mix_g, norm_ff_g,
           w_mod, b_mod, conv_w1, conv_b1, conv_dw, conv_dw_b, conv_ln_g, conv_ln_b, conv_w2,
           conv_b2, ret_w_in, ret_w_o, router_w, router_b, moe_w_gu, moe_b_gu, moe_w_dn,
           moe_b_dn, final_g):
    mod = _adaln(jnp.concatenate([c_prompt, c_sample], axis=0), w_mod, b_mod)
    x_p = x_prompt.reshape(N_PROMPT, D_MODEL)
    x_s = jnp.swapaxes(x_sample, 0, 1).reshape(N_SAMPLE, D_MODEL)
    router_wt = jnp.swapaxes(router_w, 1, 2)

    def streams(layer):
        st_p = _Stream(N_PROMPT, TM_PROMPT, SEQ // TM_PROMPT,
                       mod[layer, :BATCH].reshape(BATCH, 1, N_MOD * D_MODEL))
        st_s = _Stream(N_SAMPLE, TM_SAMPLE, None,
                       mod[layer, BATCH:].reshape(1, DEC_BATCH, N_MOD * D_MODEL))
        return st_p, st_s

    st_p, st_s = streams(0)
    u_p = _pre_conv(st_p, x_p, norm_mix_g, conv_w1, conv_b1, 0, 0)
    u_s = _pre_conv(st_s, x_s, norm_mix_g, conv_w1, conv_b1, 0, 0)
    buf_tm = jnp.swapaxes(state_conv[0], 0, 1)
    z_s, ns_tm = _conv_sample(buf_tm, u_s.reshape(DEC_SEQ, DEC_BATCH, D_MODEL),
                              conv_dw, conv_dw_b, 0)
    post_p = _mixer_post("conv", st_p, 0, 0, x_p, u_p, norm_ff_g, router_wt, router_b,
                         conv_w2, conv_b2, conv_dw, conv_dw_b, conv_ln_g, conv_ln_b)
    post_s = _mixer_post("z", st_s, 0, 0, x_s, z_s.reshape(N_SAMPLE, D_MODEL), norm_ff_g,
                         router_wt, router_b, conv_w2, conv_b2, None, None, conv_ln_g, conv_ln_b)
    x_p, x_s = _moe(0, False, st_p, st_s, post_p[0], post_s[0], post_p[1:], post_s[1:],
                    moe_w_gu, moe_b_gu, moe_w_dn, moe_b_dn, final_g)
    conv_p = u_p.reshape(BATCH, SEQ, D_MODEL)[:, SEQ - CONV_STATE:][None]
    conv_s = jnp.swapaxes(ns_tm, 0, 1)[None]

    st_p, st_s = streams(1)
    w_in_b = ret_w_in.astype(BF16)
    cos_p, sin_p = _rope_tables(jnp.arange(SEQ, dtype=I32))
    cos_s, sin_s = _rope_tables(PAST_LEN + jnp.arange(DEC_SEQ, dtype=I32))
    half = RET_DK // 2
    tps = SEQ // TM_PROMPT
    tbl_p = pl.BlockSpec((TM_PROMPT, half), lambda i: (i % tps, 0))
    tbl_s = pl.BlockSpec((None, 1, half), lambda i: (i, 0, 0))
    q_p, k_p, v_p, sg_p = _pre_ret(st_p, x_p, norm_mix_g, w_in_b, 1, 0, cos_p, sin_p, tbl_p)
    q_s, k_s, v_s, sg_s = _pre_ret(st_s, x_s, norm_mix_g, w_in_b, 1, 0,
                                   cos_s.reshape(DEC_SEQ, 1, half),
                                   sin_s.reshape(DEC_SEQ, 1, half), tbl_s)
    lg = jnp.log1p(-jnp.exp2(-5.0 - jnp.arange(RET_HEADS, dtype=F32)))
    lg_tbl = jnp.broadcast_to(lg[:, None, None], (RET_HEADS, 8, 128))
    def to_seq_major(a):
        a = jnp.swapaxes(a.reshape(DEC_SEQ, DEC_BATCH, -1), 0, 1)
        a = jnp.pad(a, ((0, 0), (0, RET_PAD - DEC_SEQ), (0, 0)))
        return a.reshape(DEC_BATCH * RET_PAD, -1)

    qkvg_s = to_seq_major(jnp.concatenate([q_s, k_s, v_s, sg_s], axis=1))
    gated_p, ret_p, gated_s, ret_s = _retention(q_p, k_p, v_p, sg_p, qkvg_s, state_ret, 0, lg_tbl)
    gated_s = jnp.swapaxes(gated_s.reshape(DEC_BATCH, RET_PAD, -1)[:, :DEC_SEQ], 0, 1)
    gated_s = gated_s.reshape(N_SAMPLE, -1)
    post_p = _mixer_post("ret", st_p, 1, 0, x_p, gated_p, norm_ff_g, router_wt, router_b, ret_w_o)
    post_s = _mixer_post("ret", st_s, 1, 0, x_s, gated_s, norm_ff_g, router_wt, router_b, ret_w_o)
    y_p, y_s = _moe(1, True, st_p, st_s, post_p[0], post_s[0], post_p[1:], post_s[1:],
                    moe_w_gu, moe_b_gu, moe_w_dn, moe_b_dn, final_g)

    y_prompt = y_p.reshape(BATCH, SEQ, D_MODEL)
    y_sample = jnp.swapaxes(y_s.reshape(DEC_SEQ, DEC_BATCH, D_MODEL), 0, 1)
    return (y_prompt, y_sample, conv_p, conv_s, ret_p, ret_s)
```

```python
import functools

import jax
import jax.numpy as jnp
from jax import lax
from jax.experimental import pallas as pl
from jax.experimental.pallas import tpu as pltpu

F32 = jnp.float32
BF16 = jnp.bfloat16
I32 = jnp.int32
U32 = jnp.uint32

D_MODEL = 1024
SEQ = 2048
BATCH = 8
DEC_BATCH = 128
DEC_SEQ = 4
PAST_LEN = 16384
CONV_WIDTH = 31
CONV_STATE = CONV_WIDTH - 1
RET_HEADS = 4
RET_DK = 256
RET_DV = 512
ROPE_BASE = 10000.0
N_EXPERTS = 32
TOP_K = 4
D_FF = 1024
SWIGLU_LIMIT = 7.0
SWIGLU_ALPHA = 1.702
N_MOD = 6
NORM_EPS = 1e-5

N_PROMPT = BATCH * SEQ
N_SAMPLE = DEC_BATCH * DEC_SEQ
N_TOK = N_PROMPT + N_SAMPLE
N_ASSIGN = N_TOK * TOP_K

TM_PROMPT = 512
TM_SAMPLE = DEC_BATCH
HALO = 32
CONV_RB = 64
CONV_LW = 128
RET_CHUNK = 256
RET_PAD = 16
RET_SS = DEC_BATCH // (BATCH * (SEQ // RET_CHUNK))
MOE_TB = 512
MOE_RT = 256
MOE_NT = N_TOK // MOE_RT
RUN_ALIGN = 8
MOE_CAP = -(-(TOP_K * MOE_RT + N_EXPERTS * (RUN_ALIGN - 1)) // 256) * 256
MOE_MAX_ROWS = N_ASSIGN + MOE_NT * N_EXPERTS * (RUN_ALIGN - 1)
MOE_NBLK = -(-MOE_MAX_ROWS // MOE_TB) + N_EXPERTS
MOE_ROWS = MOE_NBLK * MOE_TB
FF_CHUNK = 256
VMEM_LIMIT = 56 * 1024 * 1024


def _cparams(*sem):
    return pltpu.CompilerParams(dimension_semantics=sem, vmem_limit_bytes=VMEM_LIMIT)


def _silu(x):
    return x * jax.nn.sigmoid(x)


def _rms_mod(x, g, sc, sh):
    y = x * lax.rsqrt(jnp.mean(x * x, axis=-1, keepdims=True) + NORM_EPS) * g
    return y * (1.0 + sc) + sh


def _split_bf16(x):
    hi = x.astype(BF16)
    lo = (x - hi.astype(F32)).astype(BF16)
    return hi, lo


def _pack_halves(x):
    c = x.shape[1] // 2
    lo = lax.bitcast_convert_type(x[:, :c], U32)
    hi = lax.bitcast_convert_type(x[:, c:], U32)
    return (lo >> 16) | (hi & jnp.uint32(0xFFFF0000))


def _unpack_halves(p):
    lo = lax.bitcast_convert_type(p << 16, F32)
    hi = lax.bitcast_convert_type(p & jnp.uint32(0xFFFF0000), F32)
    return jnp.concatenate([lo.astype(BF16), hi.astype(BF16)], axis=1)


def _adaln_kernel(c_ref, w_ref, b_ref, o_ref):
    a = _silu(c_ref[...]).astype(BF16)
    o_ref[...] = jnp.dot(a, w_ref[...].astype(BF16), preferred_element_type=F32) + b_ref[...]


def _adaln(c_all, w_mod, b_mod):
    depth, d, nm = w_mod.shape
    n = c_all.shape[0]
    return pl.pallas_call(
        _adaln_kernel,
        grid=(depth, nm // d),
        in_specs=[
            pl.BlockSpec((n, d), lambda l, j: (0, 0)),
            pl.BlockSpec((None, d, d), lambda l, j: (l, 0, j)),
            pl.BlockSpec((None, 1, d), lambda l, j: (l, 0, j)),
        ],
        out_specs=pl.BlockSpec((None, n, d), lambda l, j: (l, 0, j)),
        out_shape=jax.ShapeDtypeStruct((depth, n, nm), F32),
        compiler_params=_cparams("parallel", "parallel"),
        name="adaln",
    )(c_all, w_mod, b_mod.reshape(depth, 1, nm))


class _Stream:
    def __init__(self, n_rows, tm, tiles_per_seq, mod):
        self.n_rows = n_rows
        self.tm = tm
        self.n_tiles = n_rows // tm
        self.tiles_per_seq = tiles_per_seq
        self.mod = mod

    def mod_spec(self, j):
        mr = self.mod.shape[1]
        if self.tiles_per_seq is None:
            return pl.BlockSpec((None, mr, D_MODEL), lambda i: (0, 0, j))
        tps = self.tiles_per_seq
        return pl.BlockSpec((None, mr, D_MODEL), lambda i: (i // tps, 0, j))

    def row_spec(self, width, col=0):
        return pl.BlockSpec((self.tm, width), lambda i: (i, col))


def _const_spec(shape):
    nd = len(shape)
    return pl.BlockSpec(shape, lambda i: (0,) * nd)


def _layer_spec(shape, layer):
    nd = len(shape)
    return pl.BlockSpec((None,) + shape, lambda i: (layer,) + (0,) * nd)


def _pre_conv_kernel(x_ref, sh_ref, sc_ref, g_ref, w1_ref, b1_ref, u_ref, w1b_ref):
    @pl.when(pl.program_id(0) == 0)
    def _():
        w1b_ref[...] = w1_ref[...].astype(BF16)

    h = _rms_mod(x_ref[...], g_ref[...], sc_ref[...], sh_ref[...])
    ag = jnp.dot(h.astype(BF16), w1b_ref[...], preferred_element_type=F32) + b1_ref[...]
    u_ref[...] = ag[:, :D_MODEL] * jax.nn.sigmoid(ag[:, D_MODEL:])


def _pre_conv(st, x, norm_g, w1, b1, layer, j):
    c2 = w1.shape[-1]
    return pl.pallas_call(
        _pre_conv_kernel,
        grid=(st.n_tiles,),
        in_specs=[
            st.row_spec(D_MODEL),
            st.mod_spec(0), st.mod_spec(1),
            _layer_spec((1, D_MODEL), layer),
            _layer_spec((D_MODEL, c2), j),
            _layer_spec((1, c2), j),
        ],
        out_specs=st.row_spec(D_MODEL),
        out_shape=jax.ShapeDtypeStruct((st.n_rows, D_MODEL), F32),
        scratch_shapes=[pltpu.VMEM((D_MODEL, c2), BF16)],
        compiler_params=_cparams("arbitrary"),
        name="pre_conv",
    )(x, st.mod, st.mod, norm_g.reshape(-1, 1, D_MODEL), w1, b1.reshape(-1, 1, c2))


def _conv_sample_kernel(buf_ref, u_ref, dw_ref, dwb_ref, z_ref, ns_ref):
    for t in range(DEC_SEQ):
        acc = jnp.broadcast_to(dwb_ref[...], u_ref.shape[1:])
        for j in range(t, CONV_STATE):
            acc = acc + buf_ref[j] * dw_ref[j - t:j - t + 1, :]
        for s in range(t + 1):
            k = CONV_STATE + s - t
            acc = acc + u_ref[s] * dw_ref[k:k + 1, :]
        z_ref[t] = acc
    ns_ref[0:CONV_STATE - DEC_SEQ] = buf_ref[DEC_SEQ:CONV_STATE]
    ns_ref[CONV_STATE - DEC_SEQ:CONV_STATE] = u_ref[...]


def _conv_sample(buf_tm, u_tm, dw, dwb, j):
    sb = 32
    c = u_tm.shape[-1]
    return pl.pallas_call(
        _conv_sample_kernel,
        grid=(DEC_BATCH // sb,),
        in_specs=[
            pl.BlockSpec((CONV_STATE, sb, c), lambda i: (0, i, 0)),
            pl.BlockSpec((DEC_SEQ, sb, c), lambda i: (0, i, 0)),
            _layer_spec((CONV_WIDTH, c), j),
            _layer_spec((1, c), j),
        ],
        out_specs=[
            pl.BlockSpec((DEC_SEQ, sb, c), lambda i: (0, i, 0)),
            pl.BlockSpec((CONV_STATE, sb, c), lambda i: (0, i, 0)),
        ],
        out_shape=[
            jax.ShapeDtypeStruct((DEC_SEQ, DEC_BATCH, c), F32),
            jax.ShapeDtypeStruct((CONV_STATE, DEC_BATCH, c), F32),
        ],
        compiler_params=_cparams("parallel"),
        name="conv_sample",
    )(buf_tm, u_tm, dw, dwb.reshape(-1, 1, c))


def _route_topk(h2, rwt_ref, rb_ref, te_ref, tw_ref):
    h_hi, h_lo = _split_bf16(h2)
    w_hi, w_lo = _split_bf16(rwt_ref[...])
    nt = (((1,), (1,)), ((), ()))
    logits = (lax.dot_general(w_hi, h_hi, nt, preferred_element_type=F32)
              + lax.dot_general(w_hi, h_lo, nt, preferred_element_type=F32)
              + lax.dot_general(w_lo, h_hi, nt, preferred_element_type=F32)
              + rb_ref[...])
    iota_e = lax.broadcasted_iota(I32, logits.shape, 0).astype(F32)
    vals, idxs = [], []
    for _ in range(TOP_K):
        m = jnp.max(logits, axis=0, keepdims=True)
        idx = jnp.min(jnp.where(logits == m, iota_e, float(N_EXPERTS)), axis=0, keepdims=True)
        logits = jnp.where(iota_e == idx, -jnp.inf, logits)
        vals.append(m)
        idxs.append(idx)
    es = [jnp.exp(v - vals[0]) for v in vals]
    tot = es[0] + es[1] + es[2] + es[3]
    for k in range(TOP_K):
        te_ref[k:k + 1, :] = idxs[k].astype(I32)
        tw_ref[k:k + 1, :] = es[k] / tot


def _mixer_post_kernel(mode, tiles_per_seq, *refs):
    it = iter(refs)
    if mode == "conv":
        u_ref, halo_ref, dw_ref, dwb_ref = next(it), next(it), next(it), next(it)
    elif mode == "z":
        z_in_ref = next(it)
    else:
        gated_ref = next(it)
    if mode in ("conv", "z"):
        lng_ref, lnb_ref = next(it), next(it)
    w_ref, b_ref = next(it), next(it)
    x_ref, gm_ref, shf_ref, scf_ref, gff_ref, rwt_ref, rb_ref = (next(it) for _ in range(7))
    x1_ref, h2_ref, te_ref, tw_ref = (next(it) for _ in range(4))
    wb_ref = next(it)
    if mode == "conv":
        ext_ref, z_ref = next(it), next(it)

    i = pl.program_id(0)

    @pl.when(i == 0)
    def _():
        wb_ref[...] = w_ref[...].astype(BF16)

    if mode == "conv":
        tm = u_ref.shape[0]
        first = (i % tiles_per_seq) == 0
        ext_ref[0:HALO, :] = jnp.where(first, 0.0, halo_ref[...])
        ext_ref[HALO:HALO + tm, :] = u_ref[...]
        off = HALO - CONV_STATE

        def row_block(r, carry):
            r0 = pl.multiple_of(r * CONV_RB, CONV_RB)
            for lc in range(D_MODEL // CONV_LW):
                ls = slice(lc * CONV_LW, (lc + 1) * CONV_LW)
                win = ext_ref[pl.ds(r0, CONV_RB + HALO), ls]
                acc = jnp.broadcast_to(dwb_ref[:, ls], (CONV_RB, CONV_LW))
                for s in range(8):
                    ws = win if s == 0 else pltpu.roll(win, CONV_RB + HALO - s, 0)
                    for k in range(CONV_WIDTH):
                        if (off + k) % 8 == s:
                            a = off + k - s
                            acc = acc + ws[a:a + CONV_RB, :] * dw_ref[k:k + 1, ls]
                z_ref[pl.ds(r0, CONV_RB), ls] = acc
            return carry

        lax.fori_loop(0, tm // CONV_RB, row_block, 0)
        z = z_ref[...]
    elif mode == "z":
        z = z_in_ref[...]

    if mode in ("conv", "z"):
        mu = jnp.mean(z, axis=-1, keepdims=True)
        zc = z - mu
        var = jnp.mean(zc * zc, axis=-1, keepdims=True)
        zn = zc * lax.rsqrt(var + NORM_EPS) * lng_ref[...] + lnb_ref[...]
        a = _silu(zn).astype(BF16)
        y = jnp.dot(a, wb_ref[...], preferred_element_type=F32) + b_ref[...]
    else:
        y = jnp.dot(gated_ref[...], wb_ref[...], preferred_element_type=F32)

    x1 = x_ref[...] + gm_ref[...] * y
    x1_ref[...] = x1
    h2 = _rms_mod(x1, gff_ref[...], scf_ref[...], shf_ref[...])
    h2_ref[...] = h2.astype(BF16)
    _route_topk(h2, rwt_ref, rb_ref, te_ref, tw_ref)


def _mixer_post(mode, st, layer, j, x, mix_in, norm_ff_g, router_wt, router_b, w, b=None,
                dw=None, dwb=None, ln_g=None, ln_b=None):
    tm = st.tm
    k_in = w.shape[-2]
    args, specs = [], []
    if mode == "conv":
        per32 = tm // HALO
        args += [mix_in, mix_in, dw, dwb.reshape(-1, 1, D_MODEL)]
        specs += [st.row_spec(D_MODEL),
                  pl.BlockSpec((HALO, D_MODEL), lambda i: (jnp.maximum(i * per32 - 1, 0), 0)),
                  _layer_spec((CONV_WIDTH, D_MODEL), j), _layer_spec((1, D_MODEL), j)]
    elif mode == "z":
        args += [mix_in]
        specs += [st.row_spec(D_MODEL)]
    else:
        args += [mix_in]
        specs += [st.row_spec(k_in)]
    if mode in ("conv", "z"):
        args += [ln_g.reshape(-1, 1, D_MODEL), ln_b.reshape(-1, 1, D_MODEL)]
        specs += [_layer_spec((1, D_MODEL), j), _layer_spec((1, D_MODEL), j)]
    if b is None:
        b = jnp.zeros((w.shape[0], D_MODEL), F32)
    args += [w, b.reshape(-1, 1, D_MODEL)]
    specs += [_layer_spec((k_in, D_MODEL), j), _layer_spec((1, D_MODEL), j)]
    args += [x, st.mod, st.mod, st.mod, norm_ff_g.reshape(-1, 1, D_MODEL), router_wt,
             router_b.reshape(-1, N_EXPERTS, 1)]
    specs += [st.row_spec(D_MODEL), st.mod_spec(2), st.mod_spec(3), st.mod_spec(4),
              _layer_spec((1, D_MODEL), layer), _layer_spec((N_EXPERTS, D_MODEL), layer),
              _layer_spec((N_EXPERTS, 1), layer)]
    scratch = [pltpu.VMEM((k_in, D_MODEL), BF16)]
    if mode == "conv":
        scratch += [pltpu.VMEM((tm + HALO, D_MODEL), F32), pltpu.VMEM((tm, D_MODEL), F32)]
    return pl.pallas_call(
        functools.partial(_mixer_post_kernel, mode, st.tiles_per_seq),
        grid=(st.n_tiles,),
        in_specs=specs,
        out_specs=[st.row_spec(D_MODEL), st.row_spec(D_MODEL),
                   pl.BlockSpec((TOP_K, tm), lambda i: (0, i)),
                   pl.BlockSpec((TOP_K, tm), lambda i: (0, i))],
        out_shape=[jax.ShapeDtypeStruct((st.n_rows, D_MODEL), F32),
                   jax.ShapeDtypeStruct((st.n_rows, D_MODEL), BF16),
                   jax.ShapeDtypeStruct((TOP_K, st.n_rows), I32),
                   jax.ShapeDtypeStruct((TOP_K, st.n_rows), F32)],
        scratch_shapes=scratch,
        compiler_params=_cparams("arbitrary"),
        name="mixer_post_" + mode,
    )(*args)


def _pre_ret_kernel(x_ref, sh_ref, sc_ref, g_ref, w_ref, cos_ref, sin_ref,
                    q_ref, k_ref, v_ref, sg_ref):
    h = _rms_mod(x_ref[...], g_ref[...], sc_ref[...], sh_ref[...]).astype(BF16)
    cos = cos_ref[...]
    sin = sin_ref[...]
    half = RET_DK // 2
    qk = RET_HEADS * RET_DK
    vd = RET_HEADS * RET_DV
    for out_ref, base, scale in ((q_ref, 0, 1.0), (k_ref, qk, RET_DK ** -0.5)):
        for hh in range(RET_HEADS):
            c0 = base + hh * RET_DK
            p = jnp.dot(h, w_ref[:, c0:c0 + RET_DK], preferred_element_type=F32)
            p1, p2 = p[:, :half], p[:, half:]
            o0 = hh * RET_DK
            out_ref[:, o0:o0 + half] = ((p1 * cos - p2 * sin) * scale).astype(BF16)
            out_ref[:, o0 + half:o0 + RET_DK] = ((p1 * sin + p2 * cos) * scale).astype(BF16)
    for hh in range(RET_HEADS):
        c0 = 2 * qk + hh * RET_DV
        v_ref[:, hh * RET_DV:(hh + 1) * RET_DV] = jnp.dot(
            h, w_ref[:, c0:c0 + RET_DV], preferred_element_type=F32).astype(BF16)
        g = jnp.dot(h, w_ref[:, vd + c0:vd + c0 + RET_DV], preferred_element_type=F32)
        sg_ref[:, hh * RET_DV:(hh + 1) * RET_DV] = _silu(g).astype(BF16)


def _pre_ret(st, x, norm_g, w_in_b, layer, j, cos, sin, tbl_spec):
    qk = RET_HEADS * RET_DK
    vd = RET_HEADS * RET_DV
    return pl.pallas_call(
        _pre_ret_kernel,
        grid=(st.n_tiles,),
        in_specs=[
            st.row_spec(D_MODEL),
            st.mod_spec(0), st.mod_spec(1),
            _layer_spec((1, D_MODEL), layer),
            _layer_spec((D_MODEL, 2 * qk + 2 * vd), j),
            tbl_spec, tbl_spec,
        ],
        out_specs=[st.row_spec(qk), st.row_spec(qk), st.row_spec(vd), st.row_spec(vd)],
        out_shape=[jax.ShapeDtypeStruct((st.n_rows, qk), BF16),
                   jax.ShapeDtypeStruct((st.n_rows, qk), BF16),
                   jax.ShapeDtypeStruct((st.n_rows, vd), BF16),
                   jax.ShapeDtypeStruct((st.n_rows, vd), BF16)],
        compiler_params=_cparams("parallel"),
        name="pre_ret",
    )(x, st.mod, st.mod, norm_g.reshape(-1, 1, D_MODEL), w_in_b, cos, sin)


def _decay_matrix(lg, ch):
    diff = (lax.broadcasted_iota(I32, (ch, ch), 0)
            - lax.broadcasted_iota(I32, (ch, ch), 1)).astype(F32)
    return jnp.where(diff >= 0.0, jnp.exp(lg * jnp.maximum(diff, 0.0)), 0.0)


def _retention_chunk(valid, lg, dec, q, k, v, sg, s_prev):
    ch = q.shape[0]
    idx = lax.broadcasted_iota(I32, (ch, 1), 0).astype(F32)
    scores = lax.dot_general(q, k, (((1,), (1,)), ((), ())), preferred_element_type=F32) * dec
    o = jnp.dot(scores.astype(BF16), v, preferred_element_type=F32)
    cross = jnp.exp(lg * (idx + 1.0))
    o = o + jnp.dot(q, s_prev.astype(BF16), preferred_element_type=F32) * cross
    kd = (k.astype(F32) * jnp.exp(lg * (valid - 1.0 - idx))).astype(BF16)
    s_new = jnp.exp(lg * float(valid)) * s_prev + lax.dot_general(
        kd, v, (((0,), (0,)), ((), ())), preferred_element_type=F32)
    mu = jnp.mean(o, axis=-1, keepdims=True)
    oc = o - mu
    var = jnp.mean(oc * oc, axis=-1, keepdims=True)
    on = oc * lax.rsqrt(var + NORM_EPS)
    return (sg.astype(F32) * on).astype(BF16), s_new


def _retention_kernel(lg_ref, q_ref, k_ref, v_ref, sg_ref, qs_ref, ks_ref, vs_ref, sgs_ref, s0_ref,
                      o_ref, s_out_ref, os_ref, ss_out_ref, s_ref, dec_ref, decs_ref):
    i = pl.program_id(0)
    n_chunks = SEQ // RET_CHUNK
    c = i % n_chunks

    @pl.when(i == 0)
    def _():
        for h in range(RET_HEADS):
            dec_ref[h] = _decay_matrix(lg_ref[h, 0:1, 0:1], RET_CHUNK)
            decs_ref[h] = _decay_matrix(lg_ref[h, 0:1, 0:1], RET_PAD)

    @pl.when(c == 0)
    def _():
        s_ref[...] = jnp.zeros_like(s_ref)

    for h in range(RET_HEADS):
        ks = slice(h * RET_DK, (h + 1) * RET_DK)
        vs = slice(h * RET_DV, (h + 1) * RET_DV)
        lg = lg_ref[h, 0:1, 0:1]
        gated, s_new = _retention_chunk(RET_CHUNK, lg, dec_ref[h], q_ref[:, ks], k_ref[:, ks],
                                        v_ref[:, vs], sg_ref[:, vs], s_ref[h])
        o_ref[:, vs] = gated
        s_ref[h] = s_new
        for b in range(RET_SS):
            rows = slice(b * RET_PAD, (b + 1) * RET_PAD)
            gated, s_new = _retention_chunk(DEC_SEQ, lg, decs_ref[h], qs_ref[rows, ks],
                                            ks_ref[rows, ks], vs_ref[rows, vs], sgs_ref[rows, vs],
                                            s0_ref[b, h])
            os_ref[rows, vs] = gated
            ss_out_ref[b, h] = s_new

    @pl.when(c == n_chunks - 1)
    def _():
        s_out_ref[...] = s_ref[...]


def _retention(q, k, v, sg, qkvg_s, s0, s0_layer, lg_tbl):
    qk = RET_HEADS * RET_DK
    vd = RET_HEADS * RET_DV
    n_chunks = SEQ // RET_CHUNK
    rows_s = RET_SS * RET_PAD
    col = lambda j: (lambda i: (i, j))
    state_p = (None, None, RET_HEADS, RET_DK, RET_DV)
    state_s = (None, RET_SS, RET_HEADS, RET_DK, RET_DV)
    return pl.pallas_call(
        _retention_kernel,
        grid=(BATCH * n_chunks,),
        in_specs=[
            pl.BlockSpec((RET_HEADS, 8, 128), lambda i: (0, 0, 0)),
            pl.BlockSpec((RET_CHUNK, qk), col(0)), pl.BlockSpec((RET_CHUNK, qk), col(0)),
            pl.BlockSpec((RET_CHUNK, vd), col(0)), pl.BlockSpec((RET_CHUNK, vd), col(0)),
            pl.BlockSpec((rows_s, qk), col(0)), pl.BlockSpec((rows_s, qk), col(1)),
            pl.BlockSpec((rows_s, vd), col(1)), pl.BlockSpec((rows_s, vd), col(2)),
            pl.BlockSpec(state_s, lambda i: (s0_layer, i, 0, 0, 0)),
        ],
        out_specs=[pl.BlockSpec((RET_CHUNK, vd), col(0)),
                   pl.BlockSpec(state_p, lambda i: (0, i // n_chunks, 0, 0, 0)),
                   pl.BlockSpec((rows_s, vd), col(0)),
                   pl.BlockSpec(state_s, lambda i: (0, i, 0, 0, 0))],
        out_shape=[jax.ShapeDtypeStruct((N_PROMPT, vd), BF16),
                   jax.ShapeDtypeStruct((1, BATCH, RET_HEADS, RET_DK, RET_DV), F32),
                   jax.ShapeDtypeStruct((DEC_BATCH * RET_PAD, vd), BF16),
                   jax.ShapeDtypeStruct((1, DEC_BATCH, RET_HEADS, RET_DK, RET_DV), F32)],
        scratch_shapes=[pltpu.VMEM((RET_HEADS, RET_DK, RET_DV), F32),
                        pltpu.VMEM((RET_HEADS, RET_CHUNK, RET_CHUNK), F32),
                        pltpu.VMEM((RET_HEADS, RET_PAD, RET_PAD), F32)],
        compiler_params=_cparams("arbitrary"),
        name="retention",
    )(lg_tbl, q, k, v, sg, qkvg_s, qkvg_s, qkvg_s, qkvg_s, s0)


def _excl_prefix(col):
    r = lax.broadcasted_iota(I32, (N_EXPERTS, N_EXPERTS), 0)
    c = lax.broadcasted_iota(I32, (N_EXPERTS, N_EXPERTS), 1)
    lane = jnp.sum(jnp.where(r < c, col, 0.0), axis=0, keepdims=True)
    return jnp.sum(jnp.where(r == c, lane, 0.0), axis=1, keepdims=True)


def _ceil_to(x, m):
    return jnp.floor((x + (m - 1.0)) * (1.0 / m)) * m


def _route_kernel(te_ref, lpos_ref, rstart_ref, rlen_ref, tot_ref, tri_ref):
    rt = te_ref.shape[2]
    iota_e = lax.broadcasted_iota(I32, (N_EXPERTS, rt), 0)
    s = lax.broadcasted_iota(I32, (rt, rt), 0)
    t = lax.broadcasted_iota(I32, (rt, rt), 1)
    tri_ref[...] = (s < t).astype(BF16)

    def tile_onehots(i):
        te = te_ref[i]
        onehots = [(te[k:k + 1, :] == iota_e).astype(F32) for k in range(TOP_K)]
        cnt = jnp.sum(onehots[0] + onehots[1] + onehots[2] + onehots[3], axis=1, keepdims=True)
        return onehots, _ceil_to(cnt, RUN_ALIGN)

    tot = lax.fori_loop(0, MOE_NT, lambda i, acc: acc + tile_onehots(i)[1],
                        jnp.zeros((N_EXPERTS, 1), F32))
    tot_ref[...] = tot

    def place(i, gbase):
        onehots, run = tile_onehots(i)
        base = _excl_prefix(run)
        for k in range(TOP_K):
            oh = onehots[k]
            before = jnp.dot(oh.astype(BF16), tri_ref[...], preferred_element_type=F32)
            pos = jnp.sum(oh * (base + before), axis=0, keepdims=True)
            lpos_ref[i, k:k + 1, :] = pos.astype(I32)
            base = base + jnp.sum(oh, axis=1, keepdims=True)
        rstart_ref[i] = jnp.broadcast_to(gbase, rstart_ref.shape[1:]).astype(I32)
        rlen_ref[i] = jnp.broadcast_to(run, rlen_ref.shape[1:]).astype(I32)
        return gbase + run

    lax.fori_loop(0, MOE_NT, place, _excl_prefix(_ceil_to(tot, MOE_TB)))


def _route(te):
    full = lambda shape: pl.BlockSpec(shape, lambda i: (0,) * len(shape))
    return pl.pallas_call(
        _route_kernel,
        grid=(1,),
        in_specs=[full(te.shape)],
        out_specs=[full(te.shape), full((MOE_NT, N_EXPERTS, 128)), full((MOE_NT, N_EXPERTS, 128)),
                   full((N_EXPERTS, 1))],
        out_shape=[jax.ShapeDtypeStruct(te.shape, I32),
                   jax.ShapeDtypeStruct((MOE_NT, N_EXPERTS, 128), I32),
                   jax.ShapeDtypeStruct((MOE_NT, N_EXPERTS, 128), I32),
                   jax.ShapeDtypeStruct((N_EXPERTS, 1), F32)],
        scratch_shapes=[pltpu.VMEM((MOE_RT, MOE_RT), BF16)],
        compiler_params=_cparams("arbitrary"),
        name="moe_route",
    )(te)


def _dispatch_kernel(n_prompt_tiles, rs_ref, rl_ref, tzs_ref, tzl_ref, nu_ref, hp_ref, hs_ref,
                     lpos_ref, xb_ref, xs_ref, sem, zero_ref):
    i = pl.program_id(0)
    slot = i % 2
    zero_sem = sem.at[2]

    def tail_copy(e):
        n = pl.multiple_of(tzl_ref[e], RUN_ALIGN)
        dst = pl.multiple_of(tzs_ref[e], RUN_ALIGN)
        return pltpu.make_async_copy(zero_ref.at[pl.ds(0, n)], xb_ref.at[pl.ds(dst, n)], zero_sem)

    def block_copy(j):
        dst = pl.multiple_of(j * MOE_TB, MOE_TB)
        return pltpu.make_async_copy(zero_ref, xb_ref.at[pl.ds(dst, MOE_TB)], zero_sem)

    @pl.when(i == 0)
    def _():
        zero_ref[...] = jnp.zeros_like(zero_ref)

        def start(e, carry):
            pl.when(tzl_ref[e] > 0)(lambda: tail_copy(e).start())
            return carry

        def wait(e, carry):
            pl.when(tzl_ref[e] > 0)(lambda: tail_copy(e).wait())
            return carry

        lax.fori_loop(0, N_EXPERTS, start, 0)
        lax.fori_loop(nu_ref[0], MOE_NBLK, lambda j, c: (block_copy(j).start(), c)[1], 0)
        lax.fori_loop(0, N_EXPERTS, wait, 0)
        lax.fori_loop(nu_ref[0], MOE_NBLK, lambda j, c: (block_copy(j).wait(), c)[1], 0)

    rows = lax.broadcasted_iota(I32, (MOE_CAP, MOE_RT), 0)
    perm = jnp.where(rows == lpos_ref[0:1, :], 1.0, 0.0)
    for k in range(1, TOP_K):
        perm = perm + jnp.where(rows == lpos_ref[k:k + 1, :], 1.0, 0.0)
    h = jnp.where(i < n_prompt_tiles, hp_ref[...], hs_ref[...])
    xs_ref[slot] = _pack_halves(jnp.dot(perm.astype(BF16), h, preferred_element_type=F32))

    def scatter(tile, wait):
        def body(e, lo):
            n = pl.multiple_of(rl_ref[tile * N_EXPERTS + e], RUN_ALIGN)
            sl = tile % 2
            src = pl.multiple_of(lo, RUN_ALIGN)
            dst = pl.multiple_of(rs_ref[tile * N_EXPERTS + e], RUN_ALIGN)
            cp = pltpu.make_async_copy(xs_ref.at[sl, pl.ds(src, n)], xb_ref.at[pl.ds(dst, n)],
                                       sem.at[sl])

            @pl.when(n > 0)
            def _():
                if wait:
                    cp.wait()
                else:
                    cp.start()

            return lo + n

        lax.fori_loop(0, N_EXPERTS, body, 0)

    scatter(i, False)

    @pl.when(i > 0)
    def _():
        scatter(i - 1, True)

    @pl.when(i == pl.num_programs(0) - 1)
    def _():
        scatter(i, True)


def _dispatch(tables, h_p, h_s, lpos):
    nt_p = h_p.shape[0] // MOE_RT
    return pl.pallas_call(
        functools.partial(_dispatch_kernel, nt_p),
        grid_spec=pltpu.PrefetchScalarGridSpec(
            num_scalar_prefetch=len(tables),
            grid=(MOE_NT,),
            in_specs=[
                pl.BlockSpec((MOE_RT, D_MODEL), lambda i, *_: (jnp.minimum(i, nt_p - 1), 0)),
                pl.BlockSpec((MOE_RT, D_MODEL), lambda i, *_: (jnp.maximum(i - nt_p, 0), 0)),
                pl.BlockSpec((None, TOP_K, MOE_RT), lambda i, *_: (i, 0, 0)),
            ],
            out_specs=pl.BlockSpec(memory_space=pl.ANY),
            scratch_shapes=[pltpu.VMEM((2, MOE_CAP, D_MODEL // 2), U32),
                            pltpu.SemaphoreType.DMA((3,)),
                            pltpu.VMEM((MOE_TB, D_MODEL // 2), U32)],
        ),
        out_shape=jax.ShapeDtypeStruct((MOE_ROWS, D_MODEL // 2), U32),
        compiler_params=_cparams("arbitrary"),
        name="moe_dispatch",
    )(*tables, h_p, h_s, lpos)


def _ffn_kernel(layer, be_ref, nu_ref, nxt_ref, bgu_ref, bdn_ref, xb_hbm, wgu_hbm, wdn_hbm,
                yb_hbm, wgu_f_ref, wdn_f_ref, wgu_b_ref, wdn_b_ref, act_ref, xbuf_ref, ybuf_ref,
                wsem, xsem, ysem):
    n_used = nu_ref[0]

    def weight_copies(ex):
        return (pltpu.make_async_copy(wgu_hbm.at[layer, ex], wgu_f_ref, wsem.at[0]),
                pltpu.make_async_copy(wdn_hbm.at[layer, ex], wdn_f_ref, wsem.at[1]))

    def block_rows(j):
        return pl.ds(pl.multiple_of(j * MOE_TB, MOE_TB), MOE_TB)

    def x_copy(j):
        return pltpu.make_async_copy(xb_hbm.at[block_rows(j)], xbuf_ref.at[j % 2], xsem.at[j % 2])

    def y_copy(j):
        return pltpu.make_async_copy(ybuf_ref.at[j % 2], yb_hbm.at[block_rows(j)], ysem.at[j % 2])

    for cp in weight_copies(be_ref[0]):
        cp.start()
    x_copy(0).start()

    def block(j, carry):
        e = be_ref[j]
        e_prev = be_ref[jnp.maximum(j - 1, 0)]

        @pl.when(j + 1 < n_used)
        def _():
            x_copy(j + 1).start()

        @pl.when((j == 0) | (e != e_prev))
        def _():
            for cp in weight_copies(e):
                cp.wait()
            wgu_b_ref[...] = wgu_f_ref[...].astype(BF16)
            wdn_b_ref[...] = wdn_f_ref[...].astype(BF16)
            nx = nxt_ref[j]

            @pl.when(nx >= 0)
            def _():
                for cp in weight_copies(nx):
                    cp.start()

        x_copy(j).wait()

        @pl.when(j >= 2)
        def _():
            y_copy(j - 2).wait()

        x = _unpack_halves(xbuf_ref[j % 2])
        bgu = bgu_ref[e]
        for c in range(D_FF // FF_CHUNK):
            gs = slice(c * FF_CHUNK, (c + 1) * FF_CHUNK)
            us = slice(D_FF + c * FF_CHUNK, D_FF + (c + 1) * FF_CHUNK)
            gate = jnp.dot(x, wgu_b_ref[:, gs], preferred_element_type=F32) + bgu[:, gs]
            up = jnp.dot(x, wgu_b_ref[:, us], preferred_element_type=F32) + bgu[:, us]
            gate = jnp.minimum(gate, SWIGLU_LIMIT)
            up = jnp.clip(up, -SWIGLU_LIMIT, SWIGLU_LIMIT)
            act = (up + 1.0) * (gate * jax.nn.sigmoid(SWIGLU_ALPHA * gate))
            act_ref[:, gs] = act.astype(BF16)
        y = jnp.dot(act_ref[...], wdn_b_ref[...], preferred_element_type=F32) + bdn_ref[e]
        ybuf_ref[j % 2] = _pack_halves(y.astype(BF16).astype(F32))
        y_copy(j).start()
        return carry

    lax.fori_loop(0, n_used, block, 0)

    @pl.when(n_used >= 2)
    def _():
        y_copy(n_used - 2).wait()

    y_copy(n_used - 1).wait()

    ybuf_ref[0] = jnp.zeros_like(ybuf_ref[0])

    def zero_copy(j):
        return pltpu.make_async_copy(ybuf_ref.at[0], yb_hbm.at[block_rows(j)], ysem.at[0])

    lax.fori_loop(n_used, MOE_NBLK, lambda j, c: (zero_copy(j).start(), c)[1], 0)
    lax.fori_loop(n_used, MOE_NBLK, lambda j, c: (zero_copy(j).wait(), c)[1], 0)


def _ffn(layer, block_e, n_used, nxt, xb, w_gu, b_gu, w_dn, b_dn):
    bias_map = lambda i, *_: (layer, 0, 0, 0)
    return pl.pallas_call(
        functools.partial(_ffn_kernel, layer),
        grid_spec=pltpu.PrefetchScalarGridSpec(
            num_scalar_prefetch=3,
            grid=(1,),
            in_specs=[
                pl.BlockSpec((None, N_EXPERTS, 1, 2 * D_FF), bias_map),
                pl.BlockSpec((None, N_EXPERTS, 1, D_MODEL), bias_map),
                pl.BlockSpec(memory_space=pl.ANY),
                pl.BlockSpec(memory_space=pl.ANY),
                pl.BlockSpec(memory_space=pl.ANY),
            ],
            out_specs=pl.BlockSpec(memory_space=pl.ANY),
            scratch_shapes=[pltpu.VMEM((D_MODEL, 2 * D_FF), F32),
                            pltpu.VMEM((D_FF, D_MODEL), F32),
                            pltpu.VMEM((D_MODEL, 2 * D_FF), BF16),
                            pltpu.VMEM((D_FF, D_MODEL), BF16),
                            pltpu.VMEM((MOE_TB, D_FF), BF16),
                            pltpu.VMEM((2, MOE_TB, D_MODEL // 2), U32),
                            pltpu.VMEM((2, MOE_TB, D_MODEL // 2), U32),
                            pltpu.SemaphoreType.DMA((2,)),
                            pltpu.SemaphoreType.DMA((2,)),
                            pltpu.SemaphoreType.DMA((2,))],
        ),
        out_shape=jax.ShapeDtypeStruct((MOE_ROWS, D_MODEL // 2), U32),
        compiler_params=_cparams("arbitrary"),
        name="moe_ffn",
    )(block_e, n_used, nxt, b_gu.reshape(b_gu.shape[0], N_EXPERTS, 1, 2 * D_FF),
      b_dn.reshape(b_dn.shape[0], N_EXPERTS, 1, D_MODEL), xb, w_gu, w_dn)


def _tile_rows(v, n_rows):
    mr = v.shape[0]
    if mr in (1, n_rows):
        return v
    return jnp.concatenate([v] * (n_rows // mr), axis=0)


def _combine_kernel(tile_off, final, rs_ref, rl_ref, x_ref, gf_ref, lpt_ref, wt_ref, fg_ref,
                    yb_ref, out_ref, ys_ref, sem):
    i = pl.program_id(0)

    def gather(tile, wait):
        def body(e, lo):
            n = pl.multiple_of(rl_ref[(tile_off + tile) * N_EXPERTS + e], RUN_ALIGN)
            sl = tile % 2
            src = pl.multiple_of(rs_ref[(tile_off + tile) * N_EXPERTS + e], RUN_ALIGN)
            dst = pl.multiple_of(lo, RUN_ALIGN)
            cp = pltpu.make_async_copy(yb_ref.at[pl.ds(src, n)], ys_ref.at[sl, pl.ds(dst, n)],
                                       sem.at[sl])

            @pl.when(n > 0)
            def _():
                if wait:
                    cp.wait()
                else:
                    cp.start()

            return lo + n

        lax.fori_loop(0, N_EXPERTS, body, 0)

    @pl.when(i == 0)
    def _():
        ys_ref[...] = jnp.zeros_like(ys_ref)
        gather(0, False)

    @pl.when(i + 1 < pl.num_programs(0))
    def _():
        gather(i + 1, False)

    gather(i, True)

    cols = lax.broadcasted_iota(I32, (MOE_RT, MOE_CAP), 1)
    w = jnp.where(cols == lpt_ref[:, 0:1], wt_ref[:, 0:1], 0.0)
    for k in range(1, TOP_K):
        w = w + jnp.where(cols == lpt_ref[:, k:k + 1], wt_ref[:, k:k + 1], 0.0)
    ys = _unpack_halves(ys_ref[i % 2])
    y = jnp.dot(w.astype(BF16), ys, preferred_element_type=F32)
    xn = x_ref[...] + _tile_rows(gf_ref[...], MOE_RT) * y
    if final:
        xn = xn * lax.rsqrt(jnp.mean(xn * xn, axis=-1, keepdims=True) + NORM_EPS) * fg_ref[...]
    out_ref[...] = xn


def _combine(tile_off, n_tiles, final, tables, x, mod, mod_map, lpos_t, wt, final_g, yb):
    pm = lambda i, *_: (i, 0)
    tm = lambda i, *_: (i + tile_off, 0)
    return pl.pallas_call(
        functools.partial(_combine_kernel, tile_off, final),
        grid_spec=pltpu.PrefetchScalarGridSpec(
            num_scalar_prefetch=len(tables),
            grid=(n_tiles,),
            in_specs=[
                pl.BlockSpec((MOE_RT, D_MODEL), pm),
                pl.BlockSpec((None, mod.shape[1], D_MODEL), mod_map),
                pl.BlockSpec((MOE_RT, TOP_K), tm),
                pl.BlockSpec((MOE_RT, TOP_K), tm),
                pl.BlockSpec((1, D_MODEL), lambda i, *_: (0, 0)),
                pl.BlockSpec(memory_space=pl.ANY),
            ],
            out_specs=pl.BlockSpec((MOE_RT, D_MODEL), pm),
            scratch_shapes=[pltpu.VMEM((2, MOE_CAP, D_MODEL // 2), U32),
                            pltpu.SemaphoreType.DMA((2,))],
        ),
        out_shape=jax.ShapeDtypeStruct(x.shape, F32),
        compiler_params=_cparams("arbitrary"),
        name="moe_combine",
    )(*tables, x, mod, lpos_t, wt, final_g.reshape(1, D_MODEL), yb)


def _moe(layer, final, st_p, st_s, x_p, x_s, post_p, post_s,
         w_gu, b_gu, w_dn, b_dn, final_g):
    te = jnp.concatenate([post_p[1], post_s[1]], axis=1)
    tw = jnp.concatenate([post_p[2], post_s[2]], axis=1)
    te = jnp.swapaxes(te.reshape(TOP_K, MOE_NT, MOE_RT), 0, 1)
    lpos, rstart, rlen, tot = _route(te)
    tot = tot[:, 0].astype(I32)
    padded = (tot + MOE_TB - 1) // MOE_TB * MOE_TB
    pends = jnp.cumsum(padded)
    n_used = (pends[-1] // MOE_TB).reshape(1)
    blk = jnp.minimum(jnp.arange(MOE_NBLK, dtype=I32), n_used[0] - 1) * MOE_TB
    block_e = jnp.minimum(jnp.sum(pends[None, :] <= blk[:, None], axis=-1),
                          N_EXPERTS - 1).astype(I32)
    ids = jnp.where(tot > 0, jnp.arange(N_EXPERTS, dtype=I32), N_EXPERTS)
    later = jnp.concatenate([lax.cummin(ids[::-1])[::-1][1:], jnp.full((1,), N_EXPERTS, I32)])
    nxt_of = jnp.where(later < N_EXPERTS, later, -1)
    of_block = block_e[:, None] == jnp.arange(N_EXPERTS, dtype=I32)[None, :]
    nxt = jnp.sum(jnp.where(of_block, nxt_of[None, :], 0), axis=-1)
    run_tables = (rstart[:, :, 0].reshape(-1), rlen[:, :, 0].reshape(-1))
    tail_tables = (pends - padded + tot, padded - tot)

    nt_p = N_PROMPT // MOE_RT
    nt_s = N_SAMPLE // MOE_RT
    xb = _dispatch(run_tables + tail_tables + (n_used,), post_p[0], post_s[0], lpos)
    yb = _ffn(layer, block_e, n_used, nxt, xb, w_gu, b_gu, w_dn, b_dn)
    lpos_t = jnp.swapaxes(lpos, 1, 2).reshape(N_TOK, TOP_K)
    wt = tw.T
    tiles_per_seq = SEQ // MOE_RT
    out_p = _combine(0, nt_p, final, run_tables, x_p, st_p.mod,
                     lambda i, *_: (i // tiles_per_seq, 0, N_MOD - 1), lpos_t, wt, final_g, yb)
    out_s = _combine(nt_p, nt_s, final, run_tables, x_s, st_s.mod,
                     lambda i, *_: (0, 0, N_MOD - 1), lpos_t, wt, final_g, yb)
    return out_p, out_s


def _rope_tables(pos):
    half = RET_DK // 2
    inv_freq = ROPE_BASE ** (-jnp.arange(half, dtype=F32) / half)
    ang = pos.astype(F32)[:, None] * inv_freq[None, :]
    return jnp.cos(ang), jnp.sin(ang)


def kernel(x_prompt, x_sample, c_prompt, c_sample, state_conv, state_ret, norm_mix_g, norm_ff_g,
           w_mod, b_mod, conv_w1, conv_b1, conv_dw, conv_dw_b, conv_ln_g, conv_ln_b, conv_w2,
           conv_b2, ret_w_in, ret_w_o, router_w, router_b, moe_w_gu, moe_b_gu, moe_w_dn,
           moe_b_dn, final_g):
    mod = _adaln(jnp.concatenate([c_prompt, c_sample], axis=0), w_mod, b_mod)
    x_p = x_prompt.reshape(N_PROMPT, D_MODEL)
    x_s = jnp.swapaxes(x_sample, 0, 1).reshape(N_SAMPLE, D_MODEL)
    router_wt = jnp.swapaxes(router_w, 1, 2)

    def streams(layer):
        st_p = _Stream(N_PROMPT, TM_PROMPT, SEQ // TM_PROMPT,
                       mod[layer, :BATCH].reshape(BATCH, 1, N_MOD * D_MODEL))
        st_s = _Stream(N_SAMPLE, TM_SAMPLE, None,
                       mod[layer, BATCH:].reshape(1, DEC_BATCH, N_MOD * D_MODEL))
        return st_p, st_s

    st_p, st_s = streams(0)
    u_p = _pre_conv(st_p, x_p, norm_mix_g, conv_w1, conv_b1, 0, 0)
    u_s = _pre_conv(st_s, x_s, norm_mix_g, conv_w1, conv_b1, 0, 0)
    buf_tm = jnp.swapaxes(state_conv[0], 0, 1)
    z_s, ns_tm = _conv_sample(buf_tm, u_s.reshape(DEC_SEQ, DEC_BATCH, D_MODEL),
                              conv_dw, conv_dw_b, 0)
    post_p = _mixer_post("conv", st_p, 0, 0, x_p, u_p, norm_ff_g, router_wt, router_b,
                         conv_w2, conv_b2, conv_dw, conv_dw_b, conv_ln_g, conv_ln_b)
    post_s = _mixer_post("z", st_s, 0, 0, x_s, z_s.reshape(N_SAMPLE, D_MODEL), norm_ff_g,
                         router_wt, router_b, conv_w2, conv_b2, None, None, conv_ln_g, conv_ln_b)
    x_p, x_s = _moe(0, False, st_p, st_s, post_p[0], post_s[0], post_p[1:], post_s[1:],
                    moe_w_gu, moe_b_gu, moe_w_dn, moe_b_dn, final_g)
    conv_p = u_p.reshape(BATCH, SEQ, D_MODEL)[:, SEQ - CONV_STATE:][None]
    conv_s = jnp.swapaxes(ns_tm, 0, 1)[None]

    st_p, st_s = streams(1)
    w_in_b = ret_w_in.astype(BF16)
    cos_p, sin_p = _rope_tables(jnp.arange(SEQ, dtype=I32))
    cos_s, sin_s = _rope_tables(PAST_LEN + jnp.arange(DEC_SEQ, dtype=I32))
    half = RET_DK // 2
    tps = SEQ // TM_PROMPT
    tbl_p = pl.BlockSpec((TM_PROMPT, half), lambda i: (i % tps, 0))
    tbl_s = pl.BlockSpec((None, 1, half), lambda i: (i, 0, 0))
    q_p, k_p, v_p, sg_p = _pre_ret(st_p, x_p, norm_mix_g, w_in_b, 1, 0, cos_p, sin_p, tbl_p)
    q_s, k_s, v_s, sg_s = _pre_ret(st_s, x_s, norm_mix_g, w_in_b, 1, 0,
                                   cos_s.reshape(DEC_SEQ, 1, half),
                                   sin_s.reshape(DEC_SEQ, 1, half), tbl_s)
    lg = jnp.log1p(-jnp.exp2(-5.0 - jnp.arange(RET_HEADS, dtype=F32)))
    lg_tbl = jnp.broadcast_to(lg[:, None, None], (RET_HEADS, 8, 128))
    def to_seq_major(a):
        a = jnp.swapaxes(a.reshape(DEC_SEQ, DEC_BATCH, -1), 0, 1)
        a = jnp.pad(a, ((0, 0), (0, RET_PAD - DEC_SEQ), (0, 0)))
        return a.reshape(DEC_BATCH * RET_PAD, -1)

    qkvg_s = to_seq_major(jnp.concatenate([q_s, k_s, v_s, sg_s], axis=1))
    gated_p, ret_p, gated_s, ret_s = _retention(q_p, k_p, v_p, sg_p, qkvg_s, state_ret, 0, lg_tbl)
    gated_s = jnp.swapaxes(gated_s.reshape(DEC_BATCH, RET_PAD, -1)[:, :DEC_SEQ], 0, 1)
    gated_s = gated_s.reshape(N_SAMPLE, -1)
    post_p = _mixer_post("ret", st_p, 1, 0, x_p, gated_p, norm_ff_g, router_wt, router_b, ret_w_o)
    post_s = _mixer_post("ret", st_s, 1, 0, x_s, gated_s, norm_ff_g, router_wt, router_b, ret_w_o)
    y_p, y_s = _moe(1, True, st_p, st_s, post_p[0], post_s[0], post_p[1:], post_s[1:],
                    moe_w_gu, moe_b_gu, moe_w_dn, moe_b_dn, final_g)

    y_prompt = y_p.reshape(BATCH, SEQ, D_MODEL)
    y_sample = jnp.swapaxes(y_s.reshape(DEC_SEQ, DEC_BATCH, D_MODEL), 0, 1)
    return (y_prompt, y_sample, conv_p, conv_s, ret_p, ret_s)
```

```python
import functools

import jax
import jax.numpy as jnp
from jax import lax
from jax.experimental import pallas as pl
from jax.experimental.pallas import tpu as pltpu

F32 = jnp.float32
BF16 = jnp.bfloat16
I32 = jnp.int32
U32 = jnp.uint32

D_MODEL = 1024
SEQ = 2048
BATCH = 8
DEC_BATCH = 128
DEC_SEQ = 4
PAST_LEN = 16384
CONV_WIDTH = 31
CONV_STATE = CONV_WIDTH - 1
RET_HEADS = 4
RET_DK = 256
RET_DV = 512
ROPE_BASE = 10000.0
N_EXPERTS = 32
TOP_K = 4
D_FF = 1024
SWIGLU_LIMIT = 7.0
SWIGLU_ALPHA = 1.702
N_MOD = 6
NORM_EPS = 1e-5

N_PROMPT = BATCH * SEQ
N_SAMPLE = DEC_BATCH * DEC_SEQ
N_TOK = N_PROMPT + N_SAMPLE
N_ASSIGN = N_TOK * TOP_K

TM_PROMPT = 512
TM_SAMPLE = DEC_BATCH
HALO = 32
CONV_RB = 64
CONV_LW = 128
RET_CHUNK = 256
RET_PAD = 16
RET_SS = DEC_BATCH // (BATCH * (SEQ // RET_CHUNK))
MOE_TB = 512
MOE_RT = 256
MOE_TPS = 2
MOE_NT = N_TOK // MOE_RT
RUN_ALIGN = 8
MOE_CAP = -(-(TOP_K * MOE_RT + N_EXPERTS * (RUN_ALIGN - 1)) // 256) * 256
MOE_MAX_ROWS = N_ASSIGN + MOE_NT * N_EXPERTS * (RUN_ALIGN - 1)
MOE_NBLK = -(-MOE_MAX_ROWS // MOE_TB) + N_EXPERTS
MOE_ROWS = MOE_NBLK * MOE_TB
FF_CHUNK = 256
VMEM_LIMIT = 56 * 1024 * 1024


def _cparams(*sem):
    return pltpu.CompilerParams(dimension_semantics=sem, vmem_limit_bytes=VMEM_LIMIT)


def _silu(x):
    return x * jax.nn.sigmoid(x)


def _rms_mod(x, g, sc, sh):
    y = x * lax.rsqrt(jnp.mean(x * x, axis=-1, keepdims=True) + NORM_EPS) * g
    return y * (1.0 + sc) + sh


def _split_bf16(x):
    hi = x.astype(BF16)
    lo = (x - hi.astype(F32)).astype(BF16)
    return hi, lo


def _pack_halves(x):
    c = x.shape[1] // 2
    lo = lax.bitcast_convert_type(x[:, :c], U32)
    hi = lax.bitcast_convert_type(x[:, c:], U32)
    return (lo >> 16) | (hi & jnp.uint32(0xFFFF0000))


def _unpack_halves(p):
    lo = lax.bitcast_convert_type(p << 16, F32)
    hi = lax.bitcast_convert_type(p & jnp.uint32(0xFFFF0000), F32)
    return jnp.concatenate([lo.astype(BF16), hi.astype(BF16)], axis=1)


def _adaln_kernel(c_ref, w_ref, b_ref, o_ref):
    a = _silu(c_ref[...]).astype(BF16)
    o_ref[...] = jnp.dot(a, w_ref[...].astype(BF16), preferred_element_type=F32) + b_ref[...]


def _adaln(c_all, w_mod, b_mod):
    depth, d, nm = w_mod.shape
    n = c_all.shape[0]
    return pl.pallas_call(
        _adaln_kernel,
        grid=(depth, nm // d),
        in_specs=[
            pl.BlockSpec((n, d), lambda l, j: (0, 0)),
            pl.BlockSpec((None, d, d), lambda l, j: (l, 0, j)),
            pl.BlockSpec((None, 1, d), lambda l, j: (l, 0, j)),
        ],
        out_specs=pl.BlockSpec((None, n, d), lambda l, j: (l, 0, j)),
        out_shape=jax.ShapeDtypeStruct((depth, n, nm), F32),
        compiler_params=_cparams("parallel", "parallel"),
        name="adaln",
    )(c_all, w_mod, b_mod.reshape(depth, 1, nm))


class _Stream:
    def __init__(self, n_rows, tm, tiles_per_seq, mod):
        self.n_rows = n_rows
        self.tm = tm
        self.n_tiles = n_rows // tm
        self.tiles_per_seq = tiles_per_seq
        self.mod = mod

    def mod_spec(self, j):
        mr = self.mod.shape[1]
        if self.tiles_per_seq is None:
            return pl.BlockSpec((None, mr, D_MODEL), lambda i: (0, 0, j))
        tps = self.tiles_per_seq
        return pl.BlockSpec((None, mr, D_MODEL), lambda i: (i // tps, 0, j))

    def row_spec(self, width, col=0):
        return pl.BlockSpec((self.tm, width), lambda i: (i, col))


def _const_spec(shape):
    nd = len(shape)
    return pl.BlockSpec(shape, lambda i: (0,) * nd)


def _layer_spec(shape, layer):
    nd = len(shape)
    return pl.BlockSpec((None,) + shape, lambda i: (layer,) + (0,) * nd)


def _pre_conv_kernel(x_ref, sh_ref, sc_ref, g_ref, w1_ref, b1_ref, u_ref, w1b_ref):
    @pl.when(pl.program_id(0) == 0)
    def _():
        w1b_ref[...] = w1_ref[...].astype(BF16)

    h = _rms_mod(x_ref[...], g_ref[...], sc_ref[...], sh_ref[...])
    ag = jnp.dot(h.astype(BF16), w1b_ref[...], preferred_element_type=F32) + b1_ref[...]
    u_ref[...] = ag[:, :D_MODEL] * jax.nn.sigmoid(ag[:, D_MODEL:])


def _pre_conv(st, x, norm_g, w1, b1, layer, j):
    c2 = w1.shape[-1]
    return pl.pallas_call(
        _pre_conv_kernel,
        grid=(st.n_tiles,),
        in_specs=[
            st.row_spec(D_MODEL),
            st.mod_spec(0), st.mod_spec(1),
            _layer_spec((1, D_MODEL), layer),
            _layer_spec((D_MODEL, c2), j),
            _layer_spec((1, c2), j),
        ],
        out_specs=st.row_spec(D_MODEL),
        out_shape=jax.ShapeDtypeStruct((st.n_rows, D_MODEL), F32),
        scratch_shapes=[pltpu.VMEM((D_MODEL, c2), BF16)],
        compiler_params=_cparams("arbitrary"),
        name="pre_conv",
    )(x, st.mod, st.mod, norm_g.reshape(-1, 1, D_MODEL), w1, b1.reshape(-1, 1, c2))


def _conv_sample_kernel(buf_ref, u_ref, dw_ref, dwb_ref, z_ref, ns_ref):
    for t in range(DEC_SEQ):
        acc = jnp.broadcast_to(dwb_ref[...], u_ref.shape[1:])
        for j in range(t, CONV_STATE):
            acc = acc + buf_ref[j] * dw_ref[j - t:j - t + 1, :]
        for s in range(t + 1):
            k = CONV_STATE + s - t
            acc = acc + u_ref[s] * dw_ref[k:k + 1, :]
        z_ref[t] = acc
    ns_ref[0:CONV_STATE - DEC_SEQ] = buf_ref[DEC_SEQ:CONV_STATE]
    ns_ref[CONV_STATE - DEC_SEQ:CONV_STATE] = u_ref[...]


def _conv_sample(buf_tm, u_tm, dw, dwb, j):
    sb = 32
    c = u_tm.shape[-1]
    return pl.pallas_call(
        _conv_sample_kernel,
        grid=(DEC_BATCH // sb,),
        in_specs=[
            pl.BlockSpec((CONV_STATE, sb, c), lambda i: (0, i, 0)),
            pl.BlockSpec((DEC_SEQ, sb, c), lambda i: (0, i, 0)),
            _layer_spec((CONV_WIDTH, c), j),
            _layer_spec((1, c), j),
        ],
        out_specs=[
            pl.BlockSpec((DEC_SEQ, sb, c), lambda i: (0, i, 0)),
            pl.BlockSpec((CONV_STATE, sb, c), lambda i: (0, i, 0)),
        ],
        out_shape=[
            jax.ShapeDtypeStruct((DEC_SEQ, DEC_BATCH, c), F32),
            jax.ShapeDtypeStruct((CONV_STATE, DEC_BATCH, c), F32),
        ],
        compiler_params=_cparams("parallel"),
        name="conv_sample",
    )(buf_tm, u_tm, dw, dwb.reshape(-1, 1, c))


def _route_topk(h2, rwt_ref, rb_ref, te_ref, tw_ref):
    h_hi, h_lo = _split_bf16(h2)
    w_hi, w_lo = _split_bf16(rwt_ref[...])
    nt = (((1,), (1,)), ((), ()))
    logits = (lax.dot_general(w_hi, h_hi, nt, preferred_element_type=F32)
              + lax.dot_general(w_hi, h_lo, nt, preferred_element_type=F32)
              + lax.dot_general(w_lo, h_hi, nt, preferred_element_type=F32)
              + rb_ref[...])
    iota_e = lax.broadcasted_iota(I32, logits.shape, 0).astype(F32)
    vals, idxs = [], []
    for _ in range(TOP_K):
        m = jnp.max(logits, axis=0, keepdims=True)
        idx = jnp.min(jnp.where(logits == m, iota_e, float(N_EXPERTS)), axis=0, keepdims=True)
        logits = jnp.where(iota_e == idx, -jnp.inf, logits)
        vals.append(m)
        idxs.append(idx)
    es = [jnp.exp(v - vals[0]) for v in vals]
    tot = es[0] + es[1] + es[2] + es[3]
    for k in range(TOP_K):
        te_ref[k:k + 1, :] = idxs[k].astype(I32)
        tw_ref[k:k + 1, :] = es[k] / tot


def _mixer_post_kernel(mode, tiles_per_seq, *refs):
    it = iter(refs)
    if mode == "conv":
        u_ref, halo_ref, dw_ref, dwb_ref = next(it), next(it), next(it), next(it)
    elif mode == "z":
        z_in_ref = next(it)
    else:
        gated_ref = next(it)
    if mode in ("conv", "z"):
        lng_ref, lnb_ref = next(it), next(it)
    w_ref, b_ref = next(it), next(it)
    x_ref, gm_ref, shf_ref, scf_ref, gff_ref, rwt_ref, rb_ref = (next(it) for _ in range(7))
    x1_ref, h2_ref, te_ref, tw_ref = (next(it) for _ in range(4))
    wb_ref = next(it)
    if mode == "conv":
        ext_ref, z_ref = next(it), next(it)

    i = pl.program_id(0)

    @pl.when(i == 0)
    def _():
        wb_ref[...] = w_ref[...].astype(BF16)

    if mode == "conv":
        tm = u_ref.shape[0]
        first = (i % tiles_per_seq) == 0
        ext_ref[0:HALO, :] = jnp.where(first, 0.0, halo_ref[...])
        ext_ref[HALO:HALO + tm, :] = u_ref[...]
        off = HALO - CONV_STATE

        def row_block(r, carry):
            r0 = pl.multiple_of(r * CONV_RB, CONV_RB)
            for lc in range(D_MODEL // CONV_LW):
                ls = slice(lc * CONV_LW, (lc + 1) * CONV_LW)
                win = ext_ref[pl.ds(r0, CONV_RB + HALO), ls]
                acc = jnp.broadcast_to(dwb_ref[:, ls], (CONV_RB, CONV_LW))
                for s in range(8):
                    ws = win if s == 0 else pltpu.roll(win, CONV_RB + HALO - s, 0)
                    for k in range(CONV_WIDTH):
                        if (off + k) % 8 == s:
                            a = off + k - s
                            acc = acc + ws[a:a + CONV_RB, :] * dw_ref[k:k + 1, ls]
                z_ref[pl.ds(r0, CONV_RB), ls] = acc
            return carry

        lax.fori_loop(0, tm // CONV_RB, row_block, 0)
        z = z_ref[...]
    elif mode == "z":
        z = z_in_ref[...]

    if mode in ("conv", "z"):
        mu = jnp.mean(z, axis=-1, keepdims=True)
        zc = z - mu
        var = jnp.mean(zc * zc, axis=-1, keepdims=True)
        zn = zc * lax.rsqrt(var + NORM_EPS) * lng_ref[...] + lnb_ref[...]
        a = _silu(zn).astype(BF16)
        y = jnp.dot(a, wb_ref[...], preferred_element_type=F32) + b_ref[...]
    else:
        y = jnp.dot(gated_ref[...], wb_ref[...], preferred_element_type=F32)

    x1 = x_ref[...] + gm_ref[...] * y
    x1_ref[...] = x1
    h2 = _rms_mod(x1, gff_ref[...], scf_ref[...], shf_ref[...])
    h2_ref[...] = h2.astype(BF16)
    _route_topk(h2, rwt_ref, rb_ref, te_ref, tw_ref)


def _mixer_post(mode, st, layer, j, x, mix_in, norm_ff_g, router_wt, router_b, w, b=None,
                dw=None, dwb=None, ln_g=None, ln_b=None):
    tm = st.tm
    k_in = w.shape[-2]
    args, specs = [], []
    if mode == "conv":
        per32 = tm // HALO
        args += [mix_in, mix_in, dw, dwb.reshape(-1, 1, D_MODEL)]
        specs += [st.row_spec(D_MODEL),
                  pl.BlockSpec((HALO, D_MODEL), lambda i: (jnp.maximum(i * per32 - 1, 0), 0)),
                  _layer_spec((CONV_WIDTH, D_MODEL), j), _layer_spec((1, D_MODEL), j)]
    elif mode == "z":
        args += [mix_in]
        specs += [st.row_spec(D_MODEL)]
    else:
        args += [mix_in]
        specs += [st.row_spec(k_in)]
    if mode in ("conv", "z"):
        args += [ln_g.reshape(-1, 1, D_MODEL), ln_b.reshape(-1, 1, D_MODEL)]
        specs += [_layer_spec((1, D_MODEL), j), _layer_spec((1, D_MODEL), j)]
    if b is None:
        b = jnp.zeros((w.shape[0], D_MODEL), F32)
    args += [w, b.reshape(-1, 1, D_MODEL)]
    specs += [_layer_spec((k_in, D_MODEL), j), _layer_spec((1, D_MODEL), j)]
    args += [x, st.mod, st.mod, st.mod, norm_ff_g.reshape(-1, 1, D_MODEL), router_wt,
             router_b.reshape(-1, N_EXPERTS, 1)]
    specs += [st.row_spec(D_MODEL), st.mod_spec(2), st.mod_spec(3), st.mod_spec(4),
              _layer_spec((1, D_MODEL), layer), _layer_spec((N_EXPERTS, D_MODEL), layer),
              _layer_spec((N_EXPERTS, 1), layer)]
    scratch = [pltpu.VMEM((k_in, D_MODEL), BF16)]
    if mode == "conv":
        scratch += [pltpu.VMEM((tm + HALO, D_MODEL), F32), pltpu.VMEM((tm, D_MODEL), F32)]
    return pl.pallas_call(
        functools.partial(_mixer_post_kernel, mode, st.tiles_per_seq),
        grid=(st.n_tiles,),
        in_specs=specs,
        out_specs=[st.row_spec(D_MODEL), st.row_spec(D_MODEL),
                   pl.BlockSpec((TOP_K, tm), lambda i: (0, i)),
                   pl.BlockSpec((TOP_K, tm), lambda i: (0, i))],
        out_shape=[jax.ShapeDtypeStruct((st.n_rows, D_MODEL), F32),
                   jax.ShapeDtypeStruct((st.n_rows, D_MODEL), BF16),
                   jax.ShapeDtypeStruct((TOP_K, st.n_rows), I32),
                   jax.ShapeDtypeStruct((TOP_K, st.n_rows), F32)],
        scratch_shapes=scratch,
        compiler_params=_cparams("arbitrary"),
        name="mixer_post_" + mode,
    )(*args)


def _pre_ret_kernel(x_ref, sh_ref, sc_ref, g_ref, w_ref, cos_ref, sin_ref,
                    q_ref, k_ref, v_ref, sg_ref):
    h = _rms_mod(x_ref[...], g_ref[...], sc_ref[...], sh_ref[...]).astype(BF16)
    cos = cos_ref[...]
    sin = sin_ref[...]
    half = RET_DK // 2
    qk = RET_HEADS * RET_DK
    vd = RET_HEADS * RET_DV
    for out_ref, base, scale in ((q_ref, 0, 1.0), (k_ref, qk, RET_DK ** -0.5)):
        for hh in range(RET_HEADS):
            c0 = base + hh * RET_DK
            p = jnp.dot(h, w_ref[:, c0:c0 + RET_DK], preferred_element_type=F32)
            p1, p2 = p[:, :half], p[:, half:]
            o0 = hh * RET_DK
            out_ref[:, o0:o0 + half] = ((p1 * cos - p2 * sin) * scale).astype(BF16)
            out_ref[:, o0 + half:o0 + RET_DK] = ((p1 * sin + p2 * cos) * scale).astype(BF16)
    for hh in range(RET_HEADS):
        c0 = 2 * qk + hh * RET_DV
        v_ref[:, hh * RET_DV:(hh + 1) * RET_DV] = jnp.dot(
            h, w_ref[:, c0:c0 + RET_DV], preferred_element_type=F32).astype(BF16)
        g = jnp.dot(h, w_ref[:, vd + c0:vd + c0 + RET_DV], preferred_element_type=F32)
        sg_ref[:, hh * RET_DV:(hh + 1) * RET_DV] = _silu(g).astype(BF16)


def _pre_ret(st, x, norm_g, w_in_b, layer, j, cos, sin, tbl_spec):
    qk = RET_HEADS * RET_DK
    vd = RET_HEADS * RET_DV
    return pl.pallas_call(
        _pre_ret_kernel,
        grid=(st.n_tiles,),
        in_specs=[
            st.row_spec(D_MODEL),
            st.mod_spec(0), st.mod_spec(1),
            _layer_spec((1, D_MODEL), layer),
            _layer_spec((D_MODEL, 2 * qk + 2 * vd), j),
            tbl_spec, tbl_spec,
        ],
        out_specs=[st.row_spec(qk), st.row_spec(qk), st.row_spec(vd), st.row_spec(vd)],
        out_shape=[jax.ShapeDtypeStruct((st.n_rows, qk), BF16),
                   jax.ShapeDtypeStruct((st.n_rows, qk), BF16),
                   jax.ShapeDtypeStruct((st.n_rows, vd), BF16),
                   jax.ShapeDtypeStruct((st.n_rows, vd), BF16)],
        compiler_params=_cparams("parallel"),
        name="pre_ret",
    )(x, st.mod, st.mod, norm_g.reshape(-1, 1, D_MODEL), w_in_b, cos, sin)


def _decay_matrix(lg, ch):
    diff = (lax.broadcasted_iota(I32, (ch, ch), 0)
            - lax.broadcasted_iota(I32, (ch, ch), 1)).astype(F32)
    return jnp.where(diff >= 0.0, jnp.exp(lg * jnp.maximum(diff, 0.0)), 0.0)


def _retention_chunk(valid, lg, dec, q, k, v, sg, s_prev):
    ch = q.shape[0]
    idx = lax.broadcasted_iota(I32, (ch, 1), 0).astype(F32)
    scores = lax.dot_general(q, k, (((1,), (1,)), ((), ())), preferred_element_type=F32) * dec
    o = jnp.dot(scores.astype(BF16), v, preferred_element_type=F32)
    cross = jnp.exp(lg * (idx + 1.0))
    o = o + jnp.dot(q, s_prev.astype(BF16), preferred_element_type=F32) * cross
    kd = (k.astype(F32) * jnp.exp(lg * (valid - 1.0 - idx))).astype(BF16)
    s_new = jnp.exp(lg * float(valid)) * s_prev + lax.dot_general(
        kd, v, (((0,), (0,)), ((), ())), preferred_element_type=F32)
    mu = jnp.mean(o, axis=-1, keepdims=True)
    oc = o - mu
    var = jnp.mean(oc * oc, axis=-1, keepdims=True)
    on = oc * lax.rsqrt(var + NORM_EPS)
    return (sg.astype(F32) * on).astype(BF16), s_new


def _retention_kernel(lg_ref, q_ref, k_ref, v_ref, sg_ref, qs_ref, ks_ref, vs_ref, sgs_ref, s0_ref,
                      o_ref, s_out_ref, os_ref, ss_out_ref, s_ref, dec_ref, decs_ref):
    i = pl.program_id(0)
    n_chunks = SEQ // RET_CHUNK
    c = i % n_chunks

    @pl.when(i == 0)
    def _():
        for h in range(RET_HEADS):
            dec_ref[h] = _decay_matrix(lg_ref[h, 0:1, 0:1], RET_CHUNK)
            decs_ref[h] = _decay_matrix(lg_ref[h, 0:1, 0:1], RET_PAD)

    @pl.when(c == 0)
    def _():
        s_ref[...] = jnp.zeros_like(s_ref)

    for h in range(RET_HEADS):
        ks = slice(h * RET_DK, (h + 1) * RET_DK)
        vs = slice(h * RET_DV, (h + 1) * RET_DV)
        lg = lg_ref[h, 0:1, 0:1]
        gated, s_new = _retention_chunk(RET_CHUNK, lg, dec_ref[h], q_ref[:, ks], k_ref[:, ks],
                                        v_ref[:, vs], sg_ref[:, vs], s_ref[h])
        o_ref[:, vs] = gated
        s_ref[h] = s_new
        for b in range(RET_SS):
            rows = slice(b * RET_PAD, (b + 1) * RET_PAD)
            gated, s_new = _retention_chunk(DEC_SEQ, lg, decs_ref[h], qs_ref[rows, ks],
                                            ks_ref[rows, ks], vs_ref[rows, vs], sgs_ref[rows, vs],
                                            s0_ref[b, h])
            os_ref[rows, vs] = gated
            ss_out_ref[b, h] = s_new

    @pl.when(c == n_chunks - 1)
    def _():
        s_out_ref[...] = s_ref[...]


def _retention(q, k, v, sg, qkvg_s, s0, s0_layer, lg_tbl):
    qk = RET_HEADS * RET_DK
    vd = RET_HEADS * RET_DV
    n_chunks = SEQ // RET_CHUNK
    rows_s = RET_SS * RET_PAD
    col = lambda j: (lambda i: (i, j))
    state_p = (None, None, RET_HEADS, RET_DK, RET_DV)
    state_s = (None, RET_SS, RET_HEADS, RET_DK, RET_DV)
    return pl.pallas_call(
        _retention_kernel,
        grid=(BATCH * n_chunks,),
        in_specs=[
            pl.BlockSpec((RET_HEADS, 8, 128), lambda i: (0, 0, 0)),
            pl.BlockSpec((RET_CHUNK, qk), col(0)), pl.BlockSpec((RET_CHUNK, qk), col(0)),
            pl.BlockSpec((RET_CHUNK, vd), col(0)), pl.BlockSpec((RET_CHUNK, vd), col(0)),
            pl.BlockSpec((rows_s, qk), col(0)), pl.BlockSpec((rows_s, qk), col(1)),
            pl.BlockSpec((rows_s, vd), col(1)), pl.BlockSpec((rows_s, vd), col(2)),
            pl.BlockSpec(state_s, lambda i: (s0_layer, i, 0, 0, 0)),
        ],
        out_specs=[pl.BlockSpec((RET_CHUNK, vd), col(0)),
                   pl.BlockSpec(state_p, lambda i: (0, i // n_chunks, 0, 0, 0)),
                   pl.BlockSpec((rows_s, vd), col(0)),
                   pl.BlockSpec(state_s, lambda i: (0, i, 0, 0, 0))],
        out_shape=[jax.ShapeDtypeStruct((N_PROMPT, vd), BF16),
                   jax.ShapeDtypeStruct((1, BATCH, RET_HEADS, RET_DK, RET_DV), F32),
                   jax.ShapeDtypeStruct((DEC_BATCH * RET_PAD, vd), BF16),
                   jax.ShapeDtypeStruct((1, DEC_BATCH, RET_HEADS, RET_DK, RET_DV), F32)],
        scratch_shapes=[pltpu.VMEM((RET_HEADS, RET_DK, RET_DV), F32),
                        pltpu.VMEM((RET_HEADS, RET_CHUNK, RET_CHUNK), F32),
                        pltpu.VMEM((RET_HEADS, RET_PAD, RET_PAD), F32)],
        compiler_params=_cparams("arbitrary"),
        name="retention",
    )(lg_tbl, q, k, v, sg, qkvg_s, qkvg_s, qkvg_s, qkvg_s, s0)


def _excl_prefix(col):
    r = lax.broadcasted_iota(I32, (N_EXPERTS, N_EXPERTS), 0)
    c = lax.broadcasted_iota(I32, (N_EXPERTS, N_EXPERTS), 1)
    lane = jnp.sum(jnp.where(r < c, col, 0.0), axis=0, keepdims=True)
    return jnp.sum(jnp.where(r == c, lane, 0.0), axis=1, keepdims=True)


def _ceil_to(x, m):
    return jnp.floor((x + (m - 1.0)) * (1.0 / m)) * m


def _route_kernel(te_ref, lpos_ref, rstart_ref, rlen_ref, tot_ref, tri_ref):
    rt = te_ref.shape[2]
    iota_e = lax.broadcasted_iota(I32, (N_EXPERTS, rt), 0)
    s = lax.broadcasted_iota(I32, (rt, rt), 0)
    t = lax.broadcasted_iota(I32, (rt, rt), 1)
    tri_ref[...] = (s < t).astype(BF16)

    def tile_onehots(i):
        te = te_ref[i]
        onehots = [(te[k:k + 1, :] == iota_e).astype(F32) for k in range(TOP_K)]
        cnt = jnp.sum(onehots[0] + onehots[1] + onehots[2] + onehots[3], axis=1, keepdims=True)
        return onehots, _ceil_to(cnt, RUN_ALIGN)

    tot = lax.fori_loop(0, MOE_NT, lambda i, acc: acc + tile_onehots(i)[1],
                        jnp.zeros((N_EXPERTS, 1), F32))
    tot_ref[...] = tot

    def place(i, gbase):
        onehots, run = tile_onehots(i)
        base = _excl_prefix(run)
        for k in range(TOP_K):
            oh = onehots[k]
            before = jnp.dot(oh.astype(BF16), tri_ref[...], preferred_element_type=F32)
            pos = jnp.sum(oh * (base + before), axis=0, keepdims=True)
            lpos_ref[i, k:k + 1, :] = pos.astype(I32)
            base = base + jnp.sum(oh, axis=1, keepdims=True)
        rstart_ref[i] = jnp.broadcast_to(gbase, rstart_ref.shape[1:]).astype(I32)
        rlen_ref[i] = jnp.broadcast_to(run, rlen_ref.shape[1:]).astype(I32)
        return gbase + run

    lax.fori_loop(0, MOE_NT, place, _excl_prefix(_ceil_to(tot, MOE_TB)))


def _route(te):
    full = lambda shape: pl.BlockSpec(shape, lambda i: (0,) * len(shape))
    return pl.pallas_call(
        _route_kernel,
        grid=(1,),
        in_specs=[full(te.shape)],
        out_specs=[full(te.shape), full((MOE_NT, N_EXPERTS, 128)), full((MOE_NT, N_EXPERTS, 128)),
                   full((N_EXPERTS, 1))],
        out_shape=[jax.ShapeDtypeStruct(te.shape, I32),
                   jax.ShapeDtypeStruct((MOE_NT, N_EXPERTS, 128), I32),
                   jax.ShapeDtypeStruct((MOE_NT, N_EXPERTS, 128), I32),
                   jax.ShapeDtypeStruct((N_EXPERTS, 1), F32)],
        scratch_shapes=[pltpu.VMEM((MOE_RT, MOE_RT), BF16)],
        compiler_params=_cparams("arbitrary"),
        name="moe_route",
    )(te)


def _dispatch_kernel(n_prompt_steps, rs_ref, rl_ref, tzs_ref, tzl_ref, nu_ref, hp_ref, hs_ref,
                     lpos_ref, xb_ref, xs_ref, sem, zero_ref):
    i = pl.program_id(0)
    zero_sem = sem.at[2]

    def tail_copy(e):
        n = pl.multiple_of(tzl_ref[e], RUN_ALIGN)
        dst = pl.multiple_of(tzs_ref[e], RUN_ALIGN)
        return pltpu.make_async_copy(zero_ref.at[pl.ds(0, n)], xb_ref.at[pl.ds(dst, n)], zero_sem)

    def block_copy(j):
        dst = pl.multiple_of(j * MOE_TB, MOE_TB)
        return pltpu.make_async_copy(zero_ref, xb_ref.at[pl.ds(dst, MOE_TB)], zero_sem)

    @pl.when(i == 0)
    def _():
        zero_ref[...] = jnp.zeros_like(zero_ref)

        def start(e, carry):
            pl.when(tzl_ref[e] > 0)(lambda: tail_copy(e).start())
            return carry

        def wait(e, carry):
            pl.when(tzl_ref[e] > 0)(lambda: tail_copy(e).wait())
            return carry

        lax.fori_loop(0, N_EXPERTS, start, 0)
        lax.fori_loop(nu_ref[0], MOE_NBLK, lambda j, c: (block_copy(j).start(), c)[1], 0)
        lax.fori_loop(0, N_EXPERTS, wait, 0)
        lax.fori_loop(nu_ref[0], MOE_NBLK, lambda j, c: (block_copy(j).wait(), c)[1], 0)

    def scatter(tile, wait):
        def body(e, lo):
            n = pl.multiple_of(rl_ref[tile * N_EXPERTS + e], RUN_ALIGN)
            src = pl.multiple_of(lo, RUN_ALIGN)
            dst = pl.multiple_of(rs_ref[tile * N_EXPERTS + e], RUN_ALIGN)
            cp = pltpu.make_async_copy(xs_ref.at[tile % (2 * MOE_TPS), pl.ds(src, n)],
                                       xb_ref.at[pl.ds(dst, n)], sem.at[(tile // MOE_TPS) % 2])

            @pl.when(n > 0)
            def _():
                if wait:
                    cp.wait()
                else:
                    cp.start()

            return lo + n

        lax.fori_loop(0, N_EXPERTS, body, 0)

    rows = lax.broadcasted_iota(I32, (MOE_CAP, MOE_RT), 0)
    for t in range(MOE_TPS):
        tile = i * MOE_TPS + t
        perm = jnp.where(rows == lpos_ref[t, 0:1, :], 1.0, 0.0)
        for k in range(1, TOP_K):
            perm = perm + jnp.where(rows == lpos_ref[t, k:k + 1, :], 1.0, 0.0)
        tr = slice(t * MOE_RT, (t + 1) * MOE_RT)
        h = jnp.where(i < n_prompt_steps, hp_ref[tr, :], hs_ref[tr, :])
        xs_ref[tile % (2 * MOE_TPS)] = _pack_halves(
            jnp.dot(perm.astype(BF16), h, preferred_element_type=F32))
        scatter(tile, False)

    @pl.when(i > 0)
    def _():
        for t in range(MOE_TPS):
            scatter((i - 1) * MOE_TPS + t, True)

    @pl.when(i == pl.num_programs(0) - 1)
    def _():
        for t in range(MOE_TPS):
            scatter(i * MOE_TPS + t, True)


def _dispatch(tables, h_p, h_s, lpos):
    rows = MOE_TPS * MOE_RT
    ns_p = h_p.shape[0] // rows
    return pl.pallas_call(
        functools.partial(_dispatch_kernel, ns_p),
        grid_spec=pltpu.PrefetchScalarGridSpec(
            num_scalar_prefetch=len(tables),
            grid=(MOE_NT // MOE_TPS,),
            in_specs=[
                pl.BlockSpec((rows, D_MODEL), lambda i, *_: (jnp.minimum(i, ns_p - 1), 0)),
                pl.BlockSpec((rows, D_MODEL), lambda i, *_: (jnp.maximum(i - ns_p, 0), 0)),
                pl.BlockSpec((MOE_TPS, TOP_K, MOE_RT), lambda i, *_: (i, 0, 0)),
            ],
            out_specs=pl.BlockSpec(memory_space=pl.ANY),
            scratch_shapes=[pltpu.VMEM((2 * MOE_TPS, MOE_CAP, D_MODEL // 2), U32),
                            pltpu.SemaphoreType.DMA((3,)),
                            pltpu.VMEM((MOE_TB, D_MODEL // 2), U32)],
        ),
        out_shape=jax.ShapeDtypeStruct((MOE_ROWS, D_MODEL // 2), U32),
        compiler_params=_cparams("arbitrary"),
        name="moe_dispatch",
    )(*tables, h_p, h_s, lpos)


def _ffn_kernel(layer, be_ref, nu_ref, nxt_ref, bgu_ref, bdn_ref, xb_hbm, wgu_hbm, wdn_hbm,
                yb_hbm, wgu_f_ref, wdn_f_ref, wgu_b_ref, wdn_b_ref, act_ref, xbuf_ref, ybuf_ref,
                wsem, xsem, ysem):
    n_used = nu_ref[0]

    def weight_copies(ex):
        return (pltpu.make_async_copy(wgu_hbm.at[layer, ex], wgu_f_ref, wsem.at[0]),
                pltpu.make_async_copy(wdn_hbm.at[layer, ex], wdn_f_ref, wsem.at[1]))

    def block_rows(j):
        return pl.ds(pl.multiple_of(j * MOE_TB, MOE_TB), MOE_TB)

    def x_copy(j):
        return pltpu.make_async_copy(xb_hbm.at[block_rows(j)], xbuf_ref.at[j % 2], xsem.at[j % 2])

    def y_copy(j):
        return pltpu.make_async_copy(ybuf_ref.at[j % 2], yb_hbm.at[block_rows(j)], ysem.at[j % 2])

    for cp in weight_copies(be_ref[0]):
        cp.start()
    x_copy(0).start()

    def block(j, carry):
        e = be_ref[j]
        e_prev = be_ref[jnp.maximum(j - 1, 0)]

        @pl.when(j + 1 < n_used)
        def _():
            x_copy(j + 1).start()

        @pl.when((j == 0) | (e != e_prev))
        def _():
            for cp in weight_copies(e):
                cp.wait()
            wgu_b_ref[...] = wgu_f_ref[...].astype(BF16)
            wdn_b_ref[...] = wdn_f_ref[...].astype(BF16)
            nx = nxt_ref[j]

            @pl.when(nx >= 0)
            def _():
                for cp in weight_copies(nx):
                    cp.start()

        x_copy(j).wait()

        @pl.when(j >= 2)
        def _():
            y_copy(j - 2).wait()

        x = _unpack_halves(xbuf_ref[j % 2])
        bgu = bgu_ref[e]
        for c in range(D_FF // FF_CHUNK):
            gs = slice(c * FF_CHUNK, (c + 1) * FF_CHUNK)
            us = slice(D_FF + c * FF_CHUNK, D_FF + (c + 1) * FF_CHUNK)
            gate = jnp.dot(x, wgu_b_ref[:, gs], preferred_element_type=F32) + bgu[:, gs]
            up = jnp.dot(x, wgu_b_ref[:, us], preferred_element_type=F32) + bgu[:, us]
            gate = jnp.minimum(gate, SWIGLU_LIMIT)
            up = jnp.clip(up, -SWIGLU_LIMIT, SWIGLU_LIMIT)
            act = (up + 1.0) * (gate * jax.nn.sigmoid(SWIGLU_ALPHA * gate))
            act_ref[:, gs] = act.astype(BF16)
        y = jnp.dot(act_ref[...], wdn_b_ref[...], preferred_element_type=F32) + bdn_ref[e]
        ybuf_ref[j % 2] = _pack_halves(y.astype(BF16).astype(F32))
        y_copy(j).start()
        return carry

    lax.fori_loop(0, n_used, block, 0)

    @pl.when(n_used >= 2)
    def _():
        y_copy(n_used - 2).wait()

    y_copy(n_used - 1).wait()

    ybuf_ref[0] = jnp.zeros_like(ybuf_ref[0])

    def zero_copy(j):
        return pltpu.make_async_copy(ybuf_ref.at[0], yb_hbm.at[block_rows(j)], ysem.at[0])

    lax.fori_loop(n_used, MOE_NBLK, lambda j, c: (zero_copy(j).start(), c)[1], 0)
    lax.fori_loop(n_used, MOE_NBLK, lambda j, c: (zero_copy(j).wait(), c)[1], 0)


def _ffn(layer, block_e, n_used, nxt, xb, w_gu, b_gu, w_dn, b_dn):
    bias_map = lambda i, *_: (layer, 0, 0, 0)
    return pl.pallas_call(
        functools.partial(_ffn_kernel, layer),
        grid_spec=pltpu.PrefetchScalarGridSpec(
            num_scalar_prefetch=3,
            grid=(1,),
            in_specs=[
                pl.BlockSpec((None, N_EXPERTS, 1, 2 * D_FF), bias_map),
                pl.BlockSpec((None, N_EXPERTS, 1, D_MODEL), bias_map),
                pl.BlockSpec(memory_space=pl.ANY),
                pl.BlockSpec(memory_space=pl.ANY),
                pl.BlockSpec(memory_space=pl.ANY),
            ],
            out_specs=pl.BlockSpec(memory_space=pl.ANY),
            scratch_shapes=[pltpu.VMEM((D_MODEL, 2 * D_FF), F32),
                            pltpu.VMEM((D_FF, D_MODEL), F32),
                            pltpu.VMEM((D_MODEL, 2 * D_FF), BF16),
                            pltpu.VMEM((D_FF, D_MODEL), BF16),
                            pltpu.VMEM((MOE_TB, D_FF), BF16),
                            pltpu.VMEM((2, MOE_TB, D_MODEL // 2), U32),
                            pltpu.VMEM((2, MOE_TB, D_MODEL // 2), U32),
                            pltpu.SemaphoreType.DMA((2,)),
                            pltpu.SemaphoreType.DMA((2,)),
                            pltpu.SemaphoreType.DMA((2,))],
        ),
        out_shape=jax.ShapeDtypeStruct((MOE_ROWS, D_MODEL // 2), U32),
        compiler_params=_cparams("arbitrary"),
        name="moe_ffn",
    )(block_e, n_used, nxt, b_gu.reshape(b_gu.shape[0], N_EXPERTS, 1, 2 * D_FF),
      b_dn.reshape(b_dn.shape[0], N_EXPERTS, 1, D_MODEL), xb, w_gu, w_dn)


def _tile_rows(v, n_rows):
    mr = v.shape[0]
    if mr in (1, n_rows):
        return v
    return jnp.concatenate([v] * (n_rows // mr), axis=0)


def _combine_kernel(tile_off, final, rs_ref, rl_ref, x_ref, gf_ref, lpt_ref, wt_ref, fg_ref,
                    yb_ref, out_ref, ys_ref, sem):
    i = pl.program_id(0)

    def gather(tile, wait):
        def body(e, lo):
            n = pl.multiple_of(rl_ref[(tile_off + tile) * N_EXPERTS + e], RUN_ALIGN)
            src = pl.multiple_of(rs_ref[(tile_off + tile) * N_EXPERTS + e], RUN_ALIGN)
            dst = pl.multiple_of(lo, RUN_ALIGN)
            cp = pltpu.make_async_copy(yb_ref.at[pl.ds(src, n)],
                                       ys_ref.at[tile % (2 * MOE_TPS), pl.ds(dst, n)],
                                       sem.at[(tile // MOE_TPS) % 2])

            @pl.when(n > 0)
            def _():
                if wait:
                    cp.wait()
                else:
                    cp.start()

            return lo + n

        lax.fori_loop(0, N_EXPERTS, body, 0)

    def gather_step(step, wait):
        for t in range(MOE_TPS):
            gather(step * MOE_TPS + t, wait)

    @pl.when(i == 0)
    def _():
        ys_ref[...] = jnp.zeros_like(ys_ref)
        gather_step(0, False)

    @pl.when(i + 1 < pl.num_programs(0))
    def _():
        gather_step(i + 1, False)

    gather_step(i, True)

    cols = lax.broadcasted_iota(I32, (MOE_RT, MOE_CAP), 1)
    gf = _tile_rows(gf_ref[...], MOE_RT)
    for t in range(MOE_TPS):
        tr = slice(t * MOE_RT, (t + 1) * MOE_RT)
        w = jnp.where(cols == lpt_ref[tr, 0:1], wt_ref[tr, 0:1], 0.0)
        for k in range(1, TOP_K):
            w = w + jnp.where(cols == lpt_ref[tr, k:k + 1], wt_ref[tr, k:k + 1], 0.0)
        ys = _unpack_halves(ys_ref[(i * MOE_TPS + t) % (2 * MOE_TPS)])
        y = jnp.dot(w.astype(BF16), ys, preferred_element_type=F32)
        xn = x_ref[tr, :] + gf * y
        if final:
            xn = xn * lax.rsqrt(jnp.mean(xn * xn, axis=-1, keepdims=True) + NORM_EPS) * fg_ref[...]
        out_ref[tr, :] = xn


def _combine(tile_off, n_tiles, final, tables, x, mod, mod_map, lpos_t, wt, final_g, yb):
    rows = MOE_TPS * MOE_RT
    step_off = tile_off // MOE_TPS
    pm = lambda i, *_: (i, 0)
    tm = lambda i, *_: (i + step_off, 0)
    return pl.pallas_call(
        functools.partial(_combine_kernel, tile_off, final),
        grid_spec=pltpu.PrefetchScalarGridSpec(
            num_scalar_prefetch=len(tables),
            grid=(n_tiles // MOE_TPS,),
            in_specs=[
                pl.BlockSpec((rows, D_MODEL), pm),
                pl.BlockSpec((None, mod.shape[1], D_MODEL), mod_map),
                pl.BlockSpec((rows, TOP_K), tm),
                pl.BlockSpec((rows, TOP_K), tm),
                pl.BlockSpec((1, D_MODEL), lambda i, *_: (0, 0)),
                pl.BlockSpec(memory_space=pl.ANY),
            ],
            out_specs=pl.BlockSpec((rows, D_MODEL), pm),
            scratch_shapes=[pltpu.VMEM((2 * MOE_TPS, MOE_CAP, D_MODEL // 2), U32),
                            pltpu.SemaphoreType.DMA((2,))],
        ),
        out_shape=jax.ShapeDtypeStruct(x.shape, F32),
        compiler_params=_cparams("arbitrary"),
        name="moe_combine",
    )(*tables, x, mod, lpos_t, wt, final_g.reshape(1, D_MODEL), yb)


def _moe(layer, final, st_p, st_s, x_p, x_s, post_p, post_s,
         w_gu, b_gu, w_dn, b_dn, final_g):
    te = jnp.concatenate([post_p[1], post_s[1]], axis=1)
    tw = jnp.concatenate([post_p[2], post_s[2]], axis=1)
    te = jnp.swapaxes(te.reshape(TOP_K, MOE_NT, MOE_RT), 0, 1)
    lpos, rstart, rlen, tot = _route(te)
    tot = tot[:, 0].astype(I32)
    padded = (tot + MOE_TB - 1) // MOE_TB * MOE_TB
    pends = jnp.cumsum(padded)
    n_used = (pends[-1] // MOE_TB).reshape(1)
    blk = jnp.minimum(jnp.arange(MOE_NBLK, dtype=I32), n_used[0] - 1) * MOE_TB
    block_e = jnp.minimum(jnp.sum(pends[None, :] <= blk[:, None], axis=-1),
                          N_EXPERTS - 1).astype(I32)
    ids = jnp.where(tot > 0, jnp.arange(N_EXPERTS, dtype=I32), N_EXPERTS)
    later = jnp.concatenate([lax.cummin(ids[::-1])[::-1][1:], jnp.full((1,), N_EXPERTS, I32)])
    nxt_of = jnp.where(later < N_EXPERTS, later, -1)
    of_block = block_e[:, None] == jnp.arange(N_EXPERTS, dtype=I32)[None, :]
    nxt = jnp.sum(jnp.where(of_block, nxt_of[None, :], 0), axis=-1)
    run_tables = (rstart[:, :, 0].reshape(-1), rlen[:, :, 0].reshape(-1))
    tail_tables = (pends - padded + tot, padded - tot)

    nt_p = N_PROMPT // MOE_RT
    nt_s = N_SAMPLE // MOE_RT
    xb = _dispatch(run_tables + tail_tables + (n_used,), post_p[0], post_s[0], lpos)
    yb = _ffn(layer, block_e, n_used, nxt, xb, w_gu, b_gu, w_dn, b_dn)
    lpos_t = jnp.swapaxes(lpos, 1, 2).reshape(N_TOK, TOP_K)
    wt = tw.T
    steps_per_seq = SEQ // (MOE_TPS * MOE_RT)
    out_p = _combine(0, nt_p, final, run_tables, x_p, st_p.mod,
                     lambda i, *_: (i // steps_per_seq, 0, N_MOD - 1), lpos_t, wt, final_g, yb)
    out_s = _combine(nt_p, nt_s, final, run_tables, x_s, st_s.mod,
                     lambda i, *_: (0, 0, N_MOD - 1), lpos_t, wt, final_g, yb)
    return out_p, out_s


def _rope_tables(pos):
    half = RET_DK // 2
    inv_freq = ROPE_BASE ** (-jnp.arange(half, dtype=F32) / half)
    ang = pos.astype(F32)[:, None] * inv_freq[None, :]
    return jnp.cos(ang), jnp.sin(ang)


def kernel(x_prompt, x_sample, c_prompt, c_sample, state_conv, state_ret, norm_mix_g, norm_ff_g,
           w_mod, b_mod, conv_w1, conv_b1, conv_dw, conv_dw_b, conv_ln_g, conv_ln_b, conv_w2,
           conv_b2, ret_w_in, ret_w_o, router_w, router_b, moe_w_gu, moe_b_gu, moe_w_dn,
           moe_b_dn, final_g):
    mod = _adaln(jnp.concatenate([c_prompt, c_sample], axis=0), w_mod, b_mod)
    x_p = x_prompt.reshape(N_PROMPT, D_MODEL)
    x_s = jnp.swapaxes(x_sample, 0, 1).reshape(N_SAMPLE, D_MODEL)
    router_wt = jnp.swapaxes(router_w, 1, 2)

    def streams(layer):
        st_p = _Stream(N_PROMPT, TM_PROMPT, SEQ // TM_PROMPT,
                       mod[layer, :BATCH].reshape(BATCH, 1, N_MOD * D_MODEL))
        st_s = _Stream(N_SAMPLE, TM_SAMPLE, None,
                       mod[layer, BATCH:].reshape(1, DEC_BATCH, N_MOD * D_MODEL))
        return st_p, st_s

    st_p, st_s = streams(0)
    u_p = _pre_conv(st_p, x_p, norm_mix_g, conv_w1, conv_b1, 0, 0)
    u_s = _pre_conv(st_s, x_s, norm_mix_g, conv_w1, conv_b1, 0, 0)
    buf_tm = jnp.swapaxes(state_conv[0], 0, 1)
    z_s, ns_tm = _conv_sample(buf_tm, u_s.reshape(DEC_SEQ, DEC_BATCH, D_MODEL),
                              conv_dw, conv_dw_b, 0)
    post_p = _mixer_post("conv", st_p, 0, 0, x_p, u_p, norm_ff_g, router_wt, router_b,
                         conv_w2, conv_b2, conv_dw, conv_dw_b, conv_ln_g, conv_ln_b)
    post_s = _mixer_post("z", st_s, 0, 0, x_s, z_s.reshape(N_SAMPLE, D_MODEL), norm_ff_g,
                         router_wt, router_b, conv_w2, conv_b2, None, None, conv_ln_g, conv_ln_b)
    x_p, x_s = _moe(0, False, st_p, st_s, post_p[0], post_s[0], post_p[1:], post_s[1:],
                    moe_w_gu, moe_b_gu, moe_w_dn, moe_b_dn, final_g)
    conv_p = u_p.reshape(BATCH, SEQ, D_MODEL)[:, SEQ - CONV_STATE:][None]
    conv_s = jnp.swapaxes(ns_tm, 0, 1)[None]

    st_p, st_s = streams(1)
    w_in_b = ret_w_in.astype(BF16)
    cos_p, sin_p = _rope_tables(jnp.arange(SEQ, dtype=I32))
    cos_s, sin_s = _rope_tables(PAST_LEN + jnp.arange(DEC_SEQ, dtype=I32))
    half = RET_DK // 2
    tps = SEQ // TM_PROMPT
    tbl_p = pl.BlockSpec((TM_PROMPT, half), lambda i: (i % tps, 0))
    tbl_s = pl.BlockSpec((None, 1, half), lambda i: (i, 0, 0))
    q_p, k_p, v_p, sg_p = _pre_ret(st_p, x_p, norm_mix_g, w_in_b, 1, 0, cos_p, sin_p, tbl_p)
    q_s, k_s, v_s, sg_s = _pre_ret(st_s, x_s, norm_mix_g, w_in_b, 1, 0,
                                   cos_s.reshape(DEC_SEQ, 1, half),
                                   sin_s.reshape(DEC_SEQ, 1, half), tbl_s)
    lg = jnp.log1p(-jnp.exp2(-5.0 - jnp.arange(RET_HEADS, dtype=F32)))
    lg_tbl = jnp.broadcast_to(lg[:, None, None], (RET_HEADS, 8, 128))
    def to_seq_major(a):
        a = jnp.swapaxes(a.reshape(DEC_SEQ, DEC_BATCH, -1), 0, 1)
        a = jnp.pad(a, ((0, 0), (0, RET_PAD - DEC_SEQ), (0, 0)))
        return a.reshape(DEC_BATCH * RET_PAD, -1)

    qkvg_s = to_seq_major(jnp.concatenate([q_s, k_s, v_s, sg_s], axis=1))
    gated_p, ret_p, gated_s, ret_s = _retention(q_p, k_p, v_p, sg_p, qkvg_s, state_ret, 0, lg_tbl)
    gated_s = jnp.swapaxes(gated_s.reshape(DEC_BATCH, RET_PAD, -1)[:, :DEC_SEQ], 0, 1)
    gated_s = gated_s.reshape(N_SAMPLE, -1)
    post_p = _mixer_post("ret", st_p, 1, 0, x_p, gated_p, norm_ff_g, router_wt, router_b, ret_w_o)
    post_s = _mixer_post("ret", st_s, 1, 0, x_s, gated_s, norm_ff_g, router_wt, router_b, ret_w_o)
    y_p, y_s = _moe(1, True, st_p, st_s, post_p[0], post_s[0], post_p[1:], post_s[1:],
                    moe_w_gu, moe_b_gu, moe_w_dn, moe_b_dn, final_g)

    y_prompt = y_p.reshape(BATCH, SEQ, D_MODEL)
    y_sample = jnp.swapaxes(y_s.reshape(DEC_SEQ, DEC_BATCH, D_MODEL), 0, 1)
    return (y_prompt, y_sample, conv_p, conv_s, ret_p, ret_s)
```

```python
import functools

import jax
import jax.numpy as jnp
from jax import lax
from jax.experimental import pallas as pl
from jax.experimental.pallas import tpu as pltpu

F32 = jnp.float32
BF16 = jnp.bfloat16
I32 = jnp.int32
U32 = jnp.uint32

D_MODEL = 1024
SEQ = 2048
BATCH = 8
DEC_BATCH = 128
DEC_SEQ = 4
PAST_LEN = 16384
CONV_WIDTH = 31
CONV_STATE = CONV_WIDTH - 1
RET_HEADS = 4
RET_DK = 256
RET_DV = 512
ROPE_BASE = 10000.0
N_EXPERTS = 32
TOP_K = 4
D_FF = 1024
SWIGLU_LIMIT = 7.0
SWIGLU_ALPHA = 1.702
N_MOD = 6
NORM_EPS = 1e-5

N_PROMPT = BATCH * SEQ
N_SAMPLE = DEC_BATCH * DEC_SEQ
N_TOK = N_PROMPT + N_SAMPLE
N_ASSIGN = N_TOK * TOP_K

TM_PROMPT = 512
TM_SAMPLE = DEC_BATCH
HALO = 32
CONV_RB = 64
CONV_LW = 128
RET_CHUNK = 256
RET_PAD = 16
RET_SS = DEC_BATCH // (BATCH * (SEQ // RET_CHUNK))
MOE_TB = 512
MOE_RT = 256
MOE_NT = N_TOK // MOE_RT
RUN_ALIGN = 8
MOE_CAP = -(-(TOP_K * MOE_RT + N_EXPERTS * (RUN_ALIGN - 1)) // 256) * 256
MOE_MAX_ROWS = N_ASSIGN + MOE_NT * N_EXPERTS * (RUN_ALIGN - 1)
MOE_NBLK = -(-MOE_MAX_ROWS // MOE_TB) + N_EXPERTS
MOE_ROWS = MOE_NBLK * MOE_TB
FF_CHUNK = 256
VMEM_LIMIT = 56 * 1024 * 1024


def _cparams(*sem):
    return pltpu.CompilerParams(dimension_semantics=sem, vmem_limit_bytes=VMEM_LIMIT)


def _silu(x):
    return x * jax.nn.sigmoid(x)


def _rms_mod(x, g, sc, sh):
    y = x * lax.rsqrt(jnp.mean(x * x, axis=-1, keepdims=True) + NORM_EPS) * g
    return y * (1.0 + sc) + sh


def _split_bf16(x):
    hi = x.astype(BF16)
    lo = (x - hi.astype(F32)).astype(BF16)
    return hi, lo


def _pack_halves(x):
    c = x.shape[1] // 2
    lo = lax.bitcast_convert_type(x[:, :c], U32)
    hi = lax.bitcast_convert_type(x[:, c:], U32)
    return (lo >> 16) | (hi & jnp.uint32(0xFFFF0000))


def _unpack_halves(p):
    lo = lax.bitcast_convert_type(p << 16, F32)
    hi = lax.bitcast_convert_type(p & jnp.uint32(0xFFFF0000), F32)
    return jnp.concatenate([lo.astype(BF16), hi.astype(BF16)], axis=1)


def _adaln_kernel(c_ref, w_ref, b_ref, o_ref):
    a = _silu(c_ref[...]).astype(BF16)
    o_ref[...] = jnp.dot(a, w_ref[...].astype(BF16), preferred_element_type=F32) + b_ref[...]


def _adaln(c_all, w_mod, b_mod):
    depth, d, nm = w_mod.shape
    n = c_all.shape[0]
    return pl.pallas_call(
        _adaln_kernel,
        grid=(depth, nm // d),
        in_specs=[
            pl.BlockSpec((n, d), lambda l, j: (0, 0)),
            pl.BlockSpec((None, d, d), lambda l, j: (l, 0, j)),
            pl.BlockSpec((None, 1, d), lambda l, j: (l, 0, j)),
        ],
        out_specs=pl.BlockSpec((None, n, d), lambda l, j: (l, 0, j)),
        out_shape=jax.ShapeDtypeStruct((depth, n, nm), F32),
        compiler_params=_cparams("parallel", "parallel"),
        name="adaln",
    )(c_all, w_mod, b_mod.reshape(depth, 1, nm))


class _Stream:
    def __init__(self, n_rows, tm, tiles_per_seq, mod):
        self.n_rows = n_rows
        self.tm = tm
        self.n_tiles = n_rows // tm
        self.tiles_per_seq = tiles_per_seq
        self.mod = mod

    def mod_spec(self, j):
        mr = self.mod.shape[1]
        if self.tiles_per_seq is None:
            return pl.BlockSpec((None, mr, D_MODEL), lambda i: (0, 0, j))
        tps = self.tiles_per_seq
        return pl.BlockSpec((None, mr, D_MODEL), lambda i: (i // tps, 0, j))

    def row_spec(self, width, col=0):
        return pl.BlockSpec((self.tm, width), lambda i: (i, col))


def _const_spec(shape):
    nd = len(shape)
    return pl.BlockSpec(shape, lambda i: (0,) * nd)


def _layer_spec(shape, layer):
    nd = len(shape)
    return pl.BlockSpec((None,) + shape, lambda i: (layer,) + (0,) * nd)


def _pre_conv_kernel(x_ref, sh_ref, sc_ref, g_ref, w1_ref, b1_ref, u_ref, w1b_ref):
    @pl.when(pl.program_id(0) == 0)
    def _():
        w1b_ref[...] = w1_ref[...].astype(BF16)

    h = _rms_mod(x_ref[...], g_ref[...], sc_ref[...], sh_ref[...])
    ag = jnp.dot(h.astype(BF16), w1b_ref[...], preferred_element_type=F32) + b1_ref[...]
    u_ref[...] = ag[:, :D_MODEL] * jax.nn.sigmoid(ag[:, D_MODEL:])


def _pre_conv(st, x, norm_g, w1, b1, layer, j):
    c2 = w1.shape[-1]
    return pl.pallas_call(
        _pre_conv_kernel,
        grid=(st.n_tiles,),
        in_specs=[
            st.row_spec(D_MODEL),
            st.mod_spec(0), st.mod_spec(1),
            _layer_spec((1, D_MODEL), layer),
            _layer_spec((D_MODEL, c2), j),
            _layer_spec((1, c2), j),
        ],
        out_specs=st.row_spec(D_MODEL),
        out_shape=jax.ShapeDtypeStruct((st.n_rows, D_MODEL), F32),
        scratch_shapes=[pltpu.VMEM((D_MODEL, c2), BF16)],
        compiler_params=_cparams("arbitrary"),
        name="pre_conv",
    )(x, st.mod, st.mod, norm_g.reshape(-1, 1, D_MODEL), w1, b1.reshape(-1, 1, c2))


def _conv_sample_kernel(buf_ref, u_ref, dw_ref, dwb_ref, z_ref, ns_ref):
    for t in range(DEC_SEQ):
        acc = jnp.broadcast_to(dwb_ref[...], u_ref.shape[1:])
        for j in range(t, CONV_STATE):
            acc = acc + buf_ref[j] * dw_ref[j - t:j - t + 1, :]
        for s in range(t + 1):
            k = CONV_STATE + s - t
            acc = acc + u_ref[s] * dw_ref[k:k + 1, :]
        z_ref[t] = acc
    ns_ref[0:CONV_STATE - DEC_SEQ] = buf_ref[DEC_SEQ:CONV_STATE]
    ns_ref[CONV_STATE - DEC_SEQ:CONV_STATE] = u_ref[...]


def _conv_sample(buf_tm, u_tm, dw, dwb, j):
    sb = 32
    c = u_tm.shape[-1]
    return pl.pallas_call(
        _conv_sample_kernel,
        grid=(DEC_BATCH // sb,),
        in_specs=[
            pl.BlockSpec((CONV_STATE, sb, c), lambda i: (0, i, 0)),
            pl.BlockSpec((DEC_SEQ, sb, c), lambda i: (0, i, 0)),
            _layer_spec((CONV_WIDTH, c), j),
            _layer_spec((1, c), j),
        ],
        out_specs=[
            pl.BlockSpec((DEC_SEQ, sb, c), lambda i: (0, i, 0)),
            pl.BlockSpec((CONV_STATE, sb, c), lambda i: (0, i, 0)),
        ],
        out_shape=[
            jax.ShapeDtypeStruct((DEC_SEQ, DEC_BATCH, c), F32),
            jax.ShapeDtypeStruct((CONV_STATE, DEC_BATCH, c), F32),
        ],
        compiler_params=_cparams("parallel"),
        name="conv_sample",
    )(buf_tm, u_tm, dw, dwb.reshape(-1, 1, c))


def _route_topk(h2, rwt_ref, rb_ref, te_ref, tw_ref):
    h_hi, h_lo = _split_bf16(h2)
    w_hi, w_lo = _split_bf16(rwt_ref[...])
    nt = (((1,), (1,)), ((), ()))
    logits = (lax.dot_general(w_hi, h_hi, nt, preferred_element_type=F32)
              + lax.dot_general(w_hi, h_lo, nt, preferred_element_type=F32)
              + lax.dot_general(w_lo, h_hi, nt, preferred_element_type=F32)
              + rb_ref[...])
    iota_e = lax.broadcasted_iota(I32, logits.shape, 0).astype(F32)
    vals, idxs = [], []
    for _ in range(TOP_K):
        m = jnp.max(logits, axis=0, keepdims=True)
        idx = jnp.min(jnp.where(logits == m, iota_e, float(N_EXPERTS)), axis=0, keepdims=True)
        logits = jnp.where(iota_e == idx, -jnp.inf, logits)
        vals.append(m)
        idxs.append(idx)
    es = [jnp.exp(v - vals[0]) for v in vals]
    tot = es[0] + es[1] + es[2] + es[3]
    for k in range(TOP_K):
        te_ref[k:k + 1, :] = idxs[k].astype(I32)
        tw_ref[k:k + 1, :] = es[k] / tot


def _mixer_post_kernel(mode, tiles_per_seq, *refs):
    it = iter(refs)
    if mode == "conv":
        u_ref, halo_ref, dw_ref, dwb_ref = next(it), next(it), next(it), next(it)
    elif mode == "z":
        z_in_ref = next(it)
    else:
        gated_ref = next(it)
    if mode in ("conv", "z"):
        lng_ref, lnb_ref = next(it), next(it)
    w_ref, b_ref = next(it), next(it)
    x_ref, gm_ref, shf_ref, scf_ref, gff_ref, rwt_ref, rb_ref = (next(it) for _ in range(7))
    x1_ref, h2_ref, te_ref, tw_ref = (next(it) for _ in range(4))
    wb_ref = next(it)
    if mode == "conv":
        ext_ref, z_ref = next(it), next(it)

    i = pl.program_id(0)

    @pl.when(i == 0)
    def _():
        wb_ref[...] = w_ref[...].astype(BF16)

    if mode == "conv":
        tm = u_ref.shape[0]
        first = (i % tiles_per_seq) == 0
        ext_ref[0:HALO, :] = jnp.where(first, 0.0, halo_ref[...])
        ext_ref[HALO:HALO + tm, :] = u_ref[...]
        off = HALO - CONV_STATE

        def row_block(r, carry):
            r0 = pl.multiple_of(r * CONV_RB, CONV_RB)
            for lc in range(D_MODEL // CONV_LW):
                ls = slice(lc * CONV_LW, (lc + 1) * CONV_LW)
                win = ext_ref[pl.ds(r0, CONV_RB + HALO), ls]
                acc = jnp.broadcast_to(dwb_ref[:, ls], (CONV_RB, CONV_LW))
                for s in range(8):
                    ws = win if s == 0 else pltpu.roll(win, CONV_RB + HALO - s, 0)
                    for k in range(CONV_WIDTH):
                        if (off + k) % 8 == s:
                            a = off + k - s
                            acc = acc + ws[a:a + CONV_RB, :] * dw_ref[k:k + 1, ls]
                z_ref[pl.ds(r0, CONV_RB), ls] = acc
            return carry

        lax.fori_loop(0, tm // CONV_RB, row_block, 0)
        z = z_ref[...]
    elif mode == "z":
        z = z_in_ref[...]

    if mode in ("conv", "z"):
        mu = jnp.mean(z, axis=-1, keepdims=True)
        zc = z - mu
        var = jnp.mean(zc * zc, axis=-1, keepdims=True)
        zn = zc * lax.rsqrt(var + NORM_EPS) * lng_ref[...] + lnb_ref[...]
        a = _silu(zn).astype(BF16)
        y = jnp.dot(a, wb_ref[...], preferred_element_type=F32) + b_ref[...]
    else:
        y = jnp.dot(gated_ref[...], wb_ref[...], preferred_element_type=F32)

    x1 = x_ref[...] + gm_ref[...] * y
    x1_ref[...] = x1
    h2 = _rms_mod(x1, gff_ref[...], scf_ref[...], shf_ref[...])
    h2_ref[...] = h2.astype(BF16)
    _route_topk(h2, rwt_ref, rb_ref, te_ref, tw_ref)


def _mixer_post(mode, st, layer, j, x, mix_in, norm_ff_g, router_wt, router_b, w, b=None,
                dw=None, dwb=None, ln_g=None, ln_b=None):
    tm = st.tm
    k_in = w.shape[-2]
    args, specs = [], []
    if mode == "conv":
        per32 = tm // HALO
        args += [mix_in, mix_in, dw, dwb.reshape(-1, 1, D_MODEL)]
        specs += [st.row_spec(D_MODEL),
                  pl.BlockSpec((HALO, D_MODEL), lambda i: (jnp.maximum(i * per32 - 1, 0), 0)),
                  _layer_spec((CONV_WIDTH, D_MODEL), j), _layer_spec((1, D_MODEL), j)]
    elif mode == "z":
        args += [mix_in]
        specs += [st.row_spec(D_MODEL)]
    else:
        args += [mix_in]
        specs += [st.row_spec(k_in)]
    if mode in ("conv", "z"):
        args += [ln_g.reshape(-1, 1, D_MODEL), ln_b.reshape(-1, 1, D_MODEL)]
        specs += [_layer_spec((1, D_MODEL), j), _layer_spec((1, D_MODEL), j)]
    if b is None:
        b = jnp.zeros((w.shape[0], D_MODEL), F32)
    args += [w, b.reshape(-1, 1, D_MODEL)]
    specs += [_layer_spec((k_in, D_MODEL), j), _layer_spec((1, D_MODEL), j)]
    args += [x, st.mod, st.mod, st.mod, norm_ff_g.reshape(-1, 1, D_MODEL), router_wt,
             router_b.reshape(-1, N_EXPERTS, 1)]
    specs += [st.row_spec(D_MODEL), st.mod_spec(2), st.mod_spec(3), st.mod_spec(4),
              _layer_spec((1, D_MODEL), layer), _layer_spec((N_EXPERTS, D_MODEL), layer),
              _layer_spec((N_EXPERTS, 1), layer)]
    scratch = [pltpu.VMEM((k_in, D_MODEL), BF16)]
    if mode == "conv":
        scratch += [pltpu.VMEM((tm + HALO, D_MODEL), F32), pltpu.VMEM((tm, D_MODEL), F32)]
    return pl.pallas_call(
        functools.partial(_mixer_post_kernel, mode, st.tiles_per_seq),
        grid=(st.n_tiles,),
        in_specs=specs,
        out_specs=[st.row_spec(D_MODEL), st.row_spec(D_MODEL),
                   pl.BlockSpec((TOP_K, tm), lambda i: (0, i)),
                   pl.BlockSpec((TOP_K, tm), lambda i: (0, i))],
        out_shape=[jax.ShapeDtypeStruct((st.n_rows, D_MODEL), F32),
                   jax.ShapeDtypeStruct((st.n_rows, D_MODEL), BF16),
                   jax.ShapeDtypeStruct((TOP_K, st.n_rows), I32),
                   jax.ShapeDtypeStruct((TOP_K, st.n_rows), F32)],
        scratch_shapes=scratch,
        compiler_params=_cparams("arbitrary"),
        name="mixer_post_" + mode,
    )(*args)


def _pre_ret_kernel(x_ref, sh_ref, sc_ref, g_ref, w_ref, cos_ref, sin_ref,
                    q_ref, k_ref, v_ref, sg_ref):
    h = _rms_mod(x_ref[...], g_ref[...], sc_ref[...], sh_ref[...]).astype(BF16)
    cos = cos_ref[...]
    sin = sin_ref[...]
    half = RET_DK // 2
    qk = RET_HEADS * RET_DK
    vd = RET_HEADS * RET_DV
    for out_ref, base, scale in ((q_ref, 0, 1.0), (k_ref, qk, RET_DK ** -0.5)):
        for hh in range(RET_HEADS):
            c0 = base + hh * RET_DK
            p = jnp.dot(h, w_ref[:, c0:c0 + RET_DK], preferred_element_type=F32)
            p1, p2 = p[:, :half], p[:, half:]
            o0 = hh * RET_DK
            out_ref[:, o0:o0 + half] = ((p1 * cos - p2 * sin) * scale).astype(BF16)
            out_ref[:, o0 + half:o0 + RET_DK] = ((p1 * sin + p2 * cos) * scale).astype(BF16)
    for hh in range(RET_HEADS):
        c0 = 2 * qk + hh * RET_DV
        v_ref[:, hh * RET_DV:(hh + 1) * RET_DV] = jnp.dot(
            h, w_ref[:, c0:c0 + RET_DV], preferred_element_type=F32).astype(BF16)
        g = jnp.dot(h, w_ref[:, vd + c0:vd + c0 + RET_DV], preferred_element_type=F32)
        sg_ref[:, hh * RET_DV:(hh + 1) * RET_DV] = _silu(g).astype(BF16)


def _pre_ret(st, x, norm_g, w_in_b, layer, j, cos, sin, tbl_spec):
    qk = RET_HEADS * RET_DK
    vd = RET_HEADS * RET_DV
    return pl.pallas_call(
        _pre_ret_kernel,
        grid=(st.n_tiles,),
        in_specs=[
            st.row_spec(D_MODEL),
            st.mod_spec(0), st.mod_spec(1),
            _layer_spec((1, D_MODEL), layer),
            _layer_spec((D_MODEL, 2 * qk + 2 * vd), j),
            tbl_spec, tbl_spec,
        ],
        out_specs=[st.row_spec(qk), st.row_spec(qk), st.row_spec(vd), st.row_spec(vd)],
        out_shape=[jax.ShapeDtypeStruct((st.n_rows, qk), BF16),
                   jax.ShapeDtypeStruct((st.n_rows, qk), BF16),
                   jax.ShapeDtypeStruct((st.n_rows, vd), BF16),
                   jax.ShapeDtypeStruct((st.n_rows, vd), BF16)],
        compiler_params=_cparams("parallel"),
        name="pre_ret",
    )(x, st.mod, st.mod, norm_g.reshape(-1, 1, D_MODEL), w_in_b, cos, sin)


def _decay_matrix(lg, ch):
    diff = (lax.broadcasted_iota(I32, (ch, ch), 0)
            - lax.broadcasted_iota(I32, (ch, ch), 1)).astype(F32)
    return jnp.where(diff >= 0.0, jnp.exp(lg * jnp.maximum(diff, 0.0)), 0.0)


def _retention_chunk(valid, lg, dec, q, k, v, sg, s_prev):
    ch = q.shape[0]
    idx = lax.broadcasted_iota(I32, (ch, 1), 0).astype(F32)
    scores = lax.dot_general(q, k, (((1,), (1,)), ((), ())), preferred_element_type=F32) * dec
    o = jnp.dot(scores.astype(BF16), v, preferred_element_type=F32)
    cross = jnp.exp(lg * (idx + 1.0))
    o = o + jnp.dot(q, s_prev.astype(BF16), preferred_element_type=F32) * cross
    kd = (k.astype(F32) * jnp.exp(lg * (valid - 1.0 - idx))).astype(BF16)
    s_new = jnp.exp(lg * float(valid)) * s_prev + lax.dot_general(
        kd, v, (((0,), (0,)), ((), ())), preferred_element_type=F32)
    mu = jnp.mean(o, axis=-1, keepdims=True)
    oc = o - mu
    var = jnp.mean(oc * oc, axis=-1, keepdims=True)
    on = oc * lax.rsqrt(var + NORM_EPS)
    return (sg.astype(F32) * on).astype(BF16), s_new


def _retention_kernel(lg_ref, q_ref, k_ref, v_ref, sg_ref, qs_ref, ks_ref, vs_ref, sgs_ref, s0_ref,
                      o_ref, s_out_ref, os_ref, ss_out_ref, s_ref, dec_ref, decs_ref):
    i = pl.program_id(0)
    n_chunks = SEQ // RET_CHUNK
    c = i % n_chunks

    @pl.when(i == 0)
    def _():
        for h in range(RET_HEADS):
            dec_ref[h] = _decay_matrix(lg_ref[h, 0:1, 0:1], RET_CHUNK)
            decs_ref[h] = _decay_matrix(lg_ref[h, 0:1, 0:1], RET_PAD)

    @pl.when(c == 0)
    def _():
        s_ref[...] = jnp.zeros_like(s_ref)

    for h in range(RET_HEADS):
        ks = slice(h * RET_DK, (h + 1) * RET_DK)
        vs = slice(h * RET_DV, (h + 1) * RET_DV)
        lg = lg_ref[h, 0:1, 0:1]
        gated, s_new = _retention_chunk(RET_CHUNK, lg, dec_ref[h], q_ref[:, ks], k_ref[:, ks],
                                        v_ref[:, vs], sg_ref[:, vs], s_ref[h])
        o_ref[:, vs] = gated
        s_ref[h] = s_new
        for b in range(RET_SS):
            rows = slice(b * RET_PAD, (b + 1) * RET_PAD)
            gated, s_new = _retention_chunk(DEC_SEQ, lg, decs_ref[h], qs_ref[rows, ks],
                                            ks_ref[rows, ks], vs_ref[rows, vs], sgs_ref[rows, vs],
                                            s0_ref[b, h])
            os_ref[rows, vs] = gated
            ss_out_ref[b, h] = s_new

    @pl.when(c == n_chunks - 1)
    def _():
        s_out_ref[...] = s_ref[...]


def _retention(q, k, v, sg, qkvg_s, s0, s0_layer, lg_tbl):
    qk = RET_HEADS * RET_DK
    vd = RET_HEADS * RET_DV
    n_chunks = SEQ // RET_CHUNK
    rows_s = RET_SS * RET_PAD
    col = lambda j: (lambda i: (i, j))
    state_p = (None, None, RET_HEADS, RET_DK, RET_DV)
    state_s = (None, RET_SS, RET_HEADS, RET_DK, RET_DV)
    return pl.pallas_call(
        _retention_kernel,
        grid=(BATCH * n_chunks,),
        in_specs=[
            pl.BlockSpec((RET_HEADS, 8, 128), lambda i: (0, 0, 0)),
            pl.BlockSpec((RET_CHUNK, qk), col(0)), pl.BlockSpec((RET_CHUNK, qk), col(0)),
            pl.BlockSpec((RET_CHUNK, vd), col(0)), pl.BlockSpec((RET_CHUNK, vd), col(0)),
            pl.BlockSpec((rows_s, qk), col(0)), pl.BlockSpec((rows_s, qk), col(1)),
            pl.BlockSpec((rows_s, vd), col(1)), pl.BlockSpec((rows_s, vd), col(2)),
            pl.BlockSpec(state_s, lambda i: (s0_layer, i, 0, 0, 0)),
        ],
        out_specs=[pl.BlockSpec((RET_CHUNK, vd), col(0)),
                   pl.BlockSpec(state_p, lambda i: (0, i // n_chunks, 0, 0, 0)),
                   pl.BlockSpec((rows_s, vd), col(0)),
                   pl.BlockSpec(state_s, lambda i: (0, i, 0, 0, 0))],
        out_shape=[jax.ShapeDtypeStruct((N_PROMPT, vd), BF16),
                   jax.ShapeDtypeStruct((1, BATCH, RET_HEADS, RET_DK, RET_DV), F32),
                   jax.ShapeDtypeStruct((DEC_BATCH * RET_PAD, vd), BF16),
                   jax.ShapeDtypeStruct((1, DEC_BATCH, RET_HEADS, RET_DK, RET_DV), F32)],
        scratch_shapes=[pltpu.VMEM((RET_HEADS, RET_DK, RET_DV), F32),
                        pltpu.VMEM((RET_HEADS, RET_CHUNK, RET_CHUNK), F32),
                        pltpu.VMEM((RET_HEADS, RET_PAD, RET_PAD), F32)],
        compiler_params=_cparams("arbitrary"),
        name="retention",
    )(lg_tbl, q, k, v, sg, qkvg_s, qkvg_s, qkvg_s, qkvg_s, s0)


def _excl_prefix(col):
    r = lax.broadcasted_iota(I32, (N_EXPERTS, N_EXPERTS), 0)
    c = lax.broadcasted_iota(I32, (N_EXPERTS, N_EXPERTS), 1)
    lane = jnp.sum(jnp.where(r < c, col, 0.0), axis=0, keepdims=True)
    return jnp.sum(jnp.where(r == c, lane, 0.0), axis=1, keepdims=True)


def _ceil_to(x, m):
    return jnp.floor((x + (m - 1.0)) * (1.0 / m)) * m


def _route_kernel(te_ref, lpos_ref, rstart_ref, rlen_ref, tot_ref, tri_ref):
    rt = te_ref.shape[2]
    iota_e = lax.broadcasted_iota(I32, (N_EXPERTS, rt), 0)
    s = lax.broadcasted_iota(I32, (rt, rt), 0)
    t = lax.broadcasted_iota(I32, (rt, rt), 1)
    tri_ref[...] = (s < t).astype(BF16)

    def tile_onehots(i):
        te = te_ref[i]
        onehots = [(te[k:k + 1, :] == iota_e).astype(F32) for k in range(TOP_K)]
        cnt = jnp.sum(onehots[0] + onehots[1] + onehots[2] + onehots[3], axis=1, keepdims=True)
        return onehots, _ceil_to(cnt, RUN_ALIGN)

    tot = lax.fori_loop(0, MOE_NT, lambda i, acc: acc + tile_onehots(i)[1],
                        jnp.zeros((N_EXPERTS, 1), F32))
    tot_ref[...] = tot

    def place(i, gbase):
        onehots, run = tile_onehots(i)
        base = _excl_prefix(run)
        for k in range(TOP_K):
            oh = onehots[k]
            before = jnp.dot(oh.astype(BF16), tri_ref[...], preferred_element_type=F32)
            pos = jnp.sum(oh * (base + before), axis=0, keepdims=True)
            lpos_ref[i, k:k + 1, :] = pos.astype(I32)
            base = base + jnp.sum(oh, axis=1, keepdims=True)
        rstart_ref[i] = jnp.broadcast_to(gbase, rstart_ref.shape[1:]).astype(I32)
        rlen_ref[i] = jnp.broadcast_to(run, rlen_ref.shape[1:]).astype(I32)
        return gbase + run

    lax.fori_loop(0, MOE_NT, place, _excl_prefix(_ceil_to(tot, MOE_TB)))


def _route(te):
    full = lambda shape: pl.BlockSpec(shape, lambda i: (0,) * len(shape))
    return pl.pallas_call(
        _route_kernel,
        grid=(1,),
        in_specs=[full(te.shape)],
        out_specs=[full(te.shape), full((MOE_NT, N_EXPERTS, 128)), full((MOE_NT, N_EXPERTS, 128)),
                   full((N_EXPERTS, 1))],
        out_shape=[jax.ShapeDtypeStruct(te.shape, I32),
                   jax.ShapeDtypeStruct((MOE_NT, N_EXPERTS, 128), I32),
                   jax.ShapeDtypeStruct((MOE_NT, N_EXPERTS, 128), I32),
                   jax.ShapeDtypeStruct((N_EXPERTS, 1), F32)],
        scratch_shapes=[pltpu.VMEM((MOE_RT, MOE_RT), BF16)],
        compiler_params=_cparams("arbitrary"),
        name="moe_route",
    )(te)


def _dispatch_kernel(n_prompt_tiles, rs_ref, rl_ref, tzs_ref, tzl_ref, nu_ref, hp_ref, hs_ref,
                     lpos_ref, xb_ref, xs_ref, sem, zero_ref):
    i = pl.program_id(0)
    slot = i % 2
    zero_sem = sem.at[2]

    def tail_copy(e):
        n = pl.multiple_of(tzl_ref[e], RUN_ALIGN)
        dst = pl.multiple_of(tzs_ref[e], RUN_ALIGN)
        return pltpu.make_async_copy(zero_ref.at[pl.ds(0, n)], xb_ref.at[pl.ds(dst, n)], zero_sem)

    def block_copy(j):
        dst = pl.multiple_of(j * MOE_TB, MOE_TB)
        return pltpu.make_async_copy(zero_ref, xb_ref.at[pl.ds(dst, MOE_TB)], zero_sem)

    @pl.when(i == 0)
    def _():
        zero_ref[...] = jnp.zeros_like(zero_ref)

        def start(e, carry):
            pl.when(tzl_ref[e] > 0)(lambda: tail_copy(e).start())
            return carry

        def wait(e, carry):
            pl.when(tzl_ref[e] > 0)(lambda: tail_copy(e).wait())
            return carry

        lax.fori_loop(0, N_EXPERTS, start, 0)
        lax.fori_loop(nu_ref[0], MOE_NBLK, lambda j, c: (block_copy(j).start(), c)[1], 0)
        lax.fori_loop(0, N_EXPERTS, wait, 0)
        lax.fori_loop(nu_ref[0], MOE_NBLK, lambda j, c: (block_copy(j).wait(), c)[1], 0)

    rows = lax.broadcasted_iota(I32, (MOE_CAP, MOE_RT), 0)
    perm = jnp.where(rows == lpos_ref[0:1, :], 1.0, 0.0)
    for k in range(1, TOP_K):
        perm = perm + jnp.where(rows == lpos_ref[k:k + 1, :], 1.0, 0.0)
    h = jnp.where(i < n_prompt_tiles, hp_ref[...], hs_ref[...])
    xs_ref[slot] = _pack_halves(jnp.dot(perm.astype(BF16), h, preferred_element_type=F32))

    def scatter(tile, wait):
        def one(e, lo, priority):
            n = pl.multiple_of(rl_ref[tile * N_EXPERTS + e], RUN_ALIGN)
            sl = tile % 2
            src = pl.multiple_of(lo, RUN_ALIGN)
            dst = pl.multiple_of(rs_ref[tile * N_EXPERTS + e], RUN_ALIGN)
            cp = pltpu.make_async_copy(xs_ref.at[sl, pl.ds(src, n)], xb_ref.at[pl.ds(dst, n)],
                                       sem.at[sl])

            @pl.when(n > 0)
            def _():
                if wait:
                    cp.wait()
                else:
                    cp.start(priority=priority)

            return lo + n

        def body(ee, lo):
            return one(2 * ee + 1, one(2 * ee, lo, 0), 1)

        lax.fori_loop(0, N_EXPERTS // 2, body, 0)

    scatter(i, False)

    @pl.when(i > 0)
    def _():
        scatter(i - 1, True)

    @pl.when(i == pl.num_programs(0) - 1)
    def _():
        scatter(i, True)


def _dispatch(tables, h_p, h_s, lpos):
    nt_p = h_p.shape[0] // MOE_RT
    return pl.pallas_call(
        functools.partial(_dispatch_kernel, nt_p),
        grid_spec=pltpu.PrefetchScalarGridSpec(
            num_scalar_prefetch=len(tables),
            grid=(MOE_NT,),
            in_specs=[
                pl.BlockSpec((MOE_RT, D_MODEL), lambda i, *_: (jnp.minimum(i, nt_p - 1), 0)),
                pl.BlockSpec((MOE_RT, D_MODEL), lambda i, *_: (jnp.maximum(i - nt_p, 0), 0)),
                pl.BlockSpec((None, TOP_K, MOE_RT), lambda i, *_: (i, 0, 0)),
            ],
            out_specs=pl.BlockSpec(memory_space=pl.ANY),
            scratch_shapes=[pltpu.VMEM((2, MOE_CAP, D_MODEL // 2), U32),
                            pltpu.SemaphoreType.DMA((3,)),
                            pltpu.VMEM((MOE_TB, D_MODEL // 2), U32)],
        ),
        out_shape=jax.ShapeDtypeStruct((MOE_ROWS, D_MODEL // 2), U32),
        compiler_params=_cparams("arbitrary"),
        name="moe_dispatch",
    )(*tables, h_p, h_s, lpos)


def _ffn_kernel(layer, be_ref, nu_ref, nxt_ref, bgu_ref, bdn_ref, xb_hbm, wgu_hbm, wdn_hbm,
                yb_hbm, wgu_f_ref, wdn_f_ref, wgu_b_ref, wdn_b_ref, act_ref, xbuf_ref, ybuf_ref,
                wsem, xsem, ysem):
    n_used = nu_ref[0]

    def weight_copies(ex):
        return (pltpu.make_async_copy(wgu_hbm.at[layer, ex], wgu_f_ref, wsem.at[0]),
                pltpu.make_async_copy(wdn_hbm.at[layer, ex], wdn_f_ref, wsem.at[1]))

    def block_rows(j):
        return pl.ds(pl.multiple_of(j * MOE_TB, MOE_TB), MOE_TB)

    def x_copy(j):
        return pltpu.make_async_copy(xb_hbm.at[block_rows(j)], xbuf_ref.at[j % 2], xsem.at[j % 2])

    def y_copy(j):
        return pltpu.make_async_copy(ybuf_ref.at[j % 2], yb_hbm.at[block_rows(j)], ysem.at[j % 2])

    for cp in weight_copies(be_ref[0]):
        cp.start()
    x_copy(0).start()

    def block(j, carry):
        e = be_ref[j]
        e_prev = be_ref[jnp.maximum(j - 1, 0)]

        @pl.when(j + 1 < n_used)
        def _():
            x_copy(j + 1).start()

        @pl.when((j == 0) | (e != e_prev))
        def _():
            for cp in weight_copies(e):
                cp.wait()
            wgu_b_ref[...] = wgu_f_ref[...].astype(BF16)
            wdn_b_ref[...] = wdn_f_ref[...].astype(BF16)
            nx = nxt_ref[j]

            @pl.when(nx >= 0)
            def _():
                for cp in weight_copies(nx):
                    cp.start()

        x_copy(j).wait()

        @pl.when(j >= 2)
        def _():
            y_copy(j - 2).wait()

        x = _unpack_halves(xbuf_ref[j % 2])
        bgu = bgu_ref[e]
        for c in range(D_FF // FF_CHUNK):
            gs = slice(c * FF_CHUNK, (c + 1) * FF_CHUNK)
            us = slice(D_FF + c * FF_CHUNK, D_FF + (c + 1) * FF_CHUNK)
            gate = jnp.dot(x, wgu_b_ref[:, gs], preferred_element_type=F32) + bgu[:, gs]
            up = jnp.dot(x, wgu_b_ref[:, us], preferred_element_type=F32) + bgu[:, us]
            gate = jnp.minimum(gate, SWIGLU_LIMIT)
            up = jnp.clip(up, -SWIGLU_LIMIT, SWIGLU_LIMIT)
            act = (up + 1.0) * (gate * jax.nn.sigmoid(SWIGLU_ALPHA * gate))
            act_ref[:, gs] = act.astype(BF16)
        y = jnp.dot(act_ref[...], wdn_b_ref[...], preferred_element_type=F32) + bdn_ref[e]
        ybuf_ref[j % 2] = _pack_halves(y.astype(BF16).astype(F32))
        y_copy(j).start()
        return carry

    lax.fori_loop(0, n_used, block, 0)

    @pl.when(n_used >= 2)
    def _():
        y_copy(n_used - 2).wait()

    y_copy(n_used - 1).wait()

    ybuf_ref[0] = jnp.zeros_like(ybuf_ref[0])

    def zero_copy(j):
        return pltpu.make_async_copy(ybuf_ref.at[0], yb_hbm.at[block_rows(j)], ysem.at[0])

    lax.fori_loop(n_used, MOE_NBLK, lambda j, c: (zero_copy(j).start(), c)[1], 0)
    lax.fori_loop(n_used, MOE_NBLK, lambda j, c: (zero_copy(j).wait(), c)[1], 0)


def _ffn(layer, block_e, n_used, nxt, xb, w_gu, b_gu, w_dn, b_dn):
    bias_map = lambda i, *_: (layer, 0, 0, 0)
    return pl.pallas_call(
        functools.partial(_ffn_kernel, layer),
        grid_spec=pltpu.PrefetchScalarGridSpec(
            num_scalar_prefetch=3,
            grid=(1,),
            in_specs=[
                pl.BlockSpec((None, N_EXPERTS, 1, 2 * D_FF), bias_map),
                pl.BlockSpec((None, N_EXPERTS, 1, D_MODEL), bias_map),
                pl.BlockSpec(memory_space=pl.ANY),
                pl.BlockSpec(memory_space=pl.ANY),
                pl.BlockSpec(memory_space=pl.ANY),
            ],
            out_specs=pl.BlockSpec(memory_space=pl.ANY),
            scratch_shapes=[pltpu.VMEM((D_MODEL, 2 * D_FF), F32),
                            pltpu.VMEM((D_FF, D_MODEL), F32),
                            pltpu.VMEM((D_MODEL, 2 * D_FF), BF16),
                            pltpu.VMEM((D_FF, D_MODEL), BF16),
                            pltpu.VMEM((MOE_TB, D_FF), BF16),
                            pltpu.VMEM((2, MOE_TB, D_MODEL // 2), U32),
                            pltpu.VMEM((2, MOE_TB, D_MODEL // 2), U32),
                            pltpu.SemaphoreType.DMA((2,)),
                            pltpu.SemaphoreType.DMA((2,)),
                            pltpu.SemaphoreType.DMA((2,))],
        ),
        out_shape=jax.ShapeDtypeStruct((MOE_ROWS, D_MODEL // 2), U32),
        compiler_params=_cparams("arbitrary"),
        name="moe_ffn",
    )(block_e, n_used, nxt, b_gu.reshape(b_gu.shape[0], N_EXPERTS, 1, 2 * D_FF),
      b_dn.reshape(b_dn.shape[0], N_EXPERTS, 1, D_MODEL), xb, w_gu, w_dn)


def _tile_rows(v, n_rows):
    mr = v.shape[0]
    if mr in (1, n_rows):
        return v
    return jnp.concatenate([v] * (n_rows // mr), axis=0)


def _combine_kernel(tile_off, final, rs_ref, rl_ref, x_ref, gf_ref, lpt_ref, wt_ref, fg_ref,
                    yb_ref, out_ref, ys_ref, sem):
    i = pl.program_id(0)

    def gather(tile, wait):
        def one(e, lo, priority):
            n = pl.multiple_of(rl_ref[(tile_off + tile) * N_EXPERTS + e], RUN_ALIGN)
            sl = tile % 2
            src = pl.multiple_of(rs_ref[(tile_off + tile) * N_EXPERTS + e], RUN_ALIGN)
            dst = pl.multiple_of(lo, RUN_ALIGN)
            cp = pltpu.make_async_copy(yb_ref.at[pl.ds(src, n)], ys_ref.at[sl, pl.ds(dst, n)],
                                       sem.at[sl])

            @pl.when(n > 0)
            def _():
                if wait:
                    cp.wait()
                else:
                    cp.start(priority=priority)

            return lo + n

        def body(ee, lo):
            return one(2 * ee + 1, one(2 * ee, lo, 0), 1)

        lax.fori_loop(0, N_EXPERTS // 2, body, 0)

    @pl.when(i == 0)
    def _():
        ys_ref[...] = jnp.zeros_like(ys_ref)
        gather(0, False)

    @pl.when(i + 1 < pl.num_programs(0))
    def _():
        gather(i + 1, False)

    gather(i, True)

    cols = lax.broadcasted_iota(I32, (MOE_RT, MOE_CAP), 1)
    w = jnp.where(cols == lpt_ref[:, 0:1], wt_ref[:, 0:1], 0.0)
    for k in range(1, TOP_K):
        w = w + jnp.where(cols == lpt_ref[:, k:k + 1], wt_ref[:, k:k + 1], 0.0)
    ys = _unpack_halves(ys_ref[i % 2])
    y = jnp.dot(w.astype(BF16), ys, preferred_element_type=F32)
    xn = x_ref[...] + _tile_rows(gf_ref[...], MOE_RT) * y
    if final:
        xn = xn * lax.rsqrt(jnp.mean(xn * xn, axis=-1, keepdims=True) + NORM_EPS) * fg_ref[...]
    out_ref[...] = xn


def _combine(tile_off, n_tiles, final, tables, x, mod, mod_map, lpos_t, wt, final_g, yb):
    pm = lambda i, *_: (i, 0)
    tm = lambda i, *_: (i + tile_off, 0)
    return pl.pallas_call(
        functools.partial(_combine_kernel, tile_off, final),
        grid_spec=pltpu.PrefetchScalarGridSpec(
            num_scalar_prefetch=len(tables),
            grid=(n_tiles,),
            in_specs=[
                pl.BlockSpec((MOE_RT, D_MODEL), pm),
                pl.BlockSpec((None, mod.shape[1], D_MODEL), mod_map),
                pl.BlockSpec((MOE_RT, TOP_K), tm),
                pl.BlockSpec((MOE_RT, TOP_K), tm),
                pl.BlockSpec((1, D_MODEL), lambda i, *_: (0, 0)),
                pl.BlockSpec(memory_space=pl.ANY),
            ],
            out_specs=pl.BlockSpec((MOE_RT, D_MODEL), pm),
            scratch_shapes=[pltpu.VMEM((2, MOE_CAP, D_MODEL // 2), U32),
                            pltpu.SemaphoreType.DMA((2,))],
        ),
        out_shape=jax.ShapeDtypeStruct(x.shape, F32),
        compiler_params=_cparams("arbitrary"),
        name="moe_combine",
    )(*tables, x, mod, lpos_t, wt, final_g.reshape(1, D_MODEL), yb)


def _moe(layer, final, st_p, st_s, x_p, x_s, post_p, post_s,
         w_gu, b_gu, w_dn, b_dn, final_g):
    te = jnp.concatenate([post_p[1], post_s[1]], axis=1)
    tw = jnp.concatenate([post_p[2], post_s[2]], axis=1)
    te = jnp.swapaxes(te.reshape(TOP_K, MOE_NT, MOE_RT), 0, 1)
    lpos, rstart, rlen, tot = _route(te)
    tot = tot[:, 0].astype(I32)
    padded = (tot + MOE_TB - 1) // MOE_TB * MOE_TB
    pends = jnp.cumsum(padded)
    n_used = (pends[-1] // MOE_TB).reshape(1)
    blk = jnp.minimum(jnp.arange(MOE_NBLK, dtype=I32), n_used[0] - 1) * MOE_TB
    block_e = jnp.minimum(jnp.sum(pends[None, :] <= blk[:, None], axis=-1),
                          N_EXPERTS - 1).astype(I32)
    ids = jnp.where(tot > 0, jnp.arange(N_EXPERTS, dtype=I32), N_EXPERTS)
    later = jnp.concatenate([lax.cummin(ids[::-1])[::-1][1:], jnp.full((1,), N_EXPERTS, I32)])
    nxt_of = jnp.where(later < N_EXPERTS, later, -1)
    of_block = block_e[:, None] == jnp.arange(N_EXPERTS, dtype=I32)[None, :]
    nxt = jnp.sum(jnp.where(of_block, nxt_of[None, :], 0), axis=-1)
    run_tables = (rstart[:, :, 0].reshape(-1), rlen[:, :, 0].reshape(-1))
    tail_tables = (pends - padded + tot, padded - tot)

    nt_p = N_PROMPT // MOE_RT
    nt_s = N_SAMPLE // MOE_RT
    xb = _dispatch(run_tables + tail_tables + (n_used,), post_p[0], post_s[0], lpos)
    yb = _ffn(layer, block_e, n_used, nxt, xb, w_gu, b_gu, w_dn, b_dn)
    lpos_t = jnp.swapaxes(lpos, 1, 2).reshape(N_TOK, TOP_K)
    wt = tw.T
    tiles_per_seq = SEQ // MOE_RT
    out_p = _combine(0, nt_p, final, run_tables, x_p, st_p.mod,
                     lambda i, *_: (i // tiles_per_seq, 0, N_MOD - 1), lpos_t, wt, final_g, yb)
    out_s = _combine(nt_p, nt_s, final, run_tables, x_s, st_s.mod,
                     lambda i, *_: (0, 0, N_MOD - 1), lpos_t, wt, final_g, yb)
    return out_p, out_s


def _rope_tables(pos):
    half = RET_DK // 2
    inv_freq = ROPE_BASE ** (-jnp.arange(half, dtype=F32) / half)
    ang = pos.astype(F32)[:, None] * inv_freq[None, :]
    return jnp.cos(ang), jnp.sin(ang)


def kernel(x_prompt, x_sample, c_prompt, c_sample, state_conv, state_ret, norm_mix_g, norm_ff_g,
           w_mod, b_mod, conv_w1, conv_b1, conv_dw, conv_dw_b, conv_ln_g, conv_ln_b, conv_w2,
           conv_b2, ret_w_in, ret_w_o, router_w, router_b, moe_w_gu, moe_b_gu, moe_w_dn,
           moe_b_dn, final_g):
    mod = _adaln(jnp.concatenate([c_prompt, c_sample], axis=0), w_mod, b_mod)
    x_p = x_prompt.reshape(N_PROMPT, D_MODEL)
    x_s = jnp.swapaxes(x_sample, 0, 1).reshape(N_SAMPLE, D_MODEL)
    router_wt = jnp.swapaxes(router_w, 1, 2)

    def streams(layer):
        st_p = _Stream(N_PROMPT, TM_PROMPT, SEQ // TM_PROMPT,
                       mod[layer, :BATCH].reshape(BATCH, 1, N_MOD * D_MODEL))
        st_s = _Stream(N_SAMPLE, TM_SAMPLE, None,
                       mod[layer, BATCH:].reshape(1, DEC_BATCH, N_MOD * D_MODEL))
        return st_p, st_s

    st_p, st_s = streams(0)
    u_p = _pre_conv(st_p, x_p, norm_mix_g, conv_w1, conv_b1, 0, 0)
    u_s = _pre_conv(st_s, x_s, norm_mix_g, conv_w1, conv_b1, 0, 0)
    buf_tm = jnp.swapaxes(state_conv[0], 0, 1)
    z_s, ns_tm = _conv_sample(buf_tm, u_s.reshape(DEC_SEQ, DEC_BATCH, D_MODEL),
                              conv_dw, conv_dw_b, 0)
    post_p = _mixer_post("conv", st_p, 0, 0, x_p, u_p, norm_ff_g, router_wt, router_b,
                         conv_w2, conv_b2, conv_dw, conv_dw_b, conv_ln_g, conv_ln_b)
    post_s = _mixer_post("z", st_s, 0, 0, x_s, z_s.reshape(N_SAMPLE, D_MODEL), norm_ff_g,
                         router_wt, router_b, conv_w2, conv_b2, None, None, conv_ln_g, conv_ln_b)
    x_p, x_s = _moe(0, False, st_p, st_s, post_p[0], post_s[0], post_p[1:], post_s[1:],
                    moe_w_gu, moe_b_gu, moe_w_dn, moe_b_dn, final_g)
    conv_p = u_p.reshape(BATCH, SEQ, D_MODEL)[:, SEQ - CONV_STATE:][None]
    conv_s = jnp.swapaxes(ns_tm, 0, 1)[None]

    st_p, st_s = streams(1)
    w_in_b = ret_w_in.astype(BF16)
    cos_p, sin_p = _rope_tables(jnp.arange(SEQ, dtype=I32))
    cos_s, sin_s = _rope_tables(PAST_LEN + jnp.arange(DEC_SEQ, dtype=I32))
    half = RET_DK // 2
    tps = SEQ // TM_PROMPT
    tbl_p = pl.BlockSpec((TM_PROMPT, half), lambda i: (i % tps, 0))
    tbl_s = pl.BlockSpec((None, 1, half), lambda i: (i, 0, 0))
    q_p, k_p, v_p, sg_p = _pre_ret(st_p, x_p, norm_mix_g, w_in_b, 1, 0, cos_p, sin_p, tbl_p)
    q_s, k_s, v_s, sg_s = _pre_ret(st_s, x_s, norm_mix_g, w_in_b, 1, 0,
                                   cos_s.reshape(DEC_SEQ, 1, half),
                                   sin_s.reshape(DEC_SEQ, 1, half), tbl_s)
    lg = jnp.log1p(-jnp.exp2(-5.0 - jnp.arange(RET_HEADS, dtype=F32)))
    lg_tbl = jnp.broadcast_to(lg[:, None, None], (RET_HEADS, 8, 128))
    def to_seq_major(a):
        a = jnp.swapaxes(a.reshape(DEC_SEQ, DEC_BATCH, -1), 0, 1)
        a = jnp.pad(a, ((0, 0), (0, RET_PAD - DEC_SEQ), (0, 0)))
        return a.reshape(DEC_BATCH * RET_PAD, -1)

    qkvg_s = to_seq_major(jnp.concatenate([q_s, k_s, v_s, sg_s], axis=1))
    gated_p, ret_p, gated_s, ret_s = _retention(q_p, k_p, v_p, sg_p, qkvg_s, state_ret, 0, lg_tbl)
    gated_s = jnp.swapaxes(gated_s.reshape(DEC_BATCH, RET_PAD, -1)[:, :DEC_SEQ], 0, 1)
    gated_s = gated_s.reshape(N_SAMPLE, -1)
    post_p = _mixer_post("ret", st_p, 1, 0, x_p, gated_p, norm_ff_g, router_wt, router_b, ret_w_o)
    post_s = _mixer_post("ret", st_s, 1, 0, x_s, gated_s, norm_ff_g, router_wt, router_b, ret_w_o)
    y_p, y_s = _moe(1, True, st_p, st_s, post_p[0], post_s[0], post_p[1:], post_s[1:],
                    moe_w_gu, moe_b_gu, moe_w_dn, moe_b_dn, final_g)

    y_prompt = y_p.reshape(BATCH, SEQ, D_MODEL)
    y_sample = jnp.swapaxes(y_s.reshape(DEC_SEQ, DEC_BATCH, D_MODEL), 0, 1)
    return (y_prompt, y_sample, conv_p, conv_s, ret_p, ret_s)
```
